```python
import jax, jax.numpy as jnp
from jax import lax
import numpy as np

D_MODEL = 2048
BATCH = 8
SEQ = 8192
DEPTH = 1

HEAD_DIM = 128
N_HEADS_DIL = 8
N_HEADS_SB = 8
DIL_PATTERNS = ((128, 1), (512, 4), (2048, 16))
BLOCK = 128
ROT_DIM = HEAD_DIM // 4
ROPE_THETA = 500000.0
D_FF = 4 * D_MODEL
PLE_DIM = 256
EPS = 1e-6
W_DIL = N_HEADS_DIL * HEAD_DIM
W_SB = N_HEADS_SB * HEAD_DIM
D_IN = 3 * W_DIL + 3 * W_SB + 2 * D_MODEL

kernel_name = "hybrid_dilated_stickbreaking_gated_block"


def rmsnorm(x, g):
    xf = x.astype(jnp.float32)
    y = xf * lax.rsqrt(jnp.mean(xf * xf, axis=-1, keepdims=True) + EPS)
    return (y * g.astype(jnp.float32)).astype(x.dtype)


def partial_rope(x, pos):
    half = ROT_DIM // 2
    inv = ROPE_THETA ** (-jnp.arange(0, ROT_DIM, 2, dtype=jnp.float32) / ROT_DIM)
    ang = pos[:, None] * inv[None, :]
    cos = jnp.cos(ang)[None, :, None, :]
    sin = jnp.sin(ang)[None, :, None, :]
    xr = x[..., :ROT_DIM].astype(jnp.float32)
    x1, x2 = xr[..., :half], xr[..., half:]
    rot = jnp.concatenate([x1 * cos - x2 * sin, x2 * cos + x1 * sin], axis=-1)
    return jnp.concatenate([rot.astype(x.dtype), x[..., ROT_DIM:]], axis=-1)


def dilated_window(q, k, v, window, dilation):
    B, S, H, Dh = q.shape
    W = window // dilation
    assert W <= BLOCK
    M = S // dilation
    Mp = -(-M // BLOCK) * BLOCK
    nb = Mp // BLOCK

    def to_blocks(t):
        t = t.astype(jnp.float32).reshape(B, M, dilation, H, Dh).transpose(0, 2, 3, 1, 4)
        t = jnp.pad(t, ((0, 0), (0, 0), (0, 0), (0, Mp - M), (0, 0)))
        return t.reshape(B, dilation, H, nb, BLOCK, Dh)

    qb, kb, vb = to_blocks(q), to_blocks(k), to_blocks(v)

    def with_prev(t):
        prev = jnp.pad(t, ((0, 0), (0, 0), (0, 0), (1, 0), (0, 0), (0, 0)))[:, :, :, :-1]
        return jnp.concatenate([prev, t], axis=-2)

    kw, vw = with_prev(kb), with_prev(vb)
    s = jnp.einsum('bdhnqe,bdhnke->bdhnqk', qb, kw) * (HEAD_DIM ** -0.5)
    n_i = jnp.arange(nb)[:, None, None]
    q_i = jnp.arange(BLOCK)[None, :, None]
    k_i = jnp.arange(2 * BLOCK)[None, None, :]
    dist = BLOCK + q_i - k_i
    valid = (dist >= 0) & (dist <= W) & ((n_i > 0) | (k_i >= BLOCK))
    s = jnp.where(valid, s, -jnp.inf)
    m = jnp.max(s, axis=-1, keepdims=True)
    e = jnp.exp(s - m)
    den = jnp.sum(e, axis=-1, keepdims=True)
    o = jnp.einsum('bdhnqk,bdhnke->bdhnqe', e, vw) / den
    lse = (m + jnp.log(den))[..., 0]

    o = o.reshape(B, dilation, H, Mp, Dh)[:, :, :, :M].transpose(0, 3, 1, 2, 4).reshape(B, S, H, Dh)
    lse = lse.reshape(B, dilation, H, Mp)[:, :, :, :M].transpose(0, 3, 1, 2).reshape(B, S, H)
    return o, lse


def dilated_mixture(q, k, v):
    outs, lses = [], []
    for window, dilation in DIL_PATTERNS:
        o, l = dilated_window(q, k, v, window, dilation)
        outs.append(o)
        lses.append(l)
    o = jnp.stack(outs, axis=0)
    w = jax.nn.softmax(jnp.stack(lses, axis=0), axis=0)
    return jnp.sum(w[..., None] * o, axis=0)


def stick_breaking(q, k, v):
    B, S, H, Dh = q.shape
    nq = S // BLOCK
    qf = q.astype(jnp.float32).transpose(0, 2, 1, 3)
    kf = k.astype(jnp.float32).transpose(0, 2, 1, 3)
    vf = v.astype(jnp.float32).transpose(0, 2, 1, 3)
    qblocks = qf.reshape(B, H, nq, BLOCK, Dh).transpose(2, 0, 1, 3, 4)
    key_pos = jnp.arange(S)

    def one_block(args):
        qb, n = args
        z = jnp.einsum('bhqe,bhke->bhqk', qb, kf) * (HEAD_DIM ** -0.5)
        q_pos = n * BLOCK + jnp.arange(BLOCK)
        causal = key_pos[None, :] < q_pos[:, None]
        log_1mb = jnp.where(causal, jax.nn.log_sigmoid(-z), 0.0)
        after = lax.cumsum(log_1mb, axis=3, reverse=True) - log_1mb
        a = jnp.where(causal, jnp.exp(jax.nn.log_sigmoid(z) + after), 0.0)
        return jnp.einsum('bhqk,bhke->bhqe', a, vf)

    o = lax.map(one_block, (qblocks, jnp.arange(nq)))
    return o.transpose(1, 0, 3, 2, 4).reshape(B, S, H, Dh)


def _fwd_setup_inputs(seed: int = 0) -> dict:
    key = jax.random.key(seed)
    ks = jax.random.split(key, 16)

    def w(k, shape, fan_in):
        return jax.random.normal(k, shape, jnp.float32) * (fan_in ** -0.5)

    def gain(k, shape):
        return 1.0 + 0.05 * jax.random.normal(k, shape, jnp.float32)

    return {
        "x": jax.random.normal(ks[0], (BATCH, SEQ, D_MODEL), jnp.float32),
        "p": jax.random.normal(ks[1], (DEPTH, BATCH, SEQ, PLE_DIM), jnp.float32),
        "g_mix": gain(ks[2], (DEPTH, D_MODEL)),
        "w_in": w(ks[3], (DEPTH, D_MODEL, D_IN), D_MODEL),
        "qn_gain": gain(ks[4], (DEPTH, HEAD_DIM)),
        "kn_gain": gain(ks[5], (DEPTH, HEAD_DIM)),
        "w_branch_a": w(ks[6], (DEPTH, W_DIL, D_MODEL), W_DIL),
        "w_branch_b": w(ks[7], (DEPTH, W_SB, D_MODEL), W_SB),
        "w_out": w(ks[8], (DEPTH, D_MODEL, D_MODEL), D_MODEL),
        "g_mlp": gain(ks[9], (DEPTH, D_MODEL)),
        "w_up": w(ks[10], (DEPTH, D_MODEL, D_FF), D_MODEL),
        "w_down": w(ks[11], (DEPTH, D_FF, D_MODEL), D_FF),
        "g_ple": gain(ks[12], (DEPTH, D_MODEL)),
        "w_ple_gate": w(ks[13], (DEPTH, D_MODEL, D_MODEL), D_MODEL),
        "w_ple_proj": w(ks[14], (DEPTH, PLE_DIM, D_MODEL), PLE_DIM),
    }


def _fwd_reference(x, p, g_mix, w_in, qn_gain, kn_gain, w_branch_a, w_branch_b, w_out,
              g_mlp, w_up, w_down, g_ple, w_ple_gate, w_ple_proj):
    B, S, _ = x.shape
    pos = jnp.arange(S, dtype=jnp.float32)
    splits = np.cumsum([W_DIL, W_DIL, W_DIL, W_SB, W_SB, W_SB, D_MODEL])
    for i in range(DEPTH):
        h = rmsnorm(x, g_mix[i])
        proj = h @ w_in[i]
        qa, ka, va, qb, kb, vb, ga, gb = jnp.split(proj, splits, axis=-1)
        qa = qa.reshape(B, S, N_HEADS_DIL, HEAD_DIM)
        ka = ka.reshape(B, S, N_HEADS_DIL, HEAD_DIM)
        va = va.reshape(B, S, N_HEADS_DIL, HEAD_DIM)
        qa = partial_rope(rmsnorm(qa, qn_gain[i]), pos)
        ka = partial_rope(rmsnorm(ka, kn_gain[i]), pos)
        ya = dilated_mixture(qa, ka, va).astype(x.dtype).reshape(B, S, W_DIL)

        qb = qb.reshape(B, S, N_HEADS_SB, HEAD_DIM)
        kb = kb.reshape(B, S, N_HEADS_SB, HEAD_DIM)
        vb = vb.reshape(B, S, N_HEADS_SB, HEAD_DIM)
        yb = stick_breaking(qb, kb, vb).astype(x.dtype).reshape(B, S, W_SB)

        merged = jax.nn.sigmoid(ga) * (ya @ w_branch_a[i]) + jax.nn.sigmoid(gb) * (yb @ w_branch_b[i])
        x = x + merged @ w_out[i]

        hm = rmsnorm(x, g_mlp[i])
        x = x + jnp.square(jax.nn.relu(hm @ w_up[i])) @ w_down[i]

        hp = rmsnorm(x, g_ple[i])
        x = x + (p[i] @ w_ple_proj[i]) * jax.nn.sigmoid(hp @ w_ple_gate[i])
    return x


import jax as _jax
import jax.numpy as _jnp

TWIN_FORMAT = 'train_step'
FWD_PARAMS = ['x', 'p', 'g_mix', 'w_in', 'qn_gain', 'kn_gain', 'w_branch_a', 'w_branch_b', 'w_out', 'g_mlp', 'w_up', 'w_down', 'g_ple', 'w_ple_gate', 'w_ple_proj']
TWIN_WEIGHTS = ['g_mix', 'w_in', 'qn_gain', 'kn_gain', 'w_branch_a', 'w_branch_b', 'w_out', 'g_mlp', 'w_up', 'w_down', 'g_ple', 'w_ple_gate', 'w_ple_proj']
TWIN_DIFF_INPUT = 'x'
TWIN_INPUTS = ['x', 'p', 'g_mix', 'w_in', 'qn_gain', 'kn_gain', 'w_branch_a', 'w_branch_b', 'w_out', 'g_mlp', 'w_up', 'w_down', 'g_ple', 'w_ple_gate', 'w_ple_proj', 'loss_target', 'm_g_mix', 'm_w_in', 'm_qn_gain', 'm_kn_gain', 'm_w_branch_a', 'm_w_branch_b', 'm_w_out', 'm_g_mlp', 'm_w_up', 'm_w_down', 'm_g_ple', 'm_w_ple_gate', 'm_w_ple_proj', 'v_g_mix', 'v_w_in', 'v_qn_gain', 'v_kn_gain', 'v_w_branch_a', 'v_w_branch_b', 'v_w_out', 'v_g_mlp', 'v_w_up', 'v_w_down', 'v_g_ple', 'v_w_ple_gate', 'v_w_ple_proj']
TWIN_OUTPUTS = ['loss', 'grad_x', 'grad_g_mix', 'grad_w_in', 'grad_qn_gain', 'grad_kn_gain', 'grad_w_branch_a', 'grad_w_branch_b', 'grad_w_out', 'grad_g_mlp', 'grad_w_up', 'grad_w_down', 'grad_g_ple', 'grad_w_ple_gate', 'grad_w_ple_proj', 'delta_g_mix', 'delta_w_in', 'delta_qn_gain', 'delta_kn_gain', 'delta_w_branch_a', 'delta_w_branch_b', 'delta_w_out', 'delta_g_mlp', 'delta_w_up', 'delta_w_down', 'delta_g_ple', 'delta_w_ple_gate', 'delta_w_ple_proj', 'new_m_g_mix', 'new_m_w_in', 'new_m_qn_gain', 'new_m_kn_gain', 'new_m_w_branch_a', 'new_m_w_branch_b', 'new_m_w_out', 'new_m_g_mlp', 'new_m_w_up', 'new_m_w_down', 'new_m_g_ple', 'new_m_w_ple_gate', 'new_m_w_ple_proj', 'new_v_g_mix', 'new_v_w_in', 'new_v_qn_gain', 'new_v_kn_gain', 'new_v_w_branch_a', 'new_v_w_branch_b', 'new_v_w_out', 'new_v_g_mlp', 'new_v_w_up', 'new_v_w_down', 'new_v_g_ple', 'new_v_w_ple_gate', 'new_v_w_ple_proj']
TWIN_LEAF_KINDS = {'loss': 'loss', 'grad_x': 'grad_x', 'grad_g_mix': 'grad_w', 'grad_w_in': 'grad_w', 'grad_qn_gain': 'grad_w', 'grad_kn_gain': 'grad_w', 'grad_w_branch_a': 'grad_w', 'grad_w_branch_b': 'grad_w', 'grad_w_out': 'grad_w', 'grad_g_mlp': 'grad_w', 'grad_w_up': 'grad_w', 'grad_w_down': 'grad_w', 'grad_g_ple': 'grad_w', 'grad_w_ple_gate': 'grad_w', 'grad_w_ple_proj': 'grad_w', 'delta_g_mix': 'delta_w', 'delta_w_in': 'delta_w', 'delta_qn_gain': 'delta_w', 'delta_kn_gain': 'delta_w', 'delta_w_branch_a': 'delta_w', 'delta_w_branch_b': 'delta_w', 'delta_w_out': 'delta_w', 'delta_g_mlp': 'delta_w', 'delta_w_up': 'delta_w', 'delta_w_down': 'delta_w', 'delta_g_ple': 'delta_w', 'delta_w_ple_gate': 'delta_w', 'delta_w_ple_proj': 'delta_w', 'new_m_g_mix': 'new_m', 'new_m_w_in': 'new_m', 'new_m_qn_gain': 'new_m', 'new_m_kn_gain': 'new_m', 'new_m_w_branch_a': 'new_m', 'new_m_w_branch_b': 'new_m', 'new_m_w_out': 'new_m', 'new_m_g_mlp': 'new_m', 'new_m_w_up': 'new_m', 'new_m_w_down': 'new_m', 'new_m_g_ple': 'new_m', 'new_m_w_ple_gate': 'new_m', 'new_m_w_ple_proj': 'new_m', 'new_v_g_mix': 'new_v', 'new_v_w_in': 'new_v', 'new_v_qn_gain': 'new_v', 'new_v_kn_gain': 'new_v', 'new_v_w_branch_a': 'new_v', 'new_v_w_branch_b': 'new_v', 'new_v_w_out': 'new_v', 'new_v_g_mlp': 'new_v', 'new_v_w_up': 'new_v', 'new_v_w_down': 'new_v', 'new_v_g_ple': 'new_v', 'new_v_w_ple_gate': 'new_v', 'new_v_w_ple_proj': 'new_v'}


def _forward(args):
    return _fwd_reference(*[args[k] for k in FWD_PARAMS])


def _output_shape():
    def fwd():
        inp = _fwd_setup_inputs(0)
        return _fwd_reference(*[inp[k] for k in FWD_PARAMS])
    out = _jax.eval_shape(fwd)
    return out.shape, out.dtype

N_MICROBATCH = 1
ADAM_LR = 0.001
ADAM_B1 = 0.9
ADAM_B2 = 0.999
ADAM_EPS = 1e-08
ADAM_WD = 0.01
ADAM_STEP = 10
PER_EXAMPLE_BATCH_AXIS = {'x': 0, 'p': 1, 'loss_target': 0}
SHARED_INPUTS = []
_WEIGHT_DTYPES = {'g_mix': _jnp.float32, 'w_in': _jnp.float32, 'qn_gain': _jnp.float32, 'kn_gain': _jnp.float32, 'w_branch_a': _jnp.float32, 'w_branch_b': _jnp.float32, 'w_out': _jnp.float32, 'g_mlp': _jnp.float32, 'w_up': _jnp.float32, 'w_down': _jnp.float32, 'g_ple': _jnp.float32, 'w_ple_gate': _jnp.float32, 'w_ple_proj': _jnp.float32}
MOMENT_SCALE = {'g_mix': 5.145341e+00, 'w_in': 1.284271e-01, 'qn_gain': 1.040033e+00, 'kn_gain': 1.040486e+00, 'w_branch_a': 1.057123e-01, 'w_branch_b': 2.533106e-01, 'w_out': 2.602537e-01, 'g_mlp': 9.637635e+01, 'w_up': 4.910513e-01, 'w_down': 8.000172e+00, 'g_ple': 1.397589e+00, 'w_ple_gate': 9.964311e-01, 'w_ple_proj': 4.448109e-01}


def _to_microbatches(a, axis):
    t = _jnp.moveaxis(a, axis, 0)
    t = t.reshape((N_MICROBATCH, t.shape[0] // N_MICROBATCH) + t.shape[1:])
    return _jnp.moveaxis(t, 1, axis + 1)


def setup_inputs(seed: int = 0) -> dict:
    inp = _fwd_setup_inputs(seed)
    key = _jax.random.fold_in(_jax.random.key(seed), 7919)
    shape, _ = _output_shape()
    out = dict(inp)
    out["loss_target"] = _jax.random.normal(_jax.random.fold_in(key, 0), shape, _jnp.float32)
    for i, name in enumerate(TWIN_WEIGHTS):
        w = inp[name].astype(_jnp.float32)
        if MOMENT_SCALE is None:
            s = _jnp.sqrt(_jnp.mean(_jnp.square(w)) + 1e-30)
        else:
            s = MOMENT_SCALE[name]
        km, kv = _jax.random.split(_jax.random.fold_in(key, i + 1))
        out[name] = w
        out["m_" + name] = s * _jax.random.normal(km, w.shape, _jnp.float32)
        out["v_" + name] = (s * s) * _jax.random.uniform(kv, w.shape, _jnp.float32, 0.5, 1.5)
    if N_MICROBATCH > 1:
        for name, axis in PER_EXAMPLE_BATCH_AXIS.items():
            out[name] = _to_microbatches(out[name], axis)
    return {'x': out['x'], 'p': out['p'], 'g_mix': out['g_mix'], 'w_in': out['w_in'], 'qn_gain': out['qn_gain'], 'kn_gain': out['kn_gain'], 'w_branch_a': out['w_branch_a'], 'w_branch_b': out['w_branch_b'], 'w_out': out['w_out'], 'g_mlp': out['g_mlp'], 'w_up': out['w_up'], 'w_down': out['w_down'], 'g_ple': out['g_ple'], 'w_ple_gate': out['w_ple_gate'], 'w_ple_proj': out['w_ple_proj'], 'loss_target': out['loss_target'], 'm_g_mix': out['m_g_mix'], 'm_w_in': out['m_w_in'], 'm_qn_gain': out['m_qn_gain'], 'm_kn_gain': out['m_kn_gain'], 'm_w_branch_a': out['m_w_branch_a'], 'm_w_branch_b': out['m_w_branch_b'], 'm_w_out': out['m_w_out'], 'm_g_mlp': out['m_g_mlp'], 'm_w_up': out['m_w_up'], 'm_w_down': out['m_w_down'], 'm_g_ple': out['m_g_ple'], 'm_w_ple_gate': out['m_w_ple_gate'], 'm_w_ple_proj': out['m_w_ple_proj'], 'v_g_mix': out['v_g_mix'], 'v_w_in': out['v_w_in'], 'v_qn_gain': out['v_qn_gain'], 'v_kn_gain': out['v_kn_gain'], 'v_w_branch_a': out['v_w_branch_a'], 'v_w_branch_b': out['v_w_branch_b'], 'v_w_out': out['v_w_out'], 'v_g_mlp': out['v_g_mlp'], 'v_w_up': out['v_w_up'], 'v_w_down': out['v_w_down'], 'v_g_ple': out['v_g_ple'], 'v_w_ple_gate': out['v_w_ple_gate'], 'v_w_ple_proj': out['v_w_ple_proj']}


def _loss(weights, diff, rest, loss_target):
    with _jax.named_scope("forward"):
        args = {**rest, TWIN_DIFF_INPUT: diff, **{k: w.astype(_WEIGHT_DTYPES[k]) for k, w in weights.items()}}
        y = _forward(args)
    with _jax.named_scope("loss_head"):
        err = _jnp.square(y.astype(_jnp.float32) - loss_target)
        return 0.5 * _jnp.sum(_jnp.mean(err, axis=-1)) if err.ndim else 0.5 * err


def _adamw(w, g, m, v):
    m = ADAM_B1 * m + (1.0 - ADAM_B1) * g
    v = ADAM_B2 * v + (1.0 - ADAM_B2) * _jnp.square(g)
    m_hat = m / (1.0 - ADAM_B1 ** ADAM_STEP)
    v_hat = v / (1.0 - ADAM_B2 ** ADAM_STEP)
    delta = -ADAM_LR * (m_hat / (_jnp.sqrt(v_hat) + ADAM_EPS) + ADAM_WD * w)
    return delta, m, v


def reference(x, p, g_mix, w_in, qn_gain, kn_gain, w_branch_a, w_branch_b, w_out, g_mlp, w_up, w_down, g_ple, w_ple_gate, w_ple_proj, loss_target, m_g_mix, m_w_in, m_qn_gain, m_kn_gain, m_w_branch_a, m_w_branch_b, m_w_out, m_g_mlp, m_w_up, m_w_down, m_g_ple, m_w_ple_gate, m_w_ple_proj, v_g_mix, v_w_in, v_qn_gain, v_kn_gain, v_w_branch_a, v_w_branch_b, v_w_out, v_g_mlp, v_w_up, v_w_down, v_g_ple, v_w_ple_gate, v_w_ple_proj):
    given = dict(x=x, p=p, g_mix=g_mix, w_in=w_in, qn_gain=qn_gain, kn_gain=kn_gain, w_branch_a=w_branch_a, w_branch_b=w_branch_b, w_out=w_out, g_mlp=g_mlp, w_up=w_up, w_down=w_down, g_ple=g_ple, w_ple_gate=w_ple_gate, w_ple_proj=w_ple_proj, loss_target=loss_target, m_g_mix=m_g_mix, m_w_in=m_w_in, m_qn_gain=m_qn_gain, m_kn_gain=m_kn_gain, m_w_branch_a=m_w_branch_a, m_w_branch_b=m_w_branch_b, m_w_out=m_w_out, m_g_mlp=m_g_mlp, m_w_up=m_w_up, m_w_down=m_w_down, m_g_ple=m_g_ple, m_w_ple_gate=m_w_ple_gate, m_w_ple_proj=m_w_ple_proj, v_g_mix=v_g_mix, v_w_in=v_w_in, v_qn_gain=v_qn_gain, v_kn_gain=v_kn_gain, v_w_branch_a=v_w_branch_a, v_w_branch_b=v_w_branch_b, v_w_out=v_w_out, v_g_mlp=v_g_mlp, v_w_up=v_w_up, v_w_down=v_w_down, v_g_ple=v_g_ple, v_w_ple_gate=v_w_ple_gate, v_w_ple_proj=v_w_ple_proj)
    weights = {n: given[n] for n in TWIN_WEIGHTS}
    shared = {n: given[n] for n in SHARED_INPUTS}
    per_example = {n: given[n] for n in ['x', 'p']}
    grad_fn = _jax.value_and_grad(_loss, argnums=(0, 1))

    def one_microbatch(ex, loss_target):
        ex = dict(ex)
        diff = ex.pop(TWIN_DIFF_INPUT)
        return grad_fn(weights, diff, {**shared, **ex}, loss_target)

    if N_MICROBATCH == 1:
        loss, (grad_w, grad_x) = one_microbatch(per_example, given["loss_target"])
    else:
        def body(carry, xs):
            loss_sum, grad_sum = carry
            l_k, (gw_k, gx_k) = one_microbatch(xs[0], xs[1])
            with _jax.named_scope("update"):
                return (loss_sum + l_k, _jax.tree.map(_jnp.add, grad_sum, gw_k)), gx_k

        init = (_jnp.zeros((), _jnp.float32), _jax.tree.map(_jnp.zeros_like, weights))
        (loss, grad_w), grad_x = _jax.lax.scan(body, init, (per_example, given["loss_target"]))
    with _jax.named_scope("update"):
        delta_w, new_m, new_v = {}, {}, {}
        for n in TWIN_WEIGHTS:
            delta_w[n], new_m[n], new_v[n] = _adamw(weights[n], grad_w[n], given["m_" + n], given["v_" + n])
    return (loss, grad_x, *[grad_w[n] for n in TWIN_WEIGHTS], *[delta_w[n] for n in TWIN_WEIGHTS],
            *[new_m[n] for n in TWIN_WEIGHTS], *[new_v[n] for n in TWIN_WEIGHTS])
```

```python
import functools

import jax
import jax.numpy as jnp
from jax import lax
from jax.experimental import pallas as pl
from jax.experimental.pallas import tpu as pltpu

F32 = jnp.float32
MXU_DTYPE = jnp.bfloat16
HEAD_DIM = 128
ROT_DIM = HEAD_DIM // 4
ROPE_THETA = 500000.0
EPS = 1e-6
DILATIONS = (1, 4, 16)
BLOCK = 128
N_SHARD = 4
ADAM_LR, ADAM_B1, ADAM_B2, ADAM_EPS, ADAM_WD, ADAM_STEP = 0.001, 0.9, 0.999, 1e-08, 0.01, 10
V7X_VMEM_BYTES = 64 * 1024 * 1024
VMEM_LIMIT = V7X_VMEM_BYTES - 8 * 1024 * 1024
MESH = pl.DeviceIdType.MESH
NEG = -1e30
SB_BLOCK = 128
CUMSUM_PASSES = 3


def _fit(dim, pref):
    if dim <= pref:
        return dim
    b = (pref // 128) * 128
    while dim % b:
        b -= 128
    return b


def _params(sem=None):
    return pltpu.CompilerParams(dimension_semantics=sem, vmem_limit_bytes=VMEM_LIMIT)


def _dot(a, b, dims=(((1,), (0,)), ((), ()))):
    return lax.dot_general(a, b, dims, preferred_element_type=F32)


NT = (((1,), (1,)), ((), ()))
TN = (((0,), (0,)), ((), ()))


def _split_dot(x, u, passes):
    out = None
    r = x
    for p in range(passes):
        hi = r.astype(MXU_DTYPE)
        part = _dot(hi, u)
        out = part if out is None else out + part
        if p + 1 < passes:
            r = r - hi.astype(F32)
    return out


def _matmul(name, a, b, *, mode, out_dtypes, bm=1024, bn=1024, bk=1024, b_cshard=False, b_off=0, n_out=None,
            extras=(), epilogue=None, out_cshard=False):
    if mode == "tn":
        K, M = a.shape
        N = b.shape[1]
    else:
        M, K = a.shape
        if mode == "nn":
            N = n_out if n_out is not None else (N_SHARD * b.shape[2] if b_cshard else b.shape[1])
        else:
            N = b.shape[1] if b_cshard else b.shape[0]
    bm, bn, bk = _fit(M, bm), _fit(N, bn), _fit(K, bk)
    nk = K // bk
    grid = (M // bm, N // bn, nk)

    if mode == "tn":
        a_spec = pl.BlockSpec((bk, bm), lambda i, j, k: (k, i))
        b_spec = pl.BlockSpec((bk, bn), lambda i, j, k: (k, j))
        dims = TN
    elif mode == "nn":
        a_spec = pl.BlockSpec((bm, bk), lambda i, j, k: (i, k))
        if b_cshard:
            cb = b.shape[2] // bn
            b_spec = pl.BlockSpec((None, bk, bn), lambda i, j, k: ((j + b_off) // cb, k, (j + b_off) % cb))
        else:
            b_spec = pl.BlockSpec((bk, bn), lambda i, j, k: (k, j + b_off))
        dims = (((1,), (0,)), ((), ()))
    else:
        a_spec = pl.BlockSpec((bm, bk), lambda i, j, k: (i, k))
        if b_cshard:
            cb = b.shape[2] // bk
            b_spec = pl.BlockSpec((None, bn, bk), lambda i, j, k: (k // cb, j, k % cb))
        else:
            b_spec = pl.BlockSpec((bn, bk), lambda i, j, k: (j, k))
        dims = NT

    ex_arrays = [e[0] for e in extras]
    ex_specs = [pl.BlockSpec((bm, bn), functools.partial(lambda i, j, k, off: (i, j + off), off=e[1])) for e in extras]
    if out_cshard:
        cbo = (N // N_SHARD) // bn
        out_shape = [jax.ShapeDtypeStruct((N_SHARD, M, N // N_SHARD), dt) for dt in out_dtypes]
        out_specs = [pl.BlockSpec((None, bm, bn), lambda i, j, k: (j // cbo, i, j % cbo)) for _ in out_dtypes]
    else:
        out_shape = [jax.ShapeDtypeStruct((M, N), dt) for dt in out_dtypes]
        out_specs = [pl.BlockSpec((bm, bn), lambda i, j, k: (i, j)) for _ in out_dtypes]
    ne, no = len(extras), len(out_dtypes)

    def kern(*refs):
        a_ref, b_ref = refs[0], refs[1]
        ex_refs = refs[2:2 + ne]
        o_refs = refs[2 + ne:2 + ne + no]
        part = _dot(a_ref[...].astype(MXU_DTYPE), b_ref[...].astype(MXU_DTYPE), dims)

        def finish(acc):
            vals = (acc,) if epilogue is None else epilogue(acc, *[r[...] for r in ex_refs])
            for r, v in zip(o_refs, vals):
                r[...] = v.astype(r.dtype)

        if nk == 1:
            finish(part)
        else:
            acc_ref = refs[2 + ne + no]
            k = pl.program_id(2)

            @pl.when(k == 0)
            def _():
                acc_ref[...] = part

            @pl.when(k > 0)
            def _():
                acc_ref[...] += part

            @pl.when(k == nk - 1)
            def _():
                finish(acc_ref[...])

    outs = pl.pallas_call(
        kern, name=name, grid=grid,
        in_specs=[a_spec, b_spec] + ex_specs, out_specs=out_specs, out_shape=out_shape,
        scratch_shapes=[pltpu.VMEM((bm, bn), F32)] if nk > 1 else [],
        compiler_params=_params(("parallel", "parallel", "arbitrary")),
    )(a, b, *ex_arrays)
    return outs


def _row(tr, w, coff=0):
    return pl.BlockSpec((tr, w), lambda i: (i, coff))


def _vec(w):
    return pl.BlockSpec((1, w), lambda i: (0, 0))


def _rows_call(name, body, n_rows, tr, ins, outs, accs=()):
    n_in, n_out = len(ins), len(outs)

    def kern(*refs):
        acc_refs = refs[n_in + n_out:]
        if acc_refs:
            @pl.when(pl.program_id(0) == 0)
            def _():
                for r in acc_refs:
                    r[...] = jnp.zeros_like(r)
        body(*refs)

    out_shape = [jax.ShapeDtypeStruct((n_rows, w), dt) for w, dt in outs] + [jax.ShapeDtypeStruct((1, w), F32) for w in accs]
    out_specs = [_row(tr, w) for w, _ in outs] + [_vec(w) for w in accs]
    return pl.pallas_call(
        kern, name=name, grid=(n_rows // tr,),
        in_specs=[s for _, s in ins], out_specs=out_specs, out_shape=out_shape,
        compiler_params=_params(("arbitrary",)),
    )(*[a for a, _ in ins])


def _rmsnorm_fwd(name, x, g):
    S, D = x.shape
    tr = _fit(S, 256)

    def body(x_ref, g_ref, h_ref):
        xv = x_ref[...]
        r = lax.rsqrt(jnp.mean(xv * xv, axis=1, keepdims=True) + EPS)
        h_ref[...] = ((xv * r) * g_ref[...]).astype(h_ref.dtype)

    return _rows_call(name, body, S, tr, [(x, _row(tr, D)), (g, _vec(D))], [(D, MXU_DTYPE)])[0]


def _rmsnorm_bwd(name, dh, x, g, resid, with_loss):
    S, D = x.shape
    tr = _fit(S, 256)

    def body(dh_ref, x_ref, g_ref, res_ref, dx_ref, dg_ref, *loss_ref):
        xv = x_ref[...]
        r = lax.rsqrt(jnp.mean(xv * xv, axis=1, keepdims=True) + EPS)
        dhv = dh_ref[...]
        u = dhv * g_ref[...]
        xr = xv * r
        dx = r * u - xr * (r * r) * jnp.mean(xv * u, axis=1, keepdims=True)
        resv = res_ref[...]
        dx_ref[...] = resv + dx
        dg_ref[...] += jnp.sum(dhv * xr, axis=0, keepdims=True)
        if with_loss:
            loss_ref[0][...] += (0.5 * D) * jnp.sum(resv * resv)

    outs = _rows_call(name, body, S, tr, [(dh, _row(tr, D)), (x, _row(tr, D)), (g, _vec(D)), (resid, _row(tr, D))],
                      [(D, F32)], accs=(D, 128) if with_loss else (D,))
    return outs


def _rope_tables(S):
    half = ROT_DIM // 2
    pos = jnp.arange(S, dtype=F32)
    inv = ROPE_THETA ** (-jnp.arange(0, ROT_DIM, 2, dtype=F32) / ROT_DIM)
    ang = pos[:, None] * inv[None, :]
    cos, sin = jnp.cos(ang), jnp.sin(ang)
    pad = HEAD_DIM - ROT_DIM
    ctab = jnp.concatenate([cos, cos, jnp.ones((S, pad), F32)], axis=1)
    atab = jnp.concatenate([-sin, jnp.zeros((S, pad + half), F32)], axis=1)
    btab = jnp.concatenate([jnp.zeros((S, half), F32), sin, jnp.zeros((S, pad), F32)], axis=1)
    return ctab, atab, btab


def _qknorm_fwd(qk, qn, kn, tabs, HW):
    S = qk.shape[0]
    tr = _fit(S, 256)
    half = ROT_DIM // 2

    def body(qk_ref, qn_ref, kn_ref, c_ref, a_ref, b_ref, q_out, k_out):
        ct, at, bt = c_ref[...], a_ref[...], b_ref[...]
        for part, (g_ref, o_ref) in enumerate(((qn_ref, q_out), (kn_ref, k_out))):
            gv = g_ref[...]
            for h in range(HW // HEAD_DIM):
                xh = qk_ref[:, part * HW + h * HEAD_DIM: part * HW + (h + 1) * HEAD_DIM]
                r = lax.rsqrt(jnp.mean(xh * xh, axis=1, keepdims=True) + EPS)
                y = (xh * r) * gv
                o = y * ct + pltpu.roll(y, HEAD_DIM - half, 1) * at + pltpu.roll(y, half, 1) * bt
                o_ref[:, h * HEAD_DIM:(h + 1) * HEAD_DIM] = o.astype(o_ref.dtype)

    ins = [(qk, _row(tr, 2 * HW)), (qn, _vec(HEAD_DIM)), (kn, _vec(HEAD_DIM))] + [(t, _row(tr, HEAD_DIM)) for t in tabs]
    return _rows_call("qknorm_fwd", body, S, tr, ins, [(HW, MXU_DTYPE), (HW, MXU_DTYPE)])


def _shift_spec(tr, w, shift, nblk):
    return pl.BlockSpec((tr, w), lambda i: (jnp.minimum(i + shift, nblk - 1), 0))


def _qknorm_bwd(qk, qn, kn, tabs, dq_parts, dk_cur, dk_prev, HW):
    S = qk.shape[0]
    tr = BLOCK
    nblk = S // tr
    half = ROT_DIM // 2
    nd = len(DILATIONS)

    def body(*refs):
        qk_ref, qn_ref, kn_ref, c_ref, a_ref, b_ref = refs[:6]
        dq_refs = refs[6:6 + nd]
        dkc_refs = refs[6 + nd:6 + 2 * nd]
        dkp_refs = refs[6 + 2 * nd:6 + 3 * nd]
        d_out, dqn_ref, dkn_ref = refs[6 + 3 * nd:]
        i = pl.program_id(0)
        ct, at, bt = c_ref[...], a_ref[...], b_ref[...]
        live = [(i + d < nblk).astype(F32) for d in DILATIONS]
        for part, (g_ref, dg_ref) in enumerate(((qn_ref, dqn_ref), (kn_ref, dkn_ref))):
            gv = g_ref[...]
            dg = jnp.zeros((1, HEAD_DIM), F32)
            for h in range(HW // HEAD_DIM):
                hs = slice(h * HEAD_DIM, (h + 1) * HEAD_DIM)
                if part == 0:
                    do = dq_refs[0][:, hs] + dq_refs[1][:, hs] + dq_refs[2][:, hs]
                else:
                    do = dkc_refs[0][:, hs] + dkc_refs[1][:, hs] + dkc_refs[2][:, hs]
                    for n in range(nd):
                        do = do + dkp_refs[n][:, hs] * live[n]
                dy = do * ct + pltpu.roll(do * at, half, 1) + pltpu.roll(do * bt, HEAD_DIM - half, 1)
                xh = qk_ref[:, part * HW + h * HEAD_DIM: part * HW + (h + 1) * HEAD_DIM]
                r = lax.rsqrt(jnp.mean(xh * xh, axis=1, keepdims=True) + EPS)
                xr = xh * r
                u = dy * gv
                dx = r * u - xr * (r * r) * jnp.mean(xh * u, axis=1, keepdims=True)
                d_out[:, part * HW + h * HEAD_DIM: part * HW + (h + 1) * HEAD_DIM] = dx.astype(d_out.dtype)
                dg = dg + jnp.sum(dy * xr, axis=0, keepdims=True)
            dg_ref[...] += dg

    ins = [(qk, _row(tr, 2 * HW)), (qn, _vec(HEAD_DIM)), (kn, _vec(HEAD_DIM))] + [(t, _row(tr, HEAD_DIM)) for t in tabs]
    ins += [(a, _row(tr, HW)) for a in dq_parts] + [(a, _row(tr, HW)) for a in dk_cur]
    ins += [(a, _shift_spec(tr, HW, d, nblk)) for a, d in zip(dk_prev, DILATIONS)]
    return _rows_call("qknorm_bwd", body, S, tr, ins, [(2 * HW, MXU_DTYPE)], accs=(HEAD_DIM, HEAD_DIM))


def _dv_sum(dv_cur, dv_prev, HW):
    S = dv_cur[0].shape[0]
    tr = BLOCK
    nblk = S // tr
    nd = len(DILATIONS)

    def body(*refs):
        i = pl.program_id(0)
        out = refs[2 * nd]
        acc = refs[0][...] + refs[1][...] + refs[2][...]
        for n, d in enumerate(DILATIONS):
            acc = acc + refs[nd + n][...] * (i + d < nblk).astype(F32)
        out[...] = acc.astype(out.dtype)

    ins = [(a, _row(tr, HW)) for a in dv_cur] + [(a, _shift_spec(tr, HW, d, nblk)) for a, d in zip(dv_prev, DILATIONS)]
    return _rows_call("dilated_dv_sum", body, S, tr, ins, [(HW, MXU_DTYPE)])[0]


def _dil_masks(n):
    qi = lax.broadcasted_iota(jnp.int32, (BLOCK, BLOCK), 0)
    ki = lax.broadcasted_iota(jnp.int32, (BLOCK, BLOCK), 1)
    return qi >= ki, (qi <= ki) & (n > 0)


def _dil_specs(d, HW, ngroups, group):
    cur = pl.BlockSpec((BLOCK, HW), lambda r, n: (n, r * ngroups + group))
    prev = pl.BlockSpec((BLOCK, HW), lambda r, n: (jnp.maximum(n - 1, 0), r * ngroups + group))
    return cur, prev


def _dilated_fwd(q, k, mid, d, HW):
    S = q.shape[0]
    M = S // d
    nb = M // BLOCK
    H = HW // HEAD_DIM
    scale = HEAD_DIM ** -0.5
    qv, kv, mv = q.reshape(M, d * HW), k.reshape(M, d * HW), mid.reshape(M, d * 4 * HW)
    qc, qp = _dil_specs(d, HW, 1, 0)
    vc, vp = _dil_specs(d, HW, 4, 0)

    def kern(q_ref, kc_ref, kp_ref, vc_ref, vp_ref, o_ref, l_ref):
        n = pl.program_id(1)
        mc, mp = _dil_masks(n)
        for h in range(H):
            hs = slice(h * HEAD_DIM, (h + 1) * HEAD_DIM)
            qh = q_ref[:, hs]
            sc = jnp.where(mc, _dot(qh, kc_ref[:, hs], NT) * scale, NEG)
            sp = jnp.where(mp, _dot(qh, kp_ref[:, hs], NT) * scale, NEG)
            mx = jnp.maximum(jnp.max(sc, axis=1, keepdims=True), jnp.max(sp, axis=1, keepdims=True))
            ec, ep = jnp.exp(sc - mx), jnp.exp(sp - mx)
            den = jnp.sum(ec, axis=1, keepdims=True) + jnp.sum(ep, axis=1, keepdims=True)
            o = _dot(ec.astype(MXU_DTYPE), vc_ref[:, hs]) + _dot(ep.astype(MXU_DTYPE), vp_ref[:, hs])
            o_ref[:, hs] = o / den
            l_ref[:, hs] = jnp.broadcast_to(mx + jnp.log(den), (BLOCK, HEAD_DIM))

    o, lse = pl.pallas_call(
        kern, name=f"dilated_fwd_d{d}", grid=(d, nb),
        in_specs=[qc, qc, qp, vc, vp],
        out_specs=[qc, qc],
        out_shape=[jax.ShapeDtypeStruct((M, d * HW), F32)] * 2,
        compiler_params=_params(("parallel", "arbitrary")),
    )(qv, kv, kv, mv, mv)
    return o.reshape(S, HW), lse.reshape(S, HW)


def _dilated_combine(os_, lses, HW):
    S = os_[0].shape[0]
    tr = _fit(S, 256)

    def body(o0, o1, o2, l0, l1, l2, ya_ref, lse_ref):
        a, b, c = l0[...], l1[...], l2[...]
        mx = jnp.maximum(jnp.maximum(a, b), c)
        ea, eb, ec = jnp.exp(a - mx), jnp.exp(b - mx), jnp.exp(c - mx)
        tot = ea + eb + ec
        ya_ref[...] = (ea * o0[...] + eb * o1[...] + ec * o2[...]) / tot
        lse_ref[...] = mx + jnp.log(tot)

    ins = [(a, _row(tr, HW)) for a in list(os_) + list(lses)]
    return _rows_call("dilated_combine", body, S, tr, ins, [(HW, F32), (HW, F32)])


def _dilated_bwd(q, k, mid, dya, ya, lse, d, HW):
    S = q.shape[0]
    M = S // d
    nb = M // BLOCK
    H = HW // HEAD_DIM
    scale = HEAD_DIM ** -0.5
    view = lambda t: t.reshape(M, d * t.shape[1])
    qc, qp = _dil_specs(d, HW, 1, 0)
    vc, vp = _dil_specs(d, HW, 4, 0)

    def kern(q_ref, kc_ref, kp_ref, vc_ref, vp_ref, dy_ref, y_ref, l_ref, dq_ref, dkc_ref, dkp_ref, dvc_ref, dvp_ref):
        n = pl.program_id(1)
        mc, mp = _dil_masks(n)
        for h in range(H):
            hs = slice(h * HEAD_DIM, (h + 1) * HEAD_DIM)
            qh, kc, kp = q_ref[:, hs], kc_ref[:, hs], kp_ref[:, hs]
            dy = dy_ref[:, hs]
            dyb = dy.astype(MXU_DTYPE)
            lt = l_ref[:, h * HEAD_DIM:h * HEAD_DIM + 1]
            delta = jnp.sum(dy * y_ref[:, hs], axis=1, keepdims=True)
            pc = jnp.where(mc, jnp.exp(_dot(qh, kc, NT) * scale - lt), 0.0)
            pp = jnp.where(mp, jnp.exp(_dot(qh, kp, NT) * scale - lt), 0.0)
            dsc = (pc * (_dot(dyb, vc_ref[:, hs], NT) - delta) * scale).astype(MXU_DTYPE)
            dsp = (pp * (_dot(dyb, vp_ref[:, hs], NT) - delta) * scale).astype(MXU_DTYPE)
            dq_ref[:, hs] = _dot(dsc, kc) + _dot(dsp, kp)
            dkc_ref[:, hs] = _dot(dsc, qh, TN)
            dkp_ref[:, hs] = _dot(dsp, qh, TN)
            dvc_ref[:, hs] = _dot(pc.astype(MXU_DTYPE), dyb, TN)
            dvp_ref[:, hs] = _dot(pp.astype(MXU_DTYPE), dyb, TN)

    outs = pl.pallas_call(
        kern, name=f"dilated_bwd_d{d}", grid=(d, nb),
        in_specs=[qc, qc, qp, vc, vp, qc, qc, qc],
        out_specs=[qc] * 5,
        out_shape=[jax.ShapeDtypeStruct((M, d * HW), F32)] * 5,
        compiler_params=_params(("parallel", "arbitrary")),
    )(view(q), view(k), view(k), view(mid), view(mid), view(dya), view(ya), view(lse))
    return [t.reshape(S, HW) for t in outs]


def _softplus_parts(z):
    sp = jnp.maximum(z, 0.0) + jnp.log1p(jnp.exp(-jnp.abs(z)))
    return -sp, z - sp


def _sb_specs(S, H, B):
    q_spec = pl.BlockSpec((B, HEAD_DIM), lambda h, i: (i, H + h))
    k_spec = pl.BlockSpec((S, HEAD_DIM), lambda h, i: (0, 2 * H + h))
    v_spec = pl.BlockSpec((S, HEAD_DIM), lambda h, i: (0, 3 * H + h))
    o_spec = pl.BlockSpec((B, HEAD_DIM), lambda h, i: (i, h))
    return q_spec, k_spec, v_spec, o_spec


def _sb_fwd(mid, HW):
    S = mid.shape[0]
    H = HW // HEAD_DIM
    B = SB_BLOCK
    scale = HEAD_DIM ** -0.5
    q_spec, k_spec, v_spec, o_spec = _sb_specs(S, H, B)

    def kern(q_ref, k_ref, v_ref, o_ref, t_ref):
        i = pl.program_id(1)
        q = q_ref[...]
        row = lax.broadcasted_iota(jnp.int32, (B, B), 0)
        col = lax.broadcasted_iota(jnp.int32, (B, B), 1)
        upper = (row > col).astype(MXU_DTYPE)
        causal = col < row

        def tile(j, run, acc, diag):
            ks = pl.multiple_of(j * B, B)
            z = _dot(q, k_ref[pl.ds(ks, B), :], NT) * scale
            m, l = _softplus_parts(z)
            if diag:
                m = jnp.where(causal, m, 0.0)
            a = jnp.exp(l + _split_dot(m, upper, CUMSUM_PASSES) + run)
            if diag:
                a = jnp.where(causal, a, 0.0)
            acc = acc + _dot(a.astype(MXU_DTYPE), v_ref[pl.ds(ks, B), :])
            return run + jnp.sum(m, axis=1, keepdims=True), acc

        run, acc = tile(i, jnp.zeros((B, 1), F32), jnp.zeros((B, HEAD_DIM), F32), True)

        def step(t, carry):
            return tile(i - 1 - t, carry[0], carry[1], False)

        run, acc = lax.fori_loop(0, i, step, (run, acc))
        o_ref[...] = acc
        t_ref[...] = jnp.broadcast_to(run, (B, HEAD_DIM))

    return pl.pallas_call(
        kern, name="stickbreak_fwd", grid=(H, S // B),
        in_specs=[q_spec, k_spec, v_spec], out_specs=[o_spec, o_spec],
        out_shape=[jax.ShapeDtypeStruct((S, HW), F32)] * 2,
        compiler_params=_params(("parallel", "arbitrary")),
    )(mid, mid, mid)


def _sb_bwd(mid, dyb, tot, HW):
    S = mid.shape[0]
    H = HW // HEAD_DIM
    B = SB_BLOCK
    scale = HEAD_DIM ** -0.5
    q_spec, k_spec, v_spec, o_spec = _sb_specs(S, H, B)
    full = pl.BlockSpec((S, HEAD_DIM), lambda h, i: (0, h))

    def kern(q_ref, k_ref, v_ref, do_ref, t_ref, dq_ref, dk_ref, dv_ref):
        i = pl.program_id(1)

        @pl.when(i == 0)
        def _():
            dk_ref[...] = jnp.zeros_like(dk_ref)
            dv_ref[...] = jnp.zeros_like(dv_ref)

        q = q_ref[...]
        do = do_ref[...].astype(MXU_DTYPE)
        total = t_ref[:, 0:1]
        row = lax.broadcasted_iota(jnp.int32, (B, B), 0)
        col = lax.broadcasted_iota(jnp.int32, (B, B), 1)
        incl = (row <= col).astype(MXU_DTYPE)
        excl = (row < col).astype(MXU_DTYPE)
        causal = col < row

        def tile(j, mrun, prun, dq, diag):
            ks = pl.multiple_of(j * B, B)
            k = k_ref[pl.ds(ks, B), :]
            z = _dot(q, k, NT) * scale
            m, l = _softplus_parts(z)
            if diag:
                m = jnp.where(causal, m, 0.0)
            a = jnp.exp(l + (total - mrun - _split_dot(m, incl, CUMSUM_PASSES)))
            if diag:
                a = jnp.where(causal, a, 0.0)
            p = a * _dot(do, v_ref[pl.ds(ks, B), :], NT)
            before = _split_dot(p, excl, CUMSUM_PASSES) + prun
            sig = jnp.exp(l)
            dz = (p * (1.0 - sig) - sig * before) * scale
            if diag:
                dz = jnp.where(causal, dz, 0.0)
            dzb = dz.astype(MXU_DTYPE)
            dq = dq + _dot(dzb, k)
            dk_ref[pl.ds(ks, B), :] += _dot(dzb, q, TN)
            dv_ref[pl.ds(ks, B), :] += _dot(a.astype(MXU_DTYPE), do, TN)
            return mrun + jnp.sum(m, axis=1, keepdims=True), prun + jnp.sum(p, axis=1, keepdims=True), dq

        def step(j, carry):
            return tile(j, carry[0], carry[1], carry[2], False)

        zero = jnp.zeros((B, 1), F32)
        mrun, prun, dq = lax.fori_loop(0, i, step, (zero, zero, jnp.zeros((B, HEAD_DIM), F32)))
        _, _, dq = tile(i, mrun, prun, dq, True)
        dq_ref[...] = dq

    return pl.pallas_call(
        kern, name="stickbreak_bwd", grid=(H, S // B),
        in_specs=[q_spec, k_spec, v_spec, o_spec, o_spec], out_specs=[o_spec, full, full],
        out_shape=[jax.ShapeDtypeStruct((S, HW), F32)] * 3,
        compiler_params=_params(("arbitrary", "arbitrary")),
    )(mid, mid, mid, dyb, tot)


ANY = pl.BlockSpec(memory_space=pl.ANY)


def _place():
    x, y, c = lax.axis_index("x"), lax.axis_index("y"), lax.axis_index("c")
    chips = [(1 - x, y), (x, 1 - y), (1 - x, 1 - y)]
    return x, y, c, chips


def _half(ref, shard, hc, rh):
    return ref.at[shard, pl.ds(pl.multiple_of(hc * rh, 8), rh), :]


def _gather_weights(shards):
    n = len(shards)

    def body(*refs):
        ins, outs = refs[:n], refs[n:2 * n]
        send, recv, lsem = refs[2 * n:]
        x, y, c, chips = _place()
        s = 2 * x + y
        sibling = (x, y, 1 - c)
        local, sent = [], []
        for i in range(n):
            rh = ins[i].shape[0] // 2
            cp = pltpu.make_async_copy(ins[i], outs[i].at[s], lsem.at[i])
            cp.start()
            local.append(cp)
            for j, (px, py) in enumerate(chips):
                cp = pltpu.make_async_remote_copy(
                    src_ref=ins[i].at[pl.ds(pl.multiple_of(c * rh, 8), rh), :], dst_ref=_half(outs[i], s, c, rh),
                    send_sem=send.at[i, j], recv_sem=recv.at[i, j], device_id=(px, py, c), device_id_type=MESH)
                cp.start()
                sent.append(cp)
        for i in range(n):
            rh = ins[i].shape[0] // 2
            for j, (px, py) in enumerate(chips):
                landed = _half(outs[i], 2 * px + py, c, rh)
                pltpu.make_async_remote_copy(src_ref=landed, dst_ref=landed, send_sem=send.at[i, j], recv_sem=recv.at[i, j],
                                             device_id=(px, py, c), device_id_type=MESH).wait_recv()
                cp = pltpu.make_async_remote_copy(src_ref=landed, dst_ref=landed, send_sem=send.at[i, 3 + j],
                                                  recv_sem=recv.at[i, 3 + j], device_id=sibling, device_id_type=MESH)
                cp.start()
                sent.append(cp)
        for i in range(n):
            rh = ins[i].shape[0] // 2
            for j, (px, py) in enumerate(chips):
                passed = _half(outs[i], 2 * px + py, 1 - c, rh)
                pltpu.make_async_remote_copy(src_ref=passed, dst_ref=passed, send_sem=send.at[i, 3 + j],
                                             recv_sem=recv.at[i, 3 + j], device_id=sibling, device_id_type=MESH).wait_recv()
        for cp in sent:
            cp.wait_send()
        for cp in local:
            cp.wait()

    return pl.pallas_call(
        body, name="weights_all_gather",
        in_specs=[ANY] * n, out_specs=[ANY] * n,
        out_shape=[jax.ShapeDtypeStruct((N_SHARD,) + w.shape, w.dtype) for w in shards],
        scratch_shapes=[pltpu.SemaphoreType.DMA((n, 6)), pltpu.SemaphoreType.DMA((n, 6)), pltpu.SemaphoreType.DMA((n,))],
    )(*shards)


def _swap_halves(grads):
    n = len(grads)

    def body(*refs):
        ins, outs = refs[:n], refs[n:2 * n]
        send, recv = refs[2 * n:]
        x, y, c, _ = _place()
        copies = []
        for i in range(n):
            rh = ins[i].shape[1] // 2
            cp = pltpu.make_async_remote_copy(
                src_ref=ins[i].at[:, pl.ds(pl.multiple_of((1 - c) * rh, 8), rh), :], dst_ref=outs[i],
                send_sem=send.at[i], recv_sem=recv.at[i], device_id=(x, y, 1 - c), device_id_type=MESH)
            cp.start()
            copies.append(cp)
        for cp in copies:
            cp.wait()

    return pl.pallas_call(
        body, name="grad_swap_halves",
        in_specs=[ANY] * n, out_specs=[ANY] * n,
        out_shape=[jax.ShapeDtypeStruct((N_SHARD, g.shape[1] // 2, g.shape[2]), F32) for g in grads],
        scratch_shapes=[pltpu.SemaphoreType.DMA((n,)), pltpu.SemaphoreType.DMA((n,))],
    )(*grads)


def _add_half(name, g, got, c_idx):
    _, R, C = g.shape
    rh = R // 2
    tr = _fit(rh, 256)
    nrb = rh // tr

    def kern(c_ref, g_ref, r_ref, o32_ref, o16_ref):
        v = g_ref[...] + r_ref[...]
        o32_ref[...] = v
        o16_ref[...] = v.astype(o16_ref.dtype)

    blk = lambda s, r, c_ref: (s, r, 0)
    return pl.pallas_call(
        kern, name=name,
        grid_spec=pltpu.PrefetchScalarGridSpec(
            num_scalar_prefetch=1, grid=(N_SHARD, nrb),
            in_specs=[pl.BlockSpec((None, tr, C), lambda s, r, c_ref: (s, c_ref[0] * nrb + r, 0)), pl.BlockSpec((None, tr, C), blk)],
            out_specs=[pl.BlockSpec((None, tr, C), blk)] * 2),
        out_shape=[jax.ShapeDtypeStruct((N_SHARD, rh, C), F32), jax.ShapeDtypeStruct((N_SHARD, rh, C), MXU_DTYPE)],
        compiler_params=_params(("arbitrary", "arbitrary")),
    )(c_idx, g, got)


def _scatter_partials(parts):
    n = len(parts)

    def body(*refs):
        ins, outs = refs[:n], refs[n:2 * n]
        send, recv = refs[2 * n:]
        x, y, c, chips = _place()
        copies = []
        for i in range(n):
            for j, (px, py) in enumerate(chips):
                cp = pltpu.make_async_remote_copy(
                    src_ref=ins[i].at[2 * px + py], dst_ref=outs[i].at[j],
                    send_sem=send.at[i, j], recv_sem=recv.at[i, j], device_id=(px, py, c), device_id_type=MESH)
                cp.start()
                copies.append(cp)
        for cp in copies:
            cp.wait()

    return pl.pallas_call(
        body, name="grad_scatter_partials",
        in_specs=[ANY] * n, out_specs=[ANY] * n,
        out_shape=[jax.ShapeDtypeStruct((3,) + p.shape[1:], p.dtype) for p in parts],
        scratch_shapes=[pltpu.SemaphoreType.DMA((n, 3)), pltpu.SemaphoreType.DMA((n, 3))],
    )(*parts)


def _add_partials(name, own, got, s_idx):
    _, rh, C = own.shape
    tr = _fit(rh, 256)

    def kern(s_ref, o_ref, a_ref, b_ref, c_ref, out_ref):
        out_ref[...] = ((o_ref[...] + a_ref[...].astype(F32)) + b_ref[...].astype(F32)) + c_ref[...].astype(F32)

    slot = lambda j: pl.BlockSpec((None, tr, C), lambda r, s_ref: (j, r, 0))
    return pl.pallas_call(
        kern, name=name,
        grid_spec=pltpu.PrefetchScalarGridSpec(
            num_scalar_prefetch=1, grid=(rh // tr,),
            in_specs=[pl.BlockSpec((None, tr, C), lambda r, s_ref: (s_ref[0], r, 0)), slot(0), slot(1), slot(2)],
            out_specs=pl.BlockSpec((tr, C), lambda r, s_ref: (r, 0))),
        out_shape=jax.ShapeDtypeStruct((rh, C), F32),
        compiler_params=_params(("arbitrary",)),
    )(s_idx, own, got, got, got)


def _join_halves(halves):
    n = len(halves)

    def body(*refs):
        ins, outs = refs[:n], refs[n:2 * n]
        send, recv, lsem = refs[2 * n:]
        x, y, c, _ = _place()
        copies, local = [], []
        for i in range(n):
            rh = ins[i].shape[0]
            mine = outs[i].at[pl.ds(pl.multiple_of(c * rh, 8), rh), :]
            lc = pltpu.make_async_copy(ins[i], mine, lsem.at[i])
            lc.start()
            local.append(lc)
            cp = pltpu.make_async_remote_copy(src_ref=ins[i], dst_ref=mine, send_sem=send.at[i], recv_sem=recv.at[i],
                                              device_id=(x, y, 1 - c), device_id_type=MESH)
            cp.start()
            copies.append(cp)
        for i in range(n):
            rh = ins[i].shape[0]
            theirs = outs[i].at[pl.ds(pl.multiple_of((1 - c) * rh, 8), rh), :]
            pltpu.make_async_remote_copy(src_ref=theirs, dst_ref=theirs, send_sem=send.at[i], recv_sem=recv.at[i],
                                         device_id=(x, y, 1 - c), device_id_type=MESH).wait_recv()
        for cp in copies:
            cp.wait_send()
        for lc in local:
            lc.wait()

    return pl.pallas_call(
        body, name="grad_join_halves",
        in_specs=[ANY] * n, out_specs=[ANY] * n,
        out_shape=[jax.ShapeDtypeStruct((2 * h.shape[0], h.shape[1]), F32) for h in halves],
        scratch_shapes=[pltpu.SemaphoreType.DMA((n,)), pltpu.SemaphoreType.DMA((n,)), pltpu.SemaphoreType.DMA((n,))],
    )(*halves)


def _all_reduce_small(v):
    rows, W = v.shape
    ndev = 8

    def body(v_ref, out_ref, buf, send, recv):
        x, y, c, _ = _place()
        me = 4 * x + 2 * y + c
        buf[me] = v_ref[...]
        copies = []
        for r in range(1, ndev):
            fx, fy, fc = (r >> 2) & 1, (r >> 1) & 1, r & 1
            peer = (x ^ fx, y ^ fy, c ^ fc)
            cp = pltpu.make_async_remote_copy(src_ref=v_ref, dst_ref=buf.at[me], send_sem=send.at[r - 1], recv_sem=recv.at[r - 1],
                                              device_id=peer, device_id_type=MESH)
            cp.start()
            copies.append(cp)
        for cp in copies:
            cp.wait()
        acc = buf[0]
        for k in range(1, ndev):
            acc = acc + buf[k]
        out_ref[...] = acc

    return pl.pallas_call(
        body, name="small_all_reduce",
        in_specs=[pl.BlockSpec(memory_space=pltpu.VMEM)], out_specs=pl.BlockSpec(memory_space=pltpu.VMEM),
        out_shape=jax.ShapeDtypeStruct((rows, W), F32),
        scratch_shapes=[pltpu.VMEM((ndev, rows, W), F32), pltpu.SemaphoreType.DMA((ndev - 1,)), pltpu.SemaphoreType.DMA((ndev - 1,))],
    )(v)


def _adamw(name, w, g, m, v):
    R, C = w.shape
    tr = _fit(R, 256)

    def body(w_ref, g_ref, m_ref, v_ref, d_ref, nm_ref, nv_ref):
        gv = g_ref[...]
        nm = ADAM_B1 * m_ref[...] + (1.0 - ADAM_B1) * gv
        nv = ADAM_B2 * v_ref[...] + (1.0 - ADAM_B2) * (gv * gv)
        m_hat = nm / (1.0 - ADAM_B1 ** ADAM_STEP)
        v_hat = nv / (1.0 - ADAM_B2 ** ADAM_STEP)
        d_ref[...] = -ADAM_LR * (m_hat / (jnp.sqrt(v_hat) + ADAM_EPS) + ADAM_WD * w_ref[...])
        nm_ref[...] = nm
        nv_ref[...] = nv

    return _rows_call(name, body, R, tr, [(a, _row(tr, C)) for a in (w, g, m, v)], [(C, F32)] * 3)


def _sigmoid(z):
    return 1.0 / (1.0 + jnp.exp(-z))


def kernel(x, p, g_mix, w_in, qn_gain, kn_gain, w_branch_a, w_branch_b, w_out, g_mlp, w_up, w_down, g_ple, w_ple_gate, w_ple_proj, loss_target, m_g_mix, m_w_in, m_qn_gain, m_kn_gain, m_w_branch_a, m_w_branch_b, m_w_out, m_g_mlp, m_w_up, m_w_down, m_g_ple, m_w_ple_gate, m_w_ple_proj, v_g_mix, v_w_in, v_qn_gain, v_kn_gain, v_w_branch_a, v_w_branch_b, v_w_out, v_g_mlp, v_w_up, v_w_down, v_g_ple, v_w_ple_gate, v_w_ple_proj):
    S, D = x.shape[1], x.shape[2]
    HW = w_branch_a.shape[1]
    x2d, tgt, p2d = x.reshape(S, D), loss_target.reshape(S, D), p.reshape(S, p.shape[-1])
    big = {"w_in": w_in, "w_branch_a": w_branch_a, "w_branch_b": w_branch_b, "w_out": w_out, "w_up": w_up,
           "w_down": w_down, "w_ple_gate": w_ple_gate, "w_ple_proj": w_ple_proj}
    moments = {"w_in": (m_w_in, v_w_in), "w_branch_a": (m_w_branch_a, v_w_branch_a), "w_branch_b": (m_w_branch_b, v_w_branch_b),
               "w_out": (m_w_out, v_w_out), "w_up": (m_w_up, v_w_up), "w_down": (m_w_down, v_w_down),
               "w_ple_gate": (m_w_ple_gate, v_w_ple_gate), "w_ple_proj": (m_w_ple_proj, v_w_ple_proj)}
    names = list(big)
    col_sharded = {"w_in", "w_branch_a", "w_branch_b", "w_up", "w_ple_proj"}
    shard2d = {k: w.reshape(w.shape[1], w.shape[2]) for k, w in big.items()}

    gathered = dict(zip(names, _gather_weights([shard2d[k].astype(MXU_DTYPE) for k in names])))
    W = {k: (gathered[k] if k in col_sharded else gathered[k].reshape(-1, gathered[k].shape[2])) for k in names}
    cin = W["w_in"].shape[2]
    bn_in = _fit(cin, 512)
    while (2 * HW) % bn_in:
        bn_in -= 128

    h = _rmsnorm_fwd("rmsnorm_mix", x2d, g_mix)
    (qk,) = _matmul("proj_qk", h, W["w_in"], mode="nn", out_dtypes=[F32], b_cshard=True, b_off=0, n_out=2 * HW, bn=bn_in, bk=D)
    (mid,) = _matmul("proj_mid", h, W["w_in"], mode="nn", out_dtypes=[MXU_DTYPE], b_cshard=True, b_off=2 * HW // bn_in,
                     n_out=4 * HW, bn=bn_in, bk=D)
    (sg,) = _matmul("proj_gates", h, W["w_in"], mode="nn", out_dtypes=[F32], b_cshard=True, b_off=6 * HW // bn_in,
                    n_out=2 * D, bn=bn_in, bk=D, epilogue=lambda acc: (_sigmoid(acc),))
    tabs = _rope_tables(S)
    qa, ka = _qknorm_fwd(qk, qn_gain, kn_gain, tabs, HW)
    dil = [_dilated_fwd(qa, ka, mid, d, HW) for d in DILATIONS]
    ya, lse = _dilated_combine([o for o, _ in dil], [l for _, l in dil], HW)
    yb, sb_tot = _sb_fwd(mid, HW)

    gate_blocks = D // _fit(D, 1024)
    (ua,) = _matmul("branch_a", ya, W["w_branch_a"], mode="nn", out_dtypes=[F32], b_cshard=True, bn=_fit(W["w_branch_a"].shape[2], 1024))
    bn_b = _fit(W["w_branch_b"].shape[2], 1024)
    ub, merged = _matmul("branch_b_merge", yb, W["w_branch_b"], mode="nn", out_dtypes=[F32, MXU_DTYPE], b_cshard=True, bn=bn_b,
                         extras=[(sg, 0), (sg, D // bn_b), (ua, 0)],
                         epilogue=lambda acc, sga, sgb, uav: (acc, sga * uav + sgb * acc))
    (x1,) = _matmul("out_proj", merged, W["w_out"], mode="nn", out_dtypes=[F32], extras=[(x2d, 0)], epilogue=lambda acc, xv: (xv + acc,))
    hm = _rmsnorm_fwd("rmsnorm_mlp", x1, g_mlp)

    def up_epilogue(acc):
        r = jnp.maximum(acc, 0.0)
        return r * r, r

    act, rup = _matmul("mlp_up", hm, W["w_up"], mode="nn", out_dtypes=[MXU_DTYPE, MXU_DTYPE], b_cshard=True,
                       bn=_fit(W["w_up"].shape[2], 1024), bk=D, epilogue=up_epilogue)
    (x2,) = _matmul("mlp_down", act, W["w_down"], mode="nn", out_dtypes=[F32], extras=[(x1, 0)], epilogue=lambda acc, xv: (xv + acc,))
    hp = _rmsnorm_fwd("rmsnorm_ple", x2, g_ple)
    (pp,) = _matmul("ple_proj", p2d, W["w_ple_proj"], mode="nn", out_dtypes=[F32], b_cshard=True, bn=_fit(W["w_ple_proj"].shape[2], 1024))

    def ple_epilogue(acc, ppv, x2v, tv):
        s = _sigmoid(acc)
        dx3 = ((x2v + ppv * s) - tv) / D
        return dx3, dx3 * s, dx3 * ppv * (s * (1.0 - s))

    dx3, d_pp, d_gate = _matmul("ple_gate_loss", hp, W["w_ple_gate"], mode="nn", out_dtypes=[F32, MXU_DTYPE, MXU_DTYPE],
                                bm=512, extras=[(pp, 0), (x2, 0), (tgt, 0)], epilogue=ple_epilogue)

    G = {}
    (G["w_ple_proj"],) = _matmul("grad_w_ple_proj", p2d, d_pp, mode="tn", out_dtypes=[F32], out_cshard=True,
                                 bn=_fit(d_pp.shape[1] // N_SHARD, 1024))
    (G["w_ple_gate"],) = _matmul("grad_w_ple_gate", hp, d_gate, mode="tn", out_dtypes=[F32])
    (d_hp,) = _matmul("ple_gate_bwd", d_gate, W["w_ple_gate"], mode="nt", out_dtypes=[F32])
    dx2, g_g_ple, loss_part = _rmsnorm_bwd("rmsnorm_ple_bwd", d_hp, x2, g_ple, dx3, True)
    (G["w_down"],) = _matmul("grad_w_down", act, dx2, mode="tn", out_dtypes=[F32])
    (d_up,) = _matmul("mlp_down_bwd", dx2, W["w_down"], mode="nt", out_dtypes=[MXU_DTYPE], extras=[(rup, 0)],
                      epilogue=lambda acc, r: (acc * (2.0 * r.astype(F32)),))
    (G["w_up"],) = _matmul("grad_w_up", hm, d_up, mode="tn", out_dtypes=[F32], out_cshard=True, bn=_fit(d_up.shape[1] // N_SHARD, 1024))
    (d_hm,) = _matmul("mlp_up_bwd", d_up, W["w_up"], mode="nt", out_dtypes=[F32], b_cshard=True, bk=_fit(W["w_up"].shape[2], 1024))
    dx1, g_g_mlp = _rmsnorm_bwd("rmsnorm_mlp_bwd", d_hm, x1, g_mlp, dx2, False)
    (G["w_out"],) = _matmul("grad_w_out", merged, dx1, mode="tn", out_dtypes=[F32])

    def merge_bwd(acc, sga, sgb, uav, ubv):
        return acc * sga, acc * sgb, acc * uav * (sga * (1.0 - sga)), acc * ubv * (sgb * (1.0 - sgb))

    bn_m = _fit(D, 1024)
    d_ua, d_ub, d_ga, d_gb = _matmul("out_proj_bwd", dx1, W["w_out"], mode="nt", out_dtypes=[MXU_DTYPE] * 4, bm=512, bn=bn_m,
                                     extras=[(sg, 0), (sg, D // bn_m), (ua, 0), (ub, 0)], epilogue=merge_bwd)
    bn_br = _fit(D // N_SHARD, 1024)
    (G["w_branch_a"],) = _matmul("grad_w_branch_a", ya, d_ua, mode="tn", out_dtypes=[F32], out_cshard=True, bn=bn_br)
    (G["w_branch_b"],) = _matmul("grad_w_branch_b", yb, d_ub, mode="tn", out_dtypes=[F32], out_cshard=True, bn=bn_br)
    (d_ya,) = _matmul("branch_a_bwd", d_ua, W["w_branch_a"], mode="nt", out_dtypes=[F32], b_cshard=True, bk=bn_br)
    (d_yb,) = _matmul("branch_b_bwd", d_ub, W["w_branch_b"], mode="nt", out_dtypes=[F32], b_cshard=True, bk=bn_br)

    dqb, dkb, dvb = _sb_bwd(mid, d_yb, sb_tot, HW)
    dil_b = [_dilated_bwd(qa, ka, mid, d_ya, ya, lse, d, HW) for d in DILATIONS]
    d_qk, g_qn, g_kn = _qknorm_bwd(qk, qn_gain, kn_gain, tabs, [t[0] for t in dil_b], [t[1] for t in dil_b],
                                   [t[2] for t in dil_b], HW)
    dva = _dv_sum([t[3] for t in dil_b], [t[4] for t in dil_b], HW)
    d_proj = jnp.concatenate([d_qk, dva, dqb.astype(MXU_DTYPE), dkb.astype(MXU_DTYPE), dvb.astype(MXU_DTYPE), d_ga, d_gb], axis=1)
    (G["w_in"],) = _matmul("grad_w_in", h, d_proj, mode="tn", out_dtypes=[F32], out_cshard=True, bn=bn_in)
    (d_h,) = _matmul("proj_bwd", d_proj, W["w_in"], mode="nt", out_dtypes=[F32], b_cshard=True, bk=_fit(cin, 1280))
    grad_x, g_g_mix = _rmsnorm_bwd("rmsnorm_mix_bwd", d_h, x2d, g_mix, dx1, False)

    c_idx = lax.axis_index("c").astype(jnp.int32).reshape(1)
    s_idx = (2 * lax.axis_index("x") + lax.axis_index("y")).astype(jnp.int32).reshape(1)
    g3 = [G[k] if k in col_sharded else G[k].reshape(N_SHARD, -1, G[k].shape[1]) for k in names]
    swapped = _swap_halves(g3)
    pre = [_add_half(f"grad_add_half_{k}", g, got, c_idx) for k, g, got in zip(names, g3, swapped)]
    landed = _scatter_partials([p16 for _, p16 in pre])
    halves = [_add_partials(f"grad_add_partials_{k}", p32, got, s_idx) for k, (p32, _), got in zip(names, pre, landed)]
    reduced = dict(zip(names, _join_halves(halves)))

    pack_w = -(-(3 * D + 3 * 128) // (8 * 128)) * 128

    def pack(v_mix, v_mlp, v_ple, v_qn, v_kn, extra):
        flat = jnp.concatenate([v_mix.reshape(-1), v_mlp.reshape(-1), v_ple.reshape(-1), v_qn.reshape(-1), v_kn.reshape(-1), extra.reshape(-1)])
        return jnp.pad(flat, (0, 8 * pack_w - flat.shape[0])).reshape(8, pack_w)

    def unpack(blk):
        flat = blk.reshape(-1)
        return (flat[:D].reshape(1, D), flat[D:2 * D].reshape(1, D), flat[2 * D:3 * D].reshape(1, D),
                flat[3 * D:3 * D + 128].reshape(1, 128), flat[3 * D + 128:3 * D + 256].reshape(1, 128), flat[3 * D + 256])

    small = _all_reduce_small(pack(g_g_mix, g_g_mlp, g_g_ple, g_qn, g_kn, loss_part))
    sw = pack(g_mix, g_mlp, g_ple, qn_gain, kn_gain, jnp.zeros((128,), F32))
    sm = pack(m_g_mix, m_g_mlp, m_g_ple, m_qn_gain, m_kn_gain, jnp.zeros((128,), F32))
    sv = pack(v_g_mix, v_g_mlp, v_g_ple, v_qn_gain, v_kn_gain, jnp.ones((128,), F32))
    s_delta, s_nm, s_nv = _adamw("adamw_small", sw, small, sm, sv)
    sg_mix, sg_mlp, sg_ple, sg_qn, sg_kn, loss = unpack(small)
    small_out = {}
    for tag, blk in (("delta", s_delta), ("new_m", s_nm), ("new_v", s_nv)):
        u = unpack(blk)
        small_out[tag] = dict(g_mix=u[0], g_mlp=u[1], g_ple=u[2], qn_gain=u[3], kn_gain=u[4])
    small_grad = dict(g_mix=sg_mix, g_mlp=sg_mlp, g_ple=sg_ple, qn_gain=sg_qn, kn_gain=sg_kn)

    big_out = {"grad": {}, "delta": {}, "new_m": {}, "new_v": {}}
    for k in names:
        shape = big[k].shape
        m2, v2 = (t.reshape(shape[1], shape[2]) for t in moments[k])
        delta, nm, nv = _adamw(f"adamw_{k}", shard2d[k], reduced[k], m2, v2)
        big_out["grad"][k] = reduced[k].reshape(shape)
        big_out["delta"][k], big_out["new_m"][k], big_out["new_v"][k] = delta.reshape(shape), nm.reshape(shape), nv.reshape(shape)

    order = ["g_mix", "w_in", "qn_gain", "kn_gain", "w_branch_a", "w_branch_b", "w_out", "g_mlp", "w_up", "w_down", "g_ple",
             "w_ple_gate", "w_ple_proj"]
    outs = [loss, grad_x.reshape(x.shape)]
    outs += [small_grad[k] if k in small_grad else big_out["grad"][k] for k in order]
    for tag in ("delta", "new_m", "new_v"):
        outs += [small_out[tag][k] if k in small_grad else big_out[tag][k] for k in order]
    return tuple(outs)
```

```python
import functools

import jax
import jax.numpy as jnp
from jax import lax
from jax.experimental import pallas as pl
from jax.experimental.pallas import tpu as pltpu

F32 = jnp.float32
MXU_DTYPE = jnp.bfloat16
HEAD_DIM = 128
ROT_DIM = HEAD_DIM // 4
ROPE_THETA = 500000.0
EPS = 1e-6
DILATIONS = (1, 4, 16)
BLOCK = 128
N_SHARD = 4
ADAM_LR, ADAM_B1, ADAM_B2, ADAM_EPS, ADAM_WD, ADAM_STEP = 0.001, 0.9, 0.999, 1e-08, 0.01, 10
V7X_VMEM_BYTES = 64 * 1024 * 1024
VMEM_LIMIT = V7X_VMEM_BYTES - 8 * 1024 * 1024
MESH = pl.DeviceIdType.MESH
NEG = -1e30
SB_BQ, SB_BK = 512, 256
CUMSUM_PASSES = 2


def _fit(dim, pref):
    if dim <= pref:
        return dim
    b = (pref // 128) * 128
    while dim % b:
        b -= 128
    return b


def _params(sem=None):
    return pltpu.CompilerParams(dimension_semantics=sem, vmem_limit_bytes=VMEM_LIMIT)


def _dot(a, b, dims=(((1,), (0,)), ((), ()))):
    return lax.dot_general(a, b, dims, preferred_element_type=F32)


NT = (((1,), (1,)), ((), ()))
TN = (((0,), (0,)), ((), ()))


def _split_dot(x, u, passes):
    out = None
    r = x
    for p in range(passes):
        hi = r.astype(MXU_DTYPE)
        part = _dot(hi, u)
        out = part if out is None else out + part
        if p + 1 < passes:
            r = r - hi.astype(F32)
    return out


def _matmul(name, a, b, *, mode, out_dtypes, bm=1024, bn=1024, bk=1024, b_cshard=False, b_off=0, n_out=None,
            extras=(), epilogue=None, out_cshard=False):
    if mode == "tn":
        K, M = a.shape
        N = b.shape[1]
    else:
        M, K = a.shape
        if mode == "nn":
            N = n_out if n_out is not None else (N_SHARD * b.shape[2] if b_cshard else b.shape[1])
        else:
            N = b.shape[1] if b_cshard else b.shape[0]
    bm, bn, bk = _fit(M, bm), _fit(N, bn), _fit(K, bk)
    nk = K // bk
    grid = (M // bm, N // bn, nk)

    if mode == "tn":
        a_spec = pl.BlockSpec((bk, bm), lambda i, j, k: (k, i))
        b_spec = pl.BlockSpec((bk, bn), lambda i, j, k: (k, j))
        dims = TN
    elif mode == "nn":
        a_spec = pl.BlockSpec((bm, bk), lambda i, j, k: (i, k))
        if b_cshard:
            cb = b.shape[2] // bn
            b_spec = pl.BlockSpec((None, bk, bn), lambda i, j, k: ((j + b_off) // cb, k, (j + b_off) % cb))
        else:
            b_spec = pl.BlockSpec((bk, bn), lambda i, j, k: (k, j + b_off))
        dims = (((1,), (0,)), ((), ()))
    else:
        a_spec = pl.BlockSpec((bm, bk), lambda i, j, k: (i, k))
        if b_cshard:
            cb = b.shape[2] // bk
            b_spec = pl.BlockSpec((None, bn, bk), lambda i, j, k: (k // cb, j, k % cb))
        else:
            b_spec = pl.BlockSpec((bn, bk), lambda i, j, k: (j, k))
        dims = NT

    ex_arrays = [e[0] for e in extras]
    ex_specs = [pl.BlockSpec((bm, bn), functools.partial(lambda i, j, k, off: (i, j + off), off=e[1])) for e in extras]
    if out_cshard:
        cbo = (N // N_SHARD) // bn
        out_shape = [jax.ShapeDtypeStruct((N_SHARD, M, N // N_SHARD), dt) for dt in out_dtypes]
        out_specs = [pl.BlockSpec((None, bm, bn), lambda i, j, k: (j // cbo, i, j % cbo)) for _ in out_dtypes]
    else:
        out_shape = [jax.ShapeDtypeStruct((M, N), dt) for dt in out_dtypes]
        out_specs = [pl.BlockSpec((bm, bn), lambda i, j, k: (i, j)) for _ in out_dtypes]
    ne, no = len(extras), len(out_dtypes)

    def kern(*refs):
        a_ref, b_ref = refs[0], refs[1]
        ex_refs = refs[2:2 + ne]
        o_refs = refs[2 + ne:2 + ne + no]
        part = _dot(a_ref[...].astype(MXU_DTYPE), b_ref[...].astype(MXU_DTYPE), dims)

        def finish(acc):
            vals = (acc,) if epilogue is None else epilogue(acc, *[r[...] for r in ex_refs])
            for r, v in zip(o_refs, vals):
                r[...] = v.astype(r.dtype)

        if nk == 1:
            finish(part)
        else:
            acc_ref = refs[2 + ne + no]
            k = pl.program_id(2)

            @pl.when(k == 0)
            def _():
                acc_ref[...] = part

            @pl.when(k > 0)
            def _():
                acc_ref[...] += part

            @pl.when(k == nk - 1)
            def _():
                finish(acc_ref[...])

    outs = pl.pallas_call(
        kern, name=name, grid=grid,
        in_specs=[a_spec, b_spec] + ex_specs, out_specs=out_specs, out_shape=out_shape,
        scratch_shapes=[pltpu.VMEM((bm, bn), F32)] if nk > 1 else [],
        compiler_params=_params(("parallel", "parallel", "arbitrary")),
    )(a, b, *ex_arrays)
    return outs


def _row(tr, w, coff=0):
    return pl.BlockSpec((tr, w), lambda i: (i, coff))


def _vec(w):
    return pl.BlockSpec((1, w), lambda i: (0, 0))


def _rows_call(name, body, n_rows, tr, ins, outs, accs=()):
    n_in, n_out = len(ins), len(outs)

    def kern(*refs):
        acc_refs = refs[n_in + n_out:]
        if acc_refs:
            @pl.when(pl.program_id(0) == 0)
            def _():
                for r in acc_refs:
                    r[...] = jnp.zeros_like(r)
        body(*refs)

    out_shape = [jax.ShapeDtypeStruct((n_rows, w), dt) for w, dt in outs] + [jax.ShapeDtypeStruct((1, w), F32) for w in accs]
    out_specs = [_row(tr, w) for w, _ in outs] + [_vec(w) for w in accs]
    return pl.pallas_call(
        kern, name=name, grid=(n_rows // tr,),
        in_specs=[s for _, s in ins], out_specs=out_specs, out_shape=out_shape,
        compiler_params=_params(("arbitrary",)),
    )(*[a for a, _ in ins])


def _rmsnorm_fwd(name, x, g):
    S, D = x.shape
    tr = _fit(S, 256)

    def body(x_ref, g_ref, h_ref):
        xv = x_ref[...]
        r = lax.rsqrt(jnp.mean(xv * xv, axis=1, keepdims=True) + EPS)
        h_ref[...] = ((xv * r) * g_ref[...]).astype(h_ref.dtype)

    return _rows_call(name, body, S, tr, [(x, _row(tr, D)), (g, _vec(D))], [(D, MXU_DTYPE)])[0]


def _rmsnorm_bwd(name, dh, x, g, resid, with_loss):
    S, D = x.shape
    tr = _fit(S, 256)

    def body(dh_ref, x_ref, g_ref, res_ref, dx_ref, dg_ref, *loss_ref):
        xv = x_ref[...]
        r = lax.rsqrt(jnp.mean(xv * xv, axis=1, keepdims=True) + EPS)
        dhv = dh_ref[...]
        u = dhv * g_ref[...]
        xr = xv * r
        dx = r * u - xr * (r * r) * jnp.mean(xv * u, axis=1, keepdims=True)
        resv = res_ref[...]
        dx_ref[...] = resv + dx
        dg_ref[...] += jnp.sum(dhv * xr, axis=0, keepdims=True)
        if with_loss:
            loss_ref[0][...] += (0.5 * D) * jnp.sum(resv * resv)

    outs = _rows_call(name, body, S, tr, [(dh, _row(tr, D)), (x, _row(tr, D)), (g, _vec(D)), (resid, _row(tr, D))],
                      [(D, F32)], accs=(D, 128) if with_loss else (D,))
    return outs


def _rope_tables(S):
    half = ROT_DIM // 2
    pos = jnp.arange(S, dtype=F32)
    inv = ROPE_THETA ** (-jnp.arange(0, ROT_DIM, 2, dtype=F32) / ROT_DIM)
    ang = pos[:, None] * inv[None, :]
    cos, sin = jnp.cos(ang), jnp.sin(ang)
    pad = HEAD_DIM - ROT_DIM
    ctab = jnp.concatenate([cos, cos, jnp.ones((S, pad), F32)], axis=1)
    atab = jnp.concatenate([-sin, jnp.zeros((S, pad + half), F32)], axis=1)
    btab = jnp.concatenate([jnp.zeros((S, half), F32), sin, jnp.zeros((S, pad), F32)], axis=1)
    return ctab, atab, btab


def _qknorm_fwd(qk, qn, kn, tabs, HW):
    S = qk.shape[0]
    tr = _fit(S, 256)
    half = ROT_DIM // 2

    def body(qk_ref, qn_ref, kn_ref, c_ref, a_ref, b_ref, q_out, k_out):
        ct, at, bt = c_ref[...], a_ref[...], b_ref[...]
        for part, (g_ref, o_ref) in enumerate(((qn_ref, q_out), (kn_ref, k_out))):
            gv = g_ref[...]
            for h in range(HW // HEAD_DIM):
                xh = qk_ref[:, part * HW + h * HEAD_DIM: part * HW + (h + 1) * HEAD_DIM]
                r = lax.rsqrt(jnp.mean(xh * xh, axis=1, keepdims=True) + EPS)
                y = (xh * r) * gv
                o = y * ct + pltpu.roll(y, HEAD_DIM - half, 1) * at + pltpu.roll(y, half, 1) * bt
                o_ref[:, h * HEAD_DIM:(h + 1) * HEAD_DIM] = o.astype(o_ref.dtype)

    ins = [(qk, _row(tr, 2 * HW)), (qn, _vec(HEAD_DIM)), (kn, _vec(HEAD_DIM))] + [(t, _row(tr, HEAD_DIM)) for t in tabs]
    return _rows_call("qknorm_fwd", body, S, tr, ins, [(HW, MXU_DTYPE), (HW, MXU_DTYPE)])


def _shift_spec(tr, w, shift, nblk):
    return pl.BlockSpec((tr, w), lambda i: (jnp.minimum(i + shift, nblk - 1), 0))


def _qknorm_bwd(qk, qn, kn, tabs, dq_parts, dk_cur, dk_prev, HW):
    S = qk.shape[0]
    tr = BLOCK
    nblk = S // tr
    half = ROT_DIM // 2
    nd = len(DILATIONS)

    def body(*refs):
        qk_ref, qn_ref, kn_ref, c_ref, a_ref, b_ref = refs[:6]
        dq_refs = refs[6:6 + nd]
        dkc_refs = refs[6 + nd:6 + 2 * nd]
        dkp_refs = refs[6 + 2 * nd:6 + 3 * nd]
        d_out, dqn_ref, dkn_ref = refs[6 + 3 * nd:]
        i = pl.program_id(0)
        ct, at, bt = c_ref[...], a_ref[...], b_ref[...]
        live = [(i + d < nblk).astype(F32) for d in DILATIONS]
        for part, (g_ref, dg_ref) in enumerate(((qn_ref, dqn_ref), (kn_ref, dkn_ref))):
            gv = g_ref[...]
            dg = jnp.zeros((1, HEAD_DIM), F32)
            for h in range(HW // HEAD_DIM):
                hs = slice(h * HEAD_DIM, (h + 1) * HEAD_DIM)
                if part == 0:
                    do = dq_refs[0][:, hs] + dq_refs[1][:, hs] + dq_refs[2][:, hs]
                else:
                    do = dkc_refs[0][:, hs] + dkc_refs[1][:, hs] + dkc_refs[2][:, hs]
                    for n in range(nd):
                        do = do + dkp_refs[n][:, hs] * live[n]
                dy = do * ct + pltpu.roll(do * at, half, 1) + pltpu.roll(do * bt, HEAD_DIM - half, 1)
                xh = qk_ref[:, part * HW + h * HEAD_DIM: part * HW + (h + 1) * HEAD_DIM]
                r = lax.rsqrt(jnp.mean(xh * xh, axis=1, keepdims=True) + EPS)
                xr = xh * r
                u = dy * gv
                dx = r * u - xr * (r * r) * jnp.mean(xh * u, axis=1, keepdims=True)
                d_out[:, part * HW + h * HEAD_DIM: part * HW + (h + 1) * HEAD_DIM] = dx.astype(d_out.dtype)
                dg = dg + jnp.sum(dy * xr, axis=0, keepdims=True)
            dg_ref[...] += dg

    ins = [(qk, _row(tr, 2 * HW)), (qn, _vec(HEAD_DIM)), (kn, _vec(HEAD_DIM))] + [(t, _row(tr, HEAD_DIM)) for t in tabs]
    ins += [(a, _row(tr, HW)) for a in dq_parts] + [(a, _row(tr, HW)) for a in dk_cur]
    ins += [(a, _shift_spec(tr, HW, d, nblk)) for a, d in zip(dk_prev, DILATIONS)]
    return _rows_call("qknorm_bwd", body, S, tr, ins, [(2 * HW, MXU_DTYPE)], accs=(HEAD_DIM, HEAD_DIM))


def _dv_sum(dv_cur, dv_prev, HW):
    S = dv_cur[0].shape[0]
    tr = BLOCK
    nblk = S // tr
    nd = len(DILATIONS)

    def body(*refs):
        i = pl.program_id(0)
        out = refs[2 * nd]
        acc = refs[0][...] + refs[1][...] + refs[2][...]
        for n, d in enumerate(DILATIONS):
            acc = acc + refs[nd + n][...] * (i + d < nblk).astype(F32)
        out[...] = acc.astype(out.dtype)

    ins = [(a, _row(tr, HW)) for a in dv_cur] + [(a, _shift_spec(tr, HW, d, nblk)) for a, d in zip(dv_prev, DILATIONS)]
    return _rows_call("dilated_dv_sum", body, S, tr, ins, [(HW, MXU_DTYPE)])[0]


def _dil_masks(n):
    qi = lax.broadcasted_iota(jnp.int32, (BLOCK, BLOCK), 0)
    ki = lax.broadcasted_iota(jnp.int32, (BLOCK, BLOCK), 1)
    return qi >= ki, (qi <= ki) & (n > 0)


def _dil_specs(d, HW, ngroups, group):
    cur = pl.BlockSpec((BLOCK, HW), lambda r, n: (n, r * ngroups + group))
    prev = pl.BlockSpec((BLOCK, HW), lambda r, n: (jnp.maximum(n - 1, 0), r * ngroups + group))
    return cur, prev


def _dilated_fwd(q, k, mid, d, HW):
    S = q.shape[0]
    M = S // d
    nb = M // BLOCK
    H = HW // HEAD_DIM
    scale = HEAD_DIM ** -0.5
    qv, kv, mv = q.reshape(M, d * HW), k.reshape(M, d * HW), mid.reshape(M, d * 4 * HW)
    qc, qp = _dil_specs(d, HW, 1, 0)
    vc, vp = _dil_specs(d, HW, 4, 0)

    def kern(q_ref, kc_ref, kp_ref, vc_ref, vp_ref, o_ref, l_ref):
        n = pl.program_id(1)
        mc, mp = _dil_masks(n)
        for h in range(H):
            hs = slice(h * HEAD_DIM, (h + 1) * HEAD_DIM)
            qh = q_ref[:, hs]
            sc = jnp.where(mc, _dot(qh, kc_ref[:, hs], NT) * scale, NEG)
            sp = jnp.where(mp, _dot(qh, kp_ref[:, hs], NT) * scale, NEG)
            mx = jnp.maximum(jnp.max(sc, axis=1, keepdims=True), jnp.max(sp, axis=1, keepdims=True))
            ec, ep = jnp.exp(sc - mx), jnp.exp(sp - mx)
            den = jnp.sum(ec, axis=1, keepdims=True) + jnp.sum(ep, axis=1, keepdims=True)
            o = _dot(ec.astype(MXU_DTYPE), vc_ref[:, hs]) + _dot(ep.astype(MXU_DTYPE), vp_ref[:, hs])
            o_ref[:, hs] = o / den
            l_ref[:, hs] = jnp.broadcast_to(mx + jnp.log(den), (BLOCK, HEAD_DIM))

    o, lse = pl.pallas_call(
        kern, name=f"dilated_fwd_d{d}", grid=(d, nb),
        in_specs=[qc, qc, qp, vc, vp],
        out_specs=[qc, qc],
        out_shape=[jax.ShapeDtypeStruct((M, d * HW), F32)] * 2,
        compiler_params=_params(("parallel", "arbitrary")),
    )(qv, kv, kv, mv, mv)
    return o.reshape(S, HW), lse.reshape(S, HW)


def _dilated_combine(os_, lses, HW):
    S = os_[0].shape[0]
    tr = _fit(S, 256)

    def body(o0, o1, o2, l0, l1, l2, ya_ref, lse_ref):
        a, b, c = l0[...], l1[...], l2[...]
        mx = jnp.maximum(jnp.maximum(a, b), c)
        ea, eb, ec = jnp.exp(a - mx), jnp.exp(b - mx), jnp.exp(c - mx)
        tot = ea + eb + ec
        ya_ref[...] = (ea * o0[...] + eb * o1[...] + ec * o2[...]) / tot
        lse_ref[...] = mx + jnp.log(tot)

    ins = [(a, _row(tr, HW)) for a in list(os_) + list(lses)]
    return _rows_call("dilated_combine", body, S, tr, ins, [(HW, F32), (HW, F32)])


def _dilated_bwd(q, k, mid, dya, ya, lse, d, HW):
    S = q.shape[0]
    M = S // d
    nb = M // BLOCK
    H = HW // HEAD_DIM
    scale = HEAD_DIM ** -0.5
    view = lambda t: t.reshape(M, d * t.shape[1])
    qc, qp = _dil_specs(d, HW, 1, 0)
    vc, vp = _dil_specs(d, HW, 4, 0)

    def kern(q_ref, kc_ref, kp_ref, vc_ref, vp_ref, dy_ref, y_ref, l_ref, dq_ref, dkc_ref, dkp_ref, dvc_ref, dvp_ref):
        n = pl.program_id(1)
        mc, mp = _dil_masks(n)
        for h in range(H):
            hs = slice(h * HEAD_DIM, (h + 1) * HEAD_DIM)
            qh, kc, kp = q_ref[:, hs], kc_ref[:, hs], kp_ref[:, hs]
            dy = dy_ref[:, hs]
            dyb = dy.astype(MXU_DTYPE)
            lt = l_ref[:, h * HEAD_DIM:h * HEAD_DIM + 1]
            delta = jnp.sum(dy * y_ref[:, hs], axis=1, keepdims=True)
            pc = jnp.where(mc, jnp.exp(_dot(qh, kc, NT) * scale - lt), 0.0)
            pp = jnp.where(mp, jnp.exp(_dot(qh, kp, NT) * scale - lt), 0.0)
            dsc = (pc * (_dot(dyb, vc_ref[:, hs], NT) - delta) * scale).astype(MXU_DTYPE)
            dsp = (pp * (_dot(dyb, vp_ref[:, hs], NT) - delta) * scale).astype(MXU_DTYPE)
            dq_ref[:, hs] = _dot(dsc, kc) + _dot(dsp, kp)
            dkc_ref[:, hs] = _dot(dsc, qh, TN)
            dkp_ref[:, hs] = _dot(dsp, qh, TN)
            dvc_ref[:, hs] = _dot(pc.astype(MXU_DTYPE), dyb, TN)
            dvp_ref[:, hs] = _dot(pp.astype(MXU_DTYPE), dyb, TN)

    outs = pl.pallas_call(
        kern, name=f"dilated_bwd_d{d}", grid=(d, nb),
        in_specs=[qc, qc, qp, vc, vp, qc, qc, qc],
        out_specs=[qc] * 5,
        out_shape=[jax.ShapeDtypeStruct((M, d * HW), F32)] * 5,
        compiler_params=_params(("parallel", "arbitrary")),
    )(view(q), view(k), view(k), view(mid), view(mid), view(dya), view(ya), view(lse))
    return [t.reshape(S, HW) for t in outs]


def _softplus_parts(z):
    sp = jnp.maximum(z, 0.0) + jnp.log1p(jnp.exp(-jnp.abs(z)))
    return -sp, z - sp


def _sb_specs(S, H):
    q_spec = pl.BlockSpec((SB_BQ, HEAD_DIM), lambda h, i: (i, H + h))
    k_spec = pl.BlockSpec((S, HEAD_DIM), lambda h, i: (0, 2 * H + h))
    v_spec = pl.BlockSpec((S, HEAD_DIM), lambda h, i: (0, 3 * H + h))
    o_spec = pl.BlockSpec((SB_BQ, HEAD_DIM), lambda h, i: (i, h))
    return q_spec, k_spec, v_spec, o_spec


def _sb_masks():
    row = lax.broadcasted_iota(jnp.int32, (SB_BQ, SB_BQ), 0)
    col = lax.broadcasted_iota(jnp.int32, (SB_BQ, SB_BQ), 1)
    tri_r = lax.broadcasted_iota(jnp.int32, (SB_BK, SB_BK), 0)
    tri_c = lax.broadcasted_iota(jnp.int32, (SB_BK, SB_BK), 1)
    return tri_r, tri_c, col < row


def _sb_fwd(mid, HW):
    S = mid.shape[0]
    H = HW // HEAD_DIM
    BQ, CH = SB_BQ, SB_BK
    NC = BQ // CH
    scale = HEAD_DIM ** -0.5
    q_spec, k_spec, v_spec, o_spec = _sb_specs(S, H)

    def kern(q_ref, k_ref, v_ref, o_ref, t_ref):
        i = pl.program_id(1)
        q = q_ref[...]
        tri_r, tri_c, causal = _sb_masks()
        upper = (tri_r > tri_c).astype(MXU_DTYPE)

        def block(j, run, acc, masked):
            ks = pl.multiple_of(j * BQ, BQ)
            z = _dot(q, k_ref[pl.ds(ks, BQ), :], NT) * scale
            m, l = _softplus_parts(z)
            if masked:
                m = jnp.where(causal, m, 0.0)
            parts = []
            for c in reversed(range(NC)):
                mc = m[:, c * CH:(c + 1) * CH]
                parts.append(l[:, c * CH:(c + 1) * CH] + (_split_dot(mc, upper, CUMSUM_PASSES) + run))
                run = run + jnp.sum(mc, axis=1, keepdims=True)
            a = jnp.exp(jnp.concatenate(parts[::-1], axis=1))
            if masked:
                a = jnp.where(causal, a, 0.0)
            return run, acc + _dot(a.astype(MXU_DTYPE), v_ref[pl.ds(ks, BQ), :])

        run, acc = block(i, jnp.zeros((BQ, 1), F32), jnp.zeros((BQ, HEAD_DIM), F32), True)
        run, acc = lax.fori_loop(0, i, lambda t, carry: block(i - 1 - t, carry[0], carry[1], False), (run, acc))
        o_ref[...] = acc
        t_ref[...] = jnp.broadcast_to(run, (BQ, HEAD_DIM))

    return pl.pallas_call(
        kern, name="stickbreak_fwd", grid=(H, S // BQ),
        in_specs=[q_spec, k_spec, v_spec], out_specs=[o_spec, o_spec],
        out_shape=[jax.ShapeDtypeStruct((S, HW), F32)] * 2,
        compiler_params=_params(("parallel", "arbitrary")),
    )(mid, mid, mid)


def _sb_bwd(mid, dyb, tot, HW):
    S = mid.shape[0]
    H = HW // HEAD_DIM
    BQ, CH = SB_BQ, SB_BK
    NC = BQ // CH
    scale = HEAD_DIM ** -0.5
    q_spec, k_spec, v_spec, o_spec = _sb_specs(S, H)
    full = pl.BlockSpec((S, HEAD_DIM), lambda h, i: (0, h))

    def kern(q_ref, k_ref, v_ref, do_ref, t_ref, dq_ref, dk_ref, dv_ref):
        i = pl.program_id(1)

        @pl.when(i == 0)
        def _():
            dk_ref[...] = jnp.zeros_like(dk_ref)
            dv_ref[...] = jnp.zeros_like(dv_ref)

        q = q_ref[...]
        do = do_ref[...].astype(MXU_DTYPE)
        total = t_ref[:, 0:1]
        tri_r, tri_c, causal = _sb_masks()
        incl = (tri_r <= tri_c).astype(MXU_DTYPE)
        excl = (tri_r < tri_c).astype(MXU_DTYPE)

        def block(j, mrun, prun, dq, masked):
            ks = pl.multiple_of(j * BQ, BQ)
            k = k_ref[pl.ds(ks, BQ), :]
            z = _dot(q, k, NT) * scale
            m, l = _softplus_parts(z)
            if masked:
                m = jnp.where(causal, m, 0.0)
            parts = []
            for c in range(NC):
                mc = m[:, c * CH:(c + 1) * CH]
                parts.append(l[:, c * CH:(c + 1) * CH] + (total - mrun - _split_dot(mc, incl, CUMSUM_PASSES)))
                mrun = mrun + jnp.sum(mc, axis=1, keepdims=True)
            a = jnp.exp(jnp.concatenate(parts, axis=1))
            if masked:
                a = jnp.where(causal, a, 0.0)
            p = a * _dot(do, v_ref[pl.ds(ks, BQ), :], NT)
            parts = []
            for c in range(NC):
                pc = p[:, c * CH:(c + 1) * CH]
                parts.append(_split_dot(pc, excl, CUMSUM_PASSES) + prun)
                prun = prun + jnp.sum(pc, axis=1, keepdims=True)
            before = jnp.concatenate(parts, axis=1)
            sig = jnp.exp(l)
            dz = (p * (1.0 - sig) - sig * before) * scale
            if masked:
                dz = jnp.where(causal, dz, 0.0)
            dzb = dz.astype(MXU_DTYPE)
            dk_ref[pl.ds(ks, BQ), :] += _dot(dzb, q, TN)
            dv_ref[pl.ds(ks, BQ), :] += _dot(a.astype(MXU_DTYPE), do, TN)
            return mrun, prun, dq + _dot(dzb, k)

        zero = jnp.zeros((BQ, 1), F32)
        carry = lax.fori_loop(0, i, lambda j, carry: block(j, *carry, False), (zero, zero, jnp.zeros((BQ, HEAD_DIM), F32)))
        dq_ref[...] = block(i, *carry, True)[2]

    return pl.pallas_call(
        kern, name="stickbreak_bwd", grid=(H, S // BQ),
        in_specs=[q_spec, k_spec, v_spec, o_spec, o_spec], out_specs=[o_spec, full, full],
        out_shape=[jax.ShapeDtypeStruct((S, HW), F32)] * 3,
        compiler_params=_params(("arbitrary", "arbitrary")),
    )(mid, mid, mid, dyb, tot)


ANY = pl.BlockSpec(memory_space=pl.ANY)


def _place():
    x, y, c = lax.axis_index("x"), lax.axis_index("y"), lax.axis_index("c")
    chips = [(1 - x, y), (x, 1 - y), (1 - x, 1 - y)]
    return x, y, c, chips


def _half(ref, shard, hc, rh):
    return ref.at[shard, pl.ds(pl.multiple_of(hc * rh, 8), rh), :]


def _gather_weights(shards):
    n = len(shards)

    def body(*refs):
        ins, outs = refs[:n], refs[n:2 * n]
        send, recv, lsem = refs[2 * n:]
        x, y, c, chips = _place()
        s = 2 * x + y
        sibling = (x, y, 1 - c)
        local, sent = [], []
        for i in range(n):
            rh = ins[i].shape[0] // 2
            cp = pltpu.make_async_copy(ins[i], outs[i].at[s], lsem.at[i])
            cp.start()
            local.append(cp)
            for j, (px, py) in enumerate(chips):
                cp = pltpu.make_async_remote_copy(
                    src_ref=ins[i].at[pl.ds(pl.multiple_of(c * rh, 8), rh), :], dst_ref=_half(outs[i], s, c, rh),
                    send_sem=send.at[i, j], recv_sem=recv.at[i, j], device_id=(px, py, c), device_id_type=MESH)
                cp.start()
                sent.append(cp)
        for i in range(n):
            rh = ins[i].shape[0] // 2
            for j, (px, py) in enumerate(chips):
                landed = _half(outs[i], 2 * px + py, c, rh)
                pltpu.make_async_remote_copy(src_ref=landed, dst_ref=landed, send_sem=send.at[i, j], recv_sem=recv.at[i, j],
                                             device_id=(px, py, c), device_id_type=MESH).wait_recv()
                cp = pltpu.make_async_remote_copy(src_ref=landed, dst_ref=landed, send_sem=send.at[i, 3 + j],
                                                  recv_sem=recv.at[i, 3 + j], device_id=sibling, device_id_type=MESH)
                cp.start()
                sent.append(cp)
        for i in range(n):
            rh = ins[i].shape[0] // 2
            for j, (px, py) in enumerate(chips):
                passed = _half(outs[i], 2 * px + py, 1 - c, rh)
                pltpu.make_async_remote_copy(src_ref=passed, dst_ref=passed, send_sem=send.at[i, 3 + j],
                                             recv_sem=recv.at[i, 3 + j], device_id=sibling, device_id_type=MESH).wait_recv()
        for cp in sent:
            cp.wait_send()
        for cp in local:
            cp.wait()

    return pl.pallas_call(
        body, name="weights_all_gather",
        in_specs=[ANY] * n, out_specs=[ANY] * n,
        out_shape=[jax.ShapeDtypeStruct((N_SHARD,) + w.shape, w.dtype) for w in shards],
        scratch_shapes=[pltpu.SemaphoreType.DMA((n, 6)), pltpu.SemaphoreType.DMA((n, 6)), pltpu.SemaphoreType.DMA((n,))],
    )(*shards)


def _swap_halves(grads):
    n = len(grads)

    def body(*refs):
        ins, outs = refs[:n], refs[n:2 * n]
        send, recv = refs[2 * n:]
        x, y, c, _ = _place()
        copies = []
        for i in range(n):
            rh = ins[i].shape[1] // 2
            cp = pltpu.make_async_remote_copy(
                src_ref=ins[i].at[:, pl.ds(pl.multiple_of((1 - c) * rh, 8), rh), :], dst_ref=outs[i],
                send_sem=send.at[i], recv_sem=recv.at[i], device_id=(x, y, 1 - c), device_id_type=MESH)
            cp.start()
            copies.append(cp)
        for cp in copies:
            cp.wait()

    return pl.pallas_call(
        body, name="grad_swap_halves",
        in_specs=[ANY] * n, out_specs=[ANY] * n,
        out_shape=[jax.ShapeDtypeStruct((N_SHARD, g.shape[1] // 2, g.shape[2]), F32) for g in grads],
        scratch_shapes=[pltpu.SemaphoreType.DMA((n,)), pltpu.SemaphoreType.DMA((n,))],
    )(*grads)


def _add_half(name, g, got, c_idx):
    _, R, C = g.shape
    rh = R // 2
    tr = _fit(rh, 256)
    nrb = rh // tr

    def kern(c_ref, g_ref, r_ref, o32_ref, o16_ref):
        v = g_ref[...] + r_ref[...]
        o32_ref[...] = v
        o16_ref[...] = v.astype(o16_ref.dtype)

    blk = lambda s, r, c_ref: (s, r, 0)
    return pl.pallas_call(
        kern, name=name,
        grid_spec=pltpu.PrefetchScalarGridSpec(
            num_scalar_prefetch=1, grid=(N_SHARD, nrb),
            in_specs=[pl.BlockSpec((None, tr, C), lambda s, r, c_ref: (s, c_ref[0] * nrb + r, 0)), pl.BlockSpec((None, tr, C), blk)],
            out_specs=[pl.BlockSpec((None, tr, C), blk)] * 2),
        out_shape=[jax.ShapeDtypeStruct((N_SHARD, rh, C), F32), jax.ShapeDtypeStruct((N_SHARD, rh, C), MXU_DTYPE)],
        compiler_params=_params(("arbitrary", "arbitrary")),
    )(c_idx, g, got)


def _scatter_partials(parts):
    n = len(parts)

    def body(*refs):
        ins, outs = refs[:n], refs[n:2 * n]
        send, recv = refs[2 * n:]
        x, y, c, chips = _place()
        copies = []
        for i in range(n):
            for j, (px, py) in enumerate(chips):
                cp = pltpu.make_async_remote_copy(
                    src_ref=ins[i].at[2 * px + py], dst_ref=outs[i].at[j],
                    send_sem=send.at[i, j], recv_sem=recv.at[i, j], device_id=(px, py, c), device_id_type=MESH)
                cp.start()
                copies.append(cp)
        for cp in copies:
            cp.wait()

    return pl.pallas_call(
        body, name="grad_scatter_partials",
        in_specs=[ANY] * n, out_specs=[ANY] * n,
        out_shape=[jax.ShapeDtypeStruct((3,) + p.shape[1:], p.dtype) for p in parts],
        scratch_shapes=[pltpu.SemaphoreType.DMA((n, 3)), pltpu.SemaphoreType.DMA((n, 3))],
    )(*parts)


def _add_partials(name, own, got, s_idx):
    _, rh, C = own.shape
    tr = _fit(rh, 256)

    def kern(s_ref, o_ref, a_ref, b_ref, c_ref, out_ref):
        out_ref[...] = ((o_ref[...] + a_ref[...].astype(F32)) + b_ref[...].astype(F32)) + c_ref[...].astype(F32)

    slot = lambda j: pl.BlockSpec((None, tr, C), lambda r, s_ref: (j, r, 0))
    return pl.pallas_call(
        kern, name=name,
        grid_spec=pltpu.PrefetchScalarGridSpec(
            num_scalar_prefetch=1, grid=(rh // tr,),
            in_specs=[pl.BlockSpec((None, tr, C), lambda r, s_ref: (s_ref[0], r, 0)), slot(0), slot(1), slot(2)],
            out_specs=pl.BlockSpec((tr, C), lambda r, s_ref: (r, 0))),
        out_shape=jax.ShapeDtypeStruct((rh, C), F32),
        compiler_params=_params(("arbitrary",)),
    )(s_idx, own, got, got, got)


def _join_halves(halves):
    n = len(halves)

    def body(*refs):
        ins, outs = refs[:n], refs[n:2 * n]
        send, recv, lsem = refs[2 * n:]
        x, y, c, _ = _place()
        copies, local = [], []
        for i in range(n):
            rh = ins[i].shape[0]
            mine = outs[i].at[pl.ds(pl.multiple_of(c * rh, 8), rh), :]
            lc = pltpu.make_async_copy(ins[i], mine, lsem.at[i])
            lc.start()
            local.append(lc)
            cp = pltpu.make_async_remote_copy(src_ref=ins[i], dst_ref=mine, send_sem=send.at[i], recv_sem=recv.at[i],
                                              device_id=(x, y, 1 - c), device_id_type=MESH)
            cp.start()
            copies.append(cp)
        for i in range(n):
            rh = ins[i].shape[0]
            theirs = outs[i].at[pl.ds(pl.multiple_of((1 - c) * rh, 8), rh), :]
            pltpu.make_async_remote_copy(src_ref=theirs, dst_ref=theirs, send_sem=send.at[i], recv_sem=recv.at[i],
                                         device_id=(x, y, 1 - c), device_id_type=MESH).wait_recv()
        for cp in copies:
            cp.wait_send()
        for lc in local:
            lc.wait()

    return pl.pallas_call(
        body, name="grad_join_halves",
        in_specs=[ANY] * n, out_specs=[ANY] * n,
        out_shape=[jax.ShapeDtypeStruct((2 * h.shape[0], h.shape[1]), F32) for h in halves],
        scratch_shapes=[pltpu.SemaphoreType.DMA((n,)), pltpu.SemaphoreType.DMA((n,)), pltpu.SemaphoreType.DMA((n,))],
    )(*halves)


def _all_reduce_small(v):
    rows, W = v.shape
    ndev = 8

    def body(v_ref, out_ref, buf, send, recv):
        x, y, c, _ = _place()
        me = 4 * x + 2 * y + c
        buf[me] = v_ref[...]
        copies = []
        for r in range(1, ndev):
            fx, fy, fc = (r >> 2) & 1, (r >> 1) & 1, r & 1
            peer = (x ^ fx, y ^ fy, c ^ fc)
            cp = pltpu.make_async_remote_copy(src_ref=v_ref, dst_ref=buf.at[me], send_sem=send.at[r - 1], recv_sem=recv.at[r - 1],
                                              device_id=peer, device_id_type=MESH)
            cp.start()
            copies.append(cp)
        for cp in copies:
            cp.wait()
        acc = buf[0]
        for k in range(1, ndev):
            acc = acc + buf[k]
        out_ref[...] = acc

    return pl.pallas_call(
        body, name="small_all_reduce",
        in_specs=[pl.BlockSpec(memory_space=pltpu.VMEM)], out_specs=pl.BlockSpec(memory_space=pltpu.VMEM),
        out_shape=jax.ShapeDtypeStruct((rows, W), F32),
        scratch_shapes=[pltpu.VMEM((ndev, rows, W), F32), pltpu.SemaphoreType.DMA((ndev - 1,)), pltpu.SemaphoreType.DMA((ndev - 1,))],
    )(v)


def _adamw(name, w, g, m, v):
    R, C = w.shape
    tr = _fit(R, 256)

    def body(w_ref, g_ref, m_ref, v_ref, d_ref, nm_ref, nv_ref):
        gv = g_ref[...]
        nm = ADAM_B1 * m_ref[...] + (1.0 - ADAM_B1) * gv
        nv = ADAM_B2 * v_ref[...] + (1.0 - ADAM_B2) * (gv * gv)
        m_hat = nm / (1.0 - ADAM_B1 ** ADAM_STEP)
        v_hat = nv / (1.0 - ADAM_B2 ** ADAM_STEP)
        d_ref[...] = -ADAM_LR * (m_hat / (jnp.sqrt(v_hat) + ADAM_EPS) + ADAM_WD * w_ref[...])
        nm_ref[...] = nm
        nv_ref[...] = nv

    return _rows_call(name, body, R, tr, [(a, _row(tr, C)) for a in (w, g, m, v)], [(C, F32)] * 3)


def _sigmoid(z):
    return 1.0 / (1.0 + jnp.exp(-z))


def kernel(x, p, g_mix, w_in, qn_gain, kn_gain, w_branch_a, w_branch_b, w_out, g_mlp, w_up, w_down, g_ple, w_ple_gate, w_ple_proj, loss_target, m_g_mix, m_w_in, m_qn_gain, m_kn_gain, m_w_branch_a, m_w_branch_b, m_w_out, m_g_mlp, m_w_up, m_w_down, m_g_ple, m_w_ple_gate, m_w_ple_proj, v_g_mix, v_w_in, v_qn_gain, v_kn_gain, v_w_branch_a, v_w_branch_b, v_w_out, v_g_mlp, v_w_up, v_w_down, v_g_ple, v_w_ple_gate, v_w_ple_proj):
    S, D = x.shape[1], x.shape[2]
    HW = w_branch_a.shape[1]
    x2d, tgt, p2d = x.reshape(S, D), loss_target.reshape(S, D), p.reshape(S, p.shape[-1])
    big = {"w_in": w_in, "w_branch_a": w_branch_a, "w_branch_b": w_branch_b, "w_out": w_out, "w_up": w_up,
           "w_down": w_down, "w_ple_gate": w_ple_gate, "w_ple_proj": w_ple_proj}
    moments = {"w_in": (m_w_in, v_w_in), "w_branch_a": (m_w_branch_a, v_w_branch_a), "w_branch_b": (m_w_branch_b, v_w_branch_b),
               "w_out": (m_w_out, v_w_out), "w_up": (m_w_up, v_w_up), "w_down": (m_w_down, v_w_down),
               "w_ple_gate": (m_w_ple_gate, v_w_ple_gate), "w_ple_proj": (m_w_ple_proj, v_w_ple_proj)}
    names = list(big)
    col_sharded = {"w_in", "w_branch_a", "w_branch_b", "w_up", "w_ple_proj"}
    shard2d = {k: w.reshape(w.shape[1], w.shape[2]) for k, w in big.items()}

    gathered = dict(zip(names, _gather_weights([shard2d[k].astype(MXU_DTYPE) for k in names])))
    W = {k: (gathered[k] if k in col_sharded else gathered[k].reshape(-1, gathered[k].shape[2])) for k in names}
    cin = W["w_in"].shape[2]
    bn_in = _fit(cin, 512)
    while (2 * HW) % bn_in:
        bn_in -= 128

    h = _rmsnorm_fwd("rmsnorm_mix", x2d, g_mix)
    (qk,) = _matmul("proj_qk", h, W["w_in"], mode="nn", out_dtypes=[F32], b_cshard=True, b_off=0, n_out=2 * HW, bn=bn_in, bk=D)
    (mid,) = _matmul("proj_mid", h, W["w_in"], mode="nn", out_dtypes=[MXU_DTYPE], b_cshard=True, b_off=2 * HW // bn_in,
                     n_out=4 * HW, bn=bn_in, bk=D)
    (sg,) = _matmul("proj_gates", h, W["w_in"], mode="nn", out_dtypes=[F32], b_cshard=True, b_off=6 * HW // bn_in,
                    n_out=2 * D, bn=bn_in, bk=D, epilogue=lambda acc: (_sigmoid(acc),))
    tabs = _rope_tables(S)
    qa, ka = _qknorm_fwd(qk, qn_gain, kn_gain, tabs, HW)
    dil = [_dilated_fwd(qa, ka, mid, d, HW) for d in DILATIONS]
    ya, lse = _dilated_combine([o for o, _ in dil], [l for _, l in dil], HW)
    yb, sb_tot = _sb_fwd(mid, HW)

    gate_blocks = D // _fit(D, 1024)
    (ua,) = _matmul("branch_a", ya, W["w_branch_a"], mode="nn", out_dtypes=[F32], b_cshard=True, bn=_fit(W["w_branch_a"].shape[2], 1024))
    bn_b = _fit(W["w_branch_b"].shape[2], 1024)
    ub, merged = _matmul("branch_b_merge", yb, W["w_branch_b"], mode="nn", out_dtypes=[F32, MXU_DTYPE], b_cshard=True, bn=bn_b,
                         extras=[(sg, 0), (sg, D // bn_b), (ua, 0)],
                         epilogue=lambda acc, sga, sgb, uav: (acc, sga * uav + sgb * acc))
    (x1,) = _matmul("out_proj", merged, W["w_out"], mode="nn", out_dtypes=[F32], extras=[(x2d, 0)], epilogue=lambda acc, xv: (xv + acc,))
    hm = _rmsnorm_fwd("rmsnorm_mlp", x1, g_mlp)

    def up_epilogue(acc):
        r = jnp.maximum(acc, 0.0)
        return r * r, r

    act, rup = _matmul("mlp_up", hm, W["w_up"], mode="nn", out_dtypes=[MXU_DTYPE, MXU_DTYPE], b_cshard=True,
                       bn=_fit(W["w_up"].shape[2], 1024), bk=D, epilogue=up_epilogue)
    (x2,) = _matmul("mlp_down", act, W["w_down"], mode="nn", out_dtypes=[F32], extras=[(x1, 0)], epilogue=lambda acc, xv: (xv + acc,))
    hp = _rmsnorm_fwd("rmsnorm_ple", x2, g_ple)
    (pp,) = _matmul("ple_proj", p2d, W["w_ple_proj"], mode="nn", out_dtypes=[F32], b_cshard=True, bn=_fit(W["w_ple_proj"].shape[2], 1024))

    def ple_epilogue(acc, ppv, x2v, tv):
        s = _sigmoid(acc)
        dx3 = ((x2v + ppv * s) - tv) / D
        return dx3, dx3 * s, dx3 * ppv * (s * (1.0 - s))

    dx3, d_pp, d_gate = _matmul("ple_gate_loss", hp, W["w_ple_gate"], mode="nn", out_dtypes=[F32, MXU_DTYPE, MXU_DTYPE],
                                bm=512, extras=[(pp, 0), (x2, 0), (tgt, 0)], epilogue=ple_epilogue)

    G = {}
    (G["w_ple_proj"],) = _matmul("grad_w_ple_proj", p2d, d_pp, mode="tn", out_dtypes=[F32], out_cshard=True,
                                 bn=_fit(d_pp.shape[1] // N_SHARD, 1024))
    (G["w_ple_gate"],) = _matmul("grad_w_ple_gate", hp, d_gate, mode="tn", out_dtypes=[F32])
    (d_hp,) = _matmul("ple_gate_bwd", d_gate, W["w_ple_gate"], mode="nt", out_dtypes=[F32])
    dx2, g_g_ple, loss_part = _rmsnorm_bwd("rmsnorm_ple_bwd", d_hp, x2, g_ple, dx3, True)
    (G["w_down"],) = _matmul("grad_w_down", act, dx2, mode="tn", out_dtypes=[F32])
    (d_up,) = _matmul("mlp_down_bwd", dx2, W["w_down"], mode="nt", out_dtypes=[MXU_DTYPE], extras=[(rup, 0)],
                      epilogue=lambda acc, r: (acc * (2.0 * r.astype(F32)),))
    (G["w_up"],) = _matmul("grad_w_up", hm, d_up, mode="tn", out_dtypes=[F32], out_cshard=True, bn=_fit(d_up.shape[1] // N_SHARD, 1024))
    (d_hm,) = _matmul("mlp_up_bwd", d_up, W["w_up"], mode="nt", out_dtypes=[F32], b_cshard=True, bk=_fit(W["w_up"].shape[2], 1024))
    dx1, g_g_mlp = _rmsnorm_bwd("rmsnorm_mlp_bwd", d_hm, x1, g_mlp, dx2, False)
    (G["w_out"],) = _matmul("grad_w_out", merged, dx1, mode="tn", out_dtypes=[F32])

    def merge_bwd(acc, sga, sgb, uav, ubv):
        return acc * sga, acc * sgb, acc * uav * (sga * (1.0 - sga)), acc * ubv * (sgb * (1.0 - sgb))

    bn_m = _fit(D, 1024)
    d_ua, d_ub, d_ga, d_gb = _matmul("out_proj_bwd", dx1, W["w_out"], mode="nt", out_dtypes=[MXU_DTYPE] * 4, bm=512, bn=bn_m,
                                     extras=[(sg, 0), (sg, D // bn_m), (ua, 0), (ub, 0)], epilogue=merge_bwd)
    bn_br = _fit(D // N_SHARD, 1024)
    (G["w_branch_a"],) = _matmul("grad_w_branch_a", ya, d_ua, mode="tn", out_dtypes=[F32], out_cshard=True, bn=bn_br)
    (G["w_branch_b"],) = _matmul("grad_w_branch_b", yb, d_ub, mode="tn", out_dtypes=[F32], out_cshard=True, bn=bn_br)
    (d_ya,) = _matmul("branch_a_bwd", d_ua, W["w_branch_a"], mode="nt", out_dtypes=[F32], b_cshard=True, bk=bn_br)
    (d_yb,) = _matmul("branch_b_bwd", d_ub, W["w_branch_b"], mode="nt", out_dtypes=[F32], b_cshard=True, bk=bn_br)

    dqb, dkb, dvb = _sb_bwd(mid, d_yb, sb_tot, HW)
    dil_b = [_dilated_bwd(qa, ka, mid, d_ya, ya, lse, d, HW) for d in DILATIONS]
    d_qk, g_qn, g_kn = _qknorm_bwd(qk, qn_gain, kn_gain, tabs, [t[0] for t in dil_b], [t[1] for t in dil_b],
                                   [t[2] for t in dil_b], HW)
    dva = _dv_sum([t[3] for t in dil_b], [t[4] for t in dil_b], HW)
    d_proj = jnp.concatenate([d_qk, dva, dqb.astype(MXU_DTYPE), dkb.astype(MXU_DTYPE), dvb.astype(MXU_DTYPE), d_ga, d_gb], axis=1)
    (G["w_in"],) = _matmul("grad_w_in", h, d_proj, mode="tn", out_dtypes=[F32], out_cshard=True, bn=bn_in)
    (d_h,) = _matmul("proj_bwd", d_proj, W["w_in"], mode="nt", out_dtypes=[F32], b_cshard=True, bk=_fit(cin, 1280))
    grad_x, g_g_mix = _rmsnorm_bwd("rmsnorm_mix_bwd", d_h, x2d, g_mix, dx1, False)

    c_idx = lax.axis_index("c").astype(jnp.int32).reshape(1)
    s_idx = (2 * lax.axis_index("x") + lax.axis_index("y")).astype(jnp.int32).reshape(1)
    g3 = [G[k] if k in col_sharded else G[k].reshape(N_SHARD, -1, G[k].shape[1]) for k in names]
    swapped = _swap_halves(g3)
    pre = [_add_half(f"grad_add_half_{k}", g, got, c_idx) for k, g, got in zip(names, g3, swapped)]
    landed = _scatter_partials([p16 for _, p16 in pre])
    halves = [_add_partials(f"grad_add_partials_{k}", p32, got, s_idx) for k, (p32, _), got in zip(names, pre, landed)]
    reduced = dict(zip(names, _join_halves(halves)))

    pack_w = -(-(3 * D + 3 * 128) // (8 * 128)) * 128

    def pack(v_mix, v_mlp, v_ple, v_qn, v_kn, extra):
        flat = jnp.concatenate([v_mix.reshape(-1), v_mlp.reshape(-1), v_ple.reshape(-1), v_qn.reshape(-1), v_kn.reshape(-1), extra.reshape(-1)])
        return jnp.pad(flat, (0, 8 * pack_w - flat.shape[0])).reshape(8, pack_w)

    def unpack(blk):
        flat = blk.reshape(-1)
        return (flat[:D].reshape(1, D), flat[D:2 * D].reshape(1, D), flat[2 * D:3 * D].reshape(1, D),
                flat[3 * D:3 * D + 128].reshape(1, 128), flat[3 * D + 128:3 * D + 256].reshape(1, 128), flat[3 * D + 256])

    small = _all_reduce_small(pack(g_g_mix, g_g_mlp, g_g_ple, g_qn, g_kn, loss_part))
    sw = pack(g_mix, g_mlp, g_ple, qn_gain, kn_gain, jnp.zeros((128,), F32))
    sm = pack(m_g_mix, m_g_mlp, m_g_ple, m_qn_gain, m_kn_gain, jnp.zeros((128,), F32))
    sv = pack(v_g_mix, v_g_mlp, v_g_ple, v_qn_gain, v_kn_gain, jnp.ones((128,), F32))
    s_delta, s_nm, s_nv = _adamw("adamw_small", sw, small, sm, sv)
    sg_mix, sg_mlp, sg_ple, sg_qn, sg_kn, loss = unpack(small)
    small_out = {}
    for tag, blk in (("delta", s_delta), ("new_m", s_nm), ("new_v", s_nv)):
        u = unpack(blk)
        small_out[tag] = dict(g_mix=u[0], g_mlp=u[1], g_ple=u[2], qn_gain=u[3], kn_gain=u[4])
    small_grad = dict(g_mix=sg_mix, g_mlp=sg_mlp, g_ple=sg_ple, qn_gain=sg_qn, kn_gain=sg_kn)

    big_out = {"grad": {}, "delta": {}, "new_m": {}, "new_v": {}}
    for k in names:
        shape = big[k].shape
        m2, v2 = (t.reshape(shape[1], shape[2]) for t in moments[k])
        delta, nm, nv = _adamw(f"adamw_{k}", shard2d[k], reduced[k], m2, v2)
        big_out["grad"][k] = reduced[k].reshape(shape)
        big_out["delta"][k], big_out["new_m"][k], big_out["new_v"][k] = delta.reshape(shape), nm.reshape(shape), nv.reshape(shape)

    order = ["g_mix", "w_in", "qn_gain", "kn_gain", "w_branch_a", "w_branch_b", "w_out", "g_mlp", "w_up", "w_down", "g_ple",
             "w_ple_gate", "w_ple_proj"]
    outs = [loss, grad_x.reshape(x.shape)]
    outs += [small_grad[k] if k in small_grad else big_out["grad"][k] for k in order]
    for tag in ("delta", "new_m", "new_v"):
        outs += [small_out[tag][k] if k in small_grad else big_out[tag][k] for k in order]
    return tuple(outs)
```

```python
import functools

import jax
import jax.numpy as jnp
from jax import lax
from jax.experimental import pallas as pl
from jax.experimental.pallas import tpu as pltpu

F32 = jnp.float32
MXU_DTYPE = jnp.bfloat16
HEAD_DIM = 128
ROT_DIM = HEAD_DIM // 4
ROPE_THETA = 500000.0
EPS = 1e-6
DILATIONS = (1, 4, 16)
BLOCK = 128
N_SHARD = 4
ADAM_LR, ADAM_B1, ADAM_B2, ADAM_EPS, ADAM_WD, ADAM_STEP = 0.001, 0.9, 0.999, 1e-08, 0.01, 10
V7X_VMEM_BYTES = 64 * 1024 * 1024
VMEM_LIMIT = V7X_VMEM_BYTES - 8 * 1024 * 1024
MESH = pl.DeviceIdType.MESH
NEG = -1e30
SB_BQ, SB_BK = 512, 256
CUMSUM_PASSES = 2


def _fit(dim, pref):
    if dim <= pref:
        return dim
    b = (pref // 128) * 128
    while dim % b:
        b -= 128
    return b


def _params(sem=None):
    return pltpu.CompilerParams(dimension_semantics=sem, vmem_limit_bytes=VMEM_LIMIT)


def _dot(a, b, dims=(((1,), (0,)), ((), ()))):
    return lax.dot_general(a, b, dims, preferred_element_type=F32)


NT = (((1,), (1,)), ((), ()))
TN = (((0,), (0,)), ((), ()))


def _split_dot(x, u, passes):
    out = None
    r = x
    for p in range(passes):
        hi = r.astype(MXU_DTYPE)
        part = _dot(hi, u)
        out = part if out is None else out + part
        if p + 1 < passes:
            r = r - hi.astype(F32)
    return out


def _matmul(name, a, b, *, mode, out_dtypes, bm=1024, bn=1024, bk=2048, b_cshard=False, b_off=0, n_out=None,
            extras=(), epilogue=None, out_cshard=False):
    if mode == "tn":
        K, M = a.shape
        N = b.shape[1]
    else:
        M, K = a.shape
        if mode == "nn":
            N = n_out if n_out is not None else (N_SHARD * b.shape[2] if b_cshard else b.shape[1])
        else:
            N = b.shape[1] if b_cshard else b.shape[0]
    bm, bn, bk = _fit(M, bm), _fit(N, bn), _fit(K, bk)
    nk = K // bk
    grid = (M // bm, N // bn, nk)

    if mode == "tn":
        a_spec = pl.BlockSpec((bk, bm), lambda i, j, k: (k, i))
        b_spec = pl.BlockSpec((bk, bn), lambda i, j, k: (k, j))
        dims = TN
    elif mode == "nn":
        a_spec = pl.BlockSpec((bm, bk), lambda i, j, k: (i, k))
        if b_cshard:
            cb = b.shape[2] // bn
            b_spec = pl.BlockSpec((None, bk, bn), lambda i, j, k: ((j + b_off) // cb, k, (j + b_off) % cb))
        else:
            b_spec = pl.BlockSpec((bk, bn), lambda i, j, k: (k, j + b_off))
        dims = (((1,), (0,)), ((), ()))
    else:
        a_spec = pl.BlockSpec((bm, bk), lambda i, j, k: (i, k))
        if b_cshard:
            cb = b.shape[2] // bk
            b_spec = pl.BlockSpec((None, bn, bk), lambda i, j, k: (k // cb, j, k % cb))
        else:
            b_spec = pl.BlockSpec((bn, bk), lambda i, j, k: (j, k))
        dims = NT

    ex_arrays = [e[0] for e in extras]
    ex_specs = [pl.BlockSpec((bm, bn), functools.partial(lambda i, j, k, off: (i, j + off), off=e[1])) for e in extras]
    if out_cshard:
        cbo = (N // N_SHARD) // bn
        out_shape = [jax.ShapeDtypeStruct((N_SHARD, M, N // N_SHARD), dt) for dt in out_dtypes]
        out_specs = [pl.BlockSpec((None, bm, bn), lambda i, j, k: (j // cbo, i, j % cbo)) for _ in out_dtypes]
    else:
        out_shape = [jax.ShapeDtypeStruct((M, N), dt) for dt in out_dtypes]
        out_specs = [pl.BlockSpec((bm, bn), lambda i, j, k: (i, j)) for _ in out_dtypes]
    ne, no = len(extras), len(out_dtypes)

    def kern(*refs):
        a_ref, b_ref = refs[0], refs[1]
        ex_refs = refs[2:2 + ne]
        o_refs = refs[2 + ne:2 + ne + no]
        part = _dot(a_ref[...].astype(MXU_DTYPE), b_ref[...].astype(MXU_DTYPE), dims)

        def finish(acc):
            vals = (acc,) if epilogue is None else epilogue(acc, *[r[...] for r in ex_refs])
            for r, v in zip(o_refs, vals):
                r[...] = v.astype(r.dtype)

        if nk == 1:
            finish(part)
        else:
            acc_ref = refs[2 + ne + no]
            k = pl.program_id(2)

            @pl.when(k == 0)
            def _():
                acc_ref[...] = part

            @pl.when(k > 0)
            def _():
                acc_ref[...] += part

            @pl.when(k == nk - 1)
            def _():
                finish(acc_ref[...])

    outs = pl.pallas_call(
        kern, name=name, grid=grid,
        in_specs=[a_spec, b_spec] + ex_specs, out_specs=out_specs, out_shape=out_shape,
        scratch_shapes=[pltpu.VMEM((bm, bn), F32)] if nk > 1 else [],
        compiler_params=_params(("parallel", "parallel", "arbitrary")),
    )(a, b, *ex_arrays)
    return outs


def _row(tr, w, coff=0):
    return pl.BlockSpec((tr, w), lambda i: (i, coff))


def _vec(w):
    return pl.BlockSpec((1, w), lambda i: (0, 0))


def _rows_call(name, body, n_rows, tr, ins, outs, accs=()):
    n_in, n_out = len(ins), len(outs)

    def kern(*refs):
        acc_refs = refs[n_in + n_out:]
        if acc_refs:
            @pl.when(pl.program_id(0) == 0)
            def _():
                for r in acc_refs:
                    r[...] = jnp.zeros_like(r)
        body(*refs)

    out_shape = [jax.ShapeDtypeStruct((n_rows, w), dt) for w, dt in outs] + [jax.ShapeDtypeStruct((1, w), F32) for w in accs]
    out_specs = [_row(tr, w) for w, _ in outs] + [_vec(w) for w in accs]
    return pl.pallas_call(
        kern, name=name, grid=(n_rows // tr,),
        in_specs=[s for _, s in ins], out_specs=out_specs, out_shape=out_shape,
        compiler_params=_params(("arbitrary",)),
    )(*[a for a, _ in ins])


def _rmsnorm_fwd(name, x, g):
    S, D = x.shape
    tr = _fit(S, 256)

    def body(x_ref, g_ref, h_ref):
        xv = x_ref[...]
        r = lax.rsqrt(jnp.mean(xv * xv, axis=1, keepdims=True) + EPS)
        h_ref[...] = ((xv * r) * g_ref[...]).astype(h_ref.dtype)

    return _rows_call(name, body, S, tr, [(x, _row(tr, D)), (g, _vec(D))], [(D, MXU_DTYPE)])[0]


def _rmsnorm_bwd(name, dh, x, g, resid, with_loss):
    S, D = x.shape
    tr = _fit(S, 256)

    def body(dh_ref, x_ref, g_ref, res_ref, dx_ref, dg_ref, *loss_ref):
        xv = x_ref[...]
        r = lax.rsqrt(jnp.mean(xv * xv, axis=1, keepdims=True) + EPS)
        dhv = dh_ref[...]
        u = dhv * g_ref[...]
        xr = xv * r
        dx = r * u - xr * (r * r) * jnp.mean(xv * u, axis=1, keepdims=True)
        resv = res_ref[...]
        dx_ref[...] = resv + dx
        dg_ref[...] += jnp.sum(dhv * xr, axis=0, keepdims=True)
        if with_loss:
            loss_ref[0][...] += (0.5 * D) * jnp.sum(resv * resv)

    outs = _rows_call(name, body, S, tr, [(dh, _row(tr, D)), (x, _row(tr, D)), (g, _vec(D)), (resid, _row(tr, D))],
                      [(D, F32)], accs=(D, 128) if with_loss else (D,))
    return outs


def _rope_tables(S):
    half = ROT_DIM // 2
    pos = jnp.arange(S, dtype=F32)
    inv = ROPE_THETA ** (-jnp.arange(0, ROT_DIM, 2, dtype=F32) / ROT_DIM)
    ang = pos[:, None] * inv[None, :]
    cos, sin = jnp.cos(ang), jnp.sin(ang)
    pad = HEAD_DIM - ROT_DIM
    ctab = jnp.concatenate([cos, cos, jnp.ones((S, pad), F32)], axis=1)
    atab = jnp.concatenate([-sin, jnp.zeros((S, pad + half), F32)], axis=1)
    btab = jnp.concatenate([jnp.zeros((S, half), F32), sin, jnp.zeros((S, pad), F32)], axis=1)
    return ctab, atab, btab


def _qknorm_fwd(qk, qn, kn, tabs, HW):
    S = qk.shape[0]
    tr = _fit(S, 256)
    half = ROT_DIM // 2

    def body(qk_ref, qn_ref, kn_ref, c_ref, a_ref, b_ref, q_out, k_out):
        ct, at, bt = c_ref[...], a_ref[...], b_ref[...]
        for part, (g_ref, o_ref) in enumerate(((qn_ref, q_out), (kn_ref, k_out))):
            gv = g_ref[...]
            for h in range(HW // HEAD_DIM):
                xh = qk_ref[:, part * HW + h * HEAD_DIM: part * HW + (h + 1) * HEAD_DIM]
                r = lax.rsqrt(jnp.mean(xh * xh, axis=1, keepdims=True) + EPS)
                y = (xh * r) * gv
                o = y * ct + pltpu.roll(y, HEAD_DIM - half, 1) * at + pltpu.roll(y, half, 1) * bt
                o_ref[:, h * HEAD_DIM:(h + 1) * HEAD_DIM] = o.astype(o_ref.dtype)

    ins = [(qk, _row(tr, 2 * HW)), (qn, _vec(HEAD_DIM)), (kn, _vec(HEAD_DIM))] + [(t, _row(tr, HEAD_DIM)) for t in tabs]
    return _rows_call("qknorm_fwd", body, S, tr, ins, [(HW, MXU_DTYPE), (HW, MXU_DTYPE)])


def _shift_spec(tr, w, shift, nblk):
    return pl.BlockSpec((tr, w), lambda i: (jnp.minimum(i + shift, nblk - 1), 0))


def _qknorm_bwd(qk, qn, kn, tabs, dq_parts, dk_cur, dk_prev, HW):
    S = qk.shape[0]
    tr = BLOCK
    nblk = S // tr
    half = ROT_DIM // 2
    nd = len(DILATIONS)

    def body(*refs):
        qk_ref, qn_ref, kn_ref, c_ref, a_ref, b_ref = refs[:6]
        dq_refs = refs[6:6 + nd]
        dkc_refs = refs[6 + nd:6 + 2 * nd]
        dkp_refs = refs[6 + 2 * nd:6 + 3 * nd]
        d_out, dqn_ref, dkn_ref = refs[6 + 3 * nd:]
        i = pl.program_id(0)
        ct, at, bt = c_ref[...], a_ref[...], b_ref[...]
        live = [(i + d < nblk).astype(F32) for d in DILATIONS]
        for part, (g_ref, dg_ref) in enumerate(((qn_ref, dqn_ref), (kn_ref, dkn_ref))):
            gv = g_ref[...]
            dg = jnp.zeros((1, HEAD_DIM), F32)
            for h in range(HW // HEAD_DIM):
                hs = slice(h * HEAD_DIM, (h + 1) * HEAD_DIM)
                if part == 0:
                    do = dq_refs[0][:, hs] + dq_refs[1][:, hs] + dq_refs[2][:, hs]
                else:
                    do = dkc_refs[0][:, hs] + dkc_refs[1][:, hs] + dkc_refs[2][:, hs]
                    for n in range(nd):
                        do = do + dkp_refs[n][:, hs] * live[n]
                dy = do * ct + pltpu.roll(do * at, half, 1) + pltpu.roll(do * bt, HEAD_DIM - half, 1)
                xh = qk_ref[:, part * HW + h * HEAD_DIM: part * HW + (h + 1) * HEAD_DIM]
                r = lax.rsqrt(jnp.mean(xh * xh, axis=1, keepdims=True) + EPS)
                xr = xh * r
                u = dy * gv
                dx = r * u - xr * (r * r) * jnp.mean(xh * u, axis=1, keepdims=True)
                d_out[:, part * HW + h * HEAD_DIM: part * HW + (h + 1) * HEAD_DIM] = dx.astype(d_out.dtype)
                dg = dg + jnp.sum(dy * xr, axis=0, keepdims=True)
            dg_ref[...] += dg

    ins = [(qk, _row(tr, 2 * HW)), (qn, _vec(HEAD_DIM)), (kn, _vec(HEAD_DIM))] + [(t, _row(tr, HEAD_DIM)) for t in tabs]
    ins += [(a, _row(tr, HW)) for a in dq_parts] + [(a, _row(tr, HW)) for a in dk_cur]
    ins += [(a, _shift_spec(tr, HW, d, nblk)) for a, d in zip(dk_prev, DILATIONS)]
    return _rows_call("qknorm_bwd", body, S, tr, ins, [(2 * HW, MXU_DTYPE)], accs=(HEAD_DIM, HEAD_DIM))


def _dv_sum(dv_cur, dv_prev, HW):
    S = dv_cur[0].shape[0]
    tr = BLOCK
    nblk = S // tr
    nd = len(DILATIONS)

    def body(*refs):
        i = pl.program_id(0)
        out = refs[2 * nd]
        acc = refs[0][...] + refs[1][...] + refs[2][...]
        for n, d in enumerate(DILATIONS):
            acc = acc + refs[nd + n][...] * (i + d < nblk).astype(F32)
        out[...] = acc.astype(out.dtype)

    ins = [(a, _row(tr, HW)) for a in dv_cur] + [(a, _shift_spec(tr, HW, d, nblk)) for a, d in zip(dv_prev, DILATIONS)]
    return _rows_call("dilated_dv_sum", body, S, tr, ins, [(HW, MXU_DTYPE)])[0]


def _dil_masks(n):
    qi = lax.broadcasted_iota(jnp.int32, (BLOCK, BLOCK), 0)
    ki = lax.broadcasted_iota(jnp.int32, (BLOCK, BLOCK), 1)
    return qi >= ki, (qi <= ki) & (n > 0)


def _dil_specs(d, HW, ngroups, group):
    cur = pl.BlockSpec((BLOCK, HW), lambda r, n: (n, r * ngroups + group))
    prev = pl.BlockSpec((BLOCK, HW), lambda r, n: (jnp.maximum(n - 1, 0), r * ngroups + group))
    return cur, prev


def _dilated_fwd(q, k, mid, d, HW):
    S = q.shape[0]
    M = S // d
    nb = M // BLOCK
    H = HW // HEAD_DIM
    scale = HEAD_DIM ** -0.5
    qv, kv, mv = q.reshape(M, d * HW), k.reshape(M, d * HW), mid.reshape(M, d * 4 * HW)
    qc, qp = _dil_specs(d, HW, 1, 0)
    vc, vp = _dil_specs(d, HW, 4, 0)

    def kern(q_ref, kc_ref, kp_ref, vc_ref, vp_ref, o_ref, l_ref):
        n = pl.program_id(1)
        mc, mp = _dil_masks(n)
        for h in range(H):
            hs = slice(h * HEAD_DIM, (h + 1) * HEAD_DIM)
            qh = q_ref[:, hs]
            sc = jnp.where(mc, _dot(qh, kc_ref[:, hs], NT) * scale, NEG)
            sp = jnp.where(mp, _dot(qh, kp_ref[:, hs], NT) * scale, NEG)
            mx = jnp.maximum(jnp.max(sc, axis=1, keepdims=True), jnp.max(sp, axis=1, keepdims=True))
            ec, ep = jnp.exp(sc - mx), jnp.exp(sp - mx)
            den = jnp.sum(ec, axis=1, keepdims=True) + jnp.sum(ep, axis=1, keepdims=True)
            o = _dot(ec.astype(MXU_DTYPE), vc_ref[:, hs]) + _dot(ep.astype(MXU_DTYPE), vp_ref[:, hs])
            o_ref[:, hs] = o / den
            l_ref[:, hs] = jnp.broadcast_to(mx + jnp.log(den), (BLOCK, HEAD_DIM))

    o, lse = pl.pallas_call(
        kern, name=f"dilated_fwd_d{d}", grid=(d, nb),
        in_specs=[qc, qc, qp, vc, vp],
        out_specs=[qc, qc],
        out_shape=[jax.ShapeDtypeStruct((M, d * HW), F32)] * 2,
        compiler_params=_params(("parallel", "arbitrary")),
    )(qv, kv, kv, mv, mv)
    return o.reshape(S, HW), lse.reshape(S, HW)


def _dilated_combine(os_, lses, HW):
    S = os_[0].shape[0]
    tr = _fit(S, 256)

    def body(o0, o1, o2, l0, l1, l2, ya_ref, lse_ref):
        a, b, c = l0[...], l1[...], l2[...]
        mx = jnp.maximum(jnp.maximum(a, b), c)
        ea, eb, ec = jnp.exp(a - mx), jnp.exp(b - mx), jnp.exp(c - mx)
        tot = ea + eb + ec
        ya_ref[...] = (ea * o0[...] + eb * o1[...] + ec * o2[...]) / tot
        lse_ref[...] = mx + jnp.log(tot)

    ins = [(a, _row(tr, HW)) for a in list(os_) + list(lses)]
    return _rows_call("dilated_combine", body, S, tr, ins, [(HW, F32), (HW, F32)])


def _dilated_bwd(q, k, mid, dya, ya, lse, d, HW):
    S = q.shape[0]
    M = S // d
    nb = M // BLOCK
    H = HW // HEAD_DIM
    scale = HEAD_DIM ** -0.5
    view = lambda t: t.reshape(M, d * t.shape[1])
    qc, qp = _dil_specs(d, HW, 1, 0)
    vc, vp = _dil_specs(d, HW, 4, 0)

    def kern(q_ref, kc_ref, kp_ref, vc_ref, vp_ref, dy_ref, y_ref, l_ref, dq_ref, dkc_ref, dkp_ref, dvc_ref, dvp_ref):
        n = pl.program_id(1)
        mc, mp = _dil_masks(n)
        for h in range(H):
            hs = slice(h * HEAD_DIM, (h + 1) * HEAD_DIM)
            qh, kc, kp = q_ref[:, hs], kc_ref[:, hs], kp_ref[:, hs]
            dy = dy_ref[:, hs]
            dyb = dy.astype(MXU_DTYPE)
            lt = l_ref[:, h * HEAD_DIM:h * HEAD_DIM + 1]
            delta = jnp.sum(dy * y_ref[:, hs], axis=1, keepdims=True)
            pc = jnp.where(mc, jnp.exp(_dot(qh, kc, NT) * scale - lt), 0.0)
            pp = jnp.where(mp, jnp.exp(_dot(qh, kp, NT) * scale - lt), 0.0)
            dsc = (pc * (_dot(dyb, vc_ref[:, hs], NT) - delta) * scale).astype(MXU_DTYPE)
            dsp = (pp * (_dot(dyb, vp_ref[:, hs], NT) - delta) * scale).astype(MXU_DTYPE)
            dq_ref[:, hs] = _dot(dsc, kc) + _dot(dsp, kp)
            dkc_ref[:, hs] = _dot(dsc, qh, TN)
            dkp_ref[:, hs] = _dot(dsp, qh, TN)
            dvc_ref[:, hs] = _dot(pc.astype(MXU_DTYPE), dyb, TN)
            dvp_ref[:, hs] = _dot(pp.astype(MXU_DTYPE), dyb, TN)

    outs = pl.pallas_call(
        kern, name=f"dilated_bwd_d{d}", grid=(d, nb),
        in_specs=[qc, qc, qp, vc, vp, qc, qc, qc],
        out_specs=[qc] * 5,
        out_shape=[jax.ShapeDtypeStruct((M, d * HW), F32)] * 5,
        compiler_params=_params(("parallel", "arbitrary")),
    )(view(q), view(k), view(k), view(mid), view(mid), view(dya), view(ya), view(lse))
    return [t.reshape(S, HW) for t in outs]


def _softplus_parts(z):
    sp = jnp.maximum(z, 0.0) + jnp.log1p(jnp.exp(-jnp.abs(z)))
    return -sp, z - sp


def _sb_specs(S, H):
    q_spec = pl.BlockSpec((SB_BQ, HEAD_DIM), lambda h, i: (i, H + h))
    k_spec = pl.BlockSpec((S, HEAD_DIM), lambda h, i: (0, 2 * H + h))
    v_spec = pl.BlockSpec((S, HEAD_DIM), lambda h, i: (0, 3 * H + h))
    o_spec = pl.BlockSpec((SB_BQ, HEAD_DIM), lambda h, i: (i, h))
    return q_spec, k_spec, v_spec, o_spec


def _sb_masks():
    row = lax.broadcasted_iota(jnp.int32, (SB_BQ, SB_BQ), 0)
    col = lax.broadcasted_iota(jnp.int32, (SB_BQ, SB_BQ), 1)
    tri_r = lax.broadcasted_iota(jnp.int32, (SB_BK, SB_BK), 0)
    tri_c = lax.broadcasted_iota(jnp.int32, (SB_BK, SB_BK), 1)
    return tri_r, tri_c, col < row


def _sb_fwd(mid, HW):
    S = mid.shape[0]
    H = HW // HEAD_DIM
    BQ, CH = SB_BQ, SB_BK
    NC = BQ // CH
    scale = HEAD_DIM ** -0.5
    q_spec, k_spec, v_spec, o_spec = _sb_specs(S, H)

    def kern(q_ref, k_ref, v_ref, o_ref, t_ref):
        i = pl.program_id(1)
        q = q_ref[...]
        tri_r, tri_c, causal = _sb_masks()
        upper = (tri_r > tri_c).astype(MXU_DTYPE)

        def block(j, run, acc, masked):
            ks = pl.multiple_of(j * BQ, BQ)
            z = _dot(q, k_ref[pl.ds(ks, BQ), :], NT) * scale
            m, l = _softplus_parts(z)
            if masked:
                m = jnp.where(causal, m, 0.0)
            parts = []
            for c in reversed(range(NC)):
                mc = m[:, c * CH:(c + 1) * CH]
                parts.append(l[:, c * CH:(c + 1) * CH] + (_split_dot(mc, upper, CUMSUM_PASSES) + run))
                run = run + jnp.sum(mc, axis=1, keepdims=True)
            a = jnp.exp(jnp.concatenate(parts[::-1], axis=1))
            if masked:
                a = jnp.where(causal, a, 0.0)
            return run, acc + _dot(a.astype(MXU_DTYPE), v_ref[pl.ds(ks, BQ), :])

        run, acc = block(i, jnp.zeros((BQ, 1), F32), jnp.zeros((BQ, HEAD_DIM), F32), True)
        run, acc = lax.fori_loop(0, i, lambda t, carry: block(i - 1 - t, carry[0], carry[1], False), (run, acc))
        o_ref[...] = acc
        t_ref[...] = jnp.broadcast_to(run, (BQ, HEAD_DIM))

    return pl.pallas_call(
        kern, name="stickbreak_fwd", grid=(H, S // BQ),
        in_specs=[q_spec, k_spec, v_spec], out_specs=[o_spec, o_spec],
        out_shape=[jax.ShapeDtypeStruct((S, HW), F32)] * 2,
        compiler_params=_params(("parallel", "arbitrary")),
    )(mid, mid, mid)


def _sb_bwd(mid, dyb, tot, HW):
    S = mid.shape[0]
    H = HW // HEAD_DIM
    BQ, CH = SB_BQ, SB_BK
    NC = BQ // CH
    scale = HEAD_DIM ** -0.5
    q_spec, k_spec, v_spec, o_spec = _sb_specs(S, H)
    full = pl.BlockSpec((S, HEAD_DIM), lambda h, i: (0, h))

    def kern(q_ref, k_ref, v_ref, do_ref, t_ref, dq_ref, dk_ref, dv_ref):
        i = pl.program_id(1)

        @pl.when(i == 0)
        def _():
            dk_ref[...] = jnp.zeros_like(dk_ref)
            dv_ref[...] = jnp.zeros_like(dv_ref)

        q = q_ref[...]
        do = do_ref[...].astype(MXU_DTYPE)
        total = t_ref[:, 0:1]
        tri_r, tri_c, causal = _sb_masks()
        incl = (tri_r <= tri_c).astype(MXU_DTYPE)
        excl = (tri_r < tri_c).astype(MXU_DTYPE)

        def block(j, mrun, prun, dq, masked):
            ks = pl.multiple_of(j * BQ, BQ)
            k = k_ref[pl.ds(ks, BQ), :]
            z = _dot(q, k, NT) * scale
            m, l = _softplus_parts(z)
            if masked:
                m = jnp.where(causal, m, 0.0)
            parts = []
            for c in range(NC):
                mc = m[:, c * CH:(c + 1) * CH]
                parts.append(l[:, c * CH:(c + 1) * CH] + (total - mrun - _split_dot(mc, incl, CUMSUM_PASSES)))
                mrun = mrun + jnp.sum(mc, axis=1, keepdims=True)
            a = jnp.exp(jnp.concatenate(parts, axis=1))
            if masked:
                a = jnp.where(causal, a, 0.0)
            p = a * _dot(do, v_ref[pl.ds(ks, BQ), :], NT)
            parts = []
            for c in range(NC):
                pc = p[:, c * CH:(c + 1) * CH]
                parts.append(_split_dot(pc, excl, CUMSUM_PASSES) + prun)
                prun = prun + jnp.sum(pc, axis=1, keepdims=True)
            before = jnp.concatenate(parts, axis=1)
            sig = jnp.exp(l)
            dz = (p * (1.0 - sig) - sig * before) * scale
            if masked:
                dz = jnp.where(causal, dz, 0.0)
            dzb = dz.astype(MXU_DTYPE)
            dk_ref[pl.ds(ks, BQ), :] += _dot(dzb, q, TN)
            dv_ref[pl.ds(ks, BQ), :] += _dot(a.astype(MXU_DTYPE), do, TN)
            return mrun, prun, dq + _dot(dzb, k)

        zero = jnp.zeros((BQ, 1), F32)
        carry = lax.fori_loop(0, i, lambda j, carry: block(j, *carry, False), (zero, zero, jnp.zeros((BQ, HEAD_DIM), F32)))
        dq_ref[...] = block(i, *carry, True)[2]

    return pl.pallas_call(
        kern, name="stickbreak_bwd", grid=(H, S // BQ),
        in_specs=[q_spec, k_spec, v_spec, o_spec, o_spec], out_specs=[o_spec, full, full],
        out_shape=[jax.ShapeDtypeStruct((S, HW), F32)] * 3,
        compiler_params=_params(("arbitrary", "arbitrary")),
    )(mid, mid, mid, dyb, tot)


ANY = pl.BlockSpec(memory_space=pl.ANY)


def _place():
    x, y, c = lax.axis_index("x"), lax.axis_index("y"), lax.axis_index("c")
    chips = [(1 - x, y), (x, 1 - y), (1 - x, 1 - y)]
    return x, y, c, chips


def _half(ref, shard, hc, rh):
    return ref.at[shard, pl.ds(pl.multiple_of(hc * rh, 8), rh), :]


def _cast_place(name, w, s_idx):
    R, C = w.shape
    tr = _fit(R, 256)

    def kern(s_ref, w_ref, o_ref):
        o_ref[...] = w_ref[...].astype(o_ref.dtype)

    return pl.pallas_call(
        kern, name=name,
        grid_spec=pltpu.PrefetchScalarGridSpec(
            num_scalar_prefetch=1, grid=(R // tr,),
            in_specs=[pl.BlockSpec((tr, C), lambda r, s_ref: (r, 0))],
            out_specs=pl.BlockSpec((None, tr, C), lambda r, s_ref: (s_ref[0], r, 0))),
        out_shape=jax.ShapeDtypeStruct((N_SHARD, R, C), MXU_DTYPE),
        compiler_params=_params(("arbitrary",)),
    )(s_idx, w)


def _gather_weights(bufs):
    n = len(bufs)

    def body(*refs):
        outs = refs[n:2 * n]
        send, recv = refs[2 * n:]
        x, y, c, chips = _place()
        s = 2 * x + y
        sibling = (x, y, 1 - c)
        sent = []
        for i in range(n):
            rh = outs[i].shape[1] // 2
            mine = _half(outs[i], s, c, rh)
            for j, (px, py) in enumerate(chips):
                cp = pltpu.make_async_remote_copy(
                    src_ref=mine, dst_ref=mine,
                    send_sem=send.at[i, j], recv_sem=recv.at[i, j], device_id=(px, py, c), device_id_type=MESH)
                cp.start()
                sent.append(cp)
        for i in range(n):
            rh = outs[i].shape[1] // 2
            for j, (px, py) in enumerate(chips):
                landed = _half(outs[i], 2 * px + py, c, rh)
                pltpu.make_async_remote_copy(src_ref=landed, dst_ref=landed, send_sem=send.at[i, j], recv_sem=recv.at[i, j],
                                             device_id=(px, py, c), device_id_type=MESH).wait_recv()
                cp = pltpu.make_async_remote_copy(src_ref=landed, dst_ref=landed, send_sem=send.at[i, 3 + j],
                                                  recv_sem=recv.at[i, 3 + j], device_id=sibling, device_id_type=MESH)
                cp.start()
                sent.append(cp)
        for i in range(n):
            rh = outs[i].shape[1] // 2
            for j, (px, py) in enumerate(chips):
                passed = _half(outs[i], 2 * px + py, 1 - c, rh)
                pltpu.make_async_remote_copy(src_ref=passed, dst_ref=passed, send_sem=send.at[i, 3 + j],
                                             recv_sem=recv.at[i, 3 + j], device_id=sibling, device_id_type=MESH).wait_recv()
        for cp in sent:
            cp.wait_send()

    return pl.pallas_call(
        body, name="weights_all_gather",
        in_specs=[ANY] * n, out_specs=[ANY] * n,
        out_shape=[jax.ShapeDtypeStruct(b.shape, b.dtype) for b in bufs],
        input_output_aliases={i: i for i in range(n)},
        scratch_shapes=[pltpu.SemaphoreType.DMA((n, 6)), pltpu.SemaphoreType.DMA((n, 6))],
    )(*bufs)


def _swap_halves(grads):
    n = len(grads)

    def body(*refs):
        ins, outs = refs[:n], refs[n:2 * n]
        send, recv = refs[2 * n:]
        x, y, c, _ = _place()
        copies = []
        for i in range(n):
            rh = ins[i].shape[1] // 2
            cp = pltpu.make_async_remote_copy(
                src_ref=ins[i].at[:, pl.ds(pl.multiple_of((1 - c) * rh, 8), rh), :], dst_ref=outs[i],
                send_sem=send.at[i], recv_sem=recv.at[i], device_id=(x, y, 1 - c), device_id_type=MESH)
            cp.start()
            copies.append(cp)
        for cp in copies:
            cp.wait()

    return pl.pallas_call(
        body, name="grad_swap_halves",
        in_specs=[ANY] * n, out_specs=[ANY] * n,
        out_shape=[jax.ShapeDtypeStruct((N_SHARD, g.shape[1] // 2, g.shape[2]), F32) for g in grads],
        scratch_shapes=[pltpu.SemaphoreType.DMA((n,)), pltpu.SemaphoreType.DMA((n,))],
    )(*grads)


def _add_half(name, g, got, c_idx):
    _, R, C = g.shape
    rh = R // 2
    tr = _fit(rh, 256)
    nrb = rh // tr

    def kern(c_ref, g_ref, r_ref, o32_ref, o16_ref):
        v = g_ref[...] + r_ref[...]
        o32_ref[...] = v
        o16_ref[...] = v.astype(o16_ref.dtype)

    blk = lambda s, r, c_ref: (s, r, 0)
    return pl.pallas_call(
        kern, name=name,
        grid_spec=pltpu.PrefetchScalarGridSpec(
            num_scalar_prefetch=1, grid=(N_SHARD, nrb),
            in_specs=[pl.BlockSpec((None, tr, C), lambda s, r, c_ref: (s, c_ref[0] * nrb + r, 0)), pl.BlockSpec((None, tr, C), blk)],
            out_specs=[pl.BlockSpec((None, tr, C), blk)] * 2),
        out_shape=[jax.ShapeDtypeStruct((N_SHARD, rh, C), F32), jax.ShapeDtypeStruct((N_SHARD, rh, C), MXU_DTYPE)],
        compiler_params=_params(("arbitrary", "arbitrary")),
    )(c_idx, g, got)


def _scatter_partials(parts):
    n = len(parts)

    def body(*refs):
        ins, outs = refs[:n], refs[n:2 * n]
        send, recv = refs[2 * n:]
        x, y, c, chips = _place()
        copies = []
        for i in range(n):
            for j, (px, py) in enumerate(chips):
                cp = pltpu.make_async_remote_copy(
                    src_ref=ins[i].at[2 * px + py], dst_ref=outs[i].at[j],
                    send_sem=send.at[i, j], recv_sem=recv.at[i, j], device_id=(px, py, c), device_id_type=MESH)
                cp.start()
                copies.append(cp)
        for cp in copies:
            cp.wait()

    return pl.pallas_call(
        body, name="grad_scatter_partials",
        in_specs=[ANY] * n, out_specs=[ANY] * n,
        out_shape=[jax.ShapeDtypeStruct((3,) + p.shape[1:], p.dtype) for p in parts],
        scratch_shapes=[pltpu.SemaphoreType.DMA((n, 3)), pltpu.SemaphoreType.DMA((n, 3))],
    )(*parts)


def _add_partials(name, own, got, s_idx):
    _, rh, C = own.shape
    tr = _fit(rh, 256)

    def kern(s_ref, o_ref, a_ref, b_ref, c_ref, out_ref):
        out_ref[...] = ((o_ref[...] + a_ref[...].astype(F32)) + b_ref[...].astype(F32)) + c_ref[...].astype(F32)

    slot = lambda j: pl.BlockSpec((None, tr, C), lambda r, s_ref: (j, r, 0))
    return pl.pallas_call(
        kern, name=name,
        grid_spec=pltpu.PrefetchScalarGridSpec(
            num_scalar_prefetch=1, grid=(rh // tr,),
            in_specs=[pl.BlockSpec((None, tr, C), lambda r, s_ref: (s_ref[0], r, 0)), slot(0), slot(1), slot(2)],
            out_specs=pl.BlockSpec((tr, C), lambda r, s_ref: (r, 0))),
        out_shape=jax.ShapeDtypeStruct((rh, C), F32),
        compiler_params=_params(("arbitrary",)),
    )(s_idx, own, got, got, got)


def _swap_reduced(halves):
    n = len(halves)

    def body(*refs):
        ins, outs = refs[:n], refs[n:2 * n]
        send, recv = refs[2 * n:]
        x, y, c, _ = _place()
        copies = []
        for i in range(n):
            cp = pltpu.make_async_remote_copy(src_ref=ins[i], dst_ref=outs[i], send_sem=send.at[i], recv_sem=recv.at[i],
                                              device_id=(x, y, 1 - c), device_id_type=MESH)
            cp.start()
            copies.append(cp)
        for cp in copies:
            cp.wait()

    return pl.pallas_call(
        body, name="grad_swap_reduced",
        in_specs=[ANY] * n, out_specs=[ANY] * n,
        out_shape=[jax.ShapeDtypeStruct(h.shape, F32) for h in halves],
        scratch_shapes=[pltpu.SemaphoreType.DMA((n,)), pltpu.SemaphoreType.DMA((n,))],
    )(*halves)


def _all_reduce_small(v):
    rows, W = v.shape
    ndev = 8

    def body(v_ref, out_ref, buf, send, recv):
        x, y, c, _ = _place()
        me = 4 * x + 2 * y + c
        buf[me] = v_ref[...]
        copies = []
        for r in range(1, ndev):
            fx, fy, fc = (r >> 2) & 1, (r >> 1) & 1, r & 1
            peer = (x ^ fx, y ^ fy, c ^ fc)
            cp = pltpu.make_async_remote_copy(src_ref=v_ref, dst_ref=buf.at[me], send_sem=send.at[r - 1], recv_sem=recv.at[r - 1],
                                              device_id=peer, device_id_type=MESH)
            cp.start()
            copies.append(cp)
        for cp in copies:
            cp.wait()
        acc = buf[0]
        for k in range(1, ndev):
            acc = acc + buf[k]
        out_ref[...] = acc

    return pl.pallas_call(
        body, name="small_all_reduce",
        in_specs=[pl.BlockSpec(memory_space=pltpu.VMEM)], out_specs=pl.BlockSpec(memory_space=pltpu.VMEM),
        out_shape=jax.ShapeDtypeStruct((rows, W), F32),
        scratch_shapes=[pltpu.VMEM((ndev, rows, W), F32), pltpu.SemaphoreType.DMA((ndev - 1,)), pltpu.SemaphoreType.DMA((ndev - 1,))],
    )(v)


def _adamw_update(gv, w_ref, m_ref, v_ref, d_ref, nm_ref, nv_ref):
    nm = ADAM_B1 * m_ref[...] + (1.0 - ADAM_B1) * gv
    nv = ADAM_B2 * v_ref[...] + (1.0 - ADAM_B2) * (gv * gv)
    m_hat = nm / (1.0 - ADAM_B1 ** ADAM_STEP)
    v_hat = nv / (1.0 - ADAM_B2 ** ADAM_STEP)
    d_ref[...] = -ADAM_LR * (m_hat / (jnp.sqrt(v_hat) + ADAM_EPS) + ADAM_WD * w_ref[...])
    nm_ref[...] = nm
    nv_ref[...] = nv


def _adamw(name, w, g, m, v):
    R, C = w.shape
    tr = _fit(R, 256)

    def body(w_ref, g_ref, m_ref, v_ref, d_ref, nm_ref, nv_ref):
        _adamw_update(g_ref[...], w_ref, m_ref, v_ref, d_ref, nm_ref, nv_ref)

    return _rows_call(name, body, R, tr, [(a, _row(tr, C)) for a in (w, g, m, v)], [(C, F32)] * 3)


def _adamw_halves(name, w, mine, theirs, m, v, c_idx):
    R, C = w.shape
    rh = R // 2
    tr = _fit(rh, 256)
    nrb = rh // tr

    def kern(c_ref, w_ref, a_ref, b_ref, m_ref, v_ref, g_ref, d_ref, nm_ref, nv_ref):
        gv = jnp.where(pl.program_id(0) // nrb == c_ref[0], a_ref[...], b_ref[...])
        g_ref[...] = gv
        _adamw_update(gv, w_ref, m_ref, v_ref, d_ref, nm_ref, nv_ref)

    full = pl.BlockSpec((tr, C), lambda r, c_ref: (r, 0))
    pick = lambda own: pl.BlockSpec((tr, C), lambda r, c_ref: (jnp.where((r // nrb == c_ref[0]) == own, r % nrb, 0), 0))
    return pl.pallas_call(
        kern, name=name,
        grid_spec=pltpu.PrefetchScalarGridSpec(
            num_scalar_prefetch=1, grid=(R // tr,),
            in_specs=[full, pick(True), pick(False), full, full], out_specs=[full] * 4),
        out_shape=[jax.ShapeDtypeStruct((R, C), F32)] * 4,
        compiler_params=_params(("arbitrary",)),
    )(c_idx, w, mine, theirs, m, v)


def _sigmoid(z):
    return 1.0 / (1.0 + jnp.exp(-z))


def kernel(x, p, g_mix, w_in, qn_gain, kn_gain, w_branch_a, w_branch_b, w_out, g_mlp, w_up, w_down, g_ple, w_ple_gate, w_ple_proj, loss_target, m_g_mix, m_w_in, m_qn_gain, m_kn_gain, m_w_branch_a, m_w_branch_b, m_w_out, m_g_mlp, m_w_up, m_w_down, m_g_ple, m_w_ple_gate, m_w_ple_proj, v_g_mix, v_w_in, v_qn_gain, v_kn_gain, v_w_branch_a, v_w_branch_b, v_w_out, v_g_mlp, v_w_up, v_w_down, v_g_ple, v_w_ple_gate, v_w_ple_proj):
    S, D = x.shape[1], x.shape[2]
    HW = w_branch_a.shape[1]
    x2d, tgt, p2d = x.reshape(S, D), loss_target.reshape(S, D), p.reshape(S, p.shape[-1])
    big = {"w_in": w_in, "w_branch_a": w_branch_a, "w_branch_b": w_branch_b, "w_out": w_out, "w_up": w_up,
           "w_down": w_down, "w_ple_gate": w_ple_gate, "w_ple_proj": w_ple_proj}
    moments = {"w_in": (m_w_in, v_w_in), "w_branch_a": (m_w_branch_a, v_w_branch_a), "w_branch_b": (m_w_branch_b, v_w_branch_b),
               "w_out": (m_w_out, v_w_out), "w_up": (m_w_up, v_w_up), "w_down": (m_w_down, v_w_down),
               "w_ple_gate": (m_w_ple_gate, v_w_ple_gate), "w_ple_proj": (m_w_ple_proj, v_w_ple_proj)}
    names = list(big)
    col_sharded = {"w_in", "w_branch_a", "w_branch_b", "w_up", "w_ple_proj"}
    shard2d = {k: w.reshape(w.shape[1], w.shape[2]) for k, w in big.items()}

    c_idx = lax.axis_index("c").astype(jnp.int32).reshape(1)
    s_idx = (2 * lax.axis_index("x") + lax.axis_index("y")).astype(jnp.int32).reshape(1)
    gathered = dict(zip(names, _gather_weights([_cast_place(f"cast_{k}", shard2d[k], s_idx) for k in names])))
    W = {k: (gathered[k] if k in col_sharded else gathered[k].reshape(-1, gathered[k].shape[2])) for k in names}
    cin = W["w_in"].shape[2]
    bn_in = _fit(cin, 512)
    while (2 * HW) % bn_in:
        bn_in -= 128

    h = _rmsnorm_fwd("rmsnorm_mix", x2d, g_mix)
    (qk,) = _matmul("proj_qk", h, W["w_in"], mode="nn", out_dtypes=[F32], b_cshard=True, b_off=0, n_out=2 * HW, bn=bn_in, bk=D)
    (mid,) = _matmul("proj_mid", h, W["w_in"], mode="nn", out_dtypes=[MXU_DTYPE], b_cshard=True, b_off=2 * HW // bn_in,
                     n_out=4 * HW, bn=bn_in, bk=D)
    (sg,) = _matmul("proj_gates", h, W["w_in"], mode="nn", out_dtypes=[F32], b_cshard=True, b_off=6 * HW // bn_in,
                    n_out=2 * D, bn=bn_in, bk=D, epilogue=lambda acc: (_sigmoid(acc),))
    tabs = _rope_tables(S)
    qa, ka = _qknorm_fwd(qk, qn_gain, kn_gain, tabs, HW)
    dil = [_dilated_fwd(qa, ka, mid, d, HW) for d in DILATIONS]
    ya, lse = _dilated_combine([o for o, _ in dil], [l for _, l in dil], HW)
    yb, sb_tot = _sb_fwd(mid, HW)

    gate_blocks = D // _fit(D, 1024)
    (ua,) = _matmul("branch_a", ya, W["w_branch_a"], mode="nn", out_dtypes=[F32], b_cshard=True, bn=_fit(W["w_branch_a"].shape[2], 1024))
    bn_b = _fit(W["w_branch_b"].shape[2], 1024)
    ub, merged = _matmul("branch_b_merge", yb, W["w_branch_b"], mode="nn", out_dtypes=[F32, MXU_DTYPE], b_cshard=True, bn=bn_b,
                         extras=[(sg, 0), (sg, D // bn_b), (ua, 0)],
                         epilogue=lambda acc, sga, sgb, uav: (acc, sga * uav + sgb * acc))
    (x1,) = _matmul("out_proj", merged, W["w_out"], mode="nn", out_dtypes=[F32], extras=[(x2d, 0)], epilogue=lambda acc, xv: (xv + acc,))
    hm = _rmsnorm_fwd("rmsnorm_mlp", x1, g_mlp)

    def up_epilogue(acc):
        r = jnp.maximum(acc, 0.0)
        return r * r, r

    act, rup = _matmul("mlp_up", hm, W["w_up"], mode="nn", out_dtypes=[MXU_DTYPE, MXU_DTYPE], b_cshard=True,
                       bn=_fit(W["w_up"].shape[2], 1024), bk=D, epilogue=up_epilogue)
    (x2,) = _matmul("mlp_down", act, W["w_down"], mode="nn", out_dtypes=[F32], extras=[(x1, 0)], epilogue=lambda acc, xv: (xv + acc,))
    hp = _rmsnorm_fwd("rmsnorm_ple", x2, g_ple)
    (pp,) = _matmul("ple_proj", p2d, W["w_ple_proj"], mode="nn", out_dtypes=[F32], b_cshard=True, bn=_fit(W["w_ple_proj"].shape[2], 1024))

    def ple_epilogue(acc, ppv, x2v, tv):
        s = _sigmoid(acc)
        dx3 = ((x2v + ppv * s) - tv) / D
        return dx3, dx3 * s, dx3 * ppv * (s * (1.0 - s))

    dx3, d_pp, d_gate = _matmul("ple_gate_loss", hp, W["w_ple_gate"], mode="nn", out_dtypes=[F32, MXU_DTYPE, MXU_DTYPE],
                                bm=512, extras=[(pp, 0), (x2, 0), (tgt, 0)], epilogue=ple_epilogue)

    G = {}
    (G["w_ple_proj"],) = _matmul("grad_w_ple_proj", p2d, d_pp, mode="tn", out_dtypes=[F32], out_cshard=True,
                                 bn=_fit(d_pp.shape[1] // N_SHARD, 1024))
    (G["w_ple_gate"],) = _matmul("grad_w_ple_gate", hp, d_gate, mode="tn", out_dtypes=[F32])
    (d_hp,) = _matmul("ple_gate_bwd", d_gate, W["w_ple_gate"], mode="nt", out_dtypes=[F32])
    dx2, g_g_ple, loss_part = _rmsnorm_bwd("rmsnorm_ple_bwd", d_hp, x2, g_ple, dx3, True)
    (G["w_down"],) = _matmul("grad_w_down", act, dx2, mode="tn", out_dtypes=[F32])
    (d_up,) = _matmul("mlp_down_bwd", dx2, W["w_down"], mode="nt", out_dtypes=[MXU_DTYPE], extras=[(rup, 0)],
                      epilogue=lambda acc, r: (acc * (2.0 * r.astype(F32)),))
    (G["w_up"],) = _matmul("grad_w_up", hm, d_up, mode="tn", out_dtypes=[F32], out_cshard=True, bn=_fit(d_up.shape[1] // N_SHARD, 1024))
    (d_hm,) = _matmul("mlp_up_bwd", d_up, W["w_up"], mode="nt", out_dtypes=[F32], b_cshard=True, bk=_fit(W["w_up"].shape[2], 2048))
    dx1, g_g_mlp = _rmsnorm_bwd("rmsnorm_mlp_bwd", d_hm, x1, g_mlp, dx2, False)
    (G["w_out"],) = _matmul("grad_w_out", merged, dx1, mode="tn", out_dtypes=[F32])

    def merge_bwd(acc, sga, sgb, uav, ubv):
        return acc * sga, acc * sgb, acc * uav * (sga * (1.0 - sga)), acc * ubv * (sgb * (1.0 - sgb))

    bn_m = _fit(D, 1024)
    d_ua, d_ub, d_ga, d_gb = _matmul("out_proj_bwd", dx1, W["w_out"], mode="nt", out_dtypes=[MXU_DTYPE] * 4, bm=512, bn=bn_m,
                                     extras=[(sg, 0), (sg, D // bn_m), (ua, 0), (ub, 0)], epilogue=merge_bwd)
    bn_br = _fit(D // N_SHARD, 1024)
    (G["w_branch_a"],) = _matmul("grad_w_branch_a", ya, d_ua, mode="tn", out_dtypes=[F32], out_cshard=True, bn=bn_br)
    (G["w_branch_b"],) = _matmul("grad_w_branch_b", yb, d_ub, mode="tn", out_dtypes=[F32], out_cshard=True, bn=bn_br)
    (d_ya,) = _matmul("branch_a_bwd", d_ua, W["w_branch_a"], mode="nt", out_dtypes=[F32], b_cshard=True, bk=bn_br)
    (d_yb,) = _matmul("branch_b_bwd", d_ub, W["w_branch_b"], mode="nt", out_dtypes=[F32], b_cshard=True, bk=bn_br)

    dqb, dkb, dvb = _sb_bwd(mid, d_yb, sb_tot, HW)
    dil_b = [_dilated_bwd(qa, ka, mid, d_ya, ya, lse, d, HW) for d in DILATIONS]
    d_qk, g_qn, g_kn = _qknorm_bwd(qk, qn_gain, kn_gain, tabs, [t[0] for t in dil_b], [t[1] for t in dil_b],
                                   [t[2] for t in dil_b], HW)
    dva = _dv_sum([t[3] for t in dil_b], [t[4] for t in dil_b], HW)
    d_proj = jnp.concatenate([d_qk, dva, dqb.astype(MXU_DTYPE), dkb.astype(MXU_DTYPE), dvb.astype(MXU_DTYPE), d_ga, d_gb], axis=1)
    (G["w_in"],) = _matmul("grad_w_in", h, d_proj, mode="tn", out_dtypes=[F32], out_cshard=True, bn=_fit(cin, 1280))
    (d_h,) = _matmul("proj_bwd", d_proj, W["w_in"], mode="nt", out_dtypes=[F32], b_cshard=True, bk=_fit(cin, 1280))
    grad_x, g_g_mix = _rmsnorm_bwd("rmsnorm_mix_bwd", d_h, x2d, g_mix, dx1, False)

    g3 =[G[k] if k in col_sharded else G[k].reshape(N_SHARD, -1, G[k].shape[1]) for k in names]
    swapped = _swap_halves(g3)
    pre = [_add_half(f"grad_add_half_{k}", g, got, c_idx) for k, g, got in zip(names, g3, swapped)]
    landed = _scatter_partials([p16 for _, p16 in pre])
    halves = [_add_partials(f"grad_add_partials_{k}", p32, got, s_idx) for k, (p32, _), got in zip(names, pre, landed)]
    others = _swap_reduced(halves)

    pack_w = -(-(3 * D + 3 * 128) // (8 * 128)) * 128

    def pack(v_mix, v_mlp, v_ple, v_qn, v_kn, extra):
        flat = jnp.concatenate([v_mix.reshape(-1), v_mlp.reshape(-1), v_ple.reshape(-1), v_qn.reshape(-1), v_kn.reshape(-1), extra.reshape(-1)])
        return jnp.pad(flat, (0, 8 * pack_w - flat.shape[0])).reshape(8, pack_w)

    def unpack(blk):
        flat = blk.reshape(-1)
        return (flat[:D].reshape(1, D), flat[D:2 * D].reshape(1, D), flat[2 * D:3 * D].reshape(1, D),
                flat[3 * D:3 * D + 128].reshape(1, 128), flat[3 * D + 128:3 * D + 256].reshape(1, 128), flat[3 * D + 256])

    small = _all_reduce_small(pack(g_g_mix, g_g_mlp, g_g_ple, g_qn, g_kn, loss_part))
    sw = pack(g_mix, g_mlp, g_ple, qn_gain, kn_gain, jnp.zeros((128,), F32))
    sm = pack(m_g_mix, m_g_mlp, m_g_ple, m_qn_gain, m_kn_gain, jnp.zeros((128,), F32))
    sv = pack(v_g_mix, v_g_mlp, v_g_ple, v_qn_gain, v_kn_gain, jnp.ones((128,), F32))
    s_delta, s_nm, s_nv = _adamw("adamw_small", sw, small, sm, sv)
    sg_mix, sg_mlp, sg_ple, sg_qn, sg_kn, loss = unpack(small)
    small_out = {}
    for tag, blk in (("delta", s_delta), ("new_m", s_nm), ("new_v", s_nv)):
        u = unpack(blk)
        small_out[tag] = dict(g_mix=u[0], g_mlp=u[1], g_ple=u[2], qn_gain=u[3], kn_gain=u[4])
    small_grad = dict(g_mix=sg_mix, g_mlp=sg_mlp, g_ple=sg_ple, qn_gain=sg_qn, kn_gain=sg_kn)

    big_out = {"grad": {}, "delta": {}, "new_m": {}, "new_v": {}}
    for k, mine, theirs in zip(names, halves, others):
        shape = big[k].shape
        m2, v2 = (t.reshape(shape[1], shape[2]) for t in moments[k])
        res = _adamw_halves(f"adamw_{k}", shard2d[k], mine, theirs, m2, v2, c_idx)
        for tag, t in zip(("grad", "delta", "new_m", "new_v"), res):
            big_out[tag][k] = t.reshape(shape)

    order = ["g_mix", "w_in", "qn_gain", "kn_gain", "w_branch_a", "w_branch_b", "w_out", "g_mlp", "w_up", "w_down", "g_ple",
             "w_ple_gate", "w_ple_proj"]
    outs = [loss, grad_x.reshape(x.shape)]
    outs += [small_grad[k] if k in small_grad else big_out["grad"][k] for k in order]
    for tag in ("delta", "new_m", "new_v"):
        outs += [small_out[tag][k] if k in small_grad else big_out[tag][k] for k in order]
    return tuple(outs)
```

```python
import functools

import jax
import jax.numpy as jnp
from jax import lax
from jax.experimental import pallas as pl
from jax.experimental.pallas import tpu as pltpu

F32 = jnp.float32
MXU_DTYPE = jnp.bfloat16
HEAD_DIM = 128
ROT_DIM = HEAD_DIM // 4
ROPE_THETA = 500000.0
EPS = 1e-6
DILATIONS = (1, 4, 16)
BLOCK = 128
N_SHARD = 4
ADAM_LR, ADAM_B1, ADAM_B2, ADAM_EPS, ADAM_WD, ADAM_STEP = 0.001, 0.9, 0.999, 1e-08, 0.01, 10
V7X_VMEM_BYTES = 64 * 1024 * 1024
VMEM_LIMIT = V7X_VMEM_BYTES - 8 * 1024 * 1024
MESH = pl.DeviceIdType.MESH
NEG = -1e30
SB_BQ, SB_BK = 512, 256
CUMSUM_PASSES = 2


def _fit(dim, pref):
    if dim <= pref:
        return dim
    b = (pref // 128) * 128
    while dim % b:
        b -= 128
    return b


def _params(sem=None):
    return pltpu.CompilerParams(dimension_semantics=sem, vmem_limit_bytes=VMEM_LIMIT)


def _dot(a, b, dims=(((1,), (0,)), ((), ()))):
    return lax.dot_general(a, b, dims, preferred_element_type=F32)


NT = (((1,), (1,)), ((), ()))
TN = (((0,), (0,)), ((), ()))


def _split_dot(x, u, passes):
    out = None
    r = x
    for p in range(passes):
        hi = r.astype(MXU_DTYPE)
        part = _dot(hi, u)
        out = part if out is None else out + part
        if p + 1 < passes:
            r = r - hi.astype(F32)
    return out


def _matmul(name, a, b, *, mode, out_dtypes, bm=1024, bn=1024, bk=2048, b_cshard=False, b_off=0, n_out=None,
            extras=(), epilogue=None, out_cshard=False):
    if mode == "tn":
        K, M = a.shape
        N = b.shape[1]
    else:
        M, K = a.shape
        if mode == "nn":
            N = n_out if n_out is not None else (N_SHARD * b.shape[2] if b_cshard else b.shape[1])
        else:
            N = b.shape[1] if b_cshard else b.shape[0]
    bm, bn, bk = _fit(M, bm), _fit(N, bn), _fit(K, bk)
    nk = K // bk
    grid = (M // bm, N // bn, nk)

    if mode == "tn":
        a_spec = pl.BlockSpec((bk, bm), lambda i, j, k: (k, i))
        b_spec = pl.BlockSpec((bk, bn), lambda i, j, k: (k, j))
        dims = TN
    elif mode == "nn":
        a_spec = pl.BlockSpec((bm, bk), lambda i, j, k: (i, k))
        if b_cshard:
            cb = b.shape[2] // bn
            b_spec = pl.BlockSpec((None, bk, bn), lambda i, j, k: ((j + b_off) // cb, k, (j + b_off) % cb))
        else:
            b_spec = pl.BlockSpec((bk, bn), lambda i, j, k: (k, j + b_off))
        dims = (((1,), (0,)), ((), ()))
    else:
        a_spec = pl.BlockSpec((bm, bk), lambda i, j, k: (i, k))
        if b_cshard:
            cb = b.shape[2] // bk
            b_spec = pl.BlockSpec((None, bn, bk), lambda i, j, k: (k // cb, j, k % cb))
        else:
            b_spec = pl.BlockSpec((bn, bk), lambda i, j, k: (j, k))
        dims = NT

    ex_arrays = [e[0] for e in extras]
    ex_specs = [pl.BlockSpec((bm, bn), functools.partial(lambda i, j, k, off: (i, j + off), off=e[1])) for e in extras]
    if out_cshard:
        cbo = (N // N_SHARD) // bn
        out_shape = [jax.ShapeDtypeStruct((N_SHARD, M, N // N_SHARD), dt) for dt in out_dtypes]
        out_specs = [pl.BlockSpec((None, bm, bn), lambda i, j, k: (j // cbo, i, j % cbo)) for _ in out_dtypes]
    else:
        out_shape = [jax.ShapeDtypeStruct((M, N), dt) for dt in out_dtypes]
        out_specs = [pl.BlockSpec((bm, bn), lambda i, j, k: (i, j)) for _ in out_dtypes]
    ne, no = len(extras), len(out_dtypes)

    def kern(*refs):
        a_ref, b_ref = refs[0], refs[1]
        ex_refs = refs[2:2 + ne]
        o_refs = refs[2 + ne:2 + ne + no]
        part = _dot(a_ref[...].astype(MXU_DTYPE), b_ref[...].astype(MXU_DTYPE), dims)

        def finish(acc):
            vals = (acc,) if epilogue is None else epilogue(acc, *[r[...] for r in ex_refs])
            for r, v in zip(o_refs, vals):
                r[...] = v.astype(r.dtype)

        if nk == 1:
            finish(part)
        else:
            acc_ref = refs[2 + ne + no]
            k = pl.program_id(2)

            @pl.when(k == 0)
            def _():
                acc_ref[...] = part

            @pl.when(k > 0)
            def _():
                acc_ref[...] += part

            @pl.when(k == nk - 1)
            def _():
                finish(acc_ref[...])

    outs = pl.pallas_call(
        kern, name=name, grid=grid,
        in_specs=[a_spec, b_spec] + ex_specs, out_specs=out_specs, out_shape=out_shape,
        scratch_shapes=[pltpu.VMEM((bm, bn), F32)] if nk > 1 else [],
        compiler_params=_params(("parallel", "parallel", "arbitrary")),
    )(a, b, *ex_arrays)
    return outs


def _row(tr, w, coff=0):
    return pl.BlockSpec((tr, w), lambda i: (i, coff))


def _vec(w):
    return pl.BlockSpec((1, w), lambda i: (0, 0))


def _rows_call(name, body, n_rows, tr, ins, outs, accs=()):
    n_in, n_out = len(ins), len(outs)

    def kern(*refs):
        acc_refs = refs[n_in + n_out:]
        if acc_refs:
            @pl.when(pl.program_id(0) == 0)
            def _():
                for r in acc_refs:
                    r[...] = jnp.zeros_like(r)
        body(*refs)

    out_shape = [jax.ShapeDtypeStruct((n_rows, w), dt) for w, dt in outs] + [jax.ShapeDtypeStruct((1, w), F32) for w in accs]
    out_specs = [_row(tr, w) for w, _ in outs] + [_vec(w) for w in accs]
    return pl.pallas_call(
        kern, name=name, grid=(n_rows // tr,),
        in_specs=[s for _, s in ins], out_specs=out_specs, out_shape=out_shape,
        compiler_params=_params(("arbitrary",)),
    )(*[a for a, _ in ins])


def _rmsnorm_fwd(name, x, g):
    S, D = x.shape
    tr = _fit(S, 256)

    def body(x_ref, g_ref, h_ref):
        xv = x_ref[...]
        r = lax.rsqrt(jnp.mean(xv * xv, axis=1, keepdims=True) + EPS)
        h_ref[...] = ((xv * r) * g_ref[...]).astype(h_ref.dtype)

    return _rows_call(name, body, S, tr, [(x, _row(tr, D)), (g, _vec(D))], [(D, MXU_DTYPE)])[0]


def _rmsnorm_bwd(name, dh, x, g, resid, with_loss):
    S, D = x.shape
    tr = _fit(S, 256)

    def body(dh_ref, x_ref, g_ref, res_ref, dx_ref, dg_ref, *loss_ref):
        xv = x_ref[...]
        r = lax.rsqrt(jnp.mean(xv * xv, axis=1, keepdims=True) + EPS)
        dhv = dh_ref[...]
        u = dhv * g_ref[...]
        xr = xv * r
        dx = r * u - xr * (r * r) * jnp.mean(xv * u, axis=1, keepdims=True)
        resv = res_ref[...]
        dx_ref[...] = resv + dx
        dg_ref[...] += jnp.sum(dhv * xr, axis=0, keepdims=True)
        if with_loss:
            loss_ref[0][...] += (0.5 * D) * jnp.sum(resv * resv)

    outs = _rows_call(name, body, S, tr, [(dh, _row(tr, D)), (x, _row(tr, D)), (g, _vec(D)), (resid, _row(tr, D))],
                      [(D, F32)], accs=(D, 128) if with_loss else (D,))
    return outs


def _rope_tables(S):
    half = ROT_DIM // 2
    pos = jnp.arange(S, dtype=F32)
    inv = ROPE_THETA ** (-jnp.arange(0, ROT_DIM, 2, dtype=F32) / ROT_DIM)
    ang = pos[:, None] * inv[None, :]
    cos, sin = jnp.cos(ang), jnp.sin(ang)
    pad = HEAD_DIM - ROT_DIM
    ctab = jnp.concatenate([cos, cos, jnp.ones((S, pad), F32)], axis=1)
    atab = jnp.concatenate([-sin, jnp.zeros((S, pad + half), F32)], axis=1)
    btab = jnp.concatenate([jnp.zeros((S, half), F32), sin, jnp.zeros((S, pad), F32)], axis=1)
    return ctab, atab, btab


def _qknorm_fwd(qk, qn, kn, tabs, HW):
    S = qk.shape[0]
    tr = _fit(S, 256)
    half = ROT_DIM // 2

    def body(qk_ref, qn_ref, kn_ref, c_ref, a_ref, b_ref, q_out, k_out):
        ct, at, bt = c_ref[...], a_ref[...], b_ref[...]
        for part, (g_ref, o_ref) in enumerate(((qn_ref, q_out), (kn_ref, k_out))):
            gv = g_ref[...]
            for h in range(HW // HEAD_DIM):
                xh = qk_ref[:, part * HW + h * HEAD_DIM: part * HW + (h + 1) * HEAD_DIM]
                r = lax.rsqrt(jnp.mean(xh * xh, axis=1, keepdims=True) + EPS)
                y = (xh * r) * gv
                o = y * ct + pltpu.roll(y, HEAD_DIM - half, 1) * at + pltpu.roll(y, half, 1) * bt
                o_ref[:, h * HEAD_DIM:(h + 1) * HEAD_DIM] = o.astype(o_ref.dtype)

    ins = [(qk, _row(tr, 2 * HW)), (qn, _vec(HEAD_DIM)), (kn, _vec(HEAD_DIM))] + [(t, _row(tr, HEAD_DIM)) for t in tabs]
    return _rows_call("qknorm_fwd", body, S, tr, ins, [(HW, MXU_DTYPE), (HW, MXU_DTYPE)])


def _shift_spec(tr, w, shift, nblk):
    return pl.BlockSpec((tr, w), lambda i: (jnp.minimum(i + shift, nblk - 1), 0))


def _qknorm_bwd(qk, qn, kn, tabs, dq_parts, dk_cur, dk_prev, HW):
    S = qk.shape[0]
    tr = BLOCK
    nblk = S // tr
    half = ROT_DIM // 2
    nd = len(DILATIONS)

    def body(*refs):
        qk_ref, qn_ref, kn_ref, c_ref, a_ref, b_ref = refs[:6]
        dq_refs = refs[6:6 + nd]
        dkc_refs = refs[6 + nd:6 + 2 * nd]
        dkp_refs = refs[6 + 2 * nd:6 + 3 * nd]
        d_out, dqn_ref, dkn_ref = refs[6 + 3 * nd:]
        i = pl.program_id(0)
        ct, at, bt = c_ref[...], a_ref[...], b_ref[...]
        live = [(i + d < nblk).astype(F32) for d in DILATIONS]
        for part, (g_ref, dg_ref) in enumerate(((qn_ref, dqn_ref), (kn_ref, dkn_ref))):
            gv = g_ref[...]
            dg = jnp.zeros((1, HEAD_DIM), F32)
            for h in range(HW // HEAD_DIM):
                hs = slice(h * HEAD_DIM, (h + 1) * HEAD_DIM)
                if part == 0:
                    do = dq_refs[0][:, hs] + dq_refs[1][:, hs] + dq_refs[2][:, hs]
                else:
                    do = dkc_refs[0][:, hs] + dkc_refs[1][:, hs] + dkc_refs[2][:, hs]
                    for n in range(nd):
                        do = do + dkp_refs[n][:, hs] * live[n]
                dy = do * ct + pltpu.roll(do * at, half, 1) + pltpu.roll(do * bt, HEAD_DIM - half, 1)
                xh = qk_ref[:, part * HW + h * HEAD_DIM: part * HW + (h + 1) * HEAD_DIM]
                r = lax.rsqrt(jnp.mean(xh * xh, axis=1, keepdims=True) + EPS)
                xr = xh * r
                u = dy * gv
                dx = r * u - xr * (r * r) * jnp.mean(xh * u, axis=1, keepdims=True)
                d_out[:, part * HW + h * HEAD_DIM: part * HW + (h + 1) * HEAD_DIM] = dx.astype(d_out.dtype)
                dg = dg + jnp.sum(dy * xr, axis=0, keepdims=True)
            dg_ref[...] += dg

    ins = [(qk, _row(tr, 2 * HW)), (qn, _vec(HEAD_DIM)), (kn, _vec(HEAD_DIM))] + [(t, _row(tr, HEAD_DIM)) for t in tabs]
    ins += [(a, _row(tr, HW)) for a in dq_parts] + [(a, _row(tr, HW)) for a in dk_cur]
    ins += [(a, _shift_spec(tr, HW, d, nblk)) for a, d in zip(dk_prev, DILATIONS)]
    return _rows_call("qknorm_bwd", body, S, tr, ins, [(2 * HW, MXU_DTYPE)], accs=(HEAD_DIM, HEAD_DIM))


def _dv_sum(dv_cur, dv_prev, HW):
    S = dv_cur[0].shape[0]
    tr = BLOCK
    nblk = S // tr
    nd = len(DILATIONS)

    def body(*refs):
        i = pl.program_id(0)
        out = refs[2 * nd]
        acc = refs[0][...] + refs[1][...] + refs[2][...]
        for n, d in enumerate(DILATIONS):
            acc = acc + refs[nd + n][...] * (i + d < nblk).astype(F32)
        out[...] = acc.astype(out.dtype)

    ins = [(a, _row(tr, HW)) for a in dv_cur] + [(a, _shift_spec(tr, HW, d, nblk)) for a, d in zip(dv_prev, DILATIONS)]
    return _rows_call("dilated_dv_sum", body, S, tr, ins, [(HW, MXU_DTYPE)])[0]


def _dil_geometry(d, HW):
    H = HW // HEAD_DIM
    hb = min(H, max(1, 8 // d))
    tb, w = BLOCK * d, hb * HEAD_DIM
    cur = pl.BlockSpec((tb, w), lambda n, g: (n, g))
    prev = pl.BlockSpec((tb, w), lambda n, g: (jnp.maximum(n - 1, 0), g))
    units = [(hh, r) for hh in range(hb) for r in range(d)]
    return H // hb, hb, tb, cur, prev, units


def _dil_mask(n):
    qi = lax.broadcasted_iota(jnp.int32, (BLOCK, 2 * BLOCK), 0)
    ki = lax.broadcasted_iota(jnp.int32, (BLOCK, 2 * BLOCK), 1)
    return (ki >= qi) & (ki <= qi + BLOCK) & ((ki >= BLOCK) | (n > 0))


def _dil_stage(ref, buf, row0=0):
    for hh in range(buf.shape[0]):
        buf[hh, row0:row0 + ref.shape[0], :] = ref[:, hh * HEAD_DIM:(hh + 1) * HEAD_DIM].astype(F32)


def _dil_unstage(buf, ref):
    for hh in range(buf.shape[0]):
        ref[:, hh * HEAD_DIM:(hh + 1) * HEAD_DIM] = buf[hh]


def _dil_rows(d, r, size):
    return pl.ds(0, size) if d == 1 else pl.ds(r, size, stride=d)


def _dil_operands(d, tb, units, q_ref, kc_ref, kp_ref, vc_ref, vp_ref, qs, kf, vf):
    _dil_stage(q_ref, qs)
    _dil_stage(kp_ref, kf)
    _dil_stage(kc_ref, kf, tb)
    _dil_stage(vp_ref, vf)
    _dil_stage(vc_ref, vf, tb)
    qu = [qs[hh, _dil_rows(d, r, BLOCK), :].astype(MXU_DTYPE) for hh, r in units]
    ku = [kf[hh, _dil_rows(d, r, 2 * BLOCK), :].astype(MXU_DTYPE) for hh, r in units]
    vu = [vf[hh, _dil_rows(d, r, 2 * BLOCK), :].astype(MXU_DTYPE) for hh, r in units]
    return qu, ku, vu


def _dilated_fwd(q, k, mid, d, HW):
    S = q.shape[0]
    scale = HEAD_DIM ** -0.5
    ng, hb, tb, cur, prev, units = _dil_geometry(d, HW)

    def kern(q_ref, kc_ref, kp_ref, vc_ref, vp_ref, o_ref, l_ref, qs, kf, vf, os_, ls):
        mask = _dil_mask(pl.program_id(0))
        qu, ku, vu = _dil_operands(d, tb, units, q_ref, kc_ref, kp_ref, vc_ref, vp_ref, qs, kf, vf)
        sc = [jnp.where(mask, _dot(a, b, NT) * scale, NEG) for a, b in zip(qu, ku)]
        mx = [jnp.max(t, axis=1, keepdims=True) for t in sc]
        ex = [jnp.exp(t - m) for t, m in zip(sc, mx)]
        den = [jnp.sum(t, axis=1, keepdims=True) for t in ex]
        out = [_dot(t.astype(MXU_DTYPE), v) / dn for t, v, dn in zip(ex, vu, den)]
        for (hh, r), o, m, dn in zip(units, out, mx, den):
            os_[hh, _dil_rows(d, r, BLOCK), :] = o
            ls[hh, _dil_rows(d, r, BLOCK), :] = jnp.broadcast_to(m + jnp.log(dn), (BLOCK, HEAD_DIM))
        _dil_unstage(os_, o_ref)
        _dil_unstage(ls, l_ref)

    return pl.pallas_call(
        kern, name=f"dilated_fwd_d{d}", grid=(S // tb, ng),
        in_specs=[cur, cur, prev, cur, prev],
        out_specs=[cur, cur],
        out_shape=[jax.ShapeDtypeStruct((S, HW), F32)] * 2,
        scratch_shapes=[pltpu.VMEM((hb, tb, HEAD_DIM), F32)] + [pltpu.VMEM((hb, 2 * tb, HEAD_DIM), F32)] * 2
        + [pltpu.VMEM((hb, tb, HEAD_DIM), F32)] * 2,
        compiler_params=_params(("arbitrary", "arbitrary")),
    )(q, k, k, mid, mid)


def _dilated_combine(os_, lses, HW):
    S = os_[0].shape[0]
    tr = _fit(S, 256)

    def body(o0, o1, o2, l0, l1, l2, ya_ref, lse_ref):
        a, b, c = l0[...], l1[...], l2[...]
        mx = jnp.maximum(jnp.maximum(a, b), c)
        ea, eb, ec = jnp.exp(a - mx), jnp.exp(b - mx), jnp.exp(c - mx)
        tot = ea + eb + ec
        ya_ref[...] = (ea * o0[...] + eb * o1[...] + ec * o2[...]) / tot
        lse_ref[...] = mx + jnp.log(tot)

    ins = [(a, _row(tr, HW)) for a in list(os_) + list(lses)]
    return _rows_call("dilated_combine", body, S, tr, ins, [(HW, F32), (HW, F32)])


def _dilated_bwd(q, k, mid, dya, ya, lse, d, HW):
    S = q.shape[0]
    scale = HEAD_DIM ** -0.5
    ng, hb, tb, cur, prev, units = _dil_geometry(d, HW)

    def kern(q_ref, kc_ref, kp_ref, vc_ref, vp_ref, dy_ref, y_ref, l_ref, dq_ref, dkc_ref, dkp_ref, dvc_ref, dvp_ref,
             qs, kf, vf, dys, ys, ls, dqs, dkcs, dkps, dvcs, dvps):
        mask = _dil_mask(pl.program_id(0))
        qu, ku, vu = _dil_operands(d, tb, units, q_ref, kc_ref, kp_ref, vc_ref, vp_ref, qs, kf, vf)
        _dil_stage(dy_ref, dys)
        _dil_stage(y_ref, ys)
        _dil_stage(l_ref, ls)
        dy = [dys[hh, _dil_rows(d, r, BLOCK), :] for hh, r in units]
        lt = [ls[hh, _dil_rows(d, r, BLOCK), :][:, 0:1] for hh, r in units]
        delta = [jnp.sum(t * ys[hh, _dil_rows(d, r, BLOCK), :], axis=1, keepdims=True) for t, (hh, r) in zip(dy, units)]
        dyb = [t.astype(MXU_DTYPE) for t in dy]
        p = [jnp.where(mask, jnp.exp(_dot(a, b, NT) * scale - l), 0.0) for a, b, l in zip(qu, ku, lt)]
        ds = [(t * (_dot(g, v, NT) - dl) * scale).astype(MXU_DTYPE) for t, g, v, dl in zip(p, dyb, vu, delta)]
        dq = [_dot(t, b) for t, b in zip(ds, ku)]
        dk = [_dot(t, a, TN) for t, a in zip(ds, qu)]
        dv = [_dot(t.astype(MXU_DTYPE), g, TN) for t, g in zip(p, dyb)]
        for (hh, r), tq, tk, tv in zip(units, dq, dk, dv):
            at = _dil_rows(d, r, BLOCK)
            dqs[hh, at, :] = tq
            dkps[hh, at, :] = tk[0:BLOCK]
            dkcs[hh, at, :] = tk[BLOCK:2 * BLOCK]
            dvps[hh, at, :] = tv[0:BLOCK]
            dvcs[hh, at, :] = tv[BLOCK:2 * BLOCK]
        for buf, ref in ((dqs, dq_ref), (dkcs, dkc_ref), (dkps, dkp_ref), (dvcs, dvc_ref), (dvps, dvp_ref)):
            _dil_unstage(buf, ref)

    return pl.pallas_call(
        kern, name=f"dilated_bwd_d{d}", grid=(S // tb, ng),
        in_specs=[cur, cur, prev, cur, prev, cur, cur, cur],
        out_specs=[cur] * 5,
        out_shape=[jax.ShapeDtypeStruct((S, HW), F32)] * 5,
        scratch_shapes=[pltpu.VMEM((hb, tb, HEAD_DIM), F32)] + [pltpu.VMEM((hb, 2 * tb, HEAD_DIM), F32)] * 2
        + [pltpu.VMEM((hb, tb, HEAD_DIM), F32)] * 8,
        compiler_params=_params(("arbitrary", "arbitrary")),
    )(q, k, k, mid, mid, dya, ya, lse)


def _softplus_parts(z):
    sp = jnp.maximum(z, 0.0) + jnp.log1p(jnp.exp(-jnp.abs(z)))
    return -sp, z - sp


def _sb_specs(S, H):
    q_spec = pl.BlockSpec((SB_BQ, HEAD_DIM), lambda h, i: (i, H + h))
    k_spec = pl.BlockSpec((S, HEAD_DIM), lambda h, i: (0, 2 * H + h))
    v_spec = pl.BlockSpec((S, HEAD_DIM), lambda h, i: (0, 3 * H + h))
    o_spec = pl.BlockSpec((SB_BQ, HEAD_DIM), lambda h, i: (i, h))
    return q_spec, k_spec, v_spec, o_spec


def _sb_masks():
    row = lax.broadcasted_iota(jnp.int32, (SB_BQ, SB_BQ), 0)
    col = lax.broadcasted_iota(jnp.int32, (SB_BQ, SB_BQ), 1)
    tri_r = lax.broadcasted_iota(jnp.int32, (SB_BK, SB_BK), 0)
    tri_c = lax.broadcasted_iota(jnp.int32, (SB_BK, SB_BK), 1)
    return tri_r, tri_c, col < row


def _sb_fwd(mid, HW):
    S = mid.shape[0]
    H = HW // HEAD_DIM
    BQ, CH = SB_BQ, SB_BK
    NC = BQ // CH
    scale = HEAD_DIM ** -0.5
    q_spec, k_spec, v_spec, o_spec = _sb_specs(S, H)

    def kern(q_ref, k_ref, v_ref, o_ref, t_ref):
        i = pl.program_id(1)
        q = q_ref[...]
        tri_r, tri_c, causal = _sb_masks()
        upper = (tri_r > tri_c).astype(MXU_DTYPE)

        def block(j, run, acc, masked):
            ks = pl.multiple_of(j * BQ, BQ)
            z = _dot(q, k_ref[pl.ds(ks, BQ), :], NT) * scale
            m, l = _softplus_parts(z)
            if masked:
                m = jnp.where(causal, m, 0.0)
            parts = []
            for c in reversed(range(NC)):
                mc = m[:, c * CH:(c + 1) * CH]
                parts.append(l[:, c * CH:(c + 1) * CH] + (_split_dot(mc, upper, CUMSUM_PASSES) + run))
                run = run + jnp.sum(mc, axis=1, keepdims=True)
            a = jnp.exp(jnp.concatenate(parts[::-1], axis=1))
            if masked:
                a = jnp.where(causal, a, 0.0)
            return run, acc + _dot(a.astype(MXU_DTYPE), v_ref[pl.ds(ks, BQ), :])

        run, acc = block(i, jnp.zeros((BQ, 1), F32), jnp.zeros((BQ, HEAD_DIM), F32), True)
        run, acc = lax.fori_loop(0, i, lambda t, carry: block(i - 1 - t, carry[0], carry[1], False), (run, acc))
        o_ref[...] = acc
        t_ref[...] = jnp.broadcast_to(run, (BQ, HEAD_DIM))

    return pl.pallas_call(
        kern, name="stickbreak_fwd", grid=(H, S // BQ),
        in_specs=[q_spec, k_spec, v_spec], out_specs=[o_spec, o_spec],
        out_shape=[jax.ShapeDtypeStruct((S, HW), F32)] * 2,
        compiler_params=_params(("parallel", "arbitrary")),
    )(mid, mid, mid)


def _sb_bwd(mid, dyb, tot, HW):
    S = mid.shape[0]
    H = HW // HEAD_DIM
    BQ, CH = SB_BQ, SB_BK
    NC = BQ // CH
    scale = HEAD_DIM ** -0.5
    q_spec, k_spec, v_spec, o_spec = _sb_specs(S, H)
    full = pl.BlockSpec((S, HEAD_DIM), lambda h, i: (0, h))

    def kern(q_ref, k_ref, v_ref, do_ref, t_ref, dq_ref, dk_ref, dv_ref):
        i = pl.program_id(1)

        @pl.when(i == 0)
        def _():
            dk_ref[...] = jnp.zeros_like(dk_ref)
            dv_ref[...] = jnp.zeros_like(dv_ref)

        q = q_ref[...]
        do = do_ref[...].astype(MXU_DTYPE)
        total = t_ref[:, 0:1]
        tri_r, tri_c, causal = _sb_masks()
        incl = (tri_r <= tri_c).astype(MXU_DTYPE)
        excl = (tri_r < tri_c).astype(MXU_DTYPE)

        def block(j, mrun, prun, dq, masked):
            ks = pl.multiple_of(j * BQ, BQ)
            k = k_ref[pl.ds(ks, BQ), :]
            z = _dot(q, k, NT) * scale
            m, l = _softplus_parts(z)
            if masked:
                m = jnp.where(causal, m, 0.0)
            parts = []
            for c in range(NC):
                mc = m[:, c * CH:(c + 1) * CH]
                parts.append(l[:, c * CH:(c + 1) * CH] + (total - mrun - _split_dot(mc, incl, CUMSUM_PASSES)))
                mrun = mrun + jnp.sum(mc, axis=1, keepdims=True)
            a = jnp.exp(jnp.concatenate(parts, axis=1))
            if masked:
                a = jnp.where(causal, a, 0.0)
            p = a * _dot(do, v_ref[pl.ds(ks, BQ), :], NT)
            parts = []
            for c in range(NC):
                pc = p[:, c * CH:(c + 1) * CH]
                parts.append(_split_dot(pc, excl, CUMSUM_PASSES) + prun)
                prun = prun + jnp.sum(pc, axis=1, keepdims=True)
            before = jnp.concatenate(parts, axis=1)
            sig = jnp.exp(l)
            dz = (p * (1.0 - sig) - sig * before) * scale
            if masked:
                dz = jnp.where(causal, dz, 0.0)
            dzb = dz.astype(MXU_DTYPE)
            dk_ref[pl.ds(ks, BQ), :] += _dot(dzb, q, TN)
            dv_ref[pl.ds(ks, BQ), :] += _dot(a.astype(MXU_DTYPE), do, TN)
            return mrun, prun, dq + _dot(dzb, k)

        zero = jnp.zeros((BQ, 1), F32)
        carry = lax.fori_loop(0, i, lambda j, carry: block(j, *carry, False), (zero, zero, jnp.zeros((BQ, HEAD_DIM), F32)))
        dq_ref[...] = block(i, *carry, True)[2]

    return pl.pallas_call(
        kern, name="stickbreak_bwd", grid=(H, S // BQ),
        in_specs=[q_spec, k_spec, v_spec, o_spec, o_spec], out_specs=[o_spec, full, full],
        out_shape=[jax.ShapeDtypeStruct((S, HW), F32)] * 3,
        compiler_params=_params(("arbitrary", "arbitrary")),
    )(mid, mid, mid, dyb, tot)


ANY = pl.BlockSpec(memory_space=pl.ANY)


def _place():
    x, y, c = lax.axis_index("x"), lax.axis_index("y"), lax.axis_index("c")
    chips = [(1 - x, y), (x, 1 - y), (1 - x, 1 - y)]
    return x, y, c, chips


def _half(ref, shard, hc, rh):
    return ref.at[shard, pl.ds(pl.multiple_of(hc * rh, 8), rh), :]


def _cast_place(name, w, s_idx):
    R, C = w.shape
    tr = _fit(R, 256)

    def kern(s_ref, w_ref, o_ref):
        o_ref[...] = w_ref[...].astype(o_ref.dtype)

    return pl.pallas_call(
        kern, name=name,
        grid_spec=pltpu.PrefetchScalarGridSpec(
            num_scalar_prefetch=1, grid=(R // tr,),
            in_specs=[pl.BlockSpec((tr, C), lambda r, s_ref: (r, 0))],
            out_specs=pl.BlockSpec((None, tr, C), lambda r, s_ref: (s_ref[0], r, 0))),
        out_shape=jax.ShapeDtypeStruct((N_SHARD, R, C), MXU_DTYPE),
        compiler_params=_params(("arbitrary",)),
    )(s_idx, w)


def _gather_weights(bufs):
    n = len(bufs)

    def body(*refs):
        outs = refs[n:2 * n]
        send, recv = refs[2 * n:]
        x, y, c, chips = _place()
        s = 2 * x + y
        sibling = (x, y, 1 - c)
        sent = []
        for i in range(n):
            rh = outs[i].shape[1] // 2
            mine = _half(outs[i], s, c, rh)
            for j, (px, py) in enumerate(chips):
                cp = pltpu.make_async_remote_copy(
                    src_ref=mine, dst_ref=mine,
                    send_sem=send.at[i, j], recv_sem=recv.at[i, j], device_id=(px, py, c), device_id_type=MESH)
                cp.start()
                sent.append(cp)
        for i in range(n):
            rh = outs[i].shape[1] // 2
            for j, (px, py) in enumerate(chips):
                landed = _half(outs[i], 2 * px + py, c, rh)
                pltpu.make_async_remote_copy(src_ref=landed, dst_ref=landed, send_sem=send.at[i, j], recv_sem=recv.at[i, j],
                                             device_id=(px, py, c), device_id_type=MESH).wait_recv()
                cp = pltpu.make_async_remote_copy(src_ref=landed, dst_ref=landed, send_sem=send.at[i, 3 + j],
                                                  recv_sem=recv.at[i, 3 + j], device_id=sibling, device_id_type=MESH)
                cp.start()
                sent.append(cp)
        for i in range(n):
            rh = outs[i].shape[1] // 2
            for j, (px, py) in enumerate(chips):
                passed = _half(outs[i], 2 * px + py, 1 - c, rh)
                pltpu.make_async_remote_copy(src_ref=passed, dst_ref=passed, send_sem=send.at[i, 3 + j],
                                             recv_sem=recv.at[i, 3 + j], device_id=sibling, device_id_type=MESH).wait_recv()
        for cp in sent:
            cp.wait_send()

    return pl.pallas_call(
        body, name="weights_all_gather",
        in_specs=[ANY] * n, out_specs=[ANY] * n,
        out_shape=[jax.ShapeDtypeStruct(b.shape, b.dtype) for b in bufs],
        input_output_aliases={i: i for i in range(n)},
        scratch_shapes=[pltpu.SemaphoreType.DMA((n, 6)), pltpu.SemaphoreType.DMA((n, 6))],
    )(*bufs)


def _swap_halves(grads):
    n = len(grads)

    def body(*refs):
        ins, outs = refs[:n], refs[n:2 * n]
        send, recv = refs[2 * n:]
        x, y, c, _ = _place()
        copies = []
        for i in range(n):
            rh = ins[i].shape[1] // 2
            cp = pltpu.make_async_remote_copy(
                src_ref=ins[i].at[:, pl.ds(pl.multiple_of((1 - c) * rh, 8), rh), :], dst_ref=outs[i],
                send_sem=send.at[i], recv_sem=recv.at[i], device_id=(x, y, 1 - c), device_id_type=MESH)
            cp.start()
            copies.append(cp)
        for cp in copies:
            cp.wait()

    return pl.pallas_call(
        body, name="grad_swap_halves",
        in_specs=[ANY] * n, out_specs=[ANY] * n,
        out_shape=[jax.ShapeDtypeStruct((N_SHARD, g.shape[1] // 2, g.shape[2]), F32) for g in grads],
        scratch_shapes=[pltpu.SemaphoreType.DMA((n,)), pltpu.SemaphoreType.DMA((n,))],
    )(*grads)


def _add_half(name, g, got, c_idx):
    _, R, C = g.shape
    rh = R // 2
    tr = _fit(rh, 256)
    nrb = rh // tr

    def kern(c_ref, g_ref, r_ref, o32_ref, o16_ref):
        v = g_ref[...] + r_ref[...]
        o32_ref[...] = v
        o16_ref[...] = v.astype(o16_ref.dtype)

    blk = lambda s, r, c_ref: (s, r, 0)
    return pl.pallas_call(
        kern, name=name,
        grid_spec=pltpu.PrefetchScalarGridSpec(
            num_scalar_prefetch=1, grid=(N_SHARD, nrb),
            in_specs=[pl.BlockSpec((None, tr, C), lambda s, r, c_ref: (s, c_ref[0] * nrb + r, 0)), pl.BlockSpec((None, tr, C), blk)],
            out_specs=[pl.BlockSpec((None, tr, C), blk)] * 2),
        out_shape=[jax.ShapeDtypeStruct((N_SHARD, rh, C), F32), jax.ShapeDtypeStruct((N_SHARD, rh, C), MXU_DTYPE)],
        compiler_params=_params(("arbitrary", "arbitrary")),
    )(c_idx, g, got)


def _scatter_partials(parts):
    n = len(parts)

    def body(*refs):
        ins, outs = refs[:n], refs[n:2 * n]
        send, recv = refs[2 * n:]
        x, y, c, chips = _place()
        copies = []
        for i in range(n):
            for j, (px, py) in enumerate(chips):
                cp = pltpu.make_async_remote_copy(
                    src_ref=ins[i].at[2 * px + py], dst_ref=outs[i].at[j],
                    send_sem=send.at[i, j], recv_sem=recv.at[i, j], device_id=(px, py, c), device_id_type=MESH)
                cp.start()
                copies.append(cp)
        for cp in copies:
            cp.wait()

    return pl.pallas_call(
        body, name="grad_scatter_partials",
        in_specs=[ANY] * n, out_specs=[ANY] * n,
        out_shape=[jax.ShapeDtypeStruct((3,) + p.shape[1:], p.dtype) for p in parts],
        scratch_shapes=[pltpu.SemaphoreType.DMA((n, 3)), pltpu.SemaphoreType.DMA((n, 3))],
    )(*parts)


def _add_partials(name, own, got, s_idx):
    _, rh, C = own.shape
    tr = _fit(rh, 256)

    def kern(s_ref, o_ref, a_ref, b_ref, c_ref, out_ref):
        out_ref[...] = ((o_ref[...] + a_ref[...].astype(F32)) + b_ref[...].astype(F32)) + c_ref[...].astype(F32)

    slot = lambda j: pl.BlockSpec((None, tr, C), lambda r, s_ref: (j, r, 0))
    return pl.pallas_call(
        kern, name=name,
        grid_spec=pltpu.PrefetchScalarGridSpec(
            num_scalar_prefetch=1, grid=(rh // tr,),
            in_specs=[pl.BlockSpec((None, tr, C), lambda r, s_ref: (s_ref[0], r, 0)), slot(0), slot(1), slot(2)],
            out_specs=pl.BlockSpec((tr, C), lambda r, s_ref: (r, 0))),
        out_shape=jax.ShapeDtypeStruct((rh, C), F32),
        compiler_params=_params(("arbitrary",)),
    )(s_idx, own, got, got, got)


def _swap_reduced(halves):
    n = len(halves)

    def body(*refs):
        ins, outs = refs[:n], refs[n:2 * n]
        send, recv = refs[2 * n:]
        x, y, c, _ = _place()
        copies = []
        for i in range(n):
            cp = pltpu.make_async_remote_copy(src_ref=ins[i], dst_ref=outs[i], send_sem=send.at[i], recv_sem=recv.at[i],
                                              device_id=(x, y, 1 - c), device_id_type=MESH)
            cp.start()
            copies.append(cp)
        for cp in copies:
            cp.wait()

    return pl.pallas_call(
        body, name="grad_swap_reduced",
        in_specs=[ANY] * n, out_specs=[ANY] * n,
        out_shape=[jax.ShapeDtypeStruct(h.shape, F32) for h in halves],
        scratch_shapes=[pltpu.SemaphoreType.DMA((n,)), pltpu.SemaphoreType.DMA((n,))],
    )(*halves)


def _all_reduce_small(v):
    rows, W = v.shape
    ndev = 8

    def body(v_ref, out_ref, buf, send, recv):
        x, y, c, _ = _place()
        me = 4 * x + 2 * y + c
        buf[me] = v_ref[...]
        copies = []
        for r in range(1, ndev):
            fx, fy, fc = (r >> 2) & 1, (r >> 1) & 1, r & 1
            peer = (x ^ fx, y ^ fy, c ^ fc)
            cp = pltpu.make_async_remote_copy(src_ref=v_ref, dst_ref=buf.at[me], send_sem=send.at[r - 1], recv_sem=recv.at[r - 1],
                                              device_id=peer, device_id_type=MESH)
            cp.start()
            copies.append(cp)
        for cp in copies:
            cp.wait()
        acc = buf[0]
        for k in range(1, ndev):
            acc = acc + buf[k]
        out_ref[...] = acc

    return pl.pallas_call(
        body, name="small_all_reduce",
        in_specs=[pl.BlockSpec(memory_space=pltpu.VMEM)], out_specs=pl.BlockSpec(memory_space=pltpu.VMEM),
        out_shape=jax.ShapeDtypeStruct((rows, W), F32),
        scratch_shapes=[pltpu.VMEM((ndev, rows, W), F32), pltpu.SemaphoreType.DMA((ndev - 1,)), pltpu.SemaphoreType.DMA((ndev - 1,))],
    )(v)


def _adamw_update(gv, w_ref, m_ref, v_ref, d_ref, nm_ref, nv_ref):
    nm = ADAM_B1 * m_ref[...] + (1.0 - ADAM_B1) * gv
    nv = ADAM_B2 * v_ref[...] + (1.0 - ADAM_B2) * (gv * gv)
    m_hat = nm / (1.0 - ADAM_B1 ** ADAM_STEP)
    v_hat = nv / (1.0 - ADAM_B2 ** ADAM_STEP)
    d_ref[...] = -ADAM_LR * (m_hat / (jnp.sqrt(v_hat) + ADAM_EPS) + ADAM_WD * w_ref[...])
    nm_ref[...] = nm
    nv_ref[...] = nv


def _adamw(name, w, g, m, v):
    R, C = w.shape
    tr = _fit(R, 256)

    def body(w_ref, g_ref, m_ref, v_ref, d_ref, nm_ref, nv_ref):
        _adamw_update(g_ref[...], w_ref, m_ref, v_ref, d_ref, nm_ref, nv_ref)

    return _rows_call(name, body, R, tr, [(a, _row(tr, C)) for a in (w, g, m, v)], [(C, F32)] * 3)


def _adamw_halves(name, w, mine, theirs, m, v, c_idx):
    R, C = w.shape
    rh = R // 2
    tr = _fit(rh, 256)
    nrb = rh // tr

    def kern(c_ref, w_ref, a_ref, b_ref, m_ref, v_ref, g_ref, d_ref, nm_ref, nv_ref):
        gv = jnp.where(pl.program_id(0) // nrb == c_ref[0], a_ref[...], b_ref[...])
        g_ref[...] = gv
        _adamw_update(gv, w_ref, m_ref, v_ref, d_ref, nm_ref, nv_ref)

    full = pl.BlockSpec((tr, C), lambda r, c_ref: (r, 0))
    pick = lambda own: pl.BlockSpec((tr, C), lambda r, c_ref: (jnp.where((r // nrb == c_ref[0]) == own, r % nrb, 0), 0))
    return pl.pallas_call(
        kern, name=name,
        grid_spec=pltpu.PrefetchScalarGridSpec(
            num_scalar_prefetch=1, grid=(R // tr,),
            in_specs=[full, pick(True), pick(False), full, full], out_specs=[full] * 4),
        out_shape=[jax.ShapeDtypeStruct((R, C), F32)] * 4,
        compiler_params=_params(("arbitrary",)),
    )(c_idx, w, mine, theirs, m, v)


def _sigmoid(z):
    return 1.0 / (1.0 + jnp.exp(-z))


def kernel(x, p, g_mix, w_in, qn_gain, kn_gain, w_branch_a, w_branch_b, w_out, g_mlp, w_up, w_down, g_ple, w_ple_gate, w_ple_proj, loss_target, m_g_mix, m_w_in, m_qn_gain, m_kn_gain, m_w_branch_a, m_w_branch_b, m_w_out, m_g_mlp, m_w_up, m_w_down, m_g_ple, m_w_ple_gate, m_w_ple_proj, v_g_mix, v_w_in, v_qn_gain, v_kn_gain, v_w_branch_a, v_w_branch_b, v_w_out, v_g_mlp, v_w_up, v_w_down, v_g_ple, v_w_ple_gate, v_w_ple_proj):
    S, D = x.shape[1], x.shape[2]
    HW = w_branch_a.shape[1]
    x2d, tgt, p2d = x.reshape(S, D), loss_target.reshape(S, D), p.reshape(S, p.shape[-1])
    big = {"w_in": w_in, "w_branch_a": w_branch_a, "w_branch_b": w_branch_b, "w_out": w_out, "w_up": w_up,
           "w_down": w_down, "w_ple_gate": w_ple_gate, "w_ple_proj": w_ple_proj}
    moments = {"w_in": (m_w_in, v_w_in), "w_branch_a": (m_w_branch_a, v_w_branch_a), "w_branch_b": (m_w_branch_b, v_w_branch_b),
               "w_out": (m_w_out, v_w_out), "w_up": (m_w_up, v_w_up), "w_down": (m_w_down, v_w_down),
               "w_ple_gate": (m_w_ple_gate, v_w_ple_gate), "w_ple_proj": (m_w_ple_proj, v_w_ple_proj)}
    names = list(big)
    col_sharded = {"w_in", "w_branch_a", "w_branch_b", "w_up", "w_ple_proj"}
    shard2d = {k: w.reshape(w.shape[1], w.shape[2]) for k, w in big.items()}

    c_idx = lax.axis_index("c").astype(jnp.int32).reshape(1)
    s_idx = (2 * lax.axis_index("x") + lax.axis_index("y")).astype(jnp.int32).reshape(1)
    gathered = dict(zip(names, _gather_weights([_cast_place(f"cast_{k}", shard2d[k], s_idx) for k in names])))
    W = {k: (gathered[k] if k in col_sharded else gathered[k].reshape(-1, gathered[k].shape[2])) for k in names}
    cin = W["w_in"].shape[2]
    bn_in = _fit(cin, 512)
    while (2 * HW) % bn_in:
        bn_in -= 128

    h = _rmsnorm_fwd("rmsnorm_mix", x2d, g_mix)
    (qk,) = _matmul("proj_qk", h, W["w_in"], mode="nn", out_dtypes=[F32], b_cshard=True, b_off=0, n_out=2 * HW, bn=bn_in, bk=D)
    (mid,) = _matmul("proj_mid", h, W["w_in"], mode="nn", out_dtypes=[MXU_DTYPE], b_cshard=True, b_off=2 * HW // bn_in,
                     n_out=4 * HW, bn=bn_in, bk=D)
    (sg,) = _matmul("proj_gates", h, W["w_in"], mode="nn", out_dtypes=[F32], b_cshard=True, b_off=6 * HW // bn_in,
                    n_out=2 * D, bn=bn_in, bk=D, epilogue=lambda acc: (_sigmoid(acc),))
    tabs = _rope_tables(S)
    qa, ka = _qknorm_fwd(qk, qn_gain, kn_gain, tabs, HW)
    dil = [_dilated_fwd(qa, ka, mid, d, HW) for d in DILATIONS]
    ya, lse = _dilated_combine([o for o, _ in dil], [l for _, l in dil], HW)
    yb, sb_tot = _sb_fwd(mid, HW)

    gate_blocks = D // _fit(D, 1024)
    (ua,) = _matmul("branch_a", ya, W["w_branch_a"], mode="nn", out_dtypes=[F32], b_cshard=True, bn=_fit(W["w_branch_a"].shape[2], 1024))
    bn_b = _fit(W["w_branch_b"].shape[2], 1024)
    ub, merged = _matmul("branch_b_merge", yb, W["w_branch_b"], mode="nn", out_dtypes=[F32, MXU_DTYPE], b_cshard=True, bn=bn_b,
                         extras=[(sg, 0), (sg, D // bn_b), (ua, 0)],
                         epilogue=lambda acc, sga, sgb, uav: (acc, sga * uav + sgb * acc))
    (x1,) = _matmul("out_proj", merged, W["w_out"], mode="nn", out_dtypes=[F32], extras=[(x2d, 0)], epilogue=lambda acc, xv: (xv + acc,))
    hm = _rmsnorm_fwd("rmsnorm_mlp", x1, g_mlp)

    def up_epilogue(acc):
        r = jnp.maximum(acc, 0.0)
        return r * r, r

    act, rup = _matmul("mlp_up", hm, W["w_up"], mode="nn", out_dtypes=[MXU_DTYPE, MXU_DTYPE], b_cshard=True,
                       bn=_fit(W["w_up"].shape[2], 1024), bk=D, epilogue=up_epilogue)
    (x2,) = _matmul("mlp_down", act, W["w_down"], mode="nn", out_dtypes=[F32], extras=[(x1, 0)], epilogue=lambda acc, xv: (xv + acc,))
    hp = _rmsnorm_fwd("rmsnorm_ple", x2, g_ple)
    (pp,) = _matmul("ple_proj", p2d, W["w_ple_proj"], mode="nn", out_dtypes=[F32], b_cshard=True, bn=_fit(W["w_ple_proj"].shape[2], 1024))

    def ple_epilogue(acc, ppv, x2v, tv):
        s = _sigmoid(acc)
        dx3 = ((x2v + ppv * s) - tv) / D
        return dx3, dx3 * s, dx3 * ppv * (s * (1.0 - s))

    dx3, d_pp, d_gate = _matmul("ple_gate_loss", hp, W["w_ple_gate"], mode="nn", out_dtypes=[F32, MXU_DTYPE, MXU_DTYPE],
                                bm=512, extras=[(pp, 0), (x2, 0), (tgt, 0)], epilogue=ple_epilogue)

    G = {}
    (G["w_ple_proj"],) = _matmul("grad_w_ple_proj", p2d, d_pp, mode="tn", out_dtypes=[F32], out_cshard=True,
                                 bn=_fit(d_pp.shape[1] // N_SHARD, 1024))
    (G["w_ple_gate"],) = _matmul("grad_w_ple_gate", hp, d_gate, mode="tn", out_dtypes=[F32])
    (d_hp,) = _matmul("ple_gate_bwd", d_gate, W["w_ple_gate"], mode="nt", out_dtypes=[F32])
    dx2, g_g_ple, loss_part = _rmsnorm_bwd("rmsnorm_ple_bwd", d_hp, x2, g_ple, dx3, True)
    (G["w_down"],) = _matmul("grad_w_down", act, dx2, mode="tn", out_dtypes=[F32])
    (d_up,) = _matmul("mlp_down_bwd", dx2, W["w_down"], mode="nt", out_dtypes=[MXU_DTYPE], extras=[(rup, 0)],
                      epilogue=lambda acc, r: (acc * (2.0 * r.astype(F32)),))
    (G["w_up"],) = _matmul("grad_w_up", hm, d_up, mode="tn", out_dtypes=[F32], out_cshard=True, bn=_fit(d_up.shape[1] // N_SHARD, 1024))
    (d_hm,) = _matmul("mlp_up_bwd", d_up, W["w_up"], mode="nt", out_dtypes=[F32], b_cshard=True, bk=_fit(W["w_up"].shape[2], 2048))
    dx1, g_g_mlp = _rmsnorm_bwd("rmsnorm_mlp_bwd", d_hm, x1, g_mlp, dx2, False)
    (G["w_out"],) = _matmul("grad_w_out", merged, dx1, mode="tn", out_dtypes=[F32])

    def merge_bwd(acc, sga, sgb, uav, ubv):
        return acc * sga, acc * sgb, acc * uav * (sga * (1.0 - sga)), acc * ubv * (sgb * (1.0 - sgb))

    bn_m = _fit(D, 1024)
    d_ua, d_ub, d_ga, d_gb = _matmul("out_proj_bwd", dx1, W["w_out"], mode="nt", out_dtypes=[MXU_DTYPE] * 4, bm=512, bn=bn_m,
                                     extras=[(sg, 0), (sg, D // bn_m), (ua, 0), (ub, 0)], epilogue=merge_bwd)
    bn_br = _fit(D // N_SHARD, 1024)
    (G["w_branch_a"],) = _matmul("grad_w_branch_a", ya, d_ua, mode="tn", out_dtypes=[F32], out_cshard=True, bn=bn_br)
    (G["w_branch_b"],) = _matmul("grad_w_branch_b", yb, d_ub, mode="tn", out_dtypes=[F32], out_cshard=True, bn=bn_br)
    (d_ya,) = _matmul("branch_a_bwd", d_ua, W["w_branch_a"], mode="nt", out_dtypes=[F32], b_cshard=True, bk=bn_br)
    (d_yb,) = _matmul("branch_b_bwd", d_ub, W["w_branch_b"], mode="nt", out_dtypes=[F32], b_cshard=True, bk=bn_br)

    dqb, dkb, dvb = _sb_bwd(mid, d_yb, sb_tot, HW)
    dil_b = [_dilated_bwd(qa, ka, mid, d_ya, ya, lse, d, HW) for d in DILATIONS]
    d_qk, g_qn, g_kn = _qknorm_bwd(qk, qn_gain, kn_gain, tabs, [t[0] for t in dil_b], [t[1] for t in dil_b],
                                   [t[2] for t in dil_b], HW)
    dva = _dv_sum([t[3] for t in dil_b], [t[4] for t in dil_b], HW)
    d_proj = jnp.concatenate([d_qk, dva, dqb.astype(MXU_DTYPE), dkb.astype(MXU_DTYPE), dvb.astype(MXU_DTYPE), d_ga, d_gb], axis=1)
    (G["w_in"],) = _matmul("grad_w_in", h, d_proj, mode="tn", out_dtypes=[F32], out_cshard=True, bn=_fit(cin, 1280))
    (d_h,) = _matmul("proj_bwd", d_proj, W["w_in"], mode="nt", out_dtypes=[F32], b_cshard=True, bk=_fit(cin, 1280))
    grad_x, g_g_mix = _rmsnorm_bwd("rmsnorm_mix_bwd", d_h, x2d, g_mix, dx1, False)

    g3 =[G[k] if k in col_sharded else G[k].reshape(N_SHARD, -1, G[k].shape[1]) for k in names]
    swapped = _swap_halves(g3)
    pre = [_add_half(f"grad_add_half_{k}", g, got, c_idx) for k, g, got in zip(names, g3, swapped)]
    landed = _scatter_partials([p16 for _, p16 in pre])
    halves = [_add_partials(f"grad_add_partials_{k}", p32, got, s_idx) for k, (p32, _), got in zip(names, pre, landed)]
    others = _swap_reduced(halves)

    pack_w = -(-(3 * D + 3 * 128) // (8 * 128)) * 128

    def pack(v_mix, v_mlp, v_ple, v_qn, v_kn, extra):
        flat = jnp.concatenate([v_mix.reshape(-1), v_mlp.reshape(-1), v_ple.reshape(-1), v_qn.reshape(-1), v_kn.reshape(-1), extra.reshape(-1)])
        return jnp.pad(flat, (0, 8 * pack_w - flat.shape[0])).reshape(8, pack_w)

    def unpack(blk):
        flat = blk.reshape(-1)
        return (flat[:D].reshape(1, D), flat[D:2 * D].reshape(1, D), flat[2 * D:3 * D].reshape(1, D),
                flat[3 * D:3 * D + 128].reshape(1, 128), flat[3 * D + 128:3 * D + 256].reshape(1, 128), flat[3 * D + 256])

    small = _all_reduce_small(pack(g_g_mix, g_g_mlp, g_g_ple, g_qn, g_kn, loss_part))
    sw = pack(g_mix, g_mlp, g_ple, qn_gain, kn_gain, jnp.zeros((128,), F32))
    sm = pack(m_g_mix, m_g_mlp, m_g_ple, m_qn_gain, m_kn_gain, jnp.zeros((128,), F32))
    sv = pack(v_g_mix, v_g_mlp, v_g_ple, v_qn_gain, v_kn_gain, jnp.ones((128,), F32))
    s_delta, s_nm, s_nv = _adamw("adamw_small", sw, small, sm, sv)
    sg_mix, sg_mlp, sg_ple, sg_qn, sg_kn, loss = unpack(small)
    small_out = {}
    for tag, blk in (("delta", s_delta), ("new_m", s_nm), ("new_v", s_nv)):
        u = unpack(blk)
        small_out[tag] = dict(g_mix=u[0], g_mlp=u[1], g_ple=u[2], qn_gain=u[3], kn_gain=u[4])
    small_grad = dict(g_mix=sg_mix, g_mlp=sg_mlp, g_ple=sg_ple, qn_gain=sg_qn, kn_gain=sg_kn)

    big_out = {"grad": {}, "delta": {}, "new_m": {}, "new_v": {}}
    for k, mine, theirs in zip(names, halves, others):
        shape = big[k].shape
        m2, v2 = (t.reshape(shape[1], shape[2]) for t in moments[k])
        res = _adamw_halves(f"adamw_{k}", shard2d[k], mine, theirs, m2, v2, c_idx)
        for tag, t in zip(("grad", "delta", "new_m", "new_v"), res):
            big_out[tag][k] = t.reshape(shape)

    order = ["g_mix", "w_in", "qn_gain", "kn_gain", "w_branch_a", "w_branch_b", "w_out", "g_mlp", "w_up", "w_down", "g_ple",
             "w_ple_gate", "w_ple_proj"]
    outs = [loss, grad_x.reshape(x.shape)]
    outs += [small_grad[k] if k in small_grad else big_out["grad"][k] for k in order]
    for tag in ("delta", "new_m", "new_v"):
        outs += [small_out[tag][k] if k in small_grad else big_out[tag][k] for k in order]
    return tuple(outs)
```

```python
import functools

import jax
import jax.numpy as jnp
from jax import lax
from jax.experimental import pallas as pl
from jax.experimental.pallas import tpu as pltpu

F32 = jnp.float32
MXU_DTYPE = jnp.bfloat16
HEAD_DIM = 128
ROT_DIM = HEAD_DIM // 4
ROPE_THETA = 500000.0
EPS = 1e-6
DILATIONS = (1, 4, 16)
BLOCK = 128
N_SHARD = 4
ADAM_LR, ADAM_B1, ADAM_B2, ADAM_EPS, ADAM_WD, ADAM_STEP = 0.001, 0.9, 0.999, 1e-08, 0.01, 10
V7X_VMEM_BYTES = 64 * 1024 * 1024
VMEM_LIMIT = V7X_VMEM_BYTES - 8 * 1024 * 1024
MESH = pl.DeviceIdType.MESH
NEG = -1e30
SB_BQ, SB_BK = 512, 256
CUMSUM_PASSES = 2


def _fit(dim, pref):
    if dim <= pref:
        return dim
    b = (pref // 128) * 128
    while dim % b:
        b -= 128
    return b


def _params(sem=None):
    return pltpu.CompilerParams(dimension_semantics=sem, vmem_limit_bytes=VMEM_LIMIT)


def _dot(a, b, dims=(((1,), (0,)), ((), ()))):
    return lax.dot_general(a, b, dims, preferred_element_type=F32)


NT = (((1,), (1,)), ((), ()))
TN = (((0,), (0,)), ((), ()))


def _split_dot(x, u, passes):
    out = None
    r = x
    for p in range(passes):
        hi = r.astype(MXU_DTYPE)
        part = _dot(hi, u)
        out = part if out is None else out + part
        if p + 1 < passes:
            r = r - hi.astype(F32)
    return out


def _matmul(name, a, b, *, mode, out_dtypes, bm=1024, bn=1024, bk=2048, b_cshard=False, b_off=0, n_out=None,
            extras=(), epilogue=None, out_cshard=False):
    if mode == "tn":
        K, M = a.shape
        N = b.shape[1]
    else:
        M, K = a.shape
        if mode == "nn":
            N = n_out if n_out is not None else (N_SHARD * b.shape[2] if b_cshard else b.shape[1])
        else:
            N = b.shape[1] if b_cshard else b.shape[0]
    bm, bn, bk = _fit(M, bm), _fit(N, bn), _fit(K, bk)
    nk = K // bk
    grid = (M // bm, N // bn, nk)

    if mode == "tn":
        a_spec = pl.BlockSpec((bk, bm), lambda i, j, k: (k, i))
        b_spec = pl.BlockSpec((bk, bn), lambda i, j, k: (k, j))
        dims = TN
    elif mode == "nn":
        a_spec = pl.BlockSpec((bm, bk), lambda i, j, k: (i, k))
        if b_cshard:
            cb = b.shape[2] // bn
            b_spec = pl.BlockSpec((None, bk, bn), lambda i, j, k: ((j + b_off) // cb, k, (j + b_off) % cb))
        else:
            b_spec = pl.BlockSpec((bk, bn), lambda i, j, k: (k, j + b_off))
        dims = (((1,), (0,)), ((), ()))
    else:
        a_spec = pl.BlockSpec((bm, bk), lambda i, j, k: (i, k))
        if b_cshard:
            cb = b.shape[2] // bk
            b_spec = pl.BlockSpec((None, bn, bk), lambda i, j, k: (k // cb, j, k % cb))
        else:
            b_spec = pl.BlockSpec((bn, bk), lambda i, j, k: (j, k))
        dims = NT

    ex_arrays = [e[0] for e in extras]
    ex_specs = [pl.BlockSpec((bm, bn), functools.partial(lambda i, j, k, off: (i, j + off), off=e[1])) for e in extras]
    if out_cshard:
        cbo = (N // N_SHARD) // bn
        out_shape = [jax.ShapeDtypeStruct((N_SHARD, M, N // N_SHARD), dt) for dt in out_dtypes]
        out_specs = [pl.BlockSpec((None, bm, bn), lambda i, j, k: (j // cbo, i, j % cbo)) for _ in out_dtypes]
    else:
        out_shape = [jax.ShapeDtypeStruct((M, N), dt) for dt in out_dtypes]
        out_specs = [pl.BlockSpec((bm, bn), lambda i, j, k: (i, j)) for _ in out_dtypes]
    ne, no = len(extras), len(out_dtypes)

    def kern(*refs):
        a_ref, b_ref = refs[0], refs[1]
        ex_refs = refs[2:2 + ne]
        o_refs = refs[2 + ne:2 + ne + no]
        part = _dot(a_ref[...].astype(MXU_DTYPE), b_ref[...].astype(MXU_DTYPE), dims)

        def finish(acc):
            vals = (acc,) * no if epilogue is None else epilogue(acc, *[r[...] for r in ex_refs])
            for r, v in zip(o_refs, vals):
                r[...] = v.astype(r.dtype)

        if nk == 1:
            finish(part)
        else:
            acc_ref = refs[2 + ne + no]
            k = pl.program_id(2)

            @pl.when(k == 0)
            def _():
                acc_ref[...] = part

            @pl.when(k > 0)
            def _():
                acc_ref[...] += part

            @pl.when(k == nk - 1)
            def _():
                finish(acc_ref[...])

    outs = pl.pallas_call(
        kern, name=name, grid=grid,
        in_specs=[a_spec, b_spec] + ex_specs, out_specs=out_specs, out_shape=out_shape,
        scratch_shapes=[pltpu.VMEM((bm, bn), F32)] if nk > 1 else [],
        compiler_params=_params(("parallel", "parallel", "arbitrary")),
    )(a, b, *ex_arrays)
    return outs


def _row(tr, w, coff=0):
    return pl.BlockSpec((tr, w), lambda i: (i, coff))


def _vec(w):
    return pl.BlockSpec((1, w), lambda i: (0, 0))


def _rows_call(name, body, n_rows, tr, ins, outs, accs=()):
    n_in, n_out = len(ins), len(outs)

    def kern(*refs):
        acc_refs = refs[n_in + n_out:]
        if acc_refs:
            @pl.when(pl.program_id(0) == 0)
            def _():
                for r in acc_refs:
                    r[...] = jnp.zeros_like(r)
        body(*refs)

    out_shape = [jax.ShapeDtypeStruct((n_rows, w), dt) for w, dt in outs] + [jax.ShapeDtypeStruct((1, w), F32) for w in accs]
    out_specs = [_row(tr, w) for w, _ in outs] + [_vec(w) for w in accs]
    return pl.pallas_call(
        kern, name=name, grid=(n_rows // tr,),
        in_specs=[s for _, s in ins], out_specs=out_specs, out_shape=out_shape,
        compiler_params=_params(("arbitrary",)),
    )(*[a for a, _ in ins])


def _rmsnorm_fwd(name, x, g):
    S, D = x.shape
    tr = _fit(S, 256)

    def body(x_ref, g_ref, h_ref):
        xv = x_ref[...]
        r = lax.rsqrt(jnp.mean(xv * xv, axis=1, keepdims=True) + EPS)
        h_ref[...] = ((xv * r) * g_ref[...]).astype(h_ref.dtype)

    return _rows_call(name, body, S, tr, [(x, _row(tr, D)), (g, _vec(D))], [(D, MXU_DTYPE)])[0]


def _rmsnorm_bwd(name, dh, x, g, resid, with_loss):
    S, D = x.shape
    tr = _fit(S, 256)

    def body(dh_ref, x_ref, g_ref, res_ref, dx_ref, dg_ref, *loss_ref):
        xv = x_ref[...]
        r = lax.rsqrt(jnp.mean(xv * xv, axis=1, keepdims=True) + EPS)
        dhv = dh_ref[...]
        u = dhv * g_ref[...]
        xr = xv * r
        dx = r * u - xr * (r * r) * jnp.mean(xv * u, axis=1, keepdims=True)
        resv = res_ref[...]
        dx_ref[...] = resv + dx
        dg_ref[...] += jnp.sum(dhv * xr, axis=0, keepdims=True)
        if with_loss:
            loss_ref[0][...] += (0.5 * D) * jnp.sum(resv * resv)

    outs = _rows_call(name, body, S, tr, [(dh, _row(tr, D)), (x, _row(tr, D)), (g, _vec(D)), (resid, _row(tr, D))],
                      [(D, F32)], accs=(D, 128) if with_loss else (D,))
    return outs


def _rope_tables(S):
    half = ROT_DIM // 2
    pos = jnp.arange(S, dtype=F32)
    inv = ROPE_THETA ** (-jnp.arange(0, ROT_DIM, 2, dtype=F32) / ROT_DIM)
    ang = pos[:, None] * inv[None, :]
    cos, sin = jnp.cos(ang), jnp.sin(ang)
    pad = HEAD_DIM - ROT_DIM
    ctab = jnp.concatenate([cos, cos, jnp.ones((S, pad), F32)], axis=1)
    atab = jnp.concatenate([-sin, jnp.zeros((S, pad + half), F32)], axis=1)
    btab = jnp.concatenate([jnp.zeros((S, half), F32), sin, jnp.zeros((S, pad), F32)], axis=1)
    return ctab, atab, btab


def _qknorm_fwd(qk, qn, kn, tabs, HW):
    S = qk.shape[0]
    tr = _fit(S, 256)
    half = ROT_DIM // 2

    def body(qk_ref, qn_ref, kn_ref, c_ref, a_ref, b_ref, q_out, k_out):
        ct, at, bt = c_ref[...], a_ref[...], b_ref[...]
        for part, (g_ref, o_ref) in enumerate(((qn_ref, q_out), (kn_ref, k_out))):
            gv = g_ref[...]
            for h in range(HW // HEAD_DIM):
                xh = qk_ref[:, part * HW + h * HEAD_DIM: part * HW + (h + 1) * HEAD_DIM]
                r = lax.rsqrt(jnp.mean(xh * xh, axis=1, keepdims=True) + EPS)
                y = (xh * r) * gv
                o = y * ct + pltpu.roll(y, HEAD_DIM - half, 1) * at + pltpu.roll(y, half, 1) * bt
                o_ref[:, h * HEAD_DIM:(h + 1) * HEAD_DIM] = o.astype(o_ref.dtype)

    ins = [(qk, _row(tr, 2 * HW)), (qn, _vec(HEAD_DIM)), (kn, _vec(HEAD_DIM))] + [(t, _row(tr, HEAD_DIM)) for t in tabs]
    return _rows_call("qknorm_fwd", body, S, tr, ins, [(HW, MXU_DTYPE), (HW, MXU_DTYPE)])


def _shift_spec(tr, w, shift, nblk):
    return pl.BlockSpec((tr, w), lambda i: (jnp.minimum(i + shift, nblk - 1), 0))


def _qknorm_bwd(qk, qn, kn, tabs, dq_parts, dk_cur, dk_prev, HW):
    S = qk.shape[0]
    tr = BLOCK
    nblk = S // tr
    half = ROT_DIM // 2
    nd = len(DILATIONS)

    def body(*refs):
        qk_ref, qn_ref, kn_ref, c_ref, a_ref, b_ref = refs[:6]
        dq_refs = refs[6:6 + nd]
        dkc_refs = refs[6 + nd:6 + 2 * nd]
        dkp_refs = refs[6 + 2 * nd:6 + 3 * nd]
        d_out, dqn_ref, dkn_ref = refs[6 + 3 * nd:]
        i = pl.program_id(0)
        ct, at, bt = c_ref[...], a_ref[...], b_ref[...]
        live = [(i + d < nblk).astype(F32) for d in DILATIONS]
        for part, (g_ref, dg_ref) in enumerate(((qn_ref, dqn_ref), (kn_ref, dkn_ref))):
            gv = g_ref[...]
            dg = jnp.zeros((1, HEAD_DIM), F32)
            for h in range(HW // HEAD_DIM):
                hs = slice(h * HEAD_DIM, (h + 1) * HEAD_DIM)
                if part == 0:
                    do = dq_refs[0][:, hs] + dq_refs[1][:, hs] + dq_refs[2][:, hs]
                else:
                    do = dkc_refs[0][:, hs] + dkc_refs[1][:, hs] + dkc_refs[2][:, hs]
                    for n in range(nd):
                        do = do + dkp_refs[n][:, hs] * live[n]
                dy = do * ct + pltpu.roll(do * at, half, 1) + pltpu.roll(do * bt, HEAD_DIM - half, 1)
                xh = qk_ref[:, part * HW + h * HEAD_DIM: part * HW + (h + 1) * HEAD_DIM]
                r = lax.rsqrt(jnp.mean(xh * xh, axis=1, keepdims=True) + EPS)
                xr = xh * r
                u = dy * gv
                dx = r * u - xr * (r * r) * jnp.mean(xh * u, axis=1, keepdims=True)
                d_out[:, part * HW + h * HEAD_DIM: part * HW + (h + 1) * HEAD_DIM] = dx.astype(d_out.dtype)
                dg = dg + jnp.sum(dy * xr, axis=0, keepdims=True)
            dg_ref[...] += dg

    ins = [(qk, _row(tr, 2 * HW)), (qn, _vec(HEAD_DIM)), (kn, _vec(HEAD_DIM))] + [(t, _row(tr, HEAD_DIM)) for t in tabs]
    ins += [(a, _row(tr, HW)) for a in dq_parts] + [(a, _row(tr, HW)) for a in dk_cur]
    ins += [(a, _shift_spec(tr, HW, d, nblk)) for a, d in zip(dk_prev, DILATIONS)]
    return _rows_call("qknorm_bwd", body, S, tr, ins, [(2 * HW, MXU_DTYPE)], accs=(HEAD_DIM, HEAD_DIM))


def _dv_sum(dv_cur, dv_prev, HW):
    S = dv_cur[0].shape[0]
    tr = BLOCK
    nblk = S // tr
    nd = len(DILATIONS)

    def body(*refs):
        i = pl.program_id(0)
        out = refs[2 * nd]
        acc = refs[0][...] + refs[1][...] + refs[2][...]
        for n, d in enumerate(DILATIONS):
            acc = acc + refs[nd + n][...] * (i + d < nblk).astype(F32)
        out[...] = acc.astype(out.dtype)

    ins = [(a, _row(tr, HW)) for a in dv_cur] + [(a, _shift_spec(tr, HW, d, nblk)) for a, d in zip(dv_prev, DILATIONS)]
    return _rows_call("dilated_dv_sum", body, S, tr, ins, [(HW, MXU_DTYPE)])[0]


def _dil_geometry(d, HW):
    H = HW // HEAD_DIM
    hb = min(H, max(1, 8 // d))
    tb, w = BLOCK * d, hb * HEAD_DIM
    cur = pl.BlockSpec((tb, w), lambda n, g: (n, g))
    prev = pl.BlockSpec((tb, w), lambda n, g: (jnp.maximum(n - 1, 0), g))
    units = [(hh, r) for hh in range(hb) for r in range(d)]
    return H // hb, hb, tb, cur, prev, units


def _dil_mask(n):
    qi = lax.broadcasted_iota(jnp.int32, (BLOCK, 2 * BLOCK), 0)
    ki = lax.broadcasted_iota(jnp.int32, (BLOCK, 2 * BLOCK), 1)
    return (ki >= qi) & (ki <= qi + BLOCK) & ((ki >= BLOCK) | (n > 0))


def _dil_stage(ref, buf, row0=0):
    for hh in range(buf.shape[0]):
        buf[hh, row0:row0 + ref.shape[0], :] = ref[:, hh * HEAD_DIM:(hh + 1) * HEAD_DIM].astype(F32)


def _dil_unstage(buf, ref):
    for hh in range(buf.shape[0]):
        ref[:, hh * HEAD_DIM:(hh + 1) * HEAD_DIM] = buf[hh]


def _dil_rows(d, r, size):
    return pl.ds(0, size) if d == 1 else pl.ds(r, size, stride=d)


def _dil_operands(d, tb, units, q_ref, kc_ref, kp_ref, vc_ref, vp_ref, qs, kf, vf):
    _dil_stage(q_ref, qs)
    _dil_stage(kp_ref, kf)
    _dil_stage(kc_ref, kf, tb)
    _dil_stage(vp_ref, vf)
    _dil_stage(vc_ref, vf, tb)
    qu = [qs[hh, _dil_rows(d, r, BLOCK), :].astype(MXU_DTYPE) for hh, r in units]
    ku = [kf[hh, _dil_rows(d, r, 2 * BLOCK), :].astype(MXU_DTYPE) for hh, r in units]
    vu = [vf[hh, _dil_rows(d, r, 2 * BLOCK), :].astype(MXU_DTYPE) for hh, r in units]
    return qu, ku, vu


def _dilated_fwd(q, k, mid, d, HW):
    S = q.shape[0]
    scale = HEAD_DIM ** -0.5
    ng, hb, tb, cur, prev, units = _dil_geometry(d, HW)

    def kern(q_ref, kc_ref, kp_ref, vc_ref, vp_ref, o_ref, l_ref, qs, kf, vf, os_, ls):
        mask = _dil_mask(pl.program_id(0))
        qu, ku, vu = _dil_operands(d, tb, units, q_ref, kc_ref, kp_ref, vc_ref, vp_ref, qs, kf, vf)
        sc = [jnp.where(mask, _dot(a, b, NT) * scale, NEG) for a, b in zip(qu, ku)]
        mx = [jnp.max(t, axis=1, keepdims=True) for t in sc]
        ex = [jnp.exp(t - m) for t, m in zip(sc, mx)]
        den = [jnp.sum(t, axis=1, keepdims=True) for t in ex]
        out = [_dot(t.astype(MXU_DTYPE), v) / dn for t, v, dn in zip(ex, vu, den)]
        for (hh, r), o, m, dn in zip(units, out, mx, den):
            os_[hh, _dil_rows(d, r, BLOCK), :] = o
            ls[hh, _dil_rows(d, r, BLOCK), :] = jnp.broadcast_to(m + jnp.log(dn), (BLOCK, HEAD_DIM))
        _dil_unstage(os_, o_ref)
        _dil_unstage(ls, l_ref)

    return pl.pallas_call(
        kern, name=f"dilated_fwd_d{d}", grid=(S // tb, ng),
        in_specs=[cur, cur, prev, cur, prev],
        out_specs=[cur, cur],
        out_shape=[jax.ShapeDtypeStruct((S, HW), F32)] * 2,
        scratch_shapes=[pltpu.VMEM((hb, tb, HEAD_DIM), F32)] + [pltpu.VMEM((hb, 2 * tb, HEAD_DIM), F32)] * 2
        + [pltpu.VMEM((hb, tb, HEAD_DIM), F32)] * 2,
        compiler_params=_params(("arbitrary", "arbitrary")),
    )(q, k, k, mid, mid)


def _dilated_combine(os_, lses, HW):
    S = os_[0].shape[0]
    tr = _fit(S, 256)

    def body(o0, o1, o2, l0, l1, l2, ya_ref, lse_ref):
        a, b, c = l0[...], l1[...], l2[...]
        mx = jnp.maximum(jnp.maximum(a, b), c)
        ea, eb, ec = jnp.exp(a - mx), jnp.exp(b - mx), jnp.exp(c - mx)
        tot = ea + eb + ec
        ya_ref[...] = (ea * o0[...] + eb * o1[...] + ec * o2[...]) / tot
        lse_ref[...] = mx + jnp.log(tot)

    ins = [(a, _row(tr, HW)) for a in list(os_) + list(lses)]
    return _rows_call("dilated_combine", body, S, tr, ins, [(HW, F32), (HW, F32)])


def _dilated_bwd(q, k, mid, dya, ya, lse, d, HW):
    S = q.shape[0]
    scale = HEAD_DIM ** -0.5
    ng, hb, tb, cur, prev, units = _dil_geometry(d, HW)

    def kern(q_ref, kc_ref, kp_ref, vc_ref, vp_ref, dy_ref, y_ref, l_ref, dq_ref, dkc_ref, dkp_ref, dvc_ref, dvp_ref,
             qs, kf, vf, dys, ys, ls, dqs, dkcs, dkps, dvcs, dvps):
        mask = _dil_mask(pl.program_id(0))
        qu, ku, vu = _dil_operands(d, tb, units, q_ref, kc_ref, kp_ref, vc_ref, vp_ref, qs, kf, vf)
        _dil_stage(dy_ref, dys)
        _dil_stage(y_ref, ys)
        _dil_stage(l_ref, ls)
        dy = [dys[hh, _dil_rows(d, r, BLOCK), :] for hh, r in units]
        lt = [ls[hh, _dil_rows(d, r, BLOCK), :][:, 0:1] for hh, r in units]
        delta = [jnp.sum(t * ys[hh, _dil_rows(d, r, BLOCK), :], axis=1, keepdims=True) for t, (hh, r) in zip(dy, units)]
        dyb = [t.astype(MXU_DTYPE) for t in dy]
        p = [jnp.where(mask, jnp.exp(_dot(a, b, NT) * scale - l), 0.0) for a, b, l in zip(qu, ku, lt)]
        ds = [(t * (_dot(g, v, NT) - dl) * scale).astype(MXU_DTYPE) for t, g, v, dl in zip(p, dyb, vu, delta)]
        dq = [_dot(t, b) for t, b in zip(ds, ku)]
        dk = [_dot(t, a, TN) for t, a in zip(ds, qu)]
        dv = [_dot(t.astype(MXU_DTYPE), g, TN) for t, g in zip(p, dyb)]
        for (hh, r), tq, tk, tv in zip(units, dq, dk, dv):
            at = _dil_rows(d, r, BLOCK)
            dqs[hh, at, :] = tq
            dkps[hh, at, :] = tk[0:BLOCK]
            dkcs[hh, at, :] = tk[BLOCK:2 * BLOCK]
            dvps[hh, at, :] = tv[0:BLOCK]
            dvcs[hh, at, :] = tv[BLOCK:2 * BLOCK]
        for buf, ref in ((dqs, dq_ref), (dkcs, dkc_ref), (dkps, dkp_ref), (dvcs, dvc_ref), (dvps, dvp_ref)):
            _dil_unstage(buf, ref)

    return pl.pallas_call(
        kern, name=f"dilated_bwd_d{d}", grid=(S // tb, ng),
        in_specs=[cur, cur, prev, cur, prev, cur, cur, cur],
        out_specs=[cur] * 5,
        out_shape=[jax.ShapeDtypeStruct((S, HW), F32)] * 5,
        scratch_shapes=[pltpu.VMEM((hb, tb, HEAD_DIM), F32)] + [pltpu.VMEM((hb, 2 * tb, HEAD_DIM), F32)] * 2
        + [pltpu.VMEM((hb, tb, HEAD_DIM), F32)] * 8,
        compiler_params=_params(("arbitrary", "arbitrary")),
    )(q, k, k, mid, mid, dya, ya, lse)


def _softplus_parts(z):
    sp = jnp.maximum(z, 0.0) + jnp.log1p(jnp.exp(-jnp.abs(z)))
    return -sp, z - sp


def _sb_specs(S, H):
    q_spec = pl.BlockSpec((SB_BQ, HEAD_DIM), lambda h, i: (i, H + h))
    k_spec = pl.BlockSpec((S, HEAD_DIM), lambda h, i: (0, 2 * H + h))
    v_spec = pl.BlockSpec((S, HEAD_DIM), lambda h, i: (0, 3 * H + h))
    o_spec = pl.BlockSpec((SB_BQ, HEAD_DIM), lambda h, i: (i, h))
    return q_spec, k_spec, v_spec, o_spec


def _sb_masks():
    row = lax.broadcasted_iota(jnp.int32, (SB_BQ, SB_BQ), 0)
    col = lax.broadcasted_iota(jnp.int32, (SB_BQ, SB_BQ), 1)
    tri_r = lax.broadcasted_iota(jnp.int32, (SB_BK, SB_BK), 0)
    tri_c = lax.broadcasted_iota(jnp.int32, (SB_BK, SB_BK), 1)
    return tri_r, tri_c, col < row


def _sb_fwd(mid, HW, ex=None):
    S = mid.shape[0]
    H = HW // HEAD_DIM
    BQ, CH = SB_BQ, SB_BK
    NC = BQ // CH
    scale = HEAD_DIM ** -0.5
    q_spec, k_spec, v_spec, o_spec = _sb_specs(S, H)

    def kern(q_ref, k_ref, v_ref, o_ref, t_ref):
        i = pl.program_id(1)
        q = q_ref[...]
        tri_r, tri_c, causal = _sb_masks()
        upper = (tri_r > tri_c).astype(MXU_DTYPE)

        def block(j, run, acc, masked):
            ks = pl.multiple_of(j * BQ, BQ)
            z = _dot(q, k_ref[pl.ds(ks, BQ), :], NT) * scale
            m, l = _softplus_parts(z)
            if masked:
                m = jnp.where(causal, m, 0.0)
            parts = []
            for c in reversed(range(NC)):
                mc = m[:, c * CH:(c + 1) * CH]
                parts.append(l[:, c * CH:(c + 1) * CH] + (_split_dot(mc, upper, CUMSUM_PASSES) + run))
                run = run + jnp.sum(mc, axis=1, keepdims=True)
            a = jnp.exp(jnp.concatenate(parts[::-1], axis=1))
            if masked:
                a = jnp.where(causal, a, 0.0)
            return run, acc + _dot(a.astype(MXU_DTYPE), v_ref[pl.ds(ks, BQ), :])

        run, acc = block(i, jnp.zeros((BQ, 1), F32), jnp.zeros((BQ, HEAD_DIM), F32), True)
        run, acc = lax.fori_loop(0, i, lambda t, carry: block(i - 1 - t, carry[0], carry[1], False), (run, acc))
        o_ref[...] = acc
        t_ref[...] = jnp.broadcast_to(run, (BQ, HEAD_DIM))

    (o, tot), rode = _host_call(kern, "stickbreak_fwd", (H, S // BQ), [q_spec, k_spec, v_spec], [o_spec, o_spec],
                                [jax.ShapeDtypeStruct((S, HW), F32)] * 2, (mid, mid, mid), ex)
    return o, tot, rode


def _sb_bwd(mid, dyb, tot, HW, ex=None):
    S = mid.shape[0]
    H = HW // HEAD_DIM
    BQ, CH = SB_BQ, SB_BK
    NC = BQ // CH
    scale = HEAD_DIM ** -0.5
    q_spec, k_spec, v_spec, o_spec = _sb_specs(S, H)
    full = pl.BlockSpec((S, HEAD_DIM), lambda h, i: (0, h))

    def kern(q_ref, k_ref, v_ref, do_ref, t_ref, dq_ref, dk_ref, dv_ref):
        i = pl.program_id(1)

        @pl.when(i == 0)
        def _():
            dk_ref[...] = jnp.zeros_like(dk_ref)
            dv_ref[...] = jnp.zeros_like(dv_ref)

        q = q_ref[...]
        do = do_ref[...].astype(MXU_DTYPE)
        total = t_ref[:, 0:1]
        tri_r, tri_c, causal = _sb_masks()
        incl = (tri_r <= tri_c).astype(MXU_DTYPE)
        excl = (tri_r < tri_c).astype(MXU_DTYPE)

        def block(j, mrun, prun, dq, masked):
            ks = pl.multiple_of(j * BQ, BQ)
            k = k_ref[pl.ds(ks, BQ), :]
            z = _dot(q, k, NT) * scale
            m, l = _softplus_parts(z)
            if masked:
                m = jnp.where(causal, m, 0.0)
            parts = []
            for c in range(NC):
                mc = m[:, c * CH:(c + 1) * CH]
                parts.append(l[:, c * CH:(c + 1) * CH] + (total - mrun - _split_dot(mc, incl, CUMSUM_PASSES)))
                mrun = mrun + jnp.sum(mc, axis=1, keepdims=True)
            a = jnp.exp(jnp.concatenate(parts, axis=1))
            if masked:
                a = jnp.where(causal, a, 0.0)
            p = a * _dot(do, v_ref[pl.ds(ks, BQ), :], NT)
            parts = []
            for c in range(NC):
                pc = p[:, c * CH:(c + 1) * CH]
                parts.append(_split_dot(pc, excl, CUMSUM_PASSES) + prun)
                prun = prun + jnp.sum(pc, axis=1, keepdims=True)
            before = jnp.concatenate(parts, axis=1)
            sig = jnp.exp(l)
            dz = (p * (1.0 - sig) - sig * before) * scale
            if masked:
                dz = jnp.where(causal, dz, 0.0)
            dzb = dz.astype(MXU_DTYPE)
            dk_ref[pl.ds(ks, BQ), :] += _dot(dzb, q, TN)
            dv_ref[pl.ds(ks, BQ), :] += _dot(a.astype(MXU_DTYPE), do, TN)
            return mrun, prun, dq + _dot(dzb, k)

        zero = jnp.zeros((BQ, 1), F32)
        carry = lax.fori_loop(0, i, lambda j, carry: block(j, *carry, False), (zero, zero, jnp.zeros((BQ, HEAD_DIM), F32)))
        dq_ref[...] = block(i, *carry, True)[2]

    grads, rode = _host_call(kern, "stickbreak_bwd", (H, S // BQ), [q_spec, k_spec, v_spec, o_spec, o_spec], [o_spec, full, full],
                             [jax.ShapeDtypeStruct((S, HW), F32)] * 3, (mid, mid, mid, dyb, tot), ex)
    return grads, rode


ANY = pl.BlockSpec(memory_space=pl.ANY)


def _place():
    x, y, c = lax.axis_index("x"), lax.axis_index("y"), lax.axis_index("c")
    chips = [(1 - x, y), (x, 1 - y), (1 - x, 1 - y)]
    return x, y, c, chips


def _half(ref, shard, hc, rh):
    return ref.at[shard, pl.ds(pl.multiple_of(hc * rh, 8), rh), :]


def _cast_place(name, w, s_idx):
    R, C = w.shape
    tr = _fit(R, 256)

    def kern(s_ref, w_ref, o_ref):
        o_ref[...] = w_ref[...].astype(o_ref.dtype)

    return pl.pallas_call(
        kern, name=name,
        grid_spec=pltpu.PrefetchScalarGridSpec(
            num_scalar_prefetch=1, grid=(R // tr,),
            in_specs=[pl.BlockSpec((tr, C), lambda r, s_ref: (r, 0))],
            out_specs=pl.BlockSpec((None, tr, C), lambda r, s_ref: (s_ref[0], r, 0))),
        out_shape=jax.ShapeDtypeStruct((N_SHARD, R, C), MXU_DTYPE),
        compiler_params=_params(("arbitrary",)),
    )(s_idx, w)


class _Exchange:
    def __init__(self, inputs, out_shape, aliases, scratch, phases):
        self.inputs, self.out_shape, self.aliases, self.scratch, self.phases = inputs, out_shape, aliases, scratch, phases


def _run_alone(name, ex):
    ni, no = len(ex.inputs), len(ex.out_shape)

    def body(*refs):
        for phase in ex.phases:
            phase(refs[:ni], refs[ni:ni + no], refs[ni + no:])

    return pl.pallas_call(
        body, name=name, in_specs=[ANY] * ni, out_specs=[ANY] * no, out_shape=ex.out_shape,
        input_output_aliases=ex.aliases, scratch_shapes=ex.scratch,
    )(*ex.inputs)


def _host_call(kern, name, grid, in_specs, out_specs, out_shape, operands, ex):
    sem = ("arbitrary", "arbitrary")
    if ex is None:
        return pl.pallas_call(kern, name=name, grid=grid, in_specs=in_specs, out_specs=out_specs, out_shape=out_shape,
                              compiler_params=_params(sem))(*operands), []
    n_in, n_out, ri, ro = len(in_specs), len(out_specs), len(ex.inputs), len(ex.out_shape)
    nsteps, nph = grid[0] * grid[1], len(ex.phases)

    def body(*refs):
        r_in, r_out = refs[n_in:n_in + ri], refs[n_in + ri + n_out:n_in + ri + n_out + ro]
        scratch = refs[n_in + ri + n_out + ro:]
        step = pl.program_id(0) * grid[1] + pl.program_id(1)
        for kph, phase in enumerate(ex.phases):
            pl.when(step == (kph * (nsteps - 1)) // (nph - 1))(functools.partial(phase, r_in, r_out, scratch))
        kern(*refs[:n_in], *refs[n_in + ri:n_in + ri + n_out])

    outs = pl.pallas_call(
        body, name=name, grid=grid, in_specs=list(in_specs) + [ANY] * ri, out_specs=list(out_specs) + [ANY] * ro,
        out_shape=list(out_shape) + list(ex.out_shape), scratch_shapes=ex.scratch,
        input_output_aliases={n_in + a: n_out + b for a, b in ex.aliases.items()},
        compiler_params=_params(sem),
    )(*operands, *ex.inputs)
    return outs[:n_out], outs[n_out:]


def _gather_exchange(bufs):
    n = len(bufs)

    def between_chips(outs, sems, i, j, chip, c, shard):
        blk = _half(outs[i], shard, c, outs[i].shape[1] // 2)
        return pltpu.make_async_remote_copy(src_ref=blk, dst_ref=blk, send_sem=sems[0].at[i, j], recv_sem=sems[1].at[i, j],
                                            device_id=(chip[0], chip[1], c), device_id_type=MESH)

    def to_sibling(outs, sems, i, j, x, y, c, shard, hc):
        blk = _half(outs[i], shard, hc, outs[i].shape[1] // 2)
        return pltpu.make_async_remote_copy(src_ref=blk, dst_ref=blk, send_sem=sems[0].at[i, 3 + j], recv_sem=sems[1].at[i, 3 + j],
                                            device_id=(x, y, 1 - c), device_id_type=MESH)

    def send_mine(ins, outs, sems):
        x, y, c, chips = _place()
        for i in range(n):
            for j, chip in enumerate(chips):
                between_chips(outs, sems, i, j, chip, c, 2 * x + y).start()

    def pass_on(ins, outs, sems):
        x, y, c, chips = _place()
        for i in range(n):
            for j, chip in enumerate(chips):
                between_chips(outs, sems, i, j, chip, c, 2 * chip[0] + chip[1]).wait_recv()
                to_sibling(outs, sems, i, j, x, y, c, 2 * chip[0] + chip[1], c).start()

    def finish(ins, outs, sems):
        x, y, c, chips = _place()
        for i in range(n):
            for j, chip in enumerate(chips):
                to_sibling(outs, sems, i, j, x, y, c, 2 * chip[0] + chip[1], 1 - c).wait_recv()
        for i in range(n):
            for j, chip in enumerate(chips):
                between_chips(outs, sems, i, j, chip, c, 2 * x + y).wait_send()
                to_sibling(outs, sems, i, j, x, y, c, 2 * chip[0] + chip[1], c).wait_send()

    return _Exchange(list(bufs), [jax.ShapeDtypeStruct(b.shape, b.dtype) for b in bufs], {i: i for i in range(n)},
                     [pltpu.SemaphoreType.DMA((n, 6)), pltpu.SemaphoreType.DMA((n, 6))], [send_mine, pass_on, finish])


def _reduce_exchange(g16, g32):
    n = len(g16)

    def copies(ins, outs, sems):
        x, y, c, _ = _place()
        for i in range(n):
            rh = ins[i].shape[1] // 2
            for r in range(1, 8):
                px, py, pc = x ^ ((r >> 2) & 1), y ^ ((r >> 1) & 1), c ^ (r & 1)
                src = _half(ins[i] if r > 1 else ins[n + i], 2 * px + py, pc, rh)
                dst = outs[2 * i + 1].at[r - 2] if r > 1 else outs[2 * i]
                yield pltpu.make_async_remote_copy(src_ref=src, dst_ref=dst, send_sem=sems[0].at[i, r - 1], recv_sem=sems[1].at[i, r - 1],
                                                   device_id=(px, py, pc), device_id_type=MESH)

    def start(ins, outs, sems):
        for cp in copies(ins, outs, sems):
            cp.start()

    def finish(ins, outs, sems):
        for cp in copies(ins, outs, sems):
            cp.wait()

    out_shape = []
    for g in g16:
        rh, C = g.shape[1] // 2, g.shape[2]
        out_shape += [jax.ShapeDtypeStruct((rh, C), F32), jax.ShapeDtypeStruct((6, rh, C), g.dtype)]
    return _Exchange(list(g16) + list(g32), out_shape, {},
                     [pltpu.SemaphoreType.DMA((n, 7)), pltpu.SemaphoreType.DMA((n, 7))], [start, finish])


def _add_direct(name, g32, from_sibling, from_chips, s_idx, c_idx):
    _, R, C = g32.shape
    rh = R // 2
    tr = _fit(rh, 256)
    nrb = rh // tr

    def kern(s_ref, c_ref, g_ref, a_ref, b_ref, out_ref):
        acc = g_ref[...] + a_ref[...]
        for k in range(6):
            acc = acc + b_ref[k].astype(F32)
        out_ref[...] = acc

    return pl.pallas_call(
        kern, name=name,
        grid_spec=pltpu.PrefetchScalarGridSpec(
            num_scalar_prefetch=2, grid=(nrb,),
            in_specs=[pl.BlockSpec((None, tr, C), lambda r, s_ref, c_ref: (s_ref[0], c_ref[0] * nrb + r, 0)),
                      pl.BlockSpec((tr, C), lambda r, s_ref, c_ref: (r, 0)),
                      pl.BlockSpec((6, tr, C), lambda r, s_ref, c_ref: (0, r, 0))],
            out_specs=pl.BlockSpec((tr, C), lambda r, s_ref, c_ref: (r, 0))),
        out_shape=jax.ShapeDtypeStruct((rh, C), F32),
        compiler_params=_params(("arbitrary",)),
    )(s_idx, c_idx, g32, from_sibling, from_chips)


def _swap_halves(grads):
    n = len(grads)

    def body(*refs):
        ins, outs = refs[:n], refs[n:2 * n]
        send, recv = refs[2 * n:]
        x, y, c, _ = _place()
        copies = []
        for i in range(n):
            rh = ins[i].shape[1] // 2
            cp = pltpu.make_async_remote_copy(
                src_ref=ins[i].at[:, pl.ds(pl.multiple_of((1 - c) * rh, 8), rh), :], dst_ref=outs[i],
                send_sem=send.at[i], recv_sem=recv.at[i], device_id=(x, y, 1 - c), device_id_type=MESH)
            cp.start()
            copies.append(cp)
        for cp in copies:
            cp.wait()

    return pl.pallas_call(
        body, name="grad_swap_halves",
        in_specs=[ANY] * n, out_specs=[ANY] * n,
        out_shape=[jax.ShapeDtypeStruct((N_SHARD, g.shape[1] // 2, g.shape[2]), F32) for g in grads],
        scratch_shapes=[pltpu.SemaphoreType.DMA((n,)), pltpu.SemaphoreType.DMA((n,))],
    )(*grads)


def _add_half(name, g, got, c_idx):
    _, R, C = g.shape
    rh = R // 2
    tr = _fit(rh, 256)
    nrb = rh // tr

    def kern(c_ref, g_ref, r_ref, o32_ref, o16_ref):
        v = g_ref[...] + r_ref[...]
        o32_ref[...] = v
        o16_ref[...] = v.astype(o16_ref.dtype)

    blk = lambda s, r, c_ref: (s, r, 0)
    return pl.pallas_call(
        kern, name=name,
        grid_spec=pltpu.PrefetchScalarGridSpec(
            num_scalar_prefetch=1, grid=(N_SHARD, nrb),
            in_specs=[pl.BlockSpec((None, tr, C), lambda s, r, c_ref: (s, c_ref[0] * nrb + r, 0)), pl.BlockSpec((None, tr, C), blk)],
            out_specs=[pl.BlockSpec((None, tr, C), blk)] * 2),
        out_shape=[jax.ShapeDtypeStruct((N_SHARD, rh, C), F32), jax.ShapeDtypeStruct((N_SHARD, rh, C), MXU_DTYPE)],
        compiler_params=_params(("arbitrary", "arbitrary")),
    )(c_idx, g, got)


def _scatter_partials(parts):
    n = len(parts)

    def body(*refs):
        ins, outs = refs[:n], refs[n:2 * n]
        send, recv = refs[2 * n:]
        x, y, c, chips = _place()
        copies = []
        for i in range(n):
            for j, (px, py) in enumerate(chips):
                cp = pltpu.make_async_remote_copy(
                    src_ref=ins[i].at[2 * px + py], dst_ref=outs[i].at[j],
                    send_sem=send.at[i, j], recv_sem=recv.at[i, j], device_id=(px, py, c), device_id_type=MESH)
                cp.start()
                copies.append(cp)
        for cp in copies:
            cp.wait()

    return pl.pallas_call(
        body, name="grad_scatter_partials",
        in_specs=[ANY] * n, out_specs=[ANY] * n,
        out_shape=[jax.ShapeDtypeStruct((3,) + p.shape[1:], p.dtype) for p in parts],
        scratch_shapes=[pltpu.SemaphoreType.DMA((n, 3)), pltpu.SemaphoreType.DMA((n, 3))],
    )(*parts)


def _add_partials(name, own, got, s_idx):
    _, rh, C = own.shape
    tr = _fit(rh, 256)

    def kern(s_ref, o_ref, a_ref, b_ref, c_ref, out_ref):
        out_ref[...] = ((o_ref[...] + a_ref[...].astype(F32)) + b_ref[...].astype(F32)) + c_ref[...].astype(F32)

    slot = lambda j: pl.BlockSpec((None, tr, C), lambda r, s_ref: (j, r, 0))
    return pl.pallas_call(
        kern, name=name,
        grid_spec=pltpu.PrefetchScalarGridSpec(
            num_scalar_prefetch=1, grid=(rh // tr,),
            in_specs=[pl.BlockSpec((None, tr, C), lambda r, s_ref: (s_ref[0], r, 0)), slot(0), slot(1), slot(2)],
            out_specs=pl.BlockSpec((tr, C), lambda r, s_ref: (r, 0))),
        out_shape=jax.ShapeDtypeStruct((rh, C), F32),
        compiler_params=_params(("arbitrary",)),
    )(s_idx, own, got, got, got)


def _swap_reduced(halves):
    n = len(halves)

    def body(*refs):
        ins, outs = refs[:n], refs[n:2 * n]
        send, recv = refs[2 * n:]
        x, y, c, _ = _place()
        copies = []
        for i in range(n):
            cp = pltpu.make_async_remote_copy(src_ref=ins[i], dst_ref=outs[i], send_sem=send.at[i], recv_sem=recv.at[i],
                                              device_id=(x, y, 1 - c), device_id_type=MESH)
            cp.start()
            copies.append(cp)
        for cp in copies:
            cp.wait()

    return pl.pallas_call(
        body, name="grad_swap_reduced",
        in_specs=[ANY] * n, out_specs=[ANY] * n,
        out_shape=[jax.ShapeDtypeStruct(h.shape, F32) for h in halves],
        scratch_shapes=[pltpu.SemaphoreType.DMA((n,)), pltpu.SemaphoreType.DMA((n,))],
    )(*halves)


def _all_reduce_small(v):
    rows, W = v.shape
    ndev = 8

    def body(v_ref, out_ref, buf, send, recv):
        x, y, c, _ = _place()
        me = 4 * x + 2 * y + c
        buf[me] = v_ref[...]
        copies = []
        for r in range(1, ndev):
            fx, fy, fc = (r >> 2) & 1, (r >> 1) & 1, r & 1
            peer = (x ^ fx, y ^ fy, c ^ fc)
            cp = pltpu.make_async_remote_copy(src_ref=v_ref, dst_ref=buf.at[me], send_sem=send.at[r - 1], recv_sem=recv.at[r - 1],
                                              device_id=peer, device_id_type=MESH)
            cp.start()
            copies.append(cp)
        for cp in copies:
            cp.wait()
        acc = buf[0]
        for k in range(1, ndev):
            acc = acc + buf[k]
        out_ref[...] = acc

    return pl.pallas_call(
        body, name="small_all_reduce",
        in_specs=[pl.BlockSpec(memory_space=pltpu.VMEM)], out_specs=pl.BlockSpec(memory_space=pltpu.VMEM),
        out_shape=jax.ShapeDtypeStruct((rows, W), F32),
        scratch_shapes=[pltpu.VMEM((ndev, rows, W), F32), pltpu.SemaphoreType.DMA((ndev - 1,)), pltpu.SemaphoreType.DMA((ndev - 1,))],
    )(v)


def _adamw_update(gv, w_ref, m_ref, v_ref, d_ref, nm_ref, nv_ref):
    nm = ADAM_B1 * m_ref[...] + (1.0 - ADAM_B1) * gv
    nv = ADAM_B2 * v_ref[...] + (1.0 - ADAM_B2) * (gv * gv)
    m_hat = nm / (1.0 - ADAM_B1 ** ADAM_STEP)
    v_hat = nv / (1.0 - ADAM_B2 ** ADAM_STEP)
    d_ref[...] = -ADAM_LR * (m_hat / (jnp.sqrt(v_hat) + ADAM_EPS) + ADAM_WD * w_ref[...])
    nm_ref[...] = nm
    nv_ref[...] = nv


def _adamw(name, w, g, m, v):
    R, C = w.shape
    tr = _fit(R, 256)

    def body(w_ref, g_ref, m_ref, v_ref, d_ref, nm_ref, nv_ref):
        _adamw_update(g_ref[...], w_ref, m_ref, v_ref, d_ref, nm_ref, nv_ref)

    return _rows_call(name, body, R, tr, [(a, _row(tr, C)) for a in (w, g, m, v)], [(C, F32)] * 3)


def _adamw_halves(name, w, mine, theirs, m, v, c_idx):
    R, C = w.shape
    rh = R // 2
    tr = _fit(rh, 256)
    nrb = rh // tr

    def kern(c_ref, w_ref, a_ref, b_ref, m_ref, v_ref, g_ref, d_ref, nm_ref, nv_ref):
        gv = jnp.where(pl.program_id(0) // nrb == c_ref[0], a_ref[...], b_ref[...])
        g_ref[...] = gv
        _adamw_update(gv, w_ref, m_ref, v_ref, d_ref, nm_ref, nv_ref)

    full = pl.BlockSpec((tr, C), lambda r, c_ref: (r, 0))
    pick = lambda own: pl.BlockSpec((tr, C), lambda r, c_ref: (jnp.where((r // nrb == c_ref[0]) == own, r % nrb, 0), 0))
    return pl.pallas_call(
        kern, name=name,
        grid_spec=pltpu.PrefetchScalarGridSpec(
            num_scalar_prefetch=1, grid=(R // tr,),
            in_specs=[full, pick(True), pick(False), full, full], out_specs=[full] * 4),
        out_shape=[jax.ShapeDtypeStruct((R, C), F32)] * 4,
        compiler_params=_params(("arbitrary",)),
    )(c_idx, w, mine, theirs, m, v)


def _sigmoid(z):
    return 1.0 / (1.0 + jnp.exp(-z))


def kernel(x, p, g_mix, w_in, qn_gain, kn_gain, w_branch_a, w_branch_b, w_out, g_mlp, w_up, w_down, g_ple, w_ple_gate, w_ple_proj, loss_target, m_g_mix, m_w_in, m_qn_gain, m_kn_gain, m_w_branch_a, m_w_branch_b, m_w_out, m_g_mlp, m_w_up, m_w_down, m_g_ple, m_w_ple_gate, m_w_ple_proj, v_g_mix, v_w_in, v_qn_gain, v_kn_gain, v_w_branch_a, v_w_branch_b, v_w_out, v_g_mlp, v_w_up, v_w_down, v_g_ple, v_w_ple_gate, v_w_ple_proj):
    S, D = x.shape[1], x.shape[2]
    HW = w_branch_a.shape[1]
    x2d, tgt, p2d = x.reshape(S, D), loss_target.reshape(S, D), p.reshape(S, p.shape[-1])
    big = {"w_in": w_in, "w_branch_a": w_branch_a, "w_branch_b": w_branch_b, "w_out": w_out, "w_up": w_up,
           "w_down": w_down, "w_ple_gate": w_ple_gate, "w_ple_proj": w_ple_proj}
    moments = {"w_in": (m_w_in, v_w_in), "w_branch_a": (m_w_branch_a, v_w_branch_a), "w_branch_b": (m_w_branch_b, v_w_branch_b),
               "w_out": (m_w_out, v_w_out), "w_up": (m_w_up, v_w_up), "w_down": (m_w_down, v_w_down),
               "w_ple_gate": (m_w_ple_gate, v_w_ple_gate), "w_ple_proj": (m_w_ple_proj, v_w_ple_proj)}
    names = list(big)
    col_sharded = {"w_in", "w_branch_a", "w_branch_b", "w_up", "w_ple_proj"}
    shard2d = {k: w.reshape(w.shape[1], w.shape[2]) for k, w in big.items()}

    c_idx = lax.axis_index("c").astype(jnp.int32).reshape(1)
    s_idx = (2 * lax.axis_index("x") + lax.axis_index("y")).astype(jnp.int32).reshape(1)
    placed = {k: _cast_place(f"cast_{k}", shard2d[k], s_idx) for k in names}
    late = [k for k in names if k != "w_in"]
    W = {"w_in": _run_alone("gather_w_in", _gather_exchange([placed["w_in"]]))[0]}
    cin = W["w_in"].shape[2]
    bn_in = _fit(cin, 512)
    while (2 * HW) % bn_in:
        bn_in -= 128

    h = _rmsnorm_fwd("rmsnorm_mix", x2d, g_mix)
    (qk,) = _matmul("proj_qk", h, W["w_in"], mode="nn", out_dtypes=[F32], b_cshard=True, b_off=0, n_out=2 * HW, bn=bn_in, bk=D)
    (mid,) = _matmul("proj_mid", h, W["w_in"], mode="nn", out_dtypes=[MXU_DTYPE], b_cshard=True, b_off=2 * HW // bn_in,
                     n_out=4 * HW, bn=bn_in, bk=D)
    (sg,) = _matmul("proj_gates", h, W["w_in"], mode="nn", out_dtypes=[F32], b_cshard=True, b_off=6 * HW // bn_in,
                    n_out=2 * D, bn=bn_in, bk=D, epilogue=lambda acc: (_sigmoid(acc),))
    tabs = _rope_tables(S)
    qa, ka = _qknorm_fwd(qk, qn_gain, kn_gain, tabs, HW)
    dil = [_dilated_fwd(qa, ka, mid, d, HW) for d in DILATIONS]
    ya, lse = _dilated_combine([o for o, _ in dil], [l for _, l in dil], HW)
    yb, sb_tot, gathered = _sb_fwd(mid, HW, _gather_exchange([placed[k] for k in late]))
    W.update({k: (g if k in col_sharded else g.reshape(-1, g.shape[2])) for k, g in zip(late, gathered)})

    gate_blocks = D // _fit(D, 1024)
    (ua,) = _matmul("branch_a", ya, W["w_branch_a"], mode="nn", out_dtypes=[F32], b_cshard=True, bn=_fit(W["w_branch_a"].shape[2], 1024))
    bn_b = _fit(W["w_branch_b"].shape[2], 1024)
    ub, merged = _matmul("branch_b_merge", yb, W["w_branch_b"], mode="nn", out_dtypes=[F32, MXU_DTYPE], b_cshard=True, bn=bn_b,
                         extras=[(sg, 0), (sg, D // bn_b), (ua, 0)],
                         epilogue=lambda acc, sga, sgb, uav: (acc, sga * uav + sgb * acc))
    (x1,) = _matmul("out_proj", merged, W["w_out"], mode="nn", out_dtypes=[F32], extras=[(x2d, 0)], epilogue=lambda acc, xv: (xv + acc,))
    hm = _rmsnorm_fwd("rmsnorm_mlp", x1, g_mlp)

    def up_epilogue(acc):
        r = jnp.maximum(acc, 0.0)
        return r * r, r

    act, rup = _matmul("mlp_up", hm, W["w_up"], mode="nn", out_dtypes=[MXU_DTYPE, MXU_DTYPE], b_cshard=True,
                       bn=_fit(W["w_up"].shape[2], 1024), bk=D, epilogue=up_epilogue)
    (x2,) = _matmul("mlp_down", act, W["w_down"], mode="nn", out_dtypes=[F32], extras=[(x1, 0)], epilogue=lambda acc, xv: (xv + acc,))
    hp = _rmsnorm_fwd("rmsnorm_ple", x2, g_ple)
    (pp,) = _matmul("ple_proj", p2d, W["w_ple_proj"], mode="nn", out_dtypes=[F32], b_cshard=True, bn=_fit(W["w_ple_proj"].shape[2], 1024))

    def ple_epilogue(acc, ppv, x2v, tv):
        s = _sigmoid(acc)
        dx3 = ((x2v + ppv * s) - tv) / D
        return dx3, dx3 * s, dx3 * ppv * (s * (1.0 - s))

    dx3, d_pp, d_gate = _matmul("ple_gate_loss", hp, W["w_ple_gate"], mode="nn", out_dtypes=[F32, MXU_DTYPE, MXU_DTYPE],
                                bm=512, extras=[(pp, 0), (x2, 0), (tgt, 0)], epilogue=ple_epilogue)

    G, G16 = {}, {}
    G["w_ple_proj"], G16["w_ple_proj"] = _matmul("grad_w_ple_proj", p2d, d_pp, mode="tn", out_dtypes=[F32, MXU_DTYPE], out_cshard=True,
                                 bn=_fit(d_pp.shape[1] // N_SHARD, 1024))
    G["w_ple_gate"], G16["w_ple_gate"] = _matmul("grad_w_ple_gate", hp, d_gate, mode="tn", out_dtypes=[F32, MXU_DTYPE])
    (d_hp,) = _matmul("ple_gate_bwd", d_gate, W["w_ple_gate"], mode="nt", out_dtypes=[F32])
    dx2, g_g_ple, loss_part = _rmsnorm_bwd("rmsnorm_ple_bwd", d_hp, x2, g_ple, dx3, True)
    G["w_down"], G16["w_down"] = _matmul("grad_w_down", act, dx2, mode="tn", out_dtypes=[F32, MXU_DTYPE])
    (d_up,) = _matmul("mlp_down_bwd", dx2, W["w_down"], mode="nt", out_dtypes=[MXU_DTYPE], extras=[(rup, 0)],
                      epilogue=lambda acc, r: (acc * (2.0 * r.astype(F32)),))
    G["w_up"], G16["w_up"] = _matmul("grad_w_up", hm, d_up, mode="tn", out_dtypes=[F32, MXU_DTYPE], out_cshard=True, bn=_fit(d_up.shape[1] // N_SHARD, 1024))
    (d_hm,) = _matmul("mlp_up_bwd", d_up, W["w_up"], mode="nt", out_dtypes=[F32], b_cshard=True, bk=_fit(W["w_up"].shape[2], 2048))
    dx1, g_g_mlp = _rmsnorm_bwd("rmsnorm_mlp_bwd", d_hm, x1, g_mlp, dx2, False)
    G["w_out"], G16["w_out"] = _matmul("grad_w_out", merged, dx1, mode="tn", out_dtypes=[F32, MXU_DTYPE])

    def merge_bwd(acc, sga, sgb, uav, ubv):
        return acc * sga, acc * sgb, acc * uav * (sga * (1.0 - sga)), acc * ubv * (sgb * (1.0 - sgb))

    bn_m = _fit(D, 1024)
    d_ua, d_ub, d_ga, d_gb = _matmul("out_proj_bwd", dx1, W["w_out"], mode="nt", out_dtypes=[MXU_DTYPE] * 4, bm=512, bn=bn_m,
                                     extras=[(sg, 0), (sg, D // bn_m), (ua, 0), (ub, 0)], epilogue=merge_bwd)
    bn_br = _fit(D // N_SHARD, 1024)
    G["w_branch_a"], G16["w_branch_a"] = _matmul("grad_w_branch_a", ya, d_ua, mode="tn", out_dtypes=[F32, MXU_DTYPE], out_cshard=True, bn=bn_br)
    G["w_branch_b"], G16["w_branch_b"] = _matmul("grad_w_branch_b", yb, d_ub, mode="tn", out_dtypes=[F32, MXU_DTYPE], out_cshard=True, bn=bn_br)
    (d_ya,) = _matmul("branch_a_bwd", d_ua, W["w_branch_a"], mode="nt", out_dtypes=[F32], b_cshard=True, bk=bn_br)
    (d_yb,) = _matmul("branch_b_bwd", d_ub, W["w_branch_b"], mode="nt", out_dtypes=[F32], b_cshard=True, bk=bn_br)

    as_shards = lambda k, g: g if k in col_sharded else g.reshape(N_SHARD, -1, g.shape[1])
    (dqb, dkb, dvb), partials = _sb_bwd(mid, d_yb, sb_tot, HW, _reduce_exchange([as_shards(k, G16[k]) for k in late],
                                                                               [as_shards(k, G[k]) for k in late]))
    dil_b = [_dilated_bwd(qa, ka, mid, d_ya, ya, lse, d, HW) for d in DILATIONS]
    d_qk, g_qn, g_kn = _qknorm_bwd(qk, qn_gain, kn_gain, tabs, [t[0] for t in dil_b], [t[1] for t in dil_b],
                                   [t[2] for t in dil_b], HW)
    dva = _dv_sum([t[3] for t in dil_b], [t[4] for t in dil_b], HW)
    d_proj = jnp.concatenate([d_qk, dva, dqb.astype(MXU_DTYPE), dkb.astype(MXU_DTYPE), dvb.astype(MXU_DTYPE), d_ga, d_gb], axis=1)
    (G["w_in"],) = _matmul("grad_w_in", h, d_proj, mode="tn", out_dtypes=[F32], out_cshard=True, bn=_fit(cin, 1280))
    (d_h,) = _matmul("proj_bwd", d_proj, W["w_in"], mode="nt", out_dtypes=[F32], b_cshard=True, bk=_fit(cin, 1280))
    grad_x, g_g_mix = _rmsnorm_bwd("rmsnorm_mix_bwd", d_h, x2d, g_mix, dx1, False)

    mine = {k: _add_direct(f"grad_add_{k}", as_shards(k, G[k]), partials[2 * n], partials[2 * n + 1], s_idx, c_idx)
            for n, k in enumerate(late)}
    (swapped,) = _swap_halves([G["w_in"]])
    p32, p16 = _add_half("grad_add_half_w_in", G["w_in"], swapped, c_idx)
    (landed,) = _scatter_partials([p16])
    mine["w_in"] = _add_partials("grad_add_partials_w_in", p32, landed, s_idx)
    halves = [mine[k] for k in names]
    others = _swap_reduced(halves)

    pack_w = -(-(3 * D + 3 * 128) // (8 * 128)) * 128

    def pack(v_mix, v_mlp, v_ple, v_qn, v_kn, extra):
        flat = jnp.concatenate([v_mix.reshape(-1), v_mlp.reshape(-1), v_ple.reshape(-1), v_qn.reshape(-1), v_kn.reshape(-1), extra.reshape(-1)])
        return jnp.pad(flat, (0, 8 * pack_w - flat.shape[0])).reshape(8, pack_w)

    def unpack(blk):
        flat = blk.reshape(-1)
        return (flat[:D].reshape(1, D), flat[D:2 * D].reshape(1, D), flat[2 * D:3 * D].reshape(1, D),
                flat[3 * D:3 * D + 128].reshape(1, 128), flat[3 * D + 128:3 * D + 256].reshape(1, 128), flat[3 * D + 256])

    small = _all_reduce_small(pack(g_g_mix, g_g_mlp, g_g_ple, g_qn, g_kn, loss_part))
    sw = pack(g_mix, g_mlp, g_ple, qn_gain, kn_gain, jnp.zeros((128,), F32))
    sm = pack(m_g_mix, m_g_mlp, m_g_ple, m_qn_gain, m_kn_gain, jnp.zeros((128,), F32))
    sv = pack(v_g_mix, v_g_mlp, v_g_ple, v_qn_gain, v_kn_gain, jnp.ones((128,), F32))
    s_delta, s_nm, s_nv = _adamw("adamw_small", sw, small, sm, sv)
    sg_mix, sg_mlp, sg_ple, sg_qn, sg_kn, loss = unpack(small)
    small_out = {}
    for tag, blk in (("delta", s_delta), ("new_m", s_nm), ("new_v", s_nv)):
        u = unpack(blk)
        small_out[tag] = dict(g_mix=u[0], g_mlp=u[1], g_ple=u[2], qn_gain=u[3], kn_gain=u[4])
    small_grad = dict(g_mix=sg_mix, g_mlp=sg_mlp, g_ple=sg_ple, qn_gain=sg_qn, kn_gain=sg_kn)

    big_out = {"grad": {}, "delta": {}, "new_m": {}, "new_v": {}}
    for k, mine, theirs in zip(names, halves, others):
        shape = big[k].shape
        m2, v2 = (t.reshape(shape[1], shape[2]) for t in moments[k])
        res = _adamw_halves(f"adamw_{k}", shard2d[k], mine, theirs, m2, v2, c_idx)
        for tag, t in zip(("grad", "delta", "new_m", "new_v"), res):
            big_out[tag][k] = t.reshape(shape)

    order = ["g_mix", "w_in", "qn_gain", "kn_gain", "w_branch_a", "w_branch_b", "w_out", "g_mlp", "w_up", "w_down", "g_ple",
             "w_ple_gate", "w_ple_proj"]
    outs = [loss, grad_x.reshape(x.shape)]
    outs += [small_grad[k] if k in small_grad else big_out["grad"][k] for k in order]
    for tag in ("delta", "new_m", "new_v"):
        outs += [small_out[tag][k] if k in small_grad else big_out[tag][k] for k in order]
    return tuple(outs)
```

```python
import functools

import jax
import jax.numpy as jnp
from jax import lax
from jax.experimental import pallas as pl
from jax.experimental.pallas import tpu as pltpu

F32 = jnp.float32
MXU_DTYPE = jnp.bfloat16
HEAD_DIM = 128
ROT_DIM = HEAD_DIM // 4
ROPE_THETA = 500000.0
EPS = 1e-6
DILATIONS = (1, 4, 16)
BLOCK = 128
N_SHARD = 4
ADAM_LR, ADAM_B1, ADAM_B2, ADAM_EPS, ADAM_WD, ADAM_STEP = 0.001, 0.9, 0.999, 1e-08, 0.01, 10
V7X_VMEM_BYTES = 64 * 1024 * 1024
VMEM_LIMIT = V7X_VMEM_BYTES - 8 * 1024 * 1024
MESH = pl.DeviceIdType.MESH
NEG = -1e30
SB_BQ, SB_BK = 512, 256
CUMSUM_PASSES = 2


def _fit(dim, pref):
    if dim <= pref:
        return dim
    b = (pref // 128) * 128
    while dim % b:
        b -= 128
    return b


def _params(sem=None):
    return pltpu.CompilerParams(dimension_semantics=sem, vmem_limit_bytes=VMEM_LIMIT)


def _dot(a, b, dims=(((1,), (0,)), ((), ()))):
    return lax.dot_general(a, b, dims, preferred_element_type=F32)


NT = (((1,), (1,)), ((), ()))
TN = (((0,), (0,)), ((), ()))


def _split_dot(x, u, passes):
    out = None
    r = x
    for p in range(passes):
        hi = r.astype(MXU_DTYPE)
        part = _dot(hi, u)
        out = part if out is None else out + part
        if p + 1 < passes:
            r = r - hi.astype(F32)
    return out


def _matmul(name, a, b, *, mode, out_dtypes, bm=1024, bn=1024, bk=2048, b_cshard=False, b_off=0, n_out=None,
            extras=(), epilogue=None, out_cshard=False):
    if mode == "tn":
        K, M = a.shape
        N = b.shape[1]
    else:
        M, K = a.shape
        if mode == "nn":
            N = n_out if n_out is not None else (N_SHARD * b.shape[2] if b_cshard else b.shape[1])
        else:
            N = b.shape[1] if b_cshard else b.shape[0]
    bm, bn, bk = _fit(M, bm), _fit(N, bn), _fit(K, bk)
    nk = K // bk
    grid = (M // bm, N // bn, nk)

    if mode == "tn":
        a_spec = pl.BlockSpec((bk, bm), lambda i, j, k: (k, i))
        b_spec = pl.BlockSpec((bk, bn), lambda i, j, k: (k, j))
        dims = TN
    elif mode == "nn":
        a_spec = pl.BlockSpec((bm, bk), lambda i, j, k: (i, k))
        if b_cshard:
            cb = b.shape[2] // bn
            b_spec = pl.BlockSpec((None, bk, bn), lambda i, j, k: ((j + b_off) // cb, k, (j + b_off) % cb))
        else:
            b_spec = pl.BlockSpec((bk, bn), lambda i, j, k: (k, j + b_off))
        dims = (((1,), (0,)), ((), ()))
    else:
        a_spec = pl.BlockSpec((bm, bk), lambda i, j, k: (i, k))
        if b_cshard:
            cb = b.shape[2] // bk
            b_spec = pl.BlockSpec((None, bn, bk), lambda i, j, k: (k // cb, j, k % cb))
        else:
            b_spec = pl.BlockSpec((bn, bk), lambda i, j, k: (j, k))
        dims = NT

    ex_arrays = [e[0] for e in extras]
    ex_specs = [pl.BlockSpec((bm, bn), functools.partial(lambda i, j, k, off: (i, j + off), off=e[1])) for e in extras]
    if out_cshard:
        cbo = (N // N_SHARD) // bn
        out_shape = [jax.ShapeDtypeStruct((N_SHARD, M, N // N_SHARD), dt) for dt in out_dtypes]
        out_specs = [pl.BlockSpec((None, bm, bn), lambda i, j, k: (j // cbo, i, j % cbo)) for _ in out_dtypes]
    else:
        out_shape = [jax.ShapeDtypeStruct((M, N), dt) for dt in out_dtypes]
        out_specs = [pl.BlockSpec((bm, bn), lambda i, j, k: (i, j)) for _ in out_dtypes]
    ne, no = len(extras), len(out_dtypes)

    def kern(*refs):
        a_ref, b_ref = refs[0], refs[1]
        ex_refs = refs[2:2 + ne]
        o_refs = refs[2 + ne:2 + ne + no]
        part = _dot(a_ref[...].astype(MXU_DTYPE), b_ref[...].astype(MXU_DTYPE), dims)

        def finish(acc):
            vals = (acc,) * no if epilogue is None else epilogue(acc, *[r[...] for r in ex_refs])
            for r, v in zip(o_refs, vals):
                r[...] = v.astype(r.dtype)

        if nk == 1:
            finish(part)
        else:
            acc_ref = refs[2 + ne + no]
            k = pl.program_id(2)

            @pl.when(k == 0)
            def _():
                acc_ref[...] = part

            @pl.when(k > 0)
            def _():
                acc_ref[...] += part

            @pl.when(k == nk - 1)
            def _():
                finish(acc_ref[...])

    outs = pl.pallas_call(
        kern, name=name, grid=grid,
        in_specs=[a_spec, b_spec] + ex_specs, out_specs=out_specs, out_shape=out_shape,
        scratch_shapes=[pltpu.VMEM((bm, bn), F32)] if nk > 1 else [],
        compiler_params=_params(("parallel", "parallel", "arbitrary")),
    )(a, b, *ex_arrays)
    return outs


def _row(tr, w, coff=0):
    return pl.BlockSpec((tr, w), lambda i: (i, coff))


def _vec(w):
    return pl.BlockSpec((1, w), lambda i: (0, 0))


def _rows_call(name, body, n_rows, tr, ins, outs, accs=()):
    n_in, n_out = len(ins), len(outs)

    def kern(*refs):
        acc_refs = refs[n_in + n_out:]
        if acc_refs:
            @pl.when(pl.program_id(0) == 0)
            def _():
                for r in acc_refs:
                    r[...] = jnp.zeros_like(r)
        body(*refs)

    out_shape = [jax.ShapeDtypeStruct((n_rows, w), dt) for w, dt in outs] + [jax.ShapeDtypeStruct((1, w), F32) for w in accs]
    out_specs = [_row(tr, w) for w, _ in outs] + [_vec(w) for w in accs]
    return pl.pallas_call(
        kern, name=name, grid=(n_rows // tr,),
        in_specs=[s for _, s in ins], out_specs=out_specs, out_shape=out_shape,
        compiler_params=_params(("arbitrary",)),
    )(*[a for a, _ in ins])


def _rmsnorm_fwd(name, x, g):
    S, D = x.shape
    tr = _fit(S, 256)

    def body(x_ref, g_ref, h_ref):
        xv = x_ref[...]
        r = lax.rsqrt(jnp.mean(xv * xv, axis=1, keepdims=True) + EPS)
        h_ref[...] = ((xv * r) * g_ref[...]).astype(h_ref.dtype)

    return _rows_call(name, body, S, tr, [(x, _row(tr, D)), (g, _vec(D))], [(D, MXU_DTYPE)])[0]


def _rmsnorm_bwd(name, dh, x, g, resid, with_loss):
    S, D = x.shape
    tr = _fit(S, 256)

    def body(dh_ref, x_ref, g_ref, res_ref, dx_ref, dg_ref, *loss_ref):
        xv = x_ref[...]
        r = lax.rsqrt(jnp.mean(xv * xv, axis=1, keepdims=True) + EPS)
        dhv = dh_ref[...]
        u = dhv * g_ref[...]
        xr = xv * r
        dx = r * u - xr * (r * r) * jnp.mean(xv * u, axis=1, keepdims=True)
        resv = res_ref[...]
        dx_ref[...] = resv + dx
        dg_ref[...] += jnp.sum(dhv * xr, axis=0, keepdims=True)
        if with_loss:
            loss_ref[0][...] += (0.5 * D) * jnp.sum(resv * resv)

    outs = _rows_call(name, body, S, tr, [(dh, _row(tr, D)), (x, _row(tr, D)), (g, _vec(D)), (resid, _row(tr, D))],
                      [(D, F32)], accs=(D, 128) if with_loss else (D,))
    return outs


def _rope_tables(S):
    half = ROT_DIM // 2
    pos = jnp.arange(S, dtype=F32)
    inv = ROPE_THETA ** (-jnp.arange(0, ROT_DIM, 2, dtype=F32) / ROT_DIM)
    ang = pos[:, None] * inv[None, :]
    cos, sin = jnp.cos(ang), jnp.sin(ang)
    pad = HEAD_DIM - ROT_DIM
    ctab = jnp.concatenate([cos, cos, jnp.ones((S, pad), F32)], axis=1)
    atab = jnp.concatenate([-sin, jnp.zeros((S, pad + half), F32)], axis=1)
    btab = jnp.concatenate([jnp.zeros((S, half), F32), sin, jnp.zeros((S, pad), F32)], axis=1)
    return ctab, atab, btab


def _qknorm_fwd(qk, qn, kn, tabs, HW):
    S = qk.shape[0]
    tr = _fit(S, 256)
    half = ROT_DIM // 2

    def body(qk_ref, qn_ref, kn_ref, c_ref, a_ref, b_ref, q_out, k_out):
        ct, at, bt = c_ref[...], a_ref[...], b_ref[...]
        for part, (g_ref, o_ref) in enumerate(((qn_ref, q_out), (kn_ref, k_out))):
            gv = g_ref[...]
            for h in range(HW // HEAD_DIM):
                xh = qk_ref[:, part * HW + h * HEAD_DIM: part * HW + (h + 1) * HEAD_DIM]
                r = lax.rsqrt(jnp.mean(xh * xh, axis=1, keepdims=True) + EPS)
                y = (xh * r) * gv
                o = y * ct + pltpu.roll(y, HEAD_DIM - half, 1) * at + pltpu.roll(y, half, 1) * bt
                o_ref[:, h * HEAD_DIM:(h + 1) * HEAD_DIM] = o.astype(o_ref.dtype)

    ins = [(qk, _row(tr, 2 * HW)), (qn, _vec(HEAD_DIM)), (kn, _vec(HEAD_DIM))] + [(t, _row(tr, HEAD_DIM)) for t in tabs]
    return _rows_call("qknorm_fwd", body, S, tr, ins, [(HW, MXU_DTYPE), (HW, MXU_DTYPE)])


def _shift_spec(tr, w, shift, nblk):
    return pl.BlockSpec((tr, w), lambda i: (jnp.minimum(i + shift, nblk - 1), 0))


def _qknorm_bwd(qk, qn, kn, tabs, dq_parts, dk_cur, dk_prev, HW):
    S = qk.shape[0]
    tr = BLOCK
    nblk = S // tr
    half = ROT_DIM // 2
    nd = len(DILATIONS)

    def body(*refs):
        qk_ref, qn_ref, kn_ref, c_ref, a_ref, b_ref = refs[:6]
        dq_refs = refs[6:6 + nd]
        dkc_refs = refs[6 + nd:6 + 2 * nd]
        dkp_refs = refs[6 + 2 * nd:6 + 3 * nd]
        d_out, dqn_ref, dkn_ref = refs[6 + 3 * nd:]
        i = pl.program_id(0)
        ct, at, bt = c_ref[...], a_ref[...], b_ref[...]
        live = [(i + d < nblk).astype(F32) for d in DILATIONS]
        for part, (g_ref, dg_ref) in enumerate(((qn_ref, dqn_ref), (kn_ref, dkn_ref))):
            gv = g_ref[...]
            dg = jnp.zeros((1, HEAD_DIM), F32)
            for h in range(HW // HEAD_DIM):
                hs = slice(h * HEAD_DIM, (h + 1) * HEAD_DIM)
                if part == 0:
                    do = dq_refs[0][:, hs] + dq_refs[1][:, hs] + dq_refs[2][:, hs]
                else:
                    do = dkc_refs[0][:, hs] + dkc_refs[1][:, hs] + dkc_refs[2][:, hs]
                    for n in range(nd):
                        do = do + dkp_refs[n][:, hs] * live[n]
                dy = do * ct + pltpu.roll(do * at, half, 1) + pltpu.roll(do * bt, HEAD_DIM - half, 1)
                xh = qk_ref[:, part * HW + h * HEAD_DIM: part * HW + (h + 1) * HEAD_DIM]
                r = lax.rsqrt(jnp.mean(xh * xh, axis=1, keepdims=True) + EPS)
                xr = xh * r
                u = dy * gv
                dx = r * u - xr * (r * r) * jnp.mean(xh * u, axis=1, keepdims=True)
                d_out[:, part * HW + h * HEAD_DIM: part * HW + (h + 1) * HEAD_DIM] = dx.astype(d_out.dtype)
                dg = dg + jnp.sum(dy * xr, axis=0, keepdims=True)
            dg_ref[...] += dg

    ins = [(qk, _row(tr, 2 * HW)), (qn, _vec(HEAD_DIM)), (kn, _vec(HEAD_DIM))] + [(t, _row(tr, HEAD_DIM)) for t in tabs]
    ins += [(a, _row(tr, HW)) for a in dq_parts] + [(a, _row(tr, HW)) for a in dk_cur]
    ins += [(a, _shift_spec(tr, HW, d, nblk)) for a, d in zip(dk_prev, DILATIONS)]
    return _rows_call("qknorm_bwd", body, S, tr, ins, [(2 * HW, MXU_DTYPE)], accs=(HEAD_DIM, HEAD_DIM))


def _dv_sum(dv_cur, dv_prev, HW):
    S = dv_cur[0].shape[0]
    tr = BLOCK
    nblk = S // tr
    nd = len(DILATIONS)

    def body(*refs):
        i = pl.program_id(0)
        out = refs[2 * nd]
        acc = refs[0][...] + refs[1][...] + refs[2][...]
        for n, d in enumerate(DILATIONS):
            acc = acc + refs[nd + n][...] * (i + d < nblk).astype(F32)
        out[...] = acc.astype(out.dtype)

    ins = [(a, _row(tr, HW)) for a in dv_cur] + [(a, _shift_spec(tr, HW, d, nblk)) for a, d in zip(dv_prev, DILATIONS)]
    return _rows_call("dilated_dv_sum", body, S, tr, ins, [(HW, MXU_DTYPE)])[0]


def _dil_geometry(d, HW):
    H = HW // HEAD_DIM
    hb = min(H, max(1, 8 // d))
    tb, w = BLOCK * d, hb * HEAD_DIM
    cur = pl.BlockSpec((tb, w), lambda n, g: (n, g))
    prev = pl.BlockSpec((tb, w), lambda n, g: (jnp.maximum(n - 1, 0), g))
    units = [(hh, r) for hh in range(hb) for r in range(d)]
    return H // hb, hb, tb, cur, prev, units


def _dil_mask(n):
    qi = lax.broadcasted_iota(jnp.int32, (BLOCK, 2 * BLOCK), 0)
    ki = lax.broadcasted_iota(jnp.int32, (BLOCK, 2 * BLOCK), 1)
    return (ki >= qi) & (ki <= qi + BLOCK) & ((ki >= BLOCK) | (n > 0))


def _dil_stage(ref, buf, row0=0):
    for hh in range(buf.shape[0]):
        buf[hh, row0:row0 + ref.shape[0], :] = ref[:, hh * HEAD_DIM:(hh + 1) * HEAD_DIM].astype(F32)


def _dil_unstage(buf, ref):
    for hh in range(buf.shape[0]):
        ref[:, hh * HEAD_DIM:(hh + 1) * HEAD_DIM] = buf[hh]


def _dil_rows(d, r, size):
    return pl.ds(0, size) if d == 1 else pl.ds(r, size, stride=d)


def _dil_operands(d, tb, units, q_ref, kc_ref, kp_ref, vc_ref, vp_ref, qs, kf, vf):
    _dil_stage(q_ref, qs)
    _dil_stage(kp_ref, kf)
    _dil_stage(kc_ref, kf, tb)
    _dil_stage(vp_ref, vf)
    _dil_stage(vc_ref, vf, tb)
    qu = [qs[hh, _dil_rows(d, r, BLOCK), :].astype(MXU_DTYPE) for hh, r in units]
    ku = [kf[hh, _dil_rows(d, r, 2 * BLOCK), :].astype(MXU_DTYPE) for hh, r in units]
    vu = [vf[hh, _dil_rows(d, r, 2 * BLOCK), :].astype(MXU_DTYPE) for hh, r in units]
    return qu, ku, vu


def _dilated_fwd(q, k, mid, d, HW):
    S = q.shape[0]
    scale = HEAD_DIM ** -0.5
    ng, hb, tb, cur, prev, units = _dil_geometry(d, HW)

    def kern(q_ref, kc_ref, kp_ref, vc_ref, vp_ref, o_ref, l_ref, qs, kf, vf, os_, ls):
        mask = _dil_mask(pl.program_id(0))
        qu, ku, vu = _dil_operands(d, tb, units, q_ref, kc_ref, kp_ref, vc_ref, vp_ref, qs, kf, vf)
        sc = [jnp.where(mask, _dot(a, b, NT) * scale, NEG) for a, b in zip(qu, ku)]
        mx = [jnp.max(t, axis=1, keepdims=True) for t in sc]
        ex = [jnp.exp(t - m) for t, m in zip(sc, mx)]
        den = [jnp.sum(t, axis=1, keepdims=True) for t in ex]
        out = [_dot(t.astype(MXU_DTYPE), v) / dn for t, v, dn in zip(ex, vu, den)]
        for (hh, r), o, m, dn in zip(units, out, mx, den):
            os_[hh, _dil_rows(d, r, BLOCK), :] = o
            ls[hh, _dil_rows(d, r, BLOCK), :] = jnp.broadcast_to(m + jnp.log(dn), (BLOCK, HEAD_DIM))
        _dil_unstage(os_, o_ref)
        _dil_unstage(ls, l_ref)

    return pl.pallas_call(
        kern, name=f"dilated_fwd_d{d}", grid=(S // tb, ng),
        in_specs=[cur, cur, prev, cur, prev],
        out_specs=[cur, cur],
        out_shape=[jax.ShapeDtypeStruct((S, HW), F32)] * 2,
        scratch_shapes=[pltpu.VMEM((hb, tb, HEAD_DIM), F32)] + [pltpu.VMEM((hb, 2 * tb, HEAD_DIM), F32)] * 2
        + [pltpu.VMEM((hb, tb, HEAD_DIM), F32)] * 2,
        compiler_params=_params(("arbitrary", "arbitrary")),
    )(q, k, k, mid, mid)


def _dilated_combine(os_, lses, HW):
    S = os_[0].shape[0]
    tr = _fit(S, 256)

    def body(o0, o1, o2, l0, l1, l2, ya_ref, lse_ref):
        a, b, c = l0[...], l1[...], l2[...]
        mx = jnp.maximum(jnp.maximum(a, b), c)
        ea, eb, ec = jnp.exp(a - mx), jnp.exp(b - mx), jnp.exp(c - mx)
        tot = ea + eb + ec
        ya_ref[...] = (ea * o0[...] + eb * o1[...] + ec * o2[...]) / tot
        lse_ref[...] = mx + jnp.log(tot)

    ins = [(a, _row(tr, HW)) for a in list(os_) + list(lses)]
    return _rows_call("dilated_combine", body, S, tr, ins, [(HW, F32), (HW, F32)])


def _dilated_bwd(q, k, mid, dya, ya, lse, d, HW):
    S = q.shape[0]
    scale = HEAD_DIM ** -0.5
    ng, hb, tb, cur, prev, units = _dil_geometry(d, HW)

    def kern(q_ref, kc_ref, kp_ref, vc_ref, vp_ref, dy_ref, y_ref, l_ref, dq_ref, dkc_ref, dkp_ref, dvc_ref, dvp_ref,
             qs, kf, vf, dys, ys, ls, dqs, dkcs, dkps, dvcs, dvps):
        mask = _dil_mask(pl.program_id(0))
        qu, ku, vu = _dil_operands(d, tb, units, q_ref, kc_ref, kp_ref, vc_ref, vp_ref, qs, kf, vf)
        _dil_stage(dy_ref, dys)
        _dil_stage(y_ref, ys)
        _dil_stage(l_ref, ls)
        dy = [dys[hh, _dil_rows(d, r, BLOCK), :] for hh, r in units]
        lt = [ls[hh, _dil_rows(d, r, BLOCK), :][:, 0:1] for hh, r in units]
        delta = [jnp.sum(t * ys[hh, _dil_rows(d, r, BLOCK), :], axis=1, keepdims=True) for t, (hh, r) in zip(dy, units)]
        dyb = [t.astype(MXU_DTYPE) for t in dy]
        p = [jnp.where(mask, jnp.exp(_dot(a, b, NT) * scale - l), 0.0) for a, b, l in zip(qu, ku, lt)]
        ds = [(t * (_dot(g, v, NT) - dl) * scale).astype(MXU_DTYPE) for t, g, v, dl in zip(p, dyb, vu, delta)]
        dq = [_dot(t, b) for t, b in zip(ds, ku)]
        dk = [_dot(t, a, TN) for t, a in zip(ds, qu)]
        dv = [_dot(t.astype(MXU_DTYPE), g, TN) for t, g in zip(p, dyb)]
        for (hh, r), tq, tk, tv in zip(units, dq, dk, dv):
            at = _dil_rows(d, r, BLOCK)
            dqs[hh, at, :] = tq
            dkps[hh, at, :] = tk[0:BLOCK]
            dkcs[hh, at, :] = tk[BLOCK:2 * BLOCK]
            dvps[hh, at, :] = tv[0:BLOCK]
            dvcs[hh, at, :] = tv[BLOCK:2 * BLOCK]
        for buf, ref in ((dqs, dq_ref), (dkcs, dkc_ref), (dkps, dkp_ref), (dvcs, dvc_ref), (dvps, dvp_ref)):
            _dil_unstage(buf, ref)

    return pl.pallas_call(
        kern, name=f"dilated_bwd_d{d}", grid=(S // tb, ng),
        in_specs=[cur, cur, prev, cur, prev, cur, cur, cur],
        out_specs=[cur] * 5,
        out_shape=[jax.ShapeDtypeStruct((S, HW), F32)] * 5,
        scratch_shapes=[pltpu.VMEM((hb, tb, HEAD_DIM), F32)] + [pltpu.VMEM((hb, 2 * tb, HEAD_DIM), F32)] * 2
        + [pltpu.VMEM((hb, tb, HEAD_DIM), F32)] * 8,
        compiler_params=_params(("arbitrary", "arbitrary")),
    )(q, k, k, mid, mid, dya, ya, lse)


def _softplus_parts(z):
    lg = jnp.log(1.0 + jnp.exp(-jnp.abs(z)))
    return -jnp.maximum(z, 0.0) - lg, jnp.minimum(z, 0.0) - lg


def _sb_specs(S, H):
    q_spec = pl.BlockSpec((SB_BQ, HEAD_DIM), lambda h, i: (i, H + h))
    k_spec = pl.BlockSpec((S, HEAD_DIM), lambda h, i: (0, 2 * H + h))
    v_spec = pl.BlockSpec((S, HEAD_DIM), lambda h, i: (0, 3 * H + h))
    o_spec = pl.BlockSpec((SB_BQ, HEAD_DIM), lambda h, i: (i, h))
    return q_spec, k_spec, v_spec, o_spec


def _sb_tri(relation):
    tri_r = lax.broadcasted_iota(jnp.int32, (SB_BK, SB_BK), 0)
    tri_c = lax.broadcasted_iota(jnp.int32, (SB_BK, SB_BK), 1)
    return relation(tri_r, tri_c).astype(MXU_DTYPE)


def _sb_scratch():
    return [pltpu.VMEM((2, SB_BQ, SB_BQ), MXU_DTYPE), pltpu.VMEM((2, SB_BQ, SB_BQ), MXU_DTYPE), pltpu.SemaphoreType.DMA((2, 2))]


def _sb_fwd(mid, HW, ex=None):
    S = mid.shape[0]
    H = HW // HEAD_DIM
    BQ, CH = SB_BQ, SB_BK
    NC, nq = BQ // CH, S // BQ
    scale = HEAD_DIM ** -0.5
    q_spec, k_spec, v_spec, o_spec = _sb_specs(S, H)

    def kern(q_ref, k_ref, v_ref, o_ref, a_hbm, s_hbm, abuf, sbuf, sems):
        h, i = pl.program_id(0), pl.program_id(1)
        q = q_ref[...]
        upper = _sb_tri(lambda r, c: r > c)
        row = lax.broadcasted_iota(jnp.int32, (BQ, BQ), 0)
        causal = lax.broadcasted_iota(jnp.int32, (BQ, BQ), 1) < row

        def save(slot, j):
            return [pltpu.make_async_copy(buf.at[slot], hbm.at[h, i, j], sems.at[w, slot])
                    for w, (buf, hbm) in enumerate(((abuf, a_hbm), (sbuf, s_hbm)))]

        def block(n, run, acc, masked):
            j = i - n
            ks = pl.multiple_of(j * BQ, BQ)
            z = _dot(q, k_ref[pl.ds(ks, BQ), :], NT) * scale
            m, l = _softplus_parts(z)
            if masked:
                m = jnp.where(causal, m, 0.0)
            parts = []
            for c in reversed(range(NC)):
                mc = m[:, c * CH:(c + 1) * CH]
                parts.append(l[:, c * CH:(c + 1) * CH] + (_split_dot(mc, upper, CUMSUM_PASSES) + run))
                run = run + jnp.sum(mc, axis=1, keepdims=True)
            a = jnp.exp(jnp.concatenate(parts[::-1], axis=1))
            sig = jnp.exp(l)
            if masked:
                a = jnp.where(causal, a, 0.0)
                sig = jnp.where(causal, sig, 0.0)
            ab = a.astype(MXU_DTYPE)
            slot = n % 2
            if not masked:
                @pl.when(n >= 2)
                def _():
                    for cp in save(slot, j):
                        cp.wait()
            abuf[slot] = ab
            sbuf[slot] = sig.astype(MXU_DTYPE)
            for cp in save(slot, j):
                cp.start()
            return run, acc + _dot(ab, v_ref[pl.ds(ks, BQ), :])

        run, acc = block(0, jnp.zeros((BQ, 1), F32), jnp.zeros((BQ, HEAD_DIM), F32), True)
        run, acc = lax.fori_loop(1, i + 1, lambda n, carry: block(n, carry[0], carry[1], False), (run, acc))
        o_ref[...] = acc
        for cp in save(0, i):
            cp.wait()

        @pl.when(i >= 1)
        def _():
            for cp in save(1, i):
                cp.wait()

    tiles = jax.ShapeDtypeStruct((H, nq, nq, BQ, BQ), MXU_DTYPE)
    (o, a_t, s_t), rode = _host_call(kern, "stickbreak_fwd", (H, nq), [q_spec, k_spec, v_spec], [o_spec, ANY, ANY],
                                     [jax.ShapeDtypeStruct((S, HW), F32), tiles, tiles], (mid, mid, mid), ex, _sb_scratch())
    return o, a_t, s_t, rode


def _sb_bwd(mid, dyb, a_t, s_t, HW, ex=None):
    S = mid.shape[0]
    H = HW // HEAD_DIM
    BQ, CH = SB_BQ, SB_BK
    NC = BQ // CH
    scale = HEAD_DIM ** -0.5
    q_spec, k_spec, v_spec, o_spec = _sb_specs(S, H)
    full = pl.BlockSpec((S, HEAD_DIM), lambda h, i: (0, h))

    def kern(q_ref, k_ref, v_ref, do_ref, a_hbm, s_hbm, dq_ref, dk_ref, dv_ref, abuf, sbuf, sems):
        h, i = pl.program_id(0), pl.program_id(1)

        @pl.when(i == 0)
        def _():
            dk_ref[...] = jnp.zeros_like(dk_ref)
            dv_ref[...] = jnp.zeros_like(dv_ref)

        q = q_ref[...]
        do = do_ref[...].astype(MXU_DTYPE)
        excl = _sb_tri(lambda r, c: r < c)

        def fetch(slot, j):
            return [pltpu.make_async_copy(hbm.at[h, i, j], buf.at[slot], sems.at[w, slot])
                    for w, (buf, hbm) in enumerate(((abuf, a_hbm), (sbuf, s_hbm)))]

        for cp in fetch(0, 0):
            cp.start()

        def block(j, carry):
            prun, dq = carry
            slot = j % 2
            for cp in fetch(slot, j):
                cp.wait()

            @pl.when(j < i)
            def _():
                for cp in fetch(1 - slot, j + 1):
                    cp.start()

            ks = pl.multiple_of(j * BQ, BQ)
            k = k_ref[pl.ds(ks, BQ), :]
            ab = abuf[slot]
            p = ab.astype(F32) * _dot(do, v_ref[pl.ds(ks, BQ), :], NT)
            parts = []
            for c in range(NC):
                pc = p[:, c * CH:(c + 1) * CH]
                parts.append(_split_dot(pc, excl, CUMSUM_PASSES) + prun)
                prun = prun + jnp.sum(pc, axis=1, keepdims=True)
            before = jnp.concatenate(parts, axis=1)
            dzb = ((p - sbuf[slot].astype(F32) * (p + before)) * scale).astype(MXU_DTYPE)
            dk_ref[pl.ds(ks, BQ), :] += _dot(dzb, q, TN)
            dv_ref[pl.ds(ks, BQ), :] += _dot(ab, do, TN)
            return prun, dq + _dot(dzb, k)

        _, dq = lax.fori_loop(0, i + 1, block, (jnp.zeros((BQ, 1), F32), jnp.zeros((BQ, HEAD_DIM), F32)))
        dq_ref[...] = dq

    grads, rode = _host_call(kern, "stickbreak_bwd", (H, S // BQ), [q_spec, k_spec, v_spec, o_spec, ANY, ANY], [o_spec, full, full],
                             [jax.ShapeDtypeStruct((S, HW), F32)] * 3, (mid, mid, mid, dyb, a_t, s_t), ex, _sb_scratch())
    return grads, rode


ANY = pl.BlockSpec(memory_space=pl.ANY)


def _place():
    x, y, c = lax.axis_index("x"), lax.axis_index("y"), lax.axis_index("c")
    chips = [(1 - x, y), (x, 1 - y), (1 - x, 1 - y)]
    return x, y, c, chips


def _half(ref, shard, hc, rh):
    return ref.at[shard, pl.ds(pl.multiple_of(hc * rh, 8), rh), :]


def _cast_place(name, w, s_idx):
    R, C = w.shape
    tr = _fit(R, 256)

    def kern(s_ref, w_ref, o_ref):
        o_ref[...] = w_ref[...].astype(o_ref.dtype)

    return pl.pallas_call(
        kern, name=name,
        grid_spec=pltpu.PrefetchScalarGridSpec(
            num_scalar_prefetch=1, grid=(R // tr,),
            in_specs=[pl.BlockSpec((tr, C), lambda r, s_ref: (r, 0))],
            out_specs=pl.BlockSpec((None, tr, C), lambda r, s_ref: (s_ref[0], r, 0))),
        out_shape=jax.ShapeDtypeStruct((N_SHARD, R, C), MXU_DTYPE),
        compiler_params=_params(("arbitrary",)),
    )(s_idx, w)


class _Exchange:
    def __init__(self, inputs, out_shape, aliases, scratch, phases):
        self.inputs, self.out_shape, self.aliases, self.scratch, self.phases = inputs, out_shape, aliases, scratch, phases


def _run_alone(name, ex):
    ni, no = len(ex.inputs), len(ex.out_shape)

    def body(*refs):
        for phase in ex.phases:
            phase(refs[:ni], refs[ni:ni + no], refs[ni + no:])

    return pl.pallas_call(
        body, name=name, in_specs=[ANY] * ni, out_specs=[ANY] * no, out_shape=ex.out_shape,
        input_output_aliases=ex.aliases, scratch_shapes=ex.scratch,
    )(*ex.inputs)


def _host_call(kern, name, grid, in_specs, out_specs, out_shape, operands, ex, scratch=()):
    sem = ("arbitrary", "arbitrary")
    if ex is None:
        return pl.pallas_call(kern, name=name, grid=grid, in_specs=in_specs, out_specs=out_specs, out_shape=out_shape,
                              scratch_shapes=list(scratch), compiler_params=_params(sem))(*operands), []
    n_in, n_out, ri, ro, ns = len(in_specs), len(out_specs), len(ex.inputs), len(ex.out_shape), len(scratch)
    nsteps, nph = grid[0] * grid[1], len(ex.phases)

    def body(*refs):
        r_in, r_out = refs[n_in:n_in + ri], refs[n_in + ri + n_out:n_in + ri + n_out + ro]
        host_scratch, ex_scratch = refs[n_in + ri + n_out + ro:][:ns], refs[n_in + ri + n_out + ro + ns:]
        step = pl.program_id(0) * grid[1] + pl.program_id(1)
        for kph, phase in enumerate(ex.phases):
            pl.when(step == (kph * (nsteps - 1)) // (nph - 1))(functools.partial(phase, r_in, r_out, ex_scratch))
        kern(*refs[:n_in], *refs[n_in + ri:n_in + ri + n_out], *host_scratch)

    outs = pl.pallas_call(
        body, name=name, grid=grid, in_specs=list(in_specs) + [ANY] * ri, out_specs=list(out_specs) + [ANY] * ro,
        out_shape=list(out_shape) + list(ex.out_shape), scratch_shapes=list(scratch) + list(ex.scratch),
        input_output_aliases={n_in + a: n_out + b for a, b in ex.aliases.items()},
        compiler_params=_params(sem),
    )(*operands, *ex.inputs)
    return outs[:n_out], outs[n_out:]


def _gather_exchange(bufs):
    n = len(bufs)

    def between_chips(outs, sems, i, j, chip, c, shard):
        blk = _half(outs[i], shard, c, outs[i].shape[1] // 2)
        return pltpu.make_async_remote_copy(src_ref=blk, dst_ref=blk, send_sem=sems[0].at[i, j], recv_sem=sems[1].at[i, j],
                                            device_id=(chip[0], chip[1], c), device_id_type=MESH)

    def to_sibling(outs, sems, i, j, x, y, c, shard, hc):
        blk = _half(outs[i], shard, hc, outs[i].shape[1] // 2)
        return pltpu.make_async_remote_copy(src_ref=blk, dst_ref=blk, send_sem=sems[0].at[i, 3 + j], recv_sem=sems[1].at[i, 3 + j],
                                            device_id=(x, y, 1 - c), device_id_type=MESH)

    def send_mine(ins, outs, sems):
        x, y, c, chips = _place()
        for i in range(n):
            for j, chip in enumerate(chips):
                between_chips(outs, sems, i, j, chip, c, 2 * x + y).start()

    def pass_on(ins, outs, sems):
        x, y, c, chips = _place()
        for i in range(n):
            for j, chip in enumerate(chips):
                between_chips(outs, sems, i, j, chip, c, 2 * chip[0] + chip[1]).wait_recv()
                to_sibling(outs, sems, i, j, x, y, c, 2 * chip[0] + chip[1], c).start()

    def finish(ins, outs, sems):
        x, y, c, chips = _place()
        for i in range(n):
            for j, chip in enumerate(chips):
                to_sibling(outs, sems, i, j, x, y, c, 2 * chip[0] + chip[1], 1 - c).wait_recv()
        for i in range(n):
            for j, chip in enumerate(chips):
                between_chips(outs, sems, i, j, chip, c, 2 * x + y).wait_send()
                to_sibling(outs, sems, i, j, x, y, c, 2 * chip[0] + chip[1], c).wait_send()

    return _Exchange(list(bufs), [jax.ShapeDtypeStruct(b.shape, b.dtype) for b in bufs], {i: i for i in range(n)},
                     [pltpu.SemaphoreType.DMA((n, 6)), pltpu.SemaphoreType.DMA((n, 6))], [send_mine, pass_on, finish])


def _reduce_exchange(g16, g32):
    n = len(g16)

    def copies(ins, outs, sems):
        x, y, c, _ = _place()
        for i in range(n):
            rh = ins[i].shape[1] // 2
            for r in range(1, 8):
                px, py, pc = x ^ ((r >> 2) & 1), y ^ ((r >> 1) & 1), c ^ (r & 1)
                src = _half(ins[i] if r > 1 else ins[n + i], 2 * px + py, pc, rh)
                dst = outs[2 * i + 1].at[r - 2] if r > 1 else outs[2 * i]
                yield pltpu.make_async_remote_copy(src_ref=src, dst_ref=dst, send_sem=sems[0].at[i, r - 1], recv_sem=sems[1].at[i, r - 1],
                                                   device_id=(px, py, pc), device_id_type=MESH)

    def start(ins, outs, sems):
        for cp in copies(ins, outs, sems):
            cp.start()

    def finish(ins, outs, sems):
        for cp in copies(ins, outs, sems):
            cp.wait()

    out_shape = []
    for g in g16:
        rh, C = g.shape[1] // 2, g.shape[2]
        out_shape += [jax.ShapeDtypeStruct((rh, C), F32), jax.ShapeDtypeStruct((6, rh, C), g.dtype)]
    return _Exchange(list(g16) + list(g32), out_shape, {},
                     [pltpu.SemaphoreType.DMA((n, 7)), pltpu.SemaphoreType.DMA((n, 7))], [start, finish])


def _add_direct(name, g32, from_sibling, from_chips, s_idx, c_idx):
    _, R, C = g32.shape
    rh = R // 2
    tr = _fit(rh, 256)
    nrb = rh // tr

    def kern(s_ref, c_ref, g_ref, a_ref, b_ref, out_ref):
        acc = g_ref[...] + a_ref[...]
        for k in range(6):
            acc = acc + b_ref[k].astype(F32)
        out_ref[...] = acc

    return pl.pallas_call(
        kern, name=name,
        grid_spec=pltpu.PrefetchScalarGridSpec(
            num_scalar_prefetch=2, grid=(nrb,),
            in_specs=[pl.BlockSpec((None, tr, C), lambda r, s_ref, c_ref: (s_ref[0], c_ref[0] * nrb + r, 0)),
                      pl.BlockSpec((tr, C), lambda r, s_ref, c_ref: (r, 0)),
                      pl.BlockSpec((6, tr, C), lambda r, s_ref, c_ref: (0, r, 0))],
            out_specs=pl.BlockSpec((tr, C), lambda r, s_ref, c_ref: (r, 0))),
        out_shape=jax.ShapeDtypeStruct((rh, C), F32),
        compiler_params=_params(("arbitrary",)),
    )(s_idx, c_idx, g32, from_sibling, from_chips)


def _swap_halves(grads):
    n = len(grads)

    def body(*refs):
        ins, outs = refs[:n], refs[n:2 * n]
        send, recv = refs[2 * n:]
        x, y, c, _ = _place()
        copies = []
        for i in range(n):
            rh = ins[i].shape[1] // 2
            cp = pltpu.make_async_remote_copy(
                src_ref=ins[i].at[:, pl.ds(pl.multiple_of((1 - c) * rh, 8), rh), :], dst_ref=outs[i],
                send_sem=send.at[i], recv_sem=recv.at[i], device_id=(x, y, 1 - c), device_id_type=MESH)
            cp.start()
            copies.append(cp)
        for cp in copies:
            cp.wait()

    return pl.pallas_call(
        body, name="grad_swap_halves",
        in_specs=[ANY] * n, out_specs=[ANY] * n,
        out_shape=[jax.ShapeDtypeStruct((N_SHARD, g.shape[1] // 2, g.shape[2]), F32) for g in grads],
        scratch_shapes=[pltpu.SemaphoreType.DMA((n,)), pltpu.SemaphoreType.DMA((n,))],
    )(*grads)


def _add_half(name, g, got, c_idx):
    _, R, C = g.shape
    rh = R // 2
    tr = _fit(rh, 256)
    nrb = rh // tr

    def kern(c_ref, g_ref, r_ref, o32_ref, o16_ref):
        v = g_ref[...] + r_ref[...]
        o32_ref[...] = v
        o16_ref[...] = v.astype(o16_ref.dtype)

    blk = lambda s, r, c_ref: (s, r, 0)
    return pl.pallas_call(
        kern, name=name,
        grid_spec=pltpu.PrefetchScalarGridSpec(
            num_scalar_prefetch=1, grid=(N_SHARD, nrb),
            in_specs=[pl.BlockSpec((None, tr, C), lambda s, r, c_ref: (s, c_ref[0] * nrb + r, 0)), pl.BlockSpec((None, tr, C), blk)],
            out_specs=[pl.BlockSpec((None, tr, C), blk)] * 2),
        out_shape=[jax.ShapeDtypeStruct((N_SHARD, rh, C), F32), jax.ShapeDtypeStruct((N_SHARD, rh, C), MXU_DTYPE)],
        compiler_params=_params(("arbitrary", "arbitrary")),
    )(c_idx, g, got)


def _scatter_partials(parts):
    n = len(parts)

    def body(*refs):
        ins, outs = refs[:n], refs[n:2 * n]
        send, recv = refs[2 * n:]
        x, y, c, chips = _place()
        copies = []
        for i in range(n):
            for j, (px, py) in enumerate(chips):
                cp = pltpu.make_async_remote_copy(
                    src_ref=ins[i].at[2 * px + py], dst_ref=outs[i].at[j],
                    send_sem=send.at[i, j], recv_sem=recv.at[i, j], device_id=(px, py, c), device_id_type=MESH)
                cp.start()
                copies.append(cp)
        for cp in copies:
            cp.wait()

    return pl.pallas_call(
        body, name="grad_scatter_partials",
        in_specs=[ANY] * n, out_specs=[ANY] * n,
        out_shape=[jax.ShapeDtypeStruct((3,) + p.shape[1:], p.dtype) for p in parts],
        scratch_shapes=[pltpu.SemaphoreType.DMA((n, 3)), pltpu.SemaphoreType.DMA((n, 3))],
    )(*parts)


def _add_partials(name, own, got, s_idx):
    _, rh, C = own.shape
    tr = _fit(rh, 256)

    def kern(s_ref, o_ref, a_ref, b_ref, c_ref, out_ref):
        out_ref[...] = ((o_ref[...] + a_ref[...].astype(F32)) + b_ref[...].astype(F32)) + c_ref[...].astype(F32)

    slot = lambda j: pl.BlockSpec((None, tr, C), lambda r, s_ref: (j, r, 0))
    return pl.pallas_call(
        kern, name=name,
        grid_spec=pltpu.PrefetchScalarGridSpec(
            num_scalar_prefetch=1, grid=(rh // tr,),
            in_specs=[pl.BlockSpec((None, tr, C), lambda r, s_ref: (s_ref[0], r, 0)), slot(0), slot(1), slot(2)],
            out_specs=pl.BlockSpec((tr, C), lambda r, s_ref: (r, 0))),
        out_shape=jax.ShapeDtypeStruct((rh, C), F32),
        compiler_params=_params(("arbitrary",)),
    )(s_idx, own, got, got, got)


def _swap_reduced(halves):
    n = len(halves)

    def body(*refs):
        ins, outs = refs[:n], refs[n:2 * n]
        send, recv = refs[2 * n:]
        x, y, c, _ = _place()
        copies = []
        for i in range(n):
            cp = pltpu.make_async_remote_copy(src_ref=ins[i], dst_ref=outs[i], send_sem=send.at[i], recv_sem=recv.at[i],
                                              device_id=(x, y, 1 - c), device_id_type=MESH)
            cp.start()
            copies.append(cp)
        for cp in copies:
            cp.wait()

    return pl.pallas_call(
        body, name="grad_swap_reduced",
        in_specs=[ANY] * n, out_specs=[ANY] * n,
        out_shape=[jax.ShapeDtypeStruct(h.shape, F32) for h in halves],
        scratch_shapes=[pltpu.SemaphoreType.DMA((n,)), pltpu.SemaphoreType.DMA((n,))],
    )(*halves)


def _all_reduce_small(v):
    rows, W = v.shape
    ndev = 8

    def body(v_ref, out_ref, buf, send, recv):
        x, y, c, _ = _place()
        me = 4 * x + 2 * y + c
        buf[me] = v_ref[...]
        copies = []
        for r in range(1, ndev):
            fx, fy, fc = (r >> 2) & 1, (r >> 1) & 1, r & 1
            peer = (x ^ fx, y ^ fy, c ^ fc)
            cp = pltpu.make_async_remote_copy(src_ref=v_ref, dst_ref=buf.at[me], send_sem=send.at[r - 1], recv_sem=recv.at[r - 1],
                                              device_id=peer, device_id_type=MESH)
            cp.start()
            copies.append(cp)
        for cp in copies:
            cp.wait()
        acc = buf[0]
        for k in range(1, ndev):
            acc = acc + buf[k]
        out_ref[...] = acc

    return pl.pallas_call(
        body, name="small_all_reduce",
        in_specs=[pl.BlockSpec(memory_space=pltpu.VMEM)], out_specs=pl.BlockSpec(memory_space=pltpu.VMEM),
        out_shape=jax.ShapeDtypeStruct((rows, W), F32),
        scratch_shapes=[pltpu.VMEM((ndev, rows, W), F32), pltpu.SemaphoreType.DMA((ndev - 1,)), pltpu.SemaphoreType.DMA((ndev - 1,))],
    )(v)


def _adamw_update(gv, w_ref, m_ref, v_ref, d_ref, nm_ref, nv_ref):
    nm = ADAM_B1 * m_ref[...] + (1.0 - ADAM_B1) * gv
    nv = ADAM_B2 * v_ref[...] + (1.0 - ADAM_B2) * (gv * gv)
    m_hat = nm / (1.0 - ADAM_B1 ** ADAM_STEP)
    v_hat = nv / (1.0 - ADAM_B2 ** ADAM_STEP)
    d_ref[...] = -ADAM_LR * (m_hat / (jnp.sqrt(v_hat) + ADAM_EPS) + ADAM_WD * w_ref[...])
    nm_ref[...] = nm
    nv_ref[...] = nv


def _adamw(name, w, g, m, v):
    R, C = w.shape
    tr = _fit(R, 256)

    def body(w_ref, g_ref, m_ref, v_ref, d_ref, nm_ref, nv_ref):
        _adamw_update(g_ref[...], w_ref, m_ref, v_ref, d_ref, nm_ref, nv_ref)

    return _rows_call(name, body, R, tr, [(a, _row(tr, C)) for a in (w, g, m, v)], [(C, F32)] * 3)


def _adamw_halves(name, w, mine, theirs, m, v, c_idx):
    R, C = w.shape
    rh = R // 2
    tr = _fit(rh, 256)
    nrb = rh // tr

    def kern(c_ref, w_ref, a_ref, b_ref, m_ref, v_ref, g_ref, d_ref, nm_ref, nv_ref):
        gv = jnp.where(pl.program_id(0) // nrb == c_ref[0], a_ref[...], b_ref[...])
        g_ref[...] = gv
        _adamw_update(gv, w_ref, m_ref, v_ref, d_ref, nm_ref, nv_ref)

    full = pl.BlockSpec((tr, C), lambda r, c_ref: (r, 0))
    pick = lambda own: pl.BlockSpec((tr, C), lambda r, c_ref: (jnp.where((r // nrb == c_ref[0]) == own, r % nrb, 0), 0))
    return pl.pallas_call(
        kern, name=name,
        grid_spec=pltpu.PrefetchScalarGridSpec(
            num_scalar_prefetch=1, grid=(R // tr,),
            in_specs=[full, pick(True), pick(False), full, full], out_specs=[full] * 4),
        out_shape=[jax.ShapeDtypeStruct((R, C), F32)] * 4,
        compiler_params=_params(("arbitrary",)),
    )(c_idx, w, mine, theirs, m, v)


def _sigmoid(z):
    return 1.0 / (1.0 + jnp.exp(-z))


def kernel(x, p, g_mix, w_in, qn_gain, kn_gain, w_branch_a, w_branch_b, w_out, g_mlp, w_up, w_down, g_ple, w_ple_gate, w_ple_proj, loss_target, m_g_mix, m_w_in, m_qn_gain, m_kn_gain, m_w_branch_a, m_w_branch_b, m_w_out, m_g_mlp, m_w_up, m_w_down, m_g_ple, m_w_ple_gate, m_w_ple_proj, v_g_mix, v_w_in, v_qn_gain, v_kn_gain, v_w_branch_a, v_w_branch_b, v_w_out, v_g_mlp, v_w_up, v_w_down, v_g_ple, v_w_ple_gate, v_w_ple_proj):
    S, D = x.shape[1], x.shape[2]
    HW = w_branch_a.shape[1]
    x2d, tgt, p2d = x.reshape(S, D), loss_target.reshape(S, D), p.reshape(S, p.shape[-1])
    big = {"w_in": w_in, "w_branch_a": w_branch_a, "w_branch_b": w_branch_b, "w_out": w_out, "w_up": w_up,
           "w_down": w_down, "w_ple_gate": w_ple_gate, "w_ple_proj": w_ple_proj}
    moments = {"w_in": (m_w_in, v_w_in), "w_branch_a": (m_w_branch_a, v_w_branch_a), "w_branch_b": (m_w_branch_b, v_w_branch_b),
               "w_out": (m_w_out, v_w_out), "w_up": (m_w_up, v_w_up), "w_down": (m_w_down, v_w_down),
               "w_ple_gate": (m_w_ple_gate, v_w_ple_gate), "w_ple_proj": (m_w_ple_proj, v_w_ple_proj)}
    names = list(big)
    col_sharded = {"w_in", "w_branch_a", "w_branch_b", "w_up", "w_ple_proj"}
    shard2d = {k: w.reshape(w.shape[1], w.shape[2]) for k, w in big.items()}

    c_idx = lax.axis_index("c").astype(jnp.int32).reshape(1)
    s_idx = (2 * lax.axis_index("x") + lax.axis_index("y")).astype(jnp.int32).reshape(1)
    placed = {k: _cast_place(f"cast_{k}", shard2d[k], s_idx) for k in names}
    late = [k for k in names if k != "w_in"]
    W = {"w_in": _run_alone("gather_w_in", _gather_exchange([placed["w_in"]]))[0]}
    cin = W["w_in"].shape[2]
    bn_in = _fit(cin, 512)
    while (2 * HW) % bn_in:
        bn_in -= 128

    h = _rmsnorm_fwd("rmsnorm_mix", x2d, g_mix)
    (qk,) = _matmul("proj_qk", h, W["w_in"], mode="nn", out_dtypes=[F32], b_cshard=True, b_off=0, n_out=2 * HW, bn=bn_in, bk=D)
    (mid,) = _matmul("proj_mid", h, W["w_in"], mode="nn", out_dtypes=[MXU_DTYPE], b_cshard=True, b_off=2 * HW // bn_in,
                     n_out=4 * HW, bn=bn_in, bk=D)
    (sg,) = _matmul("proj_gates", h, W["w_in"], mode="nn", out_dtypes=[F32], b_cshard=True, b_off=6 * HW // bn_in,
                    n_out=2 * D, bn=bn_in, bk=D, epilogue=lambda acc: (_sigmoid(acc),))
    tabs = _rope_tables(S)
    qa, ka = _qknorm_fwd(qk, qn_gain, kn_gain, tabs, HW)
    dil = [_dilated_fwd(qa, ka, mid, d, HW) for d in DILATIONS]
    ya, lse = _dilated_combine([o for o, _ in dil], [l for _, l in dil], HW)
    yb, sb_a, sb_sig, gathered = _sb_fwd(mid, HW, _gather_exchange([placed[k] for k in late]))
    W.update({k: (g if k in col_sharded else g.reshape(-1, g.shape[2])) for k, g in zip(late, gathered)})

    gate_blocks = D // _fit(D, 1024)
    (ua,) = _matmul("branch_a", ya, W["w_branch_a"], mode="nn", out_dtypes=[F32], b_cshard=True, bn=_fit(W["w_branch_a"].shape[2], 1024))
    bn_b = _fit(W["w_branch_b"].shape[2], 1024)
    ub, merged = _matmul("branch_b_merge", yb, W["w_branch_b"], mode="nn", out_dtypes=[F32, MXU_DTYPE], b_cshard=True, bn=bn_b,
                         extras=[(sg, 0), (sg, D // bn_b), (ua, 0)],
                         epilogue=lambda acc, sga, sgb, uav: (acc, sga * uav + sgb * acc))
    (x1,) = _matmul("out_proj", merged, W["w_out"], mode="nn", out_dtypes=[F32], extras=[(x2d, 0)], epilogue=lambda acc, xv: (xv + acc,))
    hm = _rmsnorm_fwd("rmsnorm_mlp", x1, g_mlp)

    def up_epilogue(acc):
        r = jnp.maximum(acc, 0.0)
        return r * r, r

    act, rup = _matmul("mlp_up", hm, W["w_up"], mode="nn", out_dtypes=[MXU_DTYPE, MXU_DTYPE], b_cshard=True,
                       bn=_fit(W["w_up"].shape[2], 1024), bk=D, epilogue=up_epilogue)
    (x2,) = _matmul("mlp_down", act, W["w_down"], mode="nn", out_dtypes=[F32], extras=[(x1, 0)], epilogue=lambda acc, xv: (xv + acc,))
    hp = _rmsnorm_fwd("rmsnorm_ple", x2, g_ple)
    (pp,) = _matmul("ple_proj", p2d, W["w_ple_proj"], mode="nn", out_dtypes=[F32], b_cshard=True, bn=_fit(W["w_ple_proj"].shape[2], 1024))

    def ple_epilogue(acc, ppv, x2v, tv):
        s = _sigmoid(acc)
        dx3 = ((x2v + ppv * s) - tv) / D
        return dx3, dx3 * s, dx3 * ppv * (s * (1.0 - s))

    dx3, d_pp, d_gate = _matmul("ple_gate_loss", hp, W["w_ple_gate"], mode="nn", out_dtypes=[F32, MXU_DTYPE, MXU_DTYPE],
                                bm=512, extras=[(pp, 0), (x2, 0), (tgt, 0)], epilogue=ple_epilogue)

    G, G16 = {}, {}
    G["w_ple_proj"], G16["w_ple_proj"] = _matmul("grad_w_ple_proj", p2d, d_pp, mode="tn", out_dtypes=[F32, MXU_DTYPE], out_cshard=True,
                                 bn=_fit(d_pp.shape[1] // N_SHARD, 1024))
    G["w_ple_gate"], G16["w_ple_gate"] = _matmul("grad_w_ple_gate", hp, d_gate, mode="tn", out_dtypes=[F32, MXU_DTYPE])
    (d_hp,) = _matmul("ple_gate_bwd", d_gate, W["w_ple_gate"], mode="nt", out_dtypes=[F32])
    dx2, g_g_ple, loss_part = _rmsnorm_bwd("rmsnorm_ple_bwd", d_hp, x2, g_ple, dx3, True)
    G["w_down"], G16["w_down"] = _matmul("grad_w_down", act, dx2, mode="tn", out_dtypes=[F32, MXU_DTYPE])
    (d_up,) = _matmul("mlp_down_bwd", dx2, W["w_down"], mode="nt", out_dtypes=[MXU_DTYPE], extras=[(rup, 0)],
                      epilogue=lambda acc, r: (acc * (2.0 * r.astype(F32)),))
    G["w_up"], G16["w_up"] = _matmul("grad_w_up", hm, d_up, mode="tn", out_dtypes=[F32, MXU_DTYPE], out_cshard=True, bn=_fit(d_up.shape[1] // N_SHARD, 1024))
    (d_hm,) = _matmul("mlp_up_bwd", d_up, W["w_up"], mode="nt", out_dtypes=[F32], b_cshard=True, bk=_fit(W["w_up"].shape[2], 2048))
    dx1, g_g_mlp = _rmsnorm_bwd("rmsnorm_mlp_bwd", d_hm, x1, g_mlp, dx2, False)
    G["w_out"], G16["w_out"] = _matmul("grad_w_out", merged, dx1, mode="tn", out_dtypes=[F32, MXU_DTYPE])

    def merge_bwd(acc, sga, sgb, uav, ubv):
        return acc * sga, acc * sgb, acc * uav * (sga * (1.0 - sga)), acc * ubv * (sgb * (1.0 - sgb))

    bn_m = _fit(D, 1024)
    d_ua, d_ub, d_ga, d_gb = _matmul("out_proj_bwd", dx1, W["w_out"], mode="nt", out_dtypes=[MXU_DTYPE] * 4, bm=512, bn=bn_m,
                                     extras=[(sg, 0), (sg, D // bn_m), (ua, 0), (ub, 0)], epilogue=merge_bwd)
    bn_br = _fit(D // N_SHARD, 1024)
    G["w_branch_a"], G16["w_branch_a"] = _matmul("grad_w_branch_a", ya, d_ua, mode="tn", out_dtypes=[F32, MXU_DTYPE], out_cshard=True, bn=bn_br)
    G["w_branch_b"], G16["w_branch_b"] = _matmul("grad_w_branch_b", yb, d_ub, mode="tn", out_dtypes=[F32, MXU_DTYPE], out_cshard=True, bn=bn_br)
    (d_ya,) = _matmul("branch_a_bwd", d_ua, W["w_branch_a"], mode="nt", out_dtypes=[F32], b_cshard=True, bk=bn_br)
    (d_yb,) = _matmul("branch_b_bwd", d_ub, W["w_branch_b"], mode="nt", out_dtypes=[F32], b_cshard=True, bk=bn_br)

    as_shards = lambda k, g: g if k in col_sharded else g.reshape(N_SHARD, -1, g.shape[1])
    (dqb, dkb, dvb), partials = _sb_bwd(mid, d_yb, sb_a, sb_sig, HW, _reduce_exchange([as_shards(k, G16[k]) for k in late],
                                                                               [as_shards(k, G[k]) for k in late]))
    dil_b = [_dilated_bwd(qa, ka, mid, d_ya, ya, lse, d, HW) for d in DILATIONS]
    d_qk, g_qn, g_kn = _qknorm_bwd(qk, qn_gain, kn_gain, tabs, [t[0] for t in dil_b], [t[1] for t in dil_b],
                                   [t[2] for t in dil_b], HW)
    dva = _dv_sum([t[3] for t in dil_b], [t[4] for t in dil_b], HW)
    d_proj = jnp.concatenate([d_qk, dva, dqb.astype(MXU_DTYPE), dkb.astype(MXU_DTYPE), dvb.astype(MXU_DTYPE), d_ga, d_gb], axis=1)
    (G["w_in"],) = _matmul("grad_w_in", h, d_proj, mode="tn", out_dtypes=[F32], out_cshard=True, bn=_fit(cin, 1280))
    (d_h,) = _matmul("proj_bwd", d_proj, W["w_in"], mode="nt", out_dtypes=[F32], b_cshard=True, bk=_fit(cin, 1280))
    grad_x, g_g_mix = _rmsnorm_bwd("rmsnorm_mix_bwd", d_h, x2d, g_mix, dx1, False)

    mine = {k: _add_direct(f"grad_add_{k}", as_shards(k, G[k]), partials[2 * n], partials[2 * n + 1], s_idx, c_idx)
            for n, k in enumerate(late)}
    (swapped,) = _swap_halves([G["w_in"]])
    p32, p16 = _add_half("grad_add_half_w_in", G["w_in"], swapped, c_idx)
    (landed,) = _scatter_partials([p16])
    mine["w_in"] = _add_partials("grad_add_partials_w_in", p32, landed, s_idx)
    halves = [mine[k] for k in names]
    others = _swap_reduced(halves)

    pack_w = -(-(3 * D + 3 * 128) // (8 * 128)) * 128

    def pack(v_mix, v_mlp, v_ple, v_qn, v_kn, extra):
        flat = jnp.concatenate([v_mix.reshape(-1), v_mlp.reshape(-1), v_ple.reshape(-1), v_qn.reshape(-1), v_kn.reshape(-1), extra.reshape(-1)])
        return jnp.pad(flat, (0, 8 * pack_w - flat.shape[0])).reshape(8, pack_w)

    def unpack(blk):
        flat = blk.reshape(-1)
        return (flat[:D].reshape(1, D), flat[D:2 * D].reshape(1, D), flat[2 * D:3 * D].reshape(1, D),
                flat[3 * D:3 * D + 128].reshape(1, 128), flat[3 * D + 128:3 * D + 256].reshape(1, 128), flat[3 * D + 256])

    small = _all_reduce_small(pack(g_g_mix, g_g_mlp, g_g_ple, g_qn, g_kn, loss_part))
    sw = pack(g_mix, g_mlp, g_ple, qn_gain, kn_gain, jnp.zeros((128,), F32))
    sm = pack(m_g_mix, m_g_mlp, m_g_ple, m_qn_gain, m_kn_gain, jnp.zeros((128,), F32))
    sv = pack(v_g_mix, v_g_mlp, v_g_ple, v_qn_gain, v_kn_gain, jnp.ones((128,), F32))
    s_delta, s_nm, s_nv = _adamw("adamw_small", sw, small, sm, sv)
    sg_mix, sg_mlp, sg_ple, sg_qn, sg_kn, loss = unpack(small)
    small_out = {}
    for tag, blk in (("delta", s_delta), ("new_m", s_nm), ("new_v", s_nv)):
        u = unpack(blk)
        small_out[tag] = dict(g_mix=u[0], g_mlp=u[1], g_ple=u[2], qn_gain=u[3], kn_gain=u[4])
    small_grad = dict(g_mix=sg_mix, g_mlp=sg_mlp, g_ple=sg_ple, qn_gain=sg_qn, kn_gain=sg_kn)

    big_out = {"grad": {}, "delta": {}, "new_m": {}, "new_v": {}}
    for k, mine, theirs in zip(names, halves, others):
        shape = big[k].shape
        m2, v2 = (t.reshape(shape[1], shape[2]) for t in moments[k])
        res = _adamw_halves(f"adamw_{k}", shard2d[k], mine, theirs, m2, v2, c_idx)
        for tag, t in zip(("grad", "delta", "new_m", "new_v"), res):
            big_out[tag][k] = t.reshape(shape)

    order = ["g_mix", "w_in", "qn_gain", "kn_gain", "w_branch_a", "w_branch_b", "w_out", "g_mlp", "w_up", "w_down", "g_ple",
             "w_ple_gate", "w_ple_proj"]
    outs = [loss, grad_x.reshape(x.shape)]
    outs += [small_grad[k] if k in small_grad else big_out["grad"][k] for k in order]
    for tag in ("delta", "new_m", "new_v"):
        outs += [small_out[tag][k] if k in small_grad else big_out[tag][k] for k in order]
    return tuple(outs)
```

```python
import functools

import jax
import jax.numpy as jnp
from jax import lax
from jax.experimental import pallas as pl
from jax.experimental.pallas import tpu as pltpu

F32 = jnp.float32
MXU_DTYPE = jnp.bfloat16
HEAD_DIM = 128
ROT_DIM = HEAD_DIM // 4
ROPE_THETA = 500000.0
EPS = 1e-6
DILATIONS = (1, 4, 16)
BLOCK = 128
N_SHARD = 4
ADAM_LR, ADAM_B1, ADAM_B2, ADAM_EPS, ADAM_WD, ADAM_STEP = 0.001, 0.9, 0.999, 1e-08, 0.01, 10
V7X_VMEM_BYTES = 64 * 1024 * 1024
VMEM_LIMIT = V7X_VMEM_BYTES - 8 * 1024 * 1024
MESH = pl.DeviceIdType.MESH
NEG = -1e30
SB_BQ, SB_BK = 512, 256
CUMSUM_PASSES = 2


def _fit(dim, pref):
    if dim <= pref:
        return dim
    b = (pref // 128) * 128
    while dim % b:
        b -= 128
    return b


def _params(sem=None):
    return pltpu.CompilerParams(dimension_semantics=sem, vmem_limit_bytes=VMEM_LIMIT)


def _dot(a, b, dims=(((1,), (0,)), ((), ()))):
    return lax.dot_general(a, b, dims, preferred_element_type=F32)


NT = (((1,), (1,)), ((), ()))
TN = (((0,), (0,)), ((), ()))


def _split_dot(x, u, passes):
    out = None
    r = x
    for p in range(passes):
        hi = r.astype(MXU_DTYPE)
        part = _dot(hi, u)
        out = part if out is None else out + part
        if p + 1 < passes:
            r = r - hi.astype(F32)
    return out


def _matmul(name, a, b, *, mode, out_dtypes, bm=1024, bn=1024, bk=2048, b_cshard=False, b_off=0, n_out=None,
            extras=(), epilogue=None, out_cshard=False, ride=None):
    if mode == "tn":
        K, M = a.shape
        N = b.shape[1]
    else:
        M, K = a.shape
        if mode == "nn":
            N = n_out if n_out is not None else (N_SHARD * b.shape[2] if b_cshard else b.shape[1])
        else:
            N = b.shape[1] if b_cshard else b.shape[0]
    bm, bn, bk = _fit(M, bm), _fit(N, bn), _fit(K, bk)
    nk = K // bk
    grid = (M // bm, N // bn, nk)

    if mode == "tn":
        a_spec = pl.BlockSpec((bk, bm), lambda i, j, k: (k, i))
        b_spec = pl.BlockSpec((bk, bn), lambda i, j, k: (k, j))
        dims = TN
    elif mode == "nn":
        a_spec = pl.BlockSpec((bm, bk), lambda i, j, k: (i, k))
        if b_cshard:
            cb = b.shape[2] // bn
            b_spec = pl.BlockSpec((None, bk, bn), lambda i, j, k: ((j + b_off) // cb, k, (j + b_off) % cb))
        else:
            b_spec = pl.BlockSpec((bk, bn), lambda i, j, k: (k, j + b_off))
        dims = (((1,), (0,)), ((), ()))
    else:
        a_spec = pl.BlockSpec((bm, bk), lambda i, j, k: (i, k))
        if b_cshard:
            cb = b.shape[2] // bk
            b_spec = pl.BlockSpec((None, bn, bk), lambda i, j, k: (k // cb, j, k % cb))
        else:
            b_spec = pl.BlockSpec((bn, bk), lambda i, j, k: (j, k))
        dims = NT

    ex_arrays = [e[0] for e in extras]
    ex_specs = [pl.BlockSpec((bm, bn), functools.partial(lambda i, j, k, off: (i, j + off), off=e[1])) for e in extras]
    if out_cshard:
        cbo = (N // N_SHARD) // bn
        out_shape = [jax.ShapeDtypeStruct((N_SHARD, M, N // N_SHARD), dt) for dt in out_dtypes]
        out_specs = [pl.BlockSpec((None, bm, bn), lambda i, j, k: (j // cbo, i, j % cbo)) for _ in out_dtypes]
    else:
        out_shape = [jax.ShapeDtypeStruct((M, N), dt) for dt in out_dtypes]
        out_specs = [pl.BlockSpec((bm, bn), lambda i, j, k: (i, j)) for _ in out_dtypes]
    ne, no = len(extras), len(out_dtypes)

    def kern(*refs):
        a_ref, b_ref = refs[0], refs[1]
        ex_refs = refs[2:2 + ne]
        o_refs = refs[2 + ne:2 + ne + no]
        part = _dot(a_ref[...].astype(MXU_DTYPE), b_ref[...].astype(MXU_DTYPE), dims)

        def finish(acc):
            vals = (acc,) * no if epilogue is None else epilogue(acc, *[r[...] for r in ex_refs])
            for r, v in zip(o_refs, vals):
                r[...] = v.astype(r.dtype)

        if nk == 1:
            finish(part)
        else:
            acc_ref = refs[2 + ne + no]
            k = pl.program_id(2)

            @pl.when(k == 0)
            def _():
                acc_ref[...] = part

            @pl.when(k > 0)
            def _():
                acc_ref[...] += part

            @pl.when(k == nk - 1)
            def _():
                finish(acc_ref[...])

    outs, rode = _host_call(kern, name, grid, [a_spec, b_spec] + ex_specs, out_specs, out_shape, (a, b, *ex_arrays), ride,
                            [pltpu.VMEM((bm, bn), F32)] if nk > 1 else [])
    return outs if ride is None else (outs, rode)


def _row(tr, w, coff=0):
    return pl.BlockSpec((tr, w), lambda i: (i, coff))


def _vec(w):
    return pl.BlockSpec((1, w), lambda i: (0, 0))


def _rows_call(name, body, n_rows, tr, ins, outs, accs=()):
    n_in, n_out = len(ins), len(outs)

    def kern(*refs):
        acc_refs = refs[n_in + n_out:]
        if acc_refs:
            @pl.when(pl.program_id(0) == 0)
            def _():
                for r in acc_refs:
                    r[...] = jnp.zeros_like(r)
        body(*refs)

    out_shape = [jax.ShapeDtypeStruct((n_rows, w), dt) for w, dt in outs] + [jax.ShapeDtypeStruct((1, w), F32) for w in accs]
    out_specs = [_row(tr, w) for w, _ in outs] + [_vec(w) for w in accs]
    return pl.pallas_call(
        kern, name=name, grid=(n_rows // tr,),
        in_specs=[s for _, s in ins], out_specs=out_specs, out_shape=out_shape,
        compiler_params=_params(("arbitrary",)),
    )(*[a for a, _ in ins])


def _rmsnorm_fwd(name, x, g):
    S, D = x.shape
    tr = _fit(S, 256)

    def body(x_ref, g_ref, h_ref):
        xv = x_ref[...]
        r = lax.rsqrt(jnp.mean(xv * xv, axis=1, keepdims=True) + EPS)
        h_ref[...] = ((xv * r) * g_ref[...]).astype(h_ref.dtype)

    return _rows_call(name, body, S, tr, [(x, _row(tr, D)), (g, _vec(D))], [(D, MXU_DTYPE)])[0]


def _rmsnorm_bwd(name, dh, x, g, resid, with_loss):
    S, D = x.shape
    tr = _fit(S, 256)

    def body(dh_ref, x_ref, g_ref, res_ref, dx_ref, dg_ref, *loss_ref):
        xv = x_ref[...]
        r = lax.rsqrt(jnp.mean(xv * xv, axis=1, keepdims=True) + EPS)
        dhv = dh_ref[...]
        u = dhv * g_ref[...]
        xr = xv * r
        dx = r * u - xr * (r * r) * jnp.mean(xv * u, axis=1, keepdims=True)
        resv = res_ref[...]
        dx_ref[...] = resv + dx
        dg_ref[...] += jnp.sum(dhv * xr, axis=0, keepdims=True)
        if with_loss:
            loss_ref[0][...] += (0.5 * D) * jnp.sum(resv * resv)

    outs = _rows_call(name, body, S, tr, [(dh, _row(tr, D)), (x, _row(tr, D)), (g, _vec(D)), (resid, _row(tr, D))],
                      [(D, F32)], accs=(D, 128) if with_loss else (D,))
    return outs


def _rope_tables(S):
    half = ROT_DIM // 2
    pos = jnp.arange(S, dtype=F32)
    inv = ROPE_THETA ** (-jnp.arange(0, ROT_DIM, 2, dtype=F32) / ROT_DIM)
    ang = pos[:, None] * inv[None, :]
    cos, sin = jnp.cos(ang), jnp.sin(ang)
    pad = HEAD_DIM - ROT_DIM
    ctab = jnp.concatenate([cos, cos, jnp.ones((S, pad), F32)], axis=1)
    atab = jnp.concatenate([-sin, jnp.zeros((S, pad + half), F32)], axis=1)
    btab = jnp.concatenate([jnp.zeros((S, half), F32), sin, jnp.zeros((S, pad), F32)], axis=1)
    return ctab, atab, btab


def _qknorm_fwd(qk, qn, kn, tabs, HW):
    S = qk.shape[0]
    tr = _fit(S, 256)
    half = ROT_DIM // 2

    def body(qk_ref, qn_ref, kn_ref, c_ref, a_ref, b_ref, q_out, k_out):
        ct, at, bt = c_ref[...], a_ref[...], b_ref[...]
        for part, (g_ref, o_ref) in enumerate(((qn_ref, q_out), (kn_ref, k_out))):
            gv = g_ref[...]
            for h in range(HW // HEAD_DIM):
                xh = qk_ref[:, part * HW + h * HEAD_DIM: part * HW + (h + 1) * HEAD_DIM]
                r = lax.rsqrt(jnp.mean(xh * xh, axis=1, keepdims=True) + EPS)
                y = (xh * r) * gv
                o = y * ct + pltpu.roll(y, HEAD_DIM - half, 1) * at + pltpu.roll(y, half, 1) * bt
                o_ref[:, h * HEAD_DIM:(h + 1) * HEAD_DIM] = o.astype(o_ref.dtype)

    ins = [(qk, _row(tr, 2 * HW)), (qn, _vec(HEAD_DIM)), (kn, _vec(HEAD_DIM))] + [(t, _row(tr, HEAD_DIM)) for t in tabs]
    return _rows_call("qknorm_fwd", body, S, tr, ins, [(HW, MXU_DTYPE), (HW, MXU_DTYPE)])


def _shift_spec(tr, w, shift, nblk):
    return pl.BlockSpec((tr, w), lambda i: (jnp.minimum(i + shift, nblk - 1), 0))


def _qknorm_bwd(qk, qn, kn, tabs, dq_parts, dk_cur, dk_prev, HW):
    S = qk.shape[0]
    tr = BLOCK
    nblk = S // tr
    half = ROT_DIM // 2
    nd = len(DILATIONS)

    def body(*refs):
        qk_ref, qn_ref, kn_ref, c_ref, a_ref, b_ref = refs[:6]
        dq_refs = refs[6:6 + nd]
        dkc_refs = refs[6 + nd:6 + 2 * nd]
        dkp_refs = refs[6 + 2 * nd:6 + 3 * nd]
        d_out, dqn_ref, dkn_ref = refs[6 + 3 * nd:]
        i = pl.program_id(0)
        ct, at, bt = c_ref[...], a_ref[...], b_ref[...]
        live = [(i + d < nblk).astype(F32) for d in DILATIONS]
        for part, (g_ref, dg_ref) in enumerate(((qn_ref, dqn_ref), (kn_ref, dkn_ref))):
            gv = g_ref[...]
            dg = jnp.zeros((1, HEAD_DIM), F32)
            for h in range(HW // HEAD_DIM):
                hs = slice(h * HEAD_DIM, (h + 1) * HEAD_DIM)
                if part == 0:
                    do = dq_refs[0][:, hs] + dq_refs[1][:, hs] + dq_refs[2][:, hs]
                else:
                    do = dkc_refs[0][:, hs] + dkc_refs[1][:, hs] + dkc_refs[2][:, hs]
                    for n in range(nd):
                        do = do + dkp_refs[n][:, hs] * live[n]
                dy = do * ct + pltpu.roll(do * at, half, 1) + pltpu.roll(do * bt, HEAD_DIM - half, 1)
                xh = qk_ref[:, part * HW + h * HEAD_DIM: part * HW + (h + 1) * HEAD_DIM]
                r = lax.rsqrt(jnp.mean(xh * xh, axis=1, keepdims=True) + EPS)
                xr = xh * r
                u = dy * gv
                dx = r * u - xr * (r * r) * jnp.mean(xh * u, axis=1, keepdims=True)
                d_out[:, part * HW + h * HEAD_DIM: part * HW + (h + 1) * HEAD_DIM] = dx.astype(d_out.dtype)
                dg = dg + jnp.sum(dy * xr, axis=0, keepdims=True)
            dg_ref[...] += dg

    ins = [(qk, _row(tr, 2 * HW)), (qn, _vec(HEAD_DIM)), (kn, _vec(HEAD_DIM))] + [(t, _row(tr, HEAD_DIM)) for t in tabs]
    ins += [(a, _row(tr, HW)) for a in dq_parts] + [(a, _row(tr, HW)) for a in dk_cur]
    ins += [(a, _shift_spec(tr, HW, d, nblk)) for a, d in zip(dk_prev, DILATIONS)]
    return _rows_call("qknorm_bwd", body, S, tr, ins, [(2 * HW, MXU_DTYPE)], accs=(HEAD_DIM, HEAD_DIM))


def _dv_sum(dv_cur, dv_prev, HW):
    S = dv_cur[0].shape[0]
    tr = BLOCK
    nblk = S // tr
    nd = len(DILATIONS)

    def body(*refs):
        i = pl.program_id(0)
        out = refs[2 * nd]
        acc = refs[0][...] + refs[1][...] + refs[2][...]
        for n, d in enumerate(DILATIONS):
            acc = acc + refs[nd + n][...] * (i + d < nblk).astype(F32)
        out[...] = acc.astype(out.dtype)

    ins = [(a, _row(tr, HW)) for a in dv_cur] + [(a, _shift_spec(tr, HW, d, nblk)) for a, d in zip(dv_prev, DILATIONS)]
    return _rows_call("dilated_dv_sum", body, S, tr, ins, [(HW, MXU_DTYPE)])[0]


def _dil_geometry(d, HW):
    H = HW // HEAD_DIM
    hb = min(H, max(1, 8 // d))
    tb, w = BLOCK * d, hb * HEAD_DIM
    cur = pl.BlockSpec((tb, w), lambda n, g: (n, g))
    prev = pl.BlockSpec((tb, w), lambda n, g: (jnp.maximum(n - 1, 0), g))
    units = [(hh, r) for hh in range(hb) for r in range(d)]
    return H // hb, hb, tb, cur, prev, units


def _dil_mask(n):
    qi = lax.broadcasted_iota(jnp.int32, (BLOCK, 2 * BLOCK), 0)
    ki = lax.broadcasted_iota(jnp.int32, (BLOCK, 2 * BLOCK), 1)
    return (ki >= qi) & (ki <= qi + BLOCK) & ((ki >= BLOCK) | (n > 0))


def _dil_stage(ref, buf, row0=0):
    for hh in range(buf.shape[0]):
        buf[hh, row0:row0 + ref.shape[0], :] = ref[:, hh * HEAD_DIM:(hh + 1) * HEAD_DIM].astype(F32)


def _dil_unstage(buf, ref):
    for hh in range(buf.shape[0]):
        ref[:, hh * HEAD_DIM:(hh + 1) * HEAD_DIM] = buf[hh]


def _dil_rows(d, r, size):
    return pl.ds(0, size) if d == 1 else pl.ds(r, size, stride=d)


def _dil_operands(d, tb, units, q_ref, kc_ref, kp_ref, vc_ref, vp_ref, qs, kf, vf):
    _dil_stage(q_ref, qs)
    _dil_stage(kp_ref, kf)
    _dil_stage(kc_ref, kf, tb)
    _dil_stage(vp_ref, vf)
    _dil_stage(vc_ref, vf, tb)
    qu = [qs[hh, _dil_rows(d, r, BLOCK), :].astype(MXU_DTYPE) for hh, r in units]
    ku = [kf[hh, _dil_rows(d, r, 2 * BLOCK), :].astype(MXU_DTYPE) for hh, r in units]
    vu = [vf[hh, _dil_rows(d, r, 2 * BLOCK), :].astype(MXU_DTYPE) for hh, r in units]
    return qu, ku, vu


def _dilated_fwd(q, k, mid, d, HW):
    S = q.shape[0]
    scale = HEAD_DIM ** -0.5
    ng, hb, tb, cur, prev, units = _dil_geometry(d, HW)

    def kern(q_ref, kc_ref, kp_ref, vc_ref, vp_ref, o_ref, l_ref, qs, kf, vf, os_, ls):
        mask = _dil_mask(pl.program_id(0))
        qu, ku, vu = _dil_operands(d, tb, units, q_ref, kc_ref, kp_ref, vc_ref, vp_ref, qs, kf, vf)
        sc = [jnp.where(mask, _dot(a, b, NT) * scale, NEG) for a, b in zip(qu, ku)]
        mx = [jnp.max(t, axis=1, keepdims=True) for t in sc]
        ex = [jnp.exp(t - m) for t, m in zip(sc, mx)]
        den = [jnp.sum(t, axis=1, keepdims=True) for t in ex]
        out = [_dot(t.astype(MXU_DTYPE), v) / dn for t, v, dn in zip(ex, vu, den)]
        for (hh, r), o, m, dn in zip(units, out, mx, den):
            os_[hh, _dil_rows(d, r, BLOCK), :] = o
            ls[hh, _dil_rows(d, r, BLOCK), :] = jnp.broadcast_to(m + jnp.log(dn), (BLOCK, HEAD_DIM))
        _dil_unstage(os_, o_ref)
        _dil_unstage(ls, l_ref)

    return pl.pallas_call(
        kern, name=f"dilated_fwd_d{d}", grid=(S // tb, ng),
        in_specs=[cur, cur, prev, cur, prev],
        out_specs=[cur, cur],
        out_shape=[jax.ShapeDtypeStruct((S, HW), F32)] * 2,
        scratch_shapes=[pltpu.VMEM((hb, tb, HEAD_DIM), F32)] + [pltpu.VMEM((hb, 2 * tb, HEAD_DIM), F32)] * 2
        + [pltpu.VMEM((hb, tb, HEAD_DIM), F32)] * 2,
        compiler_params=_params(("arbitrary", "arbitrary")),
    )(q, k, k, mid, mid)


def _dilated_combine(os_, lses, HW):
    S = os_[0].shape[0]
    tr = _fit(S, 256)

    def body(o0, o1, o2, l0, l1, l2, ya_ref, lse_ref):
        a, b, c = l0[...], l1[...], l2[...]
        mx = jnp.maximum(jnp.maximum(a, b), c)
        ea, eb, ec = jnp.exp(a - mx), jnp.exp(b - mx), jnp.exp(c - mx)
        tot = ea + eb + ec
        ya_ref[...] = (ea * o0[...] + eb * o1[...] + ec * o2[...]) / tot
        lse_ref[...] = mx + jnp.log(tot)

    ins = [(a, _row(tr, HW)) for a in list(os_) + list(lses)]
    return _rows_call("dilated_combine", body, S, tr, ins, [(HW, F32), (HW, F32)])


def _dilated_bwd(q, k, mid, dya, ya, lse, d, HW):
    S = q.shape[0]
    scale = HEAD_DIM ** -0.5
    ng, hb, tb, cur, prev, units = _dil_geometry(d, HW)

    def kern(q_ref, kc_ref, kp_ref, vc_ref, vp_ref, dy_ref, y_ref, l_ref, dq_ref, dkc_ref, dkp_ref, dvc_ref, dvp_ref,
             qs, kf, vf, dys, ys, ls, dqs, dkcs, dkps, dvcs, dvps):
        mask = _dil_mask(pl.program_id(0))
        qu, ku, vu = _dil_operands(d, tb, units, q_ref, kc_ref, kp_ref, vc_ref, vp_ref, qs, kf, vf)
        _dil_stage(dy_ref, dys)
        _dil_stage(y_ref, ys)
        _dil_stage(l_ref, ls)
        dy = [dys[hh, _dil_rows(d, r, BLOCK), :] for hh, r in units]
        lt = [ls[hh, _dil_rows(d, r, BLOCK), :][:, 0:1] for hh, r in units]
        delta = [jnp.sum(t * ys[hh, _dil_rows(d, r, BLOCK), :], axis=1, keepdims=True) for t, (hh, r) in zip(dy, units)]
        dyb = [t.astype(MXU_DTYPE) for t in dy]
        p = [jnp.where(mask, jnp.exp(_dot(a, b, NT) * scale - l), 0.0) for a, b, l in zip(qu, ku, lt)]
        ds = [(t * (_dot(g, v, NT) - dl) * scale).astype(MXU_DTYPE) for t, g, v, dl in zip(p, dyb, vu, delta)]
        dq = [_dot(t, b) for t, b in zip(ds, ku)]
        dk = [_dot(t, a, TN) for t, a in zip(ds, qu)]
        dv = [_dot(t.astype(MXU_DTYPE), g, TN) for t, g in zip(p, dyb)]
        for (hh, r), tq, tk, tv in zip(units, dq, dk, dv):
            at = _dil_rows(d, r, BLOCK)
            dqs[hh, at, :] = tq
            dkps[hh, at, :] = tk[0:BLOCK]
            dkcs[hh, at, :] = tk[BLOCK:2 * BLOCK]
            dvps[hh, at, :] = tv[0:BLOCK]
            dvcs[hh, at, :] = tv[BLOCK:2 * BLOCK]
        for buf, ref in ((dqs, dq_ref), (dkcs, dkc_ref), (dkps, dkp_ref), (dvcs, dvc_ref), (dvps, dvp_ref)):
            _dil_unstage(buf, ref)

    return pl.pallas_call(
        kern, name=f"dilated_bwd_d{d}", grid=(S // tb, ng),
        in_specs=[cur, cur, prev, cur, prev, cur, cur, cur],
        out_specs=[cur] * 5,
        out_shape=[jax.ShapeDtypeStruct((S, HW), F32)] * 5,
        scratch_shapes=[pltpu.VMEM((hb, tb, HEAD_DIM), F32)] + [pltpu.VMEM((hb, 2 * tb, HEAD_DIM), F32)] * 2
        + [pltpu.VMEM((hb, tb, HEAD_DIM), F32)] * 8,
        compiler_params=_params(("arbitrary", "arbitrary")),
    )(q, k, k, mid, mid, dya, ya, lse)


def _softplus_parts(z):
    lg = jnp.log(1.0 + jnp.exp(-jnp.abs(z)))
    return -jnp.maximum(z, 0.0) - lg, jnp.minimum(z, 0.0) - lg


def _sb_specs(S, H):
    q_spec = pl.BlockSpec((SB_BQ, HEAD_DIM), lambda h, i: (i, H + h))
    k_spec = pl.BlockSpec((S, HEAD_DIM), lambda h, i: (0, 2 * H + h))
    v_spec = pl.BlockSpec((S, HEAD_DIM), lambda h, i: (0, 3 * H + h))
    o_spec = pl.BlockSpec((SB_BQ, HEAD_DIM), lambda h, i: (i, h))
    return q_spec, k_spec, v_spec, o_spec


def _sb_tri(relation):
    tri_r = lax.broadcasted_iota(jnp.int32, (SB_BK, SB_BK), 0)
    tri_c = lax.broadcasted_iota(jnp.int32, (SB_BK, SB_BK), 1)
    return relation(tri_r, tri_c).astype(MXU_DTYPE)


def _sb_scratch():
    return [pltpu.VMEM((2, SB_BQ, SB_BQ), MXU_DTYPE), pltpu.VMEM((2, SB_BQ, SB_BQ), MXU_DTYPE), pltpu.SemaphoreType.DMA((2, 2))]


def _sb_fwd(mid, HW, ex=None):
    S = mid.shape[0]
    H = HW // HEAD_DIM
    BQ, CH = SB_BQ, SB_BK
    NC, nq = BQ // CH, S // BQ
    scale = HEAD_DIM ** -0.5
    q_spec, k_spec, v_spec, o_spec = _sb_specs(S, H)

    def kern(q_ref, k_ref, v_ref, o_ref, a_hbm, s_hbm, abuf, sbuf, sems):
        h, i = pl.program_id(0), pl.program_id(1)
        q = q_ref[...]
        upper = _sb_tri(lambda r, c: r > c)
        row = lax.broadcasted_iota(jnp.int32, (BQ, BQ), 0)
        causal = lax.broadcasted_iota(jnp.int32, (BQ, BQ), 1) < row

        def save(slot, j):
            return [pltpu.make_async_copy(buf.at[slot], hbm.at[h, i, j], sems.at[w, slot])
                    for w, (buf, hbm) in enumerate(((abuf, a_hbm), (sbuf, s_hbm)))]

        def block(n, run, acc, masked):
            j = i - n
            ks = pl.multiple_of(j * BQ, BQ)
            z = _dot(q, k_ref[pl.ds(ks, BQ), :], NT) * scale
            m, l = _softplus_parts(z)
            if masked:
                m = jnp.where(causal, m, 0.0)
            parts = []
            for c in reversed(range(NC)):
                mc = m[:, c * CH:(c + 1) * CH]
                parts.append(l[:, c * CH:(c + 1) * CH] + (_split_dot(mc, upper, CUMSUM_PASSES) + run))
                run = run + jnp.sum(mc, axis=1, keepdims=True)
            a = jnp.exp(jnp.concatenate(parts[::-1], axis=1))
            sig = jnp.exp(l)
            if masked:
                a = jnp.where(causal, a, 0.0)
                sig = jnp.where(causal, sig, 0.0)
            ab = a.astype(MXU_DTYPE)
            slot = n % 2
            if not masked:
                @pl.when(n >= 2)
                def _():
                    for cp in save(slot, j):
                        cp.wait()
            abuf[slot] = ab
            sbuf[slot] = sig.astype(MXU_DTYPE)
            for cp in save(slot, j):
                cp.start()
            return run, acc + _dot(ab, v_ref[pl.ds(ks, BQ), :])

        run, acc = block(0, jnp.zeros((BQ, 1), F32), jnp.zeros((BQ, HEAD_DIM), F32), True)
        run, acc = lax.fori_loop(1, i + 1, lambda n, carry: block(n, carry[0], carry[1], False), (run, acc))
        o_ref[...] = acc
        for cp in save(0, i):
            cp.wait()

        @pl.when(i >= 1)
        def _():
            for cp in save(1, i):
                cp.wait()

    tiles = jax.ShapeDtypeStruct((H, nq, nq, BQ, BQ), MXU_DTYPE)
    (o, a_t, s_t), rode = _host_call(kern, "stickbreak_fwd", (H, nq), [q_spec, k_spec, v_spec], [o_spec, ANY, ANY],
                                     [jax.ShapeDtypeStruct((S, HW), F32), tiles, tiles], (mid, mid, mid), ex, _sb_scratch())
    return o, a_t, s_t, rode


def _sb_bwd(mid, dyb, a_t, s_t, HW, ex=None):
    S = mid.shape[0]
    H = HW // HEAD_DIM
    BQ, CH = SB_BQ, SB_BK
    NC = BQ // CH
    scale = HEAD_DIM ** -0.5
    q_spec, k_spec, v_spec, o_spec = _sb_specs(S, H)
    full = pl.BlockSpec((S, HEAD_DIM), lambda h, i: (0, h))

    def kern(q_ref, k_ref, v_ref, do_ref, a_hbm, s_hbm, dq_ref, dk_ref, dv_ref, abuf, sbuf, sems):
        h, i = pl.program_id(0), pl.program_id(1)

        @pl.when(i == 0)
        def _():
            dk_ref[...] = jnp.zeros_like(dk_ref)
            dv_ref[...] = jnp.zeros_like(dv_ref)

        q = q_ref[...]
        do = do_ref[...].astype(MXU_DTYPE)
        excl = _sb_tri(lambda r, c: r < c)

        def fetch(slot, j):
            return [pltpu.make_async_copy(hbm.at[h, i, j], buf.at[slot], sems.at[w, slot])
                    for w, (buf, hbm) in enumerate(((abuf, a_hbm), (sbuf, s_hbm)))]

        for cp in fetch(0, 0):
            cp.start()

        def block(j, carry):
            prun, dq = carry
            slot = j % 2
            for cp in fetch(slot, j):
                cp.wait()

            @pl.when(j < i)
            def _():
                for cp in fetch(1 - slot, j + 1):
                    cp.start()

            ks = pl.multiple_of(j * BQ, BQ)
            k = k_ref[pl.ds(ks, BQ), :]
            ab = abuf[slot]
            p = ab.astype(F32) * _dot(do, v_ref[pl.ds(ks, BQ), :], NT)
            parts = []
            for c in range(NC):
                pc = p[:, c * CH:(c + 1) * CH]
                parts.append(_split_dot(pc, excl, CUMSUM_PASSES) + prun)
                prun = prun + jnp.sum(pc, axis=1, keepdims=True)
            before = jnp.concatenate(parts, axis=1)
            dzb = ((p - sbuf[slot].astype(F32) * (p + before)) * scale).astype(MXU_DTYPE)
            dk_ref[pl.ds(ks, BQ), :] += _dot(dzb, q, TN)
            dv_ref[pl.ds(ks, BQ), :] += _dot(ab, do, TN)
            return prun, dq + _dot(dzb, k)

        _, dq = lax.fori_loop(0, i + 1, block, (jnp.zeros((BQ, 1), F32), jnp.zeros((BQ, HEAD_DIM), F32)))
        dq_ref[...] = dq

    grads, rode = _host_call(kern, "stickbreak_bwd", (H, S // BQ), [q_spec, k_spec, v_spec, o_spec, ANY, ANY], [o_spec, full, full],
                             [jax.ShapeDtypeStruct((S, HW), F32)] * 3, (mid, mid, mid, dyb, a_t, s_t), ex, _sb_scratch())
    return grads, rode


ANY = pl.BlockSpec(memory_space=pl.ANY)


def _place():
    x, y, c = lax.axis_index("x"), lax.axis_index("y"), lax.axis_index("c")
    chips = [(1 - x, y), (x, 1 - y), (1 - x, 1 - y)]
    return x, y, c, chips


def _half(ref, shard, hc, rh):
    return ref.at[shard, pl.ds(pl.multiple_of(hc * rh, 8), rh), :]


def _cast_place(name, w, s_idx):
    R, C = w.shape
    tr = _fit(R, 256)

    def kern(s_ref, w_ref, o_ref):
        o_ref[...] = w_ref[...].astype(o_ref.dtype)

    return pl.pallas_call(
        kern, name=name,
        grid_spec=pltpu.PrefetchScalarGridSpec(
            num_scalar_prefetch=1, grid=(R // tr,),
            in_specs=[pl.BlockSpec((tr, C), lambda r, s_ref: (r, 0))],
            out_specs=pl.BlockSpec((None, tr, C), lambda r, s_ref: (s_ref[0], r, 0))),
        out_shape=jax.ShapeDtypeStruct((N_SHARD, R, C), MXU_DTYPE),
        compiler_params=_params(("arbitrary",)),
    )(s_idx, w)


class _Exchange:
    def __init__(self, inputs, out_shape, aliases, scratch, phases):
        self.inputs, self.out_shape, self.aliases, self.scratch, self.phases = inputs, out_shape, aliases, scratch, phases


def _run_alone(name, ex):
    ni, no = len(ex.inputs), len(ex.out_shape)

    def body(*refs):
        for phase in ex.phases:
            phase(refs[:ni], refs[ni:ni + no], refs[ni + no:])

    return pl.pallas_call(
        body, name=name, in_specs=[ANY] * ni, out_specs=[ANY] * no, out_shape=ex.out_shape,
        input_output_aliases=ex.aliases, scratch_shapes=ex.scratch,
    )(*ex.inputs)


def _host_call(kern, name, grid, in_specs, out_specs, out_shape, operands, ex, scratch=()):
    sem = ("arbitrary",) * len(grid)
    if ex is None:
        return pl.pallas_call(kern, name=name, grid=grid, in_specs=in_specs, out_specs=out_specs, out_shape=out_shape,
                              scratch_shapes=list(scratch), compiler_params=_params(sem))(*operands), []
    n_in, n_out, ri, ro, ns = len(in_specs), len(out_specs), len(ex.inputs), len(ex.out_shape), len(scratch)
    nsteps, nph = 1, len(ex.phases)
    for size in grid:
        nsteps *= size

    def body(*refs):
        r_in, r_out = refs[n_in:n_in + ri], refs[n_in + ri + n_out:n_in + ri + n_out + ro]
        host_scratch, ex_scratch = refs[n_in + ri + n_out + ro:][:ns], refs[n_in + ri + n_out + ro + ns:]
        step = 0
        for axis, size in enumerate(grid):
            step = step * size + pl.program_id(axis)
        for kph, phase in enumerate(ex.phases):
            pl.when(step == (kph * (nsteps - 1)) // (nph - 1))(functools.partial(phase, r_in, r_out, ex_scratch))
        kern(*refs[:n_in], *refs[n_in + ri:n_in + ri + n_out], *host_scratch)

    outs = pl.pallas_call(
        body, name=name, grid=grid, in_specs=list(in_specs) + [ANY] * ri, out_specs=list(out_specs) + [ANY] * ro,
        out_shape=list(out_shape) + list(ex.out_shape), scratch_shapes=list(scratch) + list(ex.scratch),
        input_output_aliases={n_in + a: n_out + b for a, b in ex.aliases.items()},
        compiler_params=_params(sem),
    )(*operands, *ex.inputs)
    return outs[:n_out], outs[n_out:]


def _gather_exchange(bufs):
    n = len(bufs)

    def between_chips(outs, sems, i, j, chip, c, shard):
        blk = _half(outs[i], shard, c, outs[i].shape[1] // 2)
        return pltpu.make_async_remote_copy(src_ref=blk, dst_ref=blk, send_sem=sems[0].at[i, j], recv_sem=sems[1].at[i, j],
                                            device_id=(chip[0], chip[1], c), device_id_type=MESH)

    def to_sibling(outs, sems, i, j, x, y, c, shard, hc):
        blk = _half(outs[i], shard, hc, outs[i].shape[1] // 2)
        return pltpu.make_async_remote_copy(src_ref=blk, dst_ref=blk, send_sem=sems[0].at[i, 3 + j], recv_sem=sems[1].at[i, 3 + j],
                                            device_id=(x, y, 1 - c), device_id_type=MESH)

    def send_mine(ins, outs, sems):
        x, y, c, chips = _place()
        for i in range(n):
            for j, chip in enumerate(chips):
                between_chips(outs, sems, i, j, chip, c, 2 * x + y).start()

    def pass_on(ins, outs, sems):
        x, y, c, chips = _place()
        for i in range(n):
            for j, chip in enumerate(chips):
                between_chips(outs, sems, i, j, chip, c, 2 * chip[0] + chip[1]).wait_recv()
                to_sibling(outs, sems, i, j, x, y, c, 2 * chip[0] + chip[1], c).start()

    def finish(ins, outs, sems):
        x, y, c, chips = _place()
        for i in range(n):
            for j, chip in enumerate(chips):
                to_sibling(outs, sems, i, j, x, y, c, 2 * chip[0] + chip[1], 1 - c).wait_recv()
        for i in range(n):
            for j, chip in enumerate(chips):
                between_chips(outs, sems, i, j, chip, c, 2 * x + y).wait_send()
                to_sibling(outs, sems, i, j, x, y, c, 2 * chip[0] + chip[1], c).wait_send()

    return _Exchange(list(bufs), [jax.ShapeDtypeStruct(b.shape, b.dtype) for b in bufs], {i: i for i in range(n)},
                     [pltpu.SemaphoreType.DMA((n, 6)), pltpu.SemaphoreType.DMA((n, 6))], [send_mine, pass_on, finish])


def _reduce_exchange(g16, g32):
    n = len(g16)

    def copies(ins, outs, sems):
        x, y, c, _ = _place()
        for i in range(n):
            rh = ins[i].shape[1] // 2
            for r in range(1, 8):
                px, py, pc = x ^ ((r >> 2) & 1), y ^ ((r >> 1) & 1), c ^ (r & 1)
                src = _half(ins[i] if r > 1 else ins[n + i], 2 * px + py, pc, rh)
                dst = outs[2 * i + 1].at[r - 2] if r > 1 else outs[2 * i]
                yield pltpu.make_async_remote_copy(src_ref=src, dst_ref=dst, send_sem=sems[0].at[i, r - 1], recv_sem=sems[1].at[i, r - 1],
                                                   device_id=(px, py, pc), device_id_type=MESH)

    def start(ins, outs, sems):
        for cp in copies(ins, outs, sems):
            cp.start()

    def finish(ins, outs, sems):
        for cp in copies(ins, outs, sems):
            cp.wait()

    out_shape = []
    for g in g16:
        rh, C = g.shape[1] // 2, g.shape[2]
        out_shape += [jax.ShapeDtypeStruct((rh, C), F32), jax.ShapeDtypeStruct((6, rh, C), g.dtype)]
    return _Exchange(list(g16) + list(g32), out_shape, {},
                     [pltpu.SemaphoreType.DMA((n, 7)), pltpu.SemaphoreType.DMA((n, 7))], [start, finish])


def _add_direct(name, g32, from_sibling, from_chips, s_idx, c_idx):
    _, R, C = g32.shape
    rh = R // 2
    tr = _fit(rh, 256)
    nrb = rh // tr

    def kern(s_ref, c_ref, g_ref, a_ref, b_ref, out_ref):
        acc = g_ref[...] + a_ref[...]
        for k in range(6):
            acc = acc + b_ref[k].astype(F32)
        out_ref[...] = acc

    return pl.pallas_call(
        kern, name=name,
        grid_spec=pltpu.PrefetchScalarGridSpec(
            num_scalar_prefetch=2, grid=(nrb,),
            in_specs=[pl.BlockSpec((None, tr, C), lambda r, s_ref, c_ref: (s_ref[0], c_ref[0] * nrb + r, 0)),
                      pl.BlockSpec((tr, C), lambda r, s_ref, c_ref: (r, 0)),
                      pl.BlockSpec((6, tr, C), lambda r, s_ref, c_ref: (0, r, 0))],
            out_specs=pl.BlockSpec((tr, C), lambda r, s_ref, c_ref: (r, 0))),
        out_shape=jax.ShapeDtypeStruct((rh, C), F32),
        compiler_params=_params(("arbitrary",)),
    )(s_idx, c_idx, g32, from_sibling, from_chips)


def _swap_reduced(halves):
    n = len(halves)

    def body(*refs):
        ins, outs = refs[:n], refs[n:2 * n]
        send, recv = refs[2 * n:]
        x, y, c, _ = _place()
        copies = []
        for i in range(n):
            cp = pltpu.make_async_remote_copy(src_ref=ins[i], dst_ref=outs[i], send_sem=send.at[i], recv_sem=recv.at[i],
                                              device_id=(x, y, 1 - c), device_id_type=MESH)
            cp.start()
            copies.append(cp)
        for cp in copies:
            cp.wait()

    return pl.pallas_call(
        body, name="grad_swap_reduced",
        in_specs=[ANY] * n, out_specs=[ANY] * n,
        out_shape=[jax.ShapeDtypeStruct(h.shape, F32) for h in halves],
        scratch_shapes=[pltpu.SemaphoreType.DMA((n,)), pltpu.SemaphoreType.DMA((n,))],
    )(*halves)


def _all_reduce_small(v):
    rows, W = v.shape
    ndev = 8

    def body(v_ref, out_ref, buf, send, recv):
        x, y, c, _ = _place()
        me = 4 * x + 2 * y + c
        buf[me] = v_ref[...]
        copies = []
        for r in range(1, ndev):
            fx, fy, fc = (r >> 2) & 1, (r >> 1) & 1, r & 1
            peer = (x ^ fx, y ^ fy, c ^ fc)
            cp = pltpu.make_async_remote_copy(src_ref=v_ref, dst_ref=buf.at[me], send_sem=send.at[r - 1], recv_sem=recv.at[r - 1],
                                              device_id=peer, device_id_type=MESH)
            cp.start()
            copies.append(cp)
        for cp in copies:
            cp.wait()
        acc = buf[0]
        for k in range(1, ndev):
            acc = acc + buf[k]
        out_ref[...] = acc

    return pl.pallas_call(
        body, name="small_all_reduce",
        in_specs=[pl.BlockSpec(memory_space=pltpu.VMEM)], out_specs=pl.BlockSpec(memory_space=pltpu.VMEM),
        out_shape=jax.ShapeDtypeStruct((rows, W), F32),
        scratch_shapes=[pltpu.VMEM((ndev, rows, W), F32), pltpu.SemaphoreType.DMA((ndev - 1,)), pltpu.SemaphoreType.DMA((ndev - 1,))],
    )(v)


def _adamw_update(gv, w_ref, m_ref, v_ref, d_ref, nm_ref, nv_ref):
    nm = ADAM_B1 * m_ref[...] + (1.0 - ADAM_B1) * gv
    nv = ADAM_B2 * v_ref[...] + (1.0 - ADAM_B2) * (gv * gv)
    m_hat = nm / (1.0 - ADAM_B1 ** ADAM_STEP)
    v_hat = nv / (1.0 - ADAM_B2 ** ADAM_STEP)
    d_ref[...] = -ADAM_LR * (m_hat / (jnp.sqrt(v_hat) + ADAM_EPS) + ADAM_WD * w_ref[...])
    nm_ref[...] = nm
    nv_ref[...] = nv


def _adamw(name, w, g, m, v):
    R, C = w.shape
    tr = _fit(R, 256)

    def body(w_ref, g_ref, m_ref, v_ref, d_ref, nm_ref, nv_ref):
        _adamw_update(g_ref[...], w_ref, m_ref, v_ref, d_ref, nm_ref, nv_ref)

    return _rows_call(name, body, R, tr, [(a, _row(tr, C)) for a in (w, g, m, v)], [(C, F32)] * 3)


def _adamw_halves(name, w, mine, theirs, m, v, c_idx):
    R, C = w.shape
    rh = R // 2
    tr = _fit(rh, 256)
    nrb = rh // tr

    def kern(c_ref, w_ref, a_ref, b_ref, m_ref, v_ref, g_ref, d_ref, nm_ref, nv_ref):
        gv = jnp.where(pl.program_id(0) // nrb == c_ref[0], a_ref[...], b_ref[...])
        g_ref[...] = gv
        _adamw_update(gv, w_ref, m_ref, v_ref, d_ref, nm_ref, nv_ref)

    full = pl.BlockSpec((tr, C), lambda r, c_ref: (r, 0))
    pick = lambda own: pl.BlockSpec((tr, C), lambda r, c_ref: (jnp.where((r // nrb == c_ref[0]) == own, r % nrb, 0), 0))
    return pl.pallas_call(
        kern, name=name,
        grid_spec=pltpu.PrefetchScalarGridSpec(
            num_scalar_prefetch=1, grid=(R // tr,),
            in_specs=[full, pick(True), pick(False), full, full], out_specs=[full] * 4),
        out_shape=[jax.ShapeDtypeStruct((R, C), F32)] * 4,
        compiler_params=_params(("arbitrary",)),
    )(c_idx, w, mine, theirs, m, v)


def _sigmoid(z):
    return 1.0 / (1.0 + jnp.exp(-z))


def kernel(x, p, g_mix, w_in, qn_gain, kn_gain, w_branch_a, w_branch_b, w_out, g_mlp, w_up, w_down, g_ple, w_ple_gate, w_ple_proj, loss_target, m_g_mix, m_w_in, m_qn_gain, m_kn_gain, m_w_branch_a, m_w_branch_b, m_w_out, m_g_mlp, m_w_up, m_w_down, m_g_ple, m_w_ple_gate, m_w_ple_proj, v_g_mix, v_w_in, v_qn_gain, v_kn_gain, v_w_branch_a, v_w_branch_b, v_w_out, v_g_mlp, v_w_up, v_w_down, v_g_ple, v_w_ple_gate, v_w_ple_proj):
    S, D = x.shape[1], x.shape[2]
    HW = w_branch_a.shape[1]
    x2d, tgt, p2d = x.reshape(S, D), loss_target.reshape(S, D), p.reshape(S, p.shape[-1])
    big = {"w_in": w_in, "w_branch_a": w_branch_a, "w_branch_b": w_branch_b, "w_out": w_out, "w_up": w_up,
           "w_down": w_down, "w_ple_gate": w_ple_gate, "w_ple_proj": w_ple_proj}
    moments = {"w_in": (m_w_in, v_w_in), "w_branch_a": (m_w_branch_a, v_w_branch_a), "w_branch_b": (m_w_branch_b, v_w_branch_b),
               "w_out": (m_w_out, v_w_out), "w_up": (m_w_up, v_w_up), "w_down": (m_w_down, v_w_down),
               "w_ple_gate": (m_w_ple_gate, v_w_ple_gate), "w_ple_proj": (m_w_ple_proj, v_w_ple_proj)}
    names = list(big)
    col_sharded = {"w_in", "w_branch_a", "w_branch_b", "w_up", "w_ple_proj"}
    shard2d = {k: w.reshape(w.shape[1], w.shape[2]) for k, w in big.items()}

    c_idx = lax.axis_index("c").astype(jnp.int32).reshape(1)
    s_idx = (2 * lax.axis_index("x") + lax.axis_index("y")).astype(jnp.int32).reshape(1)
    placed = {k: _cast_place(f"cast_{k}", shard2d[k], s_idx) for k in names}
    late = [k for k in names if k != "w_in"]
    W = {"w_in": _run_alone("gather_w_in", _gather_exchange([placed["w_in"]]))[0]}
    cin = W["w_in"].shape[2]
    bn_in = _fit(cin, 512)
    while (2 * HW) % bn_in:
        bn_in -= 128

    h = _rmsnorm_fwd("rmsnorm_mix", x2d, g_mix)
    (qk,) = _matmul("proj_qk", h, W["w_in"], mode="nn", out_dtypes=[F32], b_cshard=True, b_off=0, n_out=2 * HW, bn=bn_in, bk=D)
    (mid,) = _matmul("proj_mid", h, W["w_in"], mode="nn", out_dtypes=[MXU_DTYPE], b_cshard=True, b_off=2 * HW // bn_in,
                     n_out=4 * HW, bn=bn_in, bk=D)
    (sg,) = _matmul("proj_gates", h, W["w_in"], mode="nn", out_dtypes=[F32], b_cshard=True, b_off=6 * HW // bn_in,
                    n_out=2 * D, bn=bn_in, bk=D, epilogue=lambda acc: (_sigmoid(acc),))
    tabs = _rope_tables(S)
    qa, ka = _qknorm_fwd(qk, qn_gain, kn_gain, tabs, HW)
    dil = [_dilated_fwd(qa, ka, mid, d, HW) for d in DILATIONS]
    ya, lse = _dilated_combine([o for o, _ in dil], [l for _, l in dil], HW)
    yb, sb_a, sb_sig, gathered = _sb_fwd(mid, HW, _gather_exchange([placed[k] for k in late]))
    W.update({k: (g if k in col_sharded else g.reshape(-1, g.shape[2])) for k, g in zip(late, gathered)})

    gate_blocks = D // _fit(D, 1024)
    (ua,) = _matmul("branch_a", ya, W["w_branch_a"], mode="nn", out_dtypes=[F32], b_cshard=True, bn=_fit(W["w_branch_a"].shape[2], 1024))
    bn_b = _fit(W["w_branch_b"].shape[2], 1024)
    ub, merged = _matmul("branch_b_merge", yb, W["w_branch_b"], mode="nn", out_dtypes=[F32, MXU_DTYPE], b_cshard=True, bn=bn_b,
                         extras=[(sg, 0), (sg, D // bn_b), (ua, 0)],
                         epilogue=lambda acc, sga, sgb, uav: (acc, sga * uav + sgb * acc))
    (x1,) = _matmul("out_proj", merged, W["w_out"], mode="nn", out_dtypes=[F32], extras=[(x2d, 0)], epilogue=lambda acc, xv: (xv + acc,))
    hm = _rmsnorm_fwd("rmsnorm_mlp", x1, g_mlp)

    def up_epilogue(acc):
        r = jnp.maximum(acc, 0.0)
        return r * r, r

    act, rup = _matmul("mlp_up", hm, W["w_up"], mode="nn", out_dtypes=[MXU_DTYPE, MXU_DTYPE], b_cshard=True,
                       bn=_fit(W["w_up"].shape[2], 1024), bk=D, epilogue=up_epilogue)
    (x2,) = _matmul("mlp_down", act, W["w_down"], mode="nn", out_dtypes=[F32], extras=[(x1, 0)], epilogue=lambda acc, xv: (xv + acc,))
    hp = _rmsnorm_fwd("rmsnorm_ple", x2, g_ple)
    (pp,) = _matmul("ple_proj", p2d, W["w_ple_proj"], mode="nn", out_dtypes=[F32], b_cshard=True, bn=_fit(W["w_ple_proj"].shape[2], 1024))

    def ple_epilogue(acc, ppv, x2v, tv):
        s = _sigmoid(acc)
        dx3 = ((x2v + ppv * s) - tv) / D
        return dx3, dx3 * s, dx3 * ppv * (s * (1.0 - s))

    dx3, d_pp, d_gate = _matmul("ple_gate_loss", hp, W["w_ple_gate"], mode="nn", out_dtypes=[F32, MXU_DTYPE, MXU_DTYPE],
                                bm=512, extras=[(pp, 0), (x2, 0), (tgt, 0)], epilogue=ple_epilogue)

    G, G16 = {}, {}
    G["w_ple_proj"], G16["w_ple_proj"] = _matmul("grad_w_ple_proj", p2d, d_pp, mode="tn", out_dtypes=[F32, MXU_DTYPE], out_cshard=True,
                                 bn=_fit(d_pp.shape[1] // N_SHARD, 1024))
    G["w_ple_gate"], G16["w_ple_gate"] = _matmul("grad_w_ple_gate", hp, d_gate, mode="tn", out_dtypes=[F32, MXU_DTYPE])
    (d_hp,) = _matmul("ple_gate_bwd", d_gate, W["w_ple_gate"], mode="nt", out_dtypes=[F32])
    dx2, g_g_ple, loss_part = _rmsnorm_bwd("rmsnorm_ple_bwd", d_hp, x2, g_ple, dx3, True)
    G["w_down"], G16["w_down"] = _matmul("grad_w_down", act, dx2, mode="tn", out_dtypes=[F32, MXU_DTYPE])
    (d_up,) = _matmul("mlp_down_bwd", dx2, W["w_down"], mode="nt", out_dtypes=[MXU_DTYPE], extras=[(rup, 0)],
                      epilogue=lambda acc, r: (acc * (2.0 * r.astype(F32)),))
    G["w_up"], G16["w_up"] = _matmul("grad_w_up", hm, d_up, mode="tn", out_dtypes=[F32, MXU_DTYPE], out_cshard=True, bn=_fit(d_up.shape[1] // N_SHARD, 1024))
    (d_hm,) = _matmul("mlp_up_bwd", d_up, W["w_up"], mode="nt", out_dtypes=[F32], b_cshard=True, bk=_fit(W["w_up"].shape[2], 2048))
    dx1, g_g_mlp = _rmsnorm_bwd("rmsnorm_mlp_bwd", d_hm, x1, g_mlp, dx2, False)
    G["w_out"], G16["w_out"] = _matmul("grad_w_out", merged, dx1, mode="tn", out_dtypes=[F32, MXU_DTYPE])

    def merge_bwd(acc, sga, sgb, uav, ubv):
        return acc * sga, acc * sgb, acc * uav * (sga * (1.0 - sga)), acc * ubv * (sgb * (1.0 - sgb))

    bn_m = _fit(D, 1024)
    d_ua, d_ub, d_ga, d_gb = _matmul("out_proj_bwd", dx1, W["w_out"], mode="nt", out_dtypes=[MXU_DTYPE] * 4, bm=512, bn=bn_m,
                                     extras=[(sg, 0), (sg, D // bn_m), (ua, 0), (ub, 0)], epilogue=merge_bwd)
    bn_br = _fit(D // N_SHARD, 1024)
    G["w_branch_a"], G16["w_branch_a"] = _matmul("grad_w_branch_a", ya, d_ua, mode="tn", out_dtypes=[F32, MXU_DTYPE], out_cshard=True, bn=bn_br)
    G["w_branch_b"], G16["w_branch_b"] = _matmul("grad_w_branch_b", yb, d_ub, mode="tn", out_dtypes=[F32, MXU_DTYPE], out_cshard=True, bn=bn_br)
    (d_ya,) = _matmul("branch_a_bwd", d_ua, W["w_branch_a"], mode="nt", out_dtypes=[F32], b_cshard=True, bk=bn_br)
    (d_yb,) = _matmul("branch_b_bwd", d_ub, W["w_branch_b"], mode="nt", out_dtypes=[F32], b_cshard=True, bk=bn_br)

    as_shards = lambda k, g: g if k in col_sharded else g.reshape(N_SHARD, -1, g.shape[1])
    (dqb, dkb, dvb), partials = _sb_bwd(mid, d_yb, sb_a, sb_sig, HW, _reduce_exchange([as_shards(k, G16[k]) for k in late],
                                                                               [as_shards(k, G[k]) for k in late]))
    dil_b = [_dilated_bwd(qa, ka, mid, d_ya, ya, lse, d, HW) for d in DILATIONS]
    d_qk, g_qn, g_kn = _qknorm_bwd(qk, qn_gain, kn_gain, tabs, [t[0] for t in dil_b], [t[1] for t in dil_b],
                                   [t[2] for t in dil_b], HW)
    dva = _dv_sum([t[3] for t in dil_b], [t[4] for t in dil_b], HW)
    d_proj = jnp.concatenate([d_qk, dva, dqb.astype(MXU_DTYPE), dkb.astype(MXU_DTYPE), dvb.astype(MXU_DTYPE), d_ga, d_gb], axis=1)
    G["w_in"], G16["w_in"] = _matmul("grad_w_in", h, d_proj, mode="tn", out_dtypes=[F32, MXU_DTYPE], out_cshard=True, bn=_fit(cin, 1280))
    (d_h,), partials_in = _matmul("proj_bwd", d_proj, W["w_in"], mode="nt", out_dtypes=[F32], b_cshard=True, bk=_fit(cin, 1280),
                                  ride=_reduce_exchange([G16["w_in"]], [G["w_in"]]))
    grad_x, g_g_mix = _rmsnorm_bwd("rmsnorm_mix_bwd", d_h, x2d, g_mix, dx1, False)

    mine = {k: _add_direct(f"grad_add_{k}", as_shards(k, G[k]), partials[2 * n], partials[2 * n + 1], s_idx, c_idx)
            for n, k in enumerate(late)}
    mine["w_in"] = _add_direct("grad_add_w_in", G["w_in"], partials_in[0], partials_in[1], s_idx, c_idx)
    halves = [mine[k] for k in names]
    others = _swap_reduced(halves)

    pack_w = -(-(3 * D + 3 * 128) // (8 * 128)) * 128

    def pack(v_mix, v_mlp, v_ple, v_qn, v_kn, extra):
        flat = jnp.concatenate([v_mix.reshape(-1), v_mlp.reshape(-1), v_ple.reshape(-1), v_qn.reshape(-1), v_kn.reshape(-1), extra.reshape(-1)])
        return jnp.pad(flat, (0, 8 * pack_w - flat.shape[0])).reshape(8, pack_w)

    def unpack(blk):
        flat = blk.reshape(-1)
        return (flat[:D].reshape(1, D), flat[D:2 * D].reshape(1, D), flat[2 * D:3 * D].reshape(1, D),
                flat[3 * D:3 * D + 128].reshape(1, 128), flat[3 * D + 128:3 * D + 256].reshape(1, 128), flat[3 * D + 256])

    small = _all_reduce_small(pack(g_g_mix, g_g_mlp, g_g_ple, g_qn, g_kn, loss_part))
    sw = pack(g_mix, g_mlp, g_ple, qn_gain, kn_gain, jnp.zeros((128,), F32))
    sm = pack(m_g_mix, m_g_mlp, m_g_ple, m_qn_gain, m_kn_gain, jnp.zeros((128,), F32))
    sv = pack(v_g_mix, v_g_mlp, v_g_ple, v_qn_gain, v_kn_gain, jnp.ones((128,), F32))
    s_delta, s_nm, s_nv = _adamw("adamw_small", sw, small, sm, sv)
    sg_mix, sg_mlp, sg_ple, sg_qn, sg_kn, loss = unpack(small)
    small_out = {}
    for tag, blk in (("delta", s_delta), ("new_m", s_nm), ("new_v", s_nv)):
        u = unpack(blk)
        small_out[tag] = dict(g_mix=u[0], g_mlp=u[1], g_ple=u[2], qn_gain=u[3], kn_gain=u[4])
    small_grad = dict(g_mix=sg_mix, g_mlp=sg_mlp, g_ple=sg_ple, qn_gain=sg_qn, kn_gain=sg_kn)

    big_out = {"grad": {}, "delta": {}, "new_m": {}, "new_v": {}}
    for k, mine, theirs in zip(names, halves, others):
        shape = big[k].shape
        m2, v2 = (t.reshape(shape[1], shape[2]) for t in moments[k])
        res = _adamw_halves(f"adamw_{k}", shard2d[k], mine, theirs, m2, v2, c_idx)
        for tag, t in zip(("grad", "delta", "new_m", "new_v"), res):
            big_out[tag][k] = t.reshape(shape)

    order = ["g_mix", "w_in", "qn_gain", "kn_gain", "w_branch_a", "w_branch_b", "w_out", "g_mlp", "w_up", "w_down", "g_ple",
             "w_ple_gate", "w_ple_proj"]
    outs = [loss, grad_x.reshape(x.shape)]
    outs += [small_grad[k] if k in small_grad else big_out["grad"][k] for k in order]
    for tag in ("delta", "new_m", "new_v"):
        outs += [small_out[tag][k] if k in small_grad else big_out[tag][k] for k in order]
    return tuple(outs)
```

```python
import functools

import jax
import jax.numpy as jnp
from jax import lax
from jax.experimental import pallas as pl
from jax.experimental.pallas import tpu as pltpu

F32 = jnp.float32
MXU_DTYPE = jnp.bfloat16
HEAD_DIM = 128
ROT_DIM = HEAD_DIM // 4
ROPE_THETA = 500000.0
EPS = 1e-6
DILATIONS = (1, 4, 16)
BLOCK = 128
N_SHARD = 4
ADAM_LR, ADAM_B1, ADAM_B2, ADAM_EPS, ADAM_WD, ADAM_STEP = 0.001, 0.9, 0.999, 1e-08, 0.01, 10
V7X_VMEM_BYTES = 64 * 1024 * 1024
VMEM_LIMIT = V7X_VMEM_BYTES - 8 * 1024 * 1024
MESH = pl.DeviceIdType.MESH
NEG = -1e30
SB_BQ, SB_BK = 1024, 256
CUMSUM_PASSES = 2


def _fit(dim, pref):
    if dim <= pref:
        return dim
    b = (pref // 128) * 128
    while dim % b:
        b -= 128
    return b


def _params(sem=None):
    return pltpu.CompilerParams(dimension_semantics=sem, vmem_limit_bytes=VMEM_LIMIT)


def _dot(a, b, dims=(((1,), (0,)), ((), ()))):
    return lax.dot_general(a, b, dims, preferred_element_type=F32)


NT = (((1,), (1,)), ((), ()))
TN = (((0,), (0,)), ((), ()))


def _split_dot(x, u, passes):
    out = None
    r = x
    for p in range(passes):
        hi = r.astype(MXU_DTYPE)
        part = _dot(hi, u)
        out = part if out is None else out + part
        if p + 1 < passes:
            r = r - hi.astype(F32)
    return out


def _matmul(name, a, b, *, mode, out_dtypes, bm=1024, bn=1024, bk=2048, b_cshard=False, b_off=0, n_out=None,
            extras=(), epilogue=None, out_cshard=False, ride=None):
    if mode == "tn":
        K, M = a.shape
        N = b.shape[1]
    else:
        M, K = a.shape
        if mode == "nn":
            N = n_out if n_out is not None else (N_SHARD * b.shape[2] if b_cshard else b.shape[1])
        else:
            N = b.shape[1] if b_cshard else b.shape[0]
    bm, bn, bk = _fit(M, bm), _fit(N, bn), _fit(K, bk)
    nk = K // bk
    grid = (M // bm, N // bn, nk)

    if mode == "tn":
        a_spec = pl.BlockSpec((bk, bm), lambda i, j, k: (k, i))
        b_spec = pl.BlockSpec((bk, bn), lambda i, j, k: (k, j))
        dims = TN
    elif mode == "nn":
        a_spec = pl.BlockSpec((bm, bk), lambda i, j, k: (i, k))
        if b_cshard:
            cb = b.shape[2] // bn
            b_spec = pl.BlockSpec((None, bk, bn), lambda i, j, k: ((j + b_off) // cb, k, (j + b_off) % cb))
        else:
            b_spec = pl.BlockSpec((bk, bn), lambda i, j, k: (k, j + b_off))
        dims = (((1,), (0,)), ((), ()))
    else:
        a_spec = pl.BlockSpec((bm, bk), lambda i, j, k: (i, k))
        if b_cshard:
            cb = b.shape[2] // bk
            b_spec = pl.BlockSpec((None, bn, bk), lambda i, j, k: (k // cb, j, k % cb))
        else:
            b_spec = pl.BlockSpec((bn, bk), lambda i, j, k: (j, k))
        dims = NT

    ex_arrays = [e[0] for e in extras]
    ex_specs = [pl.BlockSpec((bm, bn), functools.partial(lambda i, j, k, off: (i, j + off), off=e[1])) for e in extras]
    if out_cshard:
        cbo = (N // N_SHARD) // bn
        out_shape = [jax.ShapeDtypeStruct((N_SHARD, M, N // N_SHARD), dt) for dt in out_dtypes]
        out_specs = [pl.BlockSpec((None, bm, bn), lambda i, j, k: (j // cbo, i, j % cbo)) for _ in out_dtypes]
    else:
        out_shape = [jax.ShapeDtypeStruct((M, N), dt) for dt in out_dtypes]
        out_specs = [pl.BlockSpec((bm, bn), lambda i, j, k: (i, j)) for _ in out_dtypes]
    ne, no = len(extras), len(out_dtypes)

    def kern(*refs):
        a_ref, b_ref = refs[0], refs[1]
        ex_refs = refs[2:2 + ne]
        o_refs = refs[2 + ne:2 + ne + no]
        part = _dot(a_ref[...].astype(MXU_DTYPE), b_ref[...].astype(MXU_DTYPE), dims)

        def finish(acc):
            vals = (acc,) * no if epilogue is None else epilogue(acc, *[r[...] for r in ex_refs])
            for r, v in zip(o_refs, vals):
                r[...] = v.astype(r.dtype)

        if nk == 1:
            finish(part)
        else:
            acc_ref = refs[2 + ne + no]
            k = pl.program_id(2)

            @pl.when(k == 0)
            def _():
                acc_ref[...] = part

            @pl.when(k > 0)
            def _():
                acc_ref[...] += part

            @pl.when(k == nk - 1)
            def _():
                finish(acc_ref[...])

    outs, rode = _host_call(kern, name, grid, [a_spec, b_spec] + ex_specs, out_specs, out_shape, (a, b, *ex_arrays), ride,
                            [pltpu.VMEM((bm, bn), F32)] if nk > 1 else [])
    return outs if ride is None else (outs, rode)


def _row(tr, w, coff=0):
    return pl.BlockSpec((tr, w), lambda i: (i, coff))


def _vec(w):
    return pl.BlockSpec((1, w), lambda i: (0, 0))


def _rows_call(name, body, n_rows, tr, ins, outs, accs=()):
    n_in, n_out = len(ins), len(outs)

    def kern(*refs):
        acc_refs = refs[n_in + n_out:]
        if acc_refs:
            @pl.when(pl.program_id(0) == 0)
            def _():
                for r in acc_refs:
                    r[...] = jnp.zeros_like(r)
        body(*refs)

    out_shape = [jax.ShapeDtypeStruct((n_rows, w), dt) for w, dt in outs] + [jax.ShapeDtypeStruct((1, w), F32) for w in accs]
    out_specs = [_row(tr, w) for w, _ in outs] + [_vec(w) for w in accs]
    return pl.pallas_call(
        kern, name=name, grid=(n_rows // tr,),
        in_specs=[s for _, s in ins], out_specs=out_specs, out_shape=out_shape,
        compiler_params=_params(("arbitrary",)),
    )(*[a for a, _ in ins])


def _rmsnorm_fwd(name, x, g):
    S, D = x.shape
    tr = _fit(S, 256)

    def body(x_ref, g_ref, h_ref):
        xv = x_ref[...]
        r = lax.rsqrt(jnp.mean(xv * xv, axis=1, keepdims=True) + EPS)
        h_ref[...] = ((xv * r) * g_ref[...]).astype(h_ref.dtype)

    return _rows_call(name, body, S, tr, [(x, _row(tr, D)), (g, _vec(D))], [(D, MXU_DTYPE)])[0]


def _rmsnorm_bwd(name, dh, x, g, resid, with_loss):
    S, D = x.shape
    tr = _fit(S, 256)

    def body(dh_ref, x_ref, g_ref, res_ref, dx_ref, dg_ref, *loss_ref):
        xv = x_ref[...]
        r = lax.rsqrt(jnp.mean(xv * xv, axis=1, keepdims=True) + EPS)
        dhv = dh_ref[...]
        u = dhv * g_ref[...]
        xr = xv * r
        dx = r * u - xr * (r * r) * jnp.mean(xv * u, axis=1, keepdims=True)
        resv = res_ref[...]
        dx_ref[...] = resv + dx
        dg_ref[...] += jnp.sum(dhv * xr, axis=0, keepdims=True)
        if with_loss:
            loss_ref[0][...] += (0.5 * D) * jnp.sum(resv * resv)

    outs = _rows_call(name, body, S, tr, [(dh, _row(tr, D)), (x, _row(tr, D)), (g, _vec(D)), (resid, _row(tr, D))],
                      [(D, F32)], accs=(D, 128) if with_loss else (D,))
    return outs


def _rope_tables(S):
    half = ROT_DIM // 2
    pos = jnp.arange(S, dtype=F32)
    inv = ROPE_THETA ** (-jnp.arange(0, ROT_DIM, 2, dtype=F32) / ROT_DIM)
    ang = pos[:, None] * inv[None, :]
    cos, sin = jnp.cos(ang), jnp.sin(ang)
    pad = HEAD_DIM - ROT_DIM
    ctab = jnp.concatenate([cos, cos, jnp.ones((S, pad), F32)], axis=1)
    atab = jnp.concatenate([-sin, jnp.zeros((S, pad + half), F32)], axis=1)
    btab = jnp.concatenate([jnp.zeros((S, half), F32), sin, jnp.zeros((S, pad), F32)], axis=1)
    return ctab, atab, btab


def _qknorm_fwd(qk, qn, kn, tabs, HW):
    S = qk.shape[0]
    tr = _fit(S, 256)
    half = ROT_DIM // 2

    def body(qk_ref, qn_ref, kn_ref, c_ref, a_ref, b_ref, q_out, k_out):
        ct, at, bt = c_ref[...], a_ref[...], b_ref[...]
        for part, (g_ref, o_ref) in enumerate(((qn_ref, q_out), (kn_ref, k_out))):
            gv = g_ref[...]
            for h in range(HW // HEAD_DIM):
                xh = qk_ref[:, part * HW + h * HEAD_DIM: part * HW + (h + 1) * HEAD_DIM]
                r = lax.rsqrt(jnp.mean(xh * xh, axis=1, keepdims=True) + EPS)
                y = (xh * r) * gv
                o = y * ct + pltpu.roll(y, HEAD_DIM - half, 1) * at + pltpu.roll(y, half, 1) * bt
                o_ref[:, h * HEAD_DIM:(h + 1) * HEAD_DIM] = o.astype(o_ref.dtype)

    ins = [(qk, _row(tr, 2 * HW)), (qn, _vec(HEAD_DIM)), (kn, _vec(HEAD_DIM))] + [(t, _row(tr, HEAD_DIM)) for t in tabs]
    return _rows_call("qknorm_fwd", body, S, tr, ins, [(HW, MXU_DTYPE), (HW, MXU_DTYPE)])


def _shift_spec(tr, w, shift, nblk):
    return pl.BlockSpec((tr, w), lambda i: (jnp.minimum(i + shift, nblk - 1), 0))


def _qknorm_bwd(qk, qn, kn, tabs, dq_parts, dk_cur, dk_prev, HW):
    S = qk.shape[0]
    tr = BLOCK
    nblk = S // tr
    half = ROT_DIM // 2
    nd = len(DILATIONS)

    def body(*refs):
        qk_ref, qn_ref, kn_ref, c_ref, a_ref, b_ref = refs[:6]
        dq_refs = refs[6:6 + nd]
        dkc_refs = refs[6 + nd:6 + 2 * nd]
        dkp_refs = refs[6 + 2 * nd:6 + 3 * nd]
        d_out, dqn_ref, dkn_ref = refs[6 + 3 * nd:]
        i = pl.program_id(0)
        ct, at, bt = c_ref[...], a_ref[...], b_ref[...]
        live = [(i + d < nblk).astype(F32) for d in DILATIONS]
        for part, (g_ref, dg_ref) in enumerate(((qn_ref, dqn_ref), (kn_ref, dkn_ref))):
            gv = g_ref[...]
            dg = jnp.zeros((1, HEAD_DIM), F32)
            for h in range(HW // HEAD_DIM):
                hs = slice(h * HEAD_DIM, (h + 1) * HEAD_DIM)
                if part == 0:
                    do = dq_refs[0][:, hs] + dq_refs[1][:, hs] + dq_refs[2][:, hs]
                else:
                    do = dkc_refs[0][:, hs] + dkc_refs[1][:, hs] + dkc_refs[2][:, hs]
                    for n in range(nd):
                        do = do + dkp_refs[n][:, hs] * live[n]
                dy = do * ct + pltpu.roll(do * at, half, 1) + pltpu.roll(do * bt, HEAD_DIM - half, 1)
                xh = qk_ref[:, part * HW + h * HEAD_DIM: part * HW + (h + 1) * HEAD_DIM]
                r = lax.rsqrt(jnp.mean(xh * xh, axis=1, keepdims=True) + EPS)
                xr = xh * r
                u = dy * gv
                dx = r * u - xr * (r * r) * jnp.mean(xh * u, axis=1, keepdims=True)
                d_out[:, part * HW + h * HEAD_DIM: part * HW + (h + 1) * HEAD_DIM] = dx.astype(d_out.dtype)
                dg = dg + jnp.sum(dy * xr, axis=0, keepdims=True)
            dg_ref[...] += dg

    ins = [(qk, _row(tr, 2 * HW)), (qn, _vec(HEAD_DIM)), (kn, _vec(HEAD_DIM))] + [(t, _row(tr, HEAD_DIM)) for t in tabs]
    ins += [(a, _row(tr, HW)) for a in dq_parts] + [(a, _row(tr, HW)) for a in dk_cur]
    ins += [(a, _shift_spec(tr, HW, d, nblk)) for a, d in zip(dk_prev, DILATIONS)]
    return _rows_call("qknorm_bwd", body, S, tr, ins, [(2 * HW, MXU_DTYPE)], accs=(HEAD_DIM, HEAD_DIM))


def _dv_sum(dv_cur, dv_prev, HW):
    S = dv_cur[0].shape[0]
    tr = BLOCK
    nblk = S // tr
    nd = len(DILATIONS)

    def body(*refs):
        i = pl.program_id(0)
        out = refs[2 * nd]
        acc = refs[0][...] + refs[1][...] + refs[2][...]
        for n, d in enumerate(DILATIONS):
            acc = acc + refs[nd + n][...] * (i + d < nblk).astype(F32)
        out[...] = acc.astype(out.dtype)

    ins = [(a, _row(tr, HW)) for a in dv_cur] + [(a, _shift_spec(tr, HW, d, nblk)) for a, d in zip(dv_prev, DILATIONS)]
    return _rows_call("dilated_dv_sum", body, S, tr, ins, [(HW, MXU_DTYPE)])[0]


def _dil_geometry(d, HW):
    H = HW // HEAD_DIM
    hb = min(H, max(1, 8 // d))
    tb, w = BLOCK * d, hb * HEAD_DIM
    cur = pl.BlockSpec((tb, w), lambda n, g: (n, g))
    prev = pl.BlockSpec((tb, w), lambda n, g: (jnp.maximum(n - 1, 0), g))
    units = [(hh, r) for hh in range(hb) for r in range(d)]
    return H // hb, hb, tb, cur, prev, units


def _dil_mask(n):
    qi = lax.broadcasted_iota(jnp.int32, (BLOCK, 2 * BLOCK), 0)
    ki = lax.broadcasted_iota(jnp.int32, (BLOCK, 2 * BLOCK), 1)
    return (ki >= qi) & (ki <= qi + BLOCK) & ((ki >= BLOCK) | (n > 0))


def _dil_stage(ref, buf, row0=0):
    for hh in range(buf.shape[0]):
        buf[hh, row0:row0 + ref.shape[0], :] = ref[:, hh * HEAD_DIM:(hh + 1) * HEAD_DIM].astype(F32)


def _dil_unstage(buf, ref):
    for hh in range(buf.shape[0]):
        ref[:, hh * HEAD_DIM:(hh + 1) * HEAD_DIM] = buf[hh]


def _dil_rows(d, r, size):
    return pl.ds(0, size) if d == 1 else pl.ds(r, size, stride=d)


def _dil_operands(d, tb, units, q_ref, kc_ref, kp_ref, vc_ref, vp_ref, qs, kf, vf):
    _dil_stage(q_ref, qs)
    _dil_stage(kp_ref, kf)
    _dil_stage(kc_ref, kf, tb)
    _dil_stage(vp_ref, vf)
    _dil_stage(vc_ref, vf, tb)
    qu = [qs[hh, _dil_rows(d, r, BLOCK), :].astype(MXU_DTYPE) for hh, r in units]
    ku = [kf[hh, _dil_rows(d, r, 2 * BLOCK), :].astype(MXU_DTYPE) for hh, r in units]
    vu = [vf[hh, _dil_rows(d, r, 2 * BLOCK), :].astype(MXU_DTYPE) for hh, r in units]
    return qu, ku, vu


def _dilated_fwd(q, k, mid, d, HW):
    S = q.shape[0]
    scale = HEAD_DIM ** -0.5
    ng, hb, tb, cur, prev, units = _dil_geometry(d, HW)

    def kern(q_ref, kc_ref, kp_ref, vc_ref, vp_ref, o_ref, l_ref, qs, kf, vf, os_, ls):
        mask = _dil_mask(pl.program_id(0))
        qu, ku, vu = _dil_operands(d, tb, units, q_ref, kc_ref, kp_ref, vc_ref, vp_ref, qs, kf, vf)
        sc = [jnp.where(mask, _dot(a, b, NT) * scale, NEG) for a, b in zip(qu, ku)]
        mx = [jnp.max(t, axis=1, keepdims=True) for t in sc]
        ex = [jnp.exp(t - m) for t, m in zip(sc, mx)]
        den = [jnp.sum(t, axis=1, keepdims=True) for t in ex]
        out = [_dot(t.astype(MXU_DTYPE), v) / dn for t, v, dn in zip(ex, vu, den)]
        for (hh, r), o, m, dn in zip(units, out, mx, den):
            os_[hh, _dil_rows(d, r, BLOCK), :] = o
            ls[hh, _dil_rows(d, r, BLOCK), :] = jnp.broadcast_to(m + jnp.log(dn), (BLOCK, HEAD_DIM))
        _dil_unstage(os_, o_ref)
        _dil_unstage(ls, l_ref)

    return pl.pallas_call(
        kern, name=f"dilated_fwd_d{d}", grid=(S // tb, ng),
        in_specs=[cur, cur, prev, cur, prev],
        out_specs=[cur, cur],
        out_shape=[jax.ShapeDtypeStruct((S, HW), F32)] * 2,
        scratch_shapes=[pltpu.VMEM((hb, tb, HEAD_DIM), F32)] + [pltpu.VMEM((hb, 2 * tb, HEAD_DIM), F32)] * 2
        + [pltpu.VMEM((hb, tb, HEAD_DIM), F32)] * 2,
        compiler_params=_params(("arbitrary", "arbitrary")),
    )(q, k, k, mid, mid)


def _dilated_combine(os_, lses, HW):
    S = os_[0].shape[0]
    tr = _fit(S, 256)

    def body(o0, o1, o2, l0, l1, l2, ya_ref, lse_ref):
        a, b, c = l0[...], l1[...], l2[...]
        mx = jnp.maximum(jnp.maximum(a, b), c)
        ea, eb, ec = jnp.exp(a - mx), jnp.exp(b - mx), jnp.exp(c - mx)
        tot = ea + eb + ec
        ya_ref[...] = (ea * o0[...] + eb * o1[...] + ec * o2[...]) / tot
        lse_ref[...] = mx + jnp.log(tot)

    ins = [(a, _row(tr, HW)) for a in list(os_) + list(lses)]
    return _rows_call("dilated_combine", body, S, tr, ins, [(HW, F32), (HW, F32)])


def _dilated_bwd(q, k, mid, dya, ya, lse, d, HW):
    S = q.shape[0]
    scale = HEAD_DIM ** -0.5
    ng, hb, tb, cur, prev, units = _dil_geometry(d, HW)

    def kern(q_ref, kc_ref, kp_ref, vc_ref, vp_ref, dy_ref, y_ref, l_ref, dq_ref, dkc_ref, dkp_ref, dvc_ref, dvp_ref,
             qs, kf, vf, dys, ys, ls, dqs, dkcs, dkps, dvcs, dvps):
        mask = _dil_mask(pl.program_id(0))
        qu, ku, vu = _dil_operands(d, tb, units, q_ref, kc_ref, kp_ref, vc_ref, vp_ref, qs, kf, vf)
        _dil_stage(dy_ref, dys)
        _dil_stage(y_ref, ys)
        _dil_stage(l_ref, ls)
        dy = [dys[hh, _dil_rows(d, r, BLOCK), :] for hh, r in units]
        lt = [ls[hh, _dil_rows(d, r, BLOCK), :][:, 0:1] for hh, r in units]
        delta = [jnp.sum(t * ys[hh, _dil_rows(d, r, BLOCK), :], axis=1, keepdims=True) for t, (hh, r) in zip(dy, units)]
        dyb = [t.astype(MXU_DTYPE) for t in dy]
        p = [jnp.where(mask, jnp.exp(_dot(a, b, NT) * scale - l), 0.0) for a, b, l in zip(qu, ku, lt)]
        ds = [(t * (_dot(g, v, NT) - dl) * scale).astype(MXU_DTYPE) for t, g, v, dl in zip(p, dyb, vu, delta)]
        dq = [_dot(t, b) for t, b in zip(ds, ku)]
        dk = [_dot(t, a, TN) for t, a in zip(ds, qu)]
        dv = [_dot(t.astype(MXU_DTYPE), g, TN) for t, g in zip(p, dyb)]
        for (hh, r), tq, tk, tv in zip(units, dq, dk, dv):
            at = _dil_rows(d, r, BLOCK)
            dqs[hh, at, :] = tq
            dkps[hh, at, :] = tk[0:BLOCK]
            dkcs[hh, at, :] = tk[BLOCK:2 * BLOCK]
            dvps[hh, at, :] = tv[0:BLOCK]
            dvcs[hh, at, :] = tv[BLOCK:2 * BLOCK]
        for buf, ref in ((dqs, dq_ref), (dkcs, dkc_ref), (dkps, dkp_ref), (dvcs, dvc_ref), (dvps, dvp_ref)):
            _dil_unstage(buf, ref)

    return pl.pallas_call(
        kern, name=f"dilated_bwd_d{d}", grid=(S // tb, ng),
        in_specs=[cur, cur, prev, cur, prev, cur, cur, cur],
        out_specs=[cur] * 5,
        out_shape=[jax.ShapeDtypeStruct((S, HW), F32)] * 5,
        scratch_shapes=[pltpu.VMEM((hb, tb, HEAD_DIM), F32)] + [pltpu.VMEM((hb, 2 * tb, HEAD_DIM), F32)] * 2
        + [pltpu.VMEM((hb, tb, HEAD_DIM), F32)] * 8,
        compiler_params=_params(("arbitrary", "arbitrary")),
    )(q, k, k, mid, mid, dya, ya, lse)


def _softplus_parts(z):
    lg = jnp.log(1.0 + jnp.exp(-jnp.abs(z)))
    return -jnp.maximum(z, 0.0) - lg, jnp.minimum(z, 0.0) - lg


def _sb_specs(S, H):
    q_spec = pl.BlockSpec((SB_BQ, HEAD_DIM), lambda h, i: (i, H + h))
    k_spec = pl.BlockSpec((S, HEAD_DIM), lambda h, i: (0, 2 * H + h))
    v_spec = pl.BlockSpec((S, HEAD_DIM), lambda h, i: (0, 3 * H + h))
    o_spec = pl.BlockSpec((SB_BQ, HEAD_DIM), lambda h, i: (i, h))
    return q_spec, k_spec, v_spec, o_spec


def _sb_tri(relation):
    tri_r = lax.broadcasted_iota(jnp.int32, (SB_BK, SB_BK), 0)
    tri_c = lax.broadcasted_iota(jnp.int32, (SB_BK, SB_BK), 1)
    return relation(tri_r, tri_c).astype(MXU_DTYPE)


def _sb_scratch():
    return [pltpu.VMEM((2, SB_BQ, SB_BQ), MXU_DTYPE), pltpu.VMEM((2, SB_BQ, SB_BQ), MXU_DTYPE), pltpu.SemaphoreType.DMA((2, 2))]


def _sb_fwd(mid, HW, ex=None):
    S = mid.shape[0]
    H = HW // HEAD_DIM
    BQ, CH = SB_BQ, SB_BK
    NC, nq = BQ // CH, S // BQ
    scale = HEAD_DIM ** -0.5
    q_spec, k_spec, v_spec, o_spec = _sb_specs(S, H)

    def kern(q_ref, k_ref, v_ref, o_ref, a_hbm, s_hbm, abuf, sbuf, sems):
        h, i = pl.program_id(0), pl.program_id(1)
        q = q_ref[...]
        upper = _sb_tri(lambda r, c: r > c)
        row = lax.broadcasted_iota(jnp.int32, (BQ, BQ), 0)
        causal = lax.broadcasted_iota(jnp.int32, (BQ, BQ), 1) < row

        def save(slot, j):
            return [pltpu.make_async_copy(buf.at[slot], hbm.at[h, i, j], sems.at[w, slot])
                    for w, (buf, hbm) in enumerate(((abuf, a_hbm), (sbuf, s_hbm)))]

        def block(n, run, acc, masked):
            j = i - n
            ks = pl.multiple_of(j * BQ, BQ)
            z = _dot(q, k_ref[pl.ds(ks, BQ), :], NT) * scale
            m, l = _softplus_parts(z)
            if masked:
                m = jnp.where(causal, m, 0.0)
            parts = []
            for c in reversed(range(NC)):
                mc = m[:, c * CH:(c + 1) * CH]
                parts.append(l[:, c * CH:(c + 1) * CH] + (_split_dot(mc, upper, CUMSUM_PASSES) + run))
                run = run + jnp.sum(mc, axis=1, keepdims=True)
            a = jnp.exp(jnp.concatenate(parts[::-1], axis=1))
            sig = jnp.exp(l)
            if masked:
                a = jnp.where(causal, a, 0.0)
                sig = jnp.where(causal, sig, 0.0)
            ab = a.astype(MXU_DTYPE)
            slot = n % 2
            if not masked:
                @pl.when(n >= 2)
                def _():
                    for cp in save(slot, j):
                        cp.wait()
            abuf[slot] = ab
            sbuf[slot] = sig.astype(MXU_DTYPE)
            for cp in save(slot, j):
                cp.start()
            return run, acc + _dot(ab, v_ref[pl.ds(ks, BQ), :])

        run, acc = block(0, jnp.zeros((BQ, 1), F32), jnp.zeros((BQ, HEAD_DIM), F32), True)
        run, acc = lax.fori_loop(1, i + 1, lambda n, carry: block(n, carry[0], carry[1], False), (run, acc))
        o_ref[...] = acc
        for cp in save(0, i):
            cp.wait()

        @pl.when(i >= 1)
        def _():
            for cp in save(1, i):
                cp.wait()

    tiles = jax.ShapeDtypeStruct((H, nq, nq, BQ, BQ), MXU_DTYPE)
    (o, a_t, s_t), rode = _host_call(kern, "stickbreak_fwd", (H, nq), [q_spec, k_spec, v_spec], [o_spec, ANY, ANY],
                                     [jax.ShapeDtypeStruct((S, HW), F32), tiles, tiles], (mid, mid, mid), ex, _sb_scratch())
    return o, a_t, s_t, rode


def _sb_bwd(mid, dyb, a_t, s_t, HW, ex=None):
    S = mid.shape[0]
    H = HW // HEAD_DIM
    BQ, CH = SB_BQ, SB_BK
    NC = BQ // CH
    scale = HEAD_DIM ** -0.5
    q_spec, k_spec, v_spec, o_spec = _sb_specs(S, H)
    full = pl.BlockSpec((S, HEAD_DIM), lambda h, i: (0, h))

    def kern(q_ref, k_ref, v_ref, do_ref, a_hbm, s_hbm, dq_ref, dk_ref, dv_ref, abuf, sbuf, sems):
        h, i = pl.program_id(0), pl.program_id(1)

        @pl.when(i == 0)
        def _():
            dk_ref[...] = jnp.zeros_like(dk_ref)
            dv_ref[...] = jnp.zeros_like(dv_ref)

        q = q_ref[...]
        do = do_ref[...].astype(MXU_DTYPE)
        excl = _sb_tri(lambda r, c: r < c)

        def fetch(slot, j):
            return [pltpu.make_async_copy(hbm.at[h, i, j], buf.at[slot], sems.at[w, slot])
                    for w, (buf, hbm) in enumerate(((abuf, a_hbm), (sbuf, s_hbm)))]

        for cp in fetch(0, 0):
            cp.start()

        def block(j, carry):
            prun, dq = carry
            slot = j % 2
            for cp in fetch(slot, j):
                cp.wait()

            @pl.when(j < i)
            def _():
                for cp in fetch(1 - slot, j + 1):
                    cp.start()

            ks = pl.multiple_of(j * BQ, BQ)
            k = k_ref[pl.ds(ks, BQ), :]
            ab = abuf[slot]
            p = ab.astype(F32) * _dot(do, v_ref[pl.ds(ks, BQ), :], NT)
            parts = []
            for c in range(NC):
                pc = p[:, c * CH:(c + 1) * CH]
                parts.append(_split_dot(pc, excl, CUMSUM_PASSES) + prun)
                prun = prun + jnp.sum(pc, axis=1, keepdims=True)
            before = jnp.concatenate(parts, axis=1)
            dzb = ((p - sbuf[slot].astype(F32) * (p + before)) * scale).astype(MXU_DTYPE)
            dk_ref[pl.ds(ks, BQ), :] += _dot(dzb, q, TN)
            dv_ref[pl.ds(ks, BQ), :] += _dot(ab, do, TN)
            return prun, dq + _dot(dzb, k)

        _, dq = lax.fori_loop(0, i + 1, block, (jnp.zeros((BQ, 1), F32), jnp.zeros((BQ, HEAD_DIM), F32)))
        dq_ref[...] = dq

    grads, rode = _host_call(kern, "stickbreak_bwd", (H, S // BQ), [q_spec, k_spec, v_spec, o_spec, ANY, ANY], [o_spec, full, full],
                             [jax.ShapeDtypeStruct((S, HW), F32)] * 3, (mid, mid, mid, dyb, a_t, s_t), ex, _sb_scratch())
    return grads, rode


ANY = pl.BlockSpec(memory_space=pl.ANY)


def _place():
    x, y, c = lax.axis_index("x"), lax.axis_index("y"), lax.axis_index("c")
    chips = [(1 - x, y), (x, 1 - y), (1 - x, 1 - y)]
    return x, y, c, chips


def _half(ref, shard, hc, rh):
    return ref.at[shard, pl.ds(pl.multiple_of(hc * rh, 8), rh), :]


def _cast_place(name, w, s_idx):
    R, C = w.shape
    tr = _fit(R, 256)

    def kern(s_ref, w_ref, o_ref):
        o_ref[...] = w_ref[...].astype(o_ref.dtype)

    return pl.pallas_call(
        kern, name=name,
        grid_spec=pltpu.PrefetchScalarGridSpec(
            num_scalar_prefetch=1, grid=(R // tr,),
            in_specs=[pl.BlockSpec((tr, C), lambda r, s_ref: (r, 0))],
            out_specs=pl.BlockSpec((None, tr, C), lambda r, s_ref: (s_ref[0], r, 0))),
        out_shape=jax.ShapeDtypeStruct((N_SHARD, R, C), MXU_DTYPE),
        compiler_params=_params(("arbitrary",)),
    )(s_idx, w)


class _Exchange:
    def __init__(self, inputs, out_shape, aliases, scratch, phases):
        self.inputs, self.out_shape, self.aliases, self.scratch, self.phases = inputs, out_shape, aliases, scratch, phases


def _run_alone(name, ex):
    ni, no = len(ex.inputs), len(ex.out_shape)

    def body(*refs):
        for phase in ex.phases:
            phase(refs[:ni], refs[ni:ni + no], refs[ni + no:])

    return pl.pallas_call(
        body, name=name, in_specs=[ANY] * ni, out_specs=[ANY] * no, out_shape=ex.out_shape,
        input_output_aliases=ex.aliases, scratch_shapes=ex.scratch,
    )(*ex.inputs)


def _host_call(kern, name, grid, in_specs, out_specs, out_shape, operands, ex, scratch=()):
    sem = ("arbitrary",) * len(grid)
    if ex is None:
        return pl.pallas_call(kern, name=name, grid=grid, in_specs=in_specs, out_specs=out_specs, out_shape=out_shape,
                              scratch_shapes=list(scratch), compiler_params=_params(sem))(*operands), []
    n_in, n_out, ri, ro, ns = len(in_specs), len(out_specs), len(ex.inputs), len(ex.out_shape), len(scratch)
    nsteps, nph = 1, len(ex.phases)
    for size in grid:
        nsteps *= size

    def body(*refs):
        r_in, r_out = refs[n_in:n_in + ri], refs[n_in + ri + n_out:n_in + ri + n_out + ro]
        host_scratch, ex_scratch = refs[n_in + ri + n_out + ro:][:ns], refs[n_in + ri + n_out + ro + ns:]
        step = 0
        for axis, size in enumerate(grid):
            step = step * size + pl.program_id(axis)
        for kph, phase in enumerate(ex.phases):
            pl.when(step == (kph * (nsteps - 1)) // (nph - 1))(functools.partial(phase, r_in, r_out, ex_scratch))
        kern(*refs[:n_in], *refs[n_in + ri:n_in + ri + n_out], *host_scratch)

    outs = pl.pallas_call(
        body, name=name, grid=grid, in_specs=list(in_specs) + [ANY] * ri, out_specs=list(out_specs) + [ANY] * ro,
        out_shape=list(out_shape) + list(ex.out_shape), scratch_shapes=list(scratch) + list(ex.scratch),
        input_output_aliases={n_in + a: n_out + b for a, b in ex.aliases.items()},
        compiler_params=_params(sem),
    )(*operands, *ex.inputs)
    return outs[:n_out], outs[n_out:]


def _gather_exchange(bufs):
    n = len(bufs)

    def between_chips(outs, sems, i, j, chip, c, shard):
        blk = _half(outs[i], shard, c, outs[i].shape[1] // 2)
        return pltpu.make_async_remote_copy(src_ref=blk, dst_ref=blk, send_sem=sems[0].at[i, j], recv_sem=sems[1].at[i, j],
                                            device_id=(chip[0], chip[1], c), device_id_type=MESH)

    def to_sibling(outs, sems, i, j, x, y, c, shard, hc):
        blk = _half(outs[i], shard, hc, outs[i].shape[1] // 2)
        return pltpu.make_async_remote_copy(src_ref=blk, dst_ref=blk, send_sem=sems[0].at[i, 3 + j], recv_sem=sems[1].at[i, 3 + j],
                                            device_id=(x, y, 1 - c), device_id_type=MESH)

    def send_mine(ins, outs, sems):
        x, y, c, chips = _place()
        for i in range(n):
            for j, chip in enumerate(chips):
                between_chips(outs, sems, i, j, chip, c, 2 * x + y).start()

    def pass_on(ins, outs, sems):
        x, y, c, chips = _place()
        for i in range(n):
            for j, chip in enumerate(chips):
                between_chips(outs, sems, i, j, chip, c, 2 * chip[0] + chip[1]).wait_recv()
                to_sibling(outs, sems, i, j, x, y, c, 2 * chip[0] + chip[1], c).start()

    def finish(ins, outs, sems):
        x, y, c, chips = _place()
        for i in range(n):
            for j, chip in enumerate(chips):
                to_sibling(outs, sems, i, j, x, y, c, 2 * chip[0] + chip[1], 1 - c).wait_recv()
        for i in range(n):
            for j, chip in enumerate(chips):
                between_chips(outs, sems, i, j, chip, c, 2 * x + y).wait_send()
                to_sibling(outs, sems, i, j, x, y, c, 2 * chip[0] + chip[1], c).wait_send()

    return _Exchange(list(bufs), [jax.ShapeDtypeStruct(b.shape, b.dtype) for b in bufs], {i: i for i in range(n)},
                     [pltpu.SemaphoreType.DMA((n, 6)), pltpu.SemaphoreType.DMA((n, 6))], [send_mine, pass_on, finish])


def _reduce_exchange(g16, g32):
    n = len(g16)

    def copies(ins, outs, sems):
        x, y, c, _ = _place()
        for i in range(n):
            rh = ins[i].shape[1] // 2
            for r in range(1, 8):
                px, py, pc = x ^ ((r >> 2) & 1), y ^ ((r >> 1) & 1), c ^ (r & 1)
                src = _half(ins[i] if r > 1 else ins[n + i], 2 * px + py, pc, rh)
                dst = outs[2 * i + 1].at[r - 2] if r > 1 else outs[2 * i]
                yield pltpu.make_async_remote_copy(src_ref=src, dst_ref=dst, send_sem=sems[0].at[i, r - 1], recv_sem=sems[1].at[i, r - 1],
                                                   device_id=(px, py, pc), device_id_type=MESH)

    def start(ins, outs, sems):
        for cp in copies(ins, outs, sems):
            cp.start()

    def finish(ins, outs, sems):
        for cp in copies(ins, outs, sems):
            cp.wait()

    out_shape = []
    for g in g16:
        rh, C = g.shape[1] // 2, g.shape[2]
        out_shape += [jax.ShapeDtypeStruct((rh, C), F32), jax.ShapeDtypeStruct((6, rh, C), g.dtype)]
    return _Exchange(list(g16) + list(g32), out_shape, {},
                     [pltpu.SemaphoreType.DMA((n, 7)), pltpu.SemaphoreType.DMA((n, 7))], [start, finish])


def _add_direct(name, g32, from_sibling, from_chips, s_idx, c_idx):
    _, R, C = g32.shape
    rh = R // 2
    tr = _fit(rh, 256)
    nrb = rh // tr

    def kern(s_ref, c_ref, g_ref, a_ref, b_ref, out_ref):
        acc = g_ref[...] + a_ref[...]
        for k in range(6):
            acc = acc + b_ref[k].astype(F32)
        out_ref[...] = acc

    return pl.pallas_call(
        kern, name=name,
        grid_spec=pltpu.PrefetchScalarGridSpec(
            num_scalar_prefetch=2, grid=(nrb,),
            in_specs=[pl.BlockSpec((None, tr, C), lambda r, s_ref, c_ref: (s_ref[0], c_ref[0] * nrb + r, 0)),
                      pl.BlockSpec((tr, C), lambda r, s_ref, c_ref: (r, 0)),
                      pl.BlockSpec((6, tr, C), lambda r, s_ref, c_ref: (0, r, 0))],
            out_specs=pl.BlockSpec((tr, C), lambda r, s_ref, c_ref: (r, 0))),
        out_shape=jax.ShapeDtypeStruct((rh, C), F32),
        compiler_params=_params(("arbitrary",)),
    )(s_idx, c_idx, g32, from_sibling, from_chips)


def _swap_reduced(halves):
    n = len(halves)

    def body(*refs):
        ins, outs = refs[:n], refs[n:2 * n]
        send, recv = refs[2 * n:]
        x, y, c, _ = _place()
        copies = []
        for i in range(n):
            cp = pltpu.make_async_remote_copy(src_ref=ins[i], dst_ref=outs[i], send_sem=send.at[i], recv_sem=recv.at[i],
                                              device_id=(x, y, 1 - c), device_id_type=MESH)
            cp.start()
            copies.append(cp)
        for cp in copies:
            cp.wait()

    return pl.pallas_call(
        body, name="grad_swap_reduced",
        in_specs=[ANY] * n, out_specs=[ANY] * n,
        out_shape=[jax.ShapeDtypeStruct(h.shape, F32) for h in halves],
        scratch_shapes=[pltpu.SemaphoreType.DMA((n,)), pltpu.SemaphoreType.DMA((n,))],
    )(*halves)


def _all_reduce_small(v):
    rows, W = v.shape
    ndev = 8

    def body(v_ref, out_ref, buf, send, recv):
        x, y, c, _ = _place()
        me = 4 * x + 2 * y + c
        buf[me] = v_ref[...]
        copies = []
        for r in range(1, ndev):
            fx, fy, fc = (r >> 2) & 1, (r >> 1) & 1, r & 1
            peer = (x ^ fx, y ^ fy, c ^ fc)
            cp = pltpu.make_async_remote_copy(src_ref=v_ref, dst_ref=buf.at[me], send_sem=send.at[r - 1], recv_sem=recv.at[r - 1],
                                              device_id=peer, device_id_type=MESH)
            cp.start()
            copies.append(cp)
        for cp in copies:
            cp.wait()
        acc = buf[0]
        for k in range(1, ndev):
            acc = acc + buf[k]
        out_ref[...] = acc

    return pl.pallas_call(
        body, name="small_all_reduce",
        in_specs=[pl.BlockSpec(memory_space=pltpu.VMEM)], out_specs=pl.BlockSpec(memory_space=pltpu.VMEM),
        out_shape=jax.ShapeDtypeStruct((rows, W), F32),
        scratch_shapes=[pltpu.VMEM((ndev, rows, W), F32), pltpu.SemaphoreType.DMA((ndev - 1,)), pltpu.SemaphoreType.DMA((ndev - 1,))],
    )(v)


def _adamw_update(gv, w_ref, m_ref, v_ref, d_ref, nm_ref, nv_ref):
    nm = ADAM_B1 * m_ref[...] + (1.0 - ADAM_B1) * gv
    nv = ADAM_B2 * v_ref[...] + (1.0 - ADAM_B2) * (gv * gv)
    m_hat = nm / (1.0 - ADAM_B1 ** ADAM_STEP)
    v_hat = nv / (1.0 - ADAM_B2 ** ADAM_STEP)
    d_ref[...] = -ADAM_LR * (m_hat / (jnp.sqrt(v_hat) + ADAM_EPS) + ADAM_WD * w_ref[...])
    nm_ref[...] = nm
    nv_ref[...] = nv


def _adamw(name, w, g, m, v):
    R, C = w.shape
    tr = _fit(R, 256)

    def body(w_ref, g_ref, m_ref, v_ref, d_ref, nm_ref, nv_ref):
        _adamw_update(g_ref[...], w_ref, m_ref, v_ref, d_ref, nm_ref, nv_ref)

    return _rows_call(name, body, R, tr, [(a, _row(tr, C)) for a in (w, g, m, v)], [(C, F32)] * 3)


def _adamw_halves(name, w, mine, theirs, m, v, c_idx):
    R, C = w.shape
    rh = R // 2
    tr = _fit(rh, 256)
    nrb = rh // tr

    def kern(c_ref, w_ref, a_ref, b_ref, m_ref, v_ref, g_ref, d_ref, nm_ref, nv_ref):
        gv = jnp.where(pl.program_id(0) // nrb == c_ref[0], a_ref[...], b_ref[...])
        g_ref[...] = gv
        _adamw_update(gv, w_ref, m_ref, v_ref, d_ref, nm_ref, nv_ref)

    full = pl.BlockSpec((tr, C), lambda r, c_ref: (r, 0))
    pick = lambda own: pl.BlockSpec((tr, C), lambda r, c_ref: (jnp.where((r // nrb == c_ref[0]) == own, r % nrb, 0), 0))
    return pl.pallas_call(
        kern, name=name,
        grid_spec=pltpu.PrefetchScalarGridSpec(
            num_scalar_prefetch=1, grid=(R // tr,),
            in_specs=[full, pick(True), pick(False), full, full], out_specs=[full] * 4),
        out_shape=[jax.ShapeDtypeStruct((R, C), F32)] * 4,
        compiler_params=_params(("arbitrary",)),
    )(c_idx, w, mine, theirs, m, v)


def _sigmoid(z):
    return 1.0 / (1.0 + jnp.exp(-z))


def kernel(x, p, g_mix, w_in, qn_gain, kn_gain, w_branch_a, w_branch_b, w_out, g_mlp, w_up, w_down, g_ple, w_ple_gate, w_ple_proj, loss_target, m_g_mix, m_w_in, m_qn_gain, m_kn_gain, m_w_branch_a, m_w_branch_b, m_w_out, m_g_mlp, m_w_up, m_w_down, m_g_ple, m_w_ple_gate, m_w_ple_proj, v_g_mix, v_w_in, v_qn_gain, v_kn_gain, v_w_branch_a, v_w_branch_b, v_w_out, v_g_mlp, v_w_up, v_w_down, v_g_ple, v_w_ple_gate, v_w_ple_proj):
    S, D = x.shape[1], x.shape[2]
    HW = w_branch_a.shape[1]
    x2d, tgt, p2d = x.reshape(S, D), loss_target.reshape(S, D), p.reshape(S, p.shape[-1])
    big = {"w_in": w_in, "w_branch_a": w_branch_a, "w_branch_b": w_branch_b, "w_out": w_out, "w_up": w_up,
           "w_down": w_down, "w_ple_gate": w_ple_gate, "w_ple_proj": w_ple_proj}
    moments = {"w_in": (m_w_in, v_w_in), "w_branch_a": (m_w_branch_a, v_w_branch_a), "w_branch_b": (m_w_branch_b, v_w_branch_b),
               "w_out": (m_w_out, v_w_out), "w_up": (m_w_up, v_w_up), "w_down": (m_w_down, v_w_down),
               "w_ple_gate": (m_w_ple_gate, v_w_ple_gate), "w_ple_proj": (m_w_ple_proj, v_w_ple_proj)}
    names = list(big)
    col_sharded = {"w_in", "w_branch_a", "w_branch_b", "w_up", "w_ple_proj"}
    shard2d = {k: w.reshape(w.shape[1], w.shape[2]) for k, w in big.items()}

    c_idx = lax.axis_index("c").astype(jnp.int32).reshape(1)
    s_idx = (2 * lax.axis_index("x") + lax.axis_index("y")).astype(jnp.int32).reshape(1)
    placed = {k: _cast_place(f"cast_{k}", shard2d[k], s_idx) for k in names}
    late = [k for k in names if k != "w_in"]
    W = {"w_in": _run_alone("gather_w_in", _gather_exchange([placed["w_in"]]))[0]}
    cin = W["w_in"].shape[2]
    bn_in = _fit(cin, 512)
    while (2 * HW) % bn_in:
        bn_in -= 128

    h = _rmsnorm_fwd("rmsnorm_mix", x2d, g_mix)
    (qk,) = _matmul("proj_qk", h, W["w_in"], mode="nn", out_dtypes=[F32], b_cshard=True, b_off=0, n_out=2 * HW, bn=bn_in, bk=D)
    (mid,) = _matmul("proj_mid", h, W["w_in"], mode="nn", out_dtypes=[MXU_DTYPE], b_cshard=True, b_off=2 * HW // bn_in,
                     n_out=4 * HW, bn=bn_in, bk=D)
    (sg,) = _matmul("proj_gates", h, W["w_in"], mode="nn", out_dtypes=[F32], b_cshard=True, b_off=6 * HW // bn_in,
                    n_out=2 * D, bn=bn_in, bk=D, epilogue=lambda acc: (_sigmoid(acc),))
    tabs = _rope_tables(S)
    qa, ka = _qknorm_fwd(qk, qn_gain, kn_gain, tabs, HW)
    dil = [_dilated_fwd(qa, ka, mid, d, HW) for d in DILATIONS]
    ya, lse = _dilated_combine([o for o, _ in dil], [l for _, l in dil], HW)
    yb, sb_a, sb_sig, gathered = _sb_fwd(mid, HW, _gather_exchange([placed[k] for k in late]))
    W.update({k: (g if k in col_sharded else g.reshape(-1, g.shape[2])) for k, g in zip(late, gathered)})

    gate_blocks = D // _fit(D, 1024)
    (ua,) = _matmul("branch_a", ya, W["w_branch_a"], mode="nn", out_dtypes=[F32], b_cshard=True, bn=_fit(W["w_branch_a"].shape[2], 1024))
    bn_b = _fit(W["w_branch_b"].shape[2], 1024)
    ub, merged = _matmul("branch_b_merge", yb, W["w_branch_b"], mode="nn", out_dtypes=[F32, MXU_DTYPE], b_cshard=True, bn=bn_b,
                         extras=[(sg, 0), (sg, D // bn_b), (ua, 0)],
                         epilogue=lambda acc, sga, sgb, uav: (acc, sga * uav + sgb * acc))
    (x1,) = _matmul("out_proj", merged, W["w_out"], mode="nn", out_dtypes=[F32], extras=[(x2d, 0)], epilogue=lambda acc, xv: (xv + acc,))
    hm = _rmsnorm_fwd("rmsnorm_mlp", x1, g_mlp)

    def up_epilogue(acc):
        r = jnp.maximum(acc, 0.0)
        return r * r, r

    act, rup = _matmul("mlp_up", hm, W["w_up"], mode="nn", out_dtypes=[MXU_DTYPE, MXU_DTYPE], b_cshard=True,
                       bn=_fit(W["w_up"].shape[2], 1024), bk=D, epilogue=up_epilogue)
    (x2,) = _matmul("mlp_down", act, W["w_down"], mode="nn", out_dtypes=[F32], extras=[(x1, 0)], epilogue=lambda acc, xv: (xv + acc,))
    hp = _rmsnorm_fwd("rmsnorm_ple", x2, g_ple)
    (pp,) = _matmul("ple_proj", p2d, W["w_ple_proj"], mode="nn", out_dtypes=[F32], b_cshard=True, bn=_fit(W["w_ple_proj"].shape[2], 1024))

    def ple_epilogue(acc, ppv, x2v, tv):
        s = _sigmoid(acc)
        dx3 = ((x2v + ppv * s) - tv) / D
        return dx3, dx3 * s, dx3 * ppv * (s * (1.0 - s))

    dx3, d_pp, d_gate = _matmul("ple_gate_loss", hp, W["w_ple_gate"], mode="nn", out_dtypes=[F32, MXU_DTYPE, MXU_DTYPE],
                                bm=512, extras=[(pp, 0), (x2, 0), (tgt, 0)], epilogue=ple_epilogue)

    G, G16 = {}, {}
    G["w_ple_proj"], G16["w_ple_proj"] = _matmul("grad_w_ple_proj", p2d, d_pp, mode="tn", out_dtypes=[F32, MXU_DTYPE], out_cshard=True,
                                 bn=_fit(d_pp.shape[1] // N_SHARD, 1024))
    G["w_ple_gate"], G16["w_ple_gate"] = _matmul("grad_w_ple_gate", hp, d_gate, mode="tn", out_dtypes=[F32, MXU_DTYPE])
    (d_hp,) = _matmul("ple_gate_bwd", d_gate, W["w_ple_gate"], mode="nt", out_dtypes=[F32])
    dx2, g_g_ple, loss_part = _rmsnorm_bwd("rmsnorm_ple_bwd", d_hp, x2, g_ple, dx3, True)
    G["w_down"], G16["w_down"] = _matmul("grad_w_down", act, dx2, mode="tn", out_dtypes=[F32, MXU_DTYPE])
    (d_up,) = _matmul("mlp_down_bwd", dx2, W["w_down"], mode="nt", out_dtypes=[MXU_DTYPE], extras=[(rup, 0)],
                      epilogue=lambda acc, r: (acc * (2.0 * r.astype(F32)),))
    G["w_up"], G16["w_up"] = _matmul("grad_w_up", hm, d_up, mode="tn", out_dtypes=[F32, MXU_DTYPE], out_cshard=True, bn=_fit(d_up.shape[1] // N_SHARD, 1024))
    (d_hm,) = _matmul("mlp_up_bwd", d_up, W["w_up"], mode="nt", out_dtypes=[F32], b_cshard=True, bk=_fit(W["w_up"].shape[2], 2048))
    dx1, g_g_mlp = _rmsnorm_bwd("rmsnorm_mlp_bwd", d_hm, x1, g_mlp, dx2, False)
    G["w_out"], G16["w_out"] = _matmul("grad_w_out", merged, dx1, mode="tn", out_dtypes=[F32, MXU_DTYPE])

    def merge_bwd(acc, sga, sgb, uav, ubv):
        return acc * sga, acc * sgb, acc * uav * (sga * (1.0 - sga)), acc * ubv * (sgb * (1.0 - sgb))

    bn_m = _fit(D, 1024)
    d_ua, d_ub, d_ga, d_gb = _matmul("out_proj_bwd", dx1, W["w_out"], mode="nt", out_dtypes=[MXU_DTYPE] * 4, bm=512, bn=bn_m,
                                     extras=[(sg, 0), (sg, D // bn_m), (ua, 0), (ub, 0)], epilogue=merge_bwd)
    bn_br = _fit(D // N_SHARD, 1024)
    G["w_branch_a"], G16["w_branch_a"] = _matmul("grad_w_branch_a", ya, d_ua, mode="tn", out_dtypes=[F32, MXU_DTYPE], out_cshard=True, bn=bn_br)
    G["w_branch_b"], G16["w_branch_b"] = _matmul("grad_w_branch_b", yb, d_ub, mode="tn", out_dtypes=[F32, MXU_DTYPE], out_cshard=True, bn=bn_br)
    (d_ya,) = _matmul("branch_a_bwd", d_ua, W["w_branch_a"], mode="nt", out_dtypes=[F32], b_cshard=True, bk=bn_br)
    (d_yb,) = _matmul("branch_b_bwd", d_ub, W["w_branch_b"], mode="nt", out_dtypes=[F32], b_cshard=True, bk=bn_br)

    as_shards = lambda k, g: g if k in col_sharded else g.reshape(N_SHARD, -1, g.shape[1])
    (dqb, dkb, dvb), partials = _sb_bwd(mid, d_yb, sb_a, sb_sig, HW, _reduce_exchange([as_shards(k, G16[k]) for k in late],
                                                                               [as_shards(k, G[k]) for k in late]))
    dil_b = [_dilated_bwd(qa, ka, mid, d_ya, ya, lse, d, HW) for d in DILATIONS]
    d_qk, g_qn, g_kn = _qknorm_bwd(qk, qn_gain, kn_gain, tabs, [t[0] for t in dil_b], [t[1] for t in dil_b],
                                   [t[2] for t in dil_b], HW)
    dva = _dv_sum([t[3] for t in dil_b], [t[4] for t in dil_b], HW)
    d_proj = jnp.concatenate([d_qk, dva, dqb.astype(MXU_DTYPE), dkb.astype(MXU_DTYPE), dvb.astype(MXU_DTYPE), d_ga, d_gb], axis=1)
    G["w_in"], G16["w_in"] = _matmul("grad_w_in", h, d_proj, mode="tn", out_dtypes=[F32, MXU_DTYPE], out_cshard=True, bn=_fit(cin, 1280))
    (d_h,), partials_in = _matmul("proj_bwd", d_proj, W["w_in"], mode="nt", out_dtypes=[F32], b_cshard=True, bk=_fit(cin, 1280),
                                  ride=_reduce_exchange([G16["w_in"]], [G["w_in"]]))
    grad_x, g_g_mix = _rmsnorm_bwd("rmsnorm_mix_bwd", d_h, x2d, g_mix, dx1, False)

    mine = {k: _add_direct(f"grad_add_{k}", as_shards(k, G[k]), partials[2 * n], partials[2 * n + 1], s_idx, c_idx)
            for n, k in enumerate(late)}
    mine["w_in"] = _add_direct("grad_add_w_in", G["w_in"], partials_in[0], partials_in[1], s_idx, c_idx)
    halves = [mine[k] for k in names]
    others = _swap_reduced(halves)

    pack_w = -(-(3 * D + 3 * 128) // (8 * 128)) * 128

    def pack(v_mix, v_mlp, v_ple, v_qn, v_kn, extra):
        flat = jnp.concatenate([v_mix.reshape(-1), v_mlp.reshape(-1), v_ple.reshape(-1), v_qn.reshape(-1), v_kn.reshape(-1), extra.reshape(-1)])
        return jnp.pad(flat, (0, 8 * pack_w - flat.shape[0])).reshape(8, pack_w)

    def unpack(blk):
        flat = blk.reshape(-1)
        return (flat[:D].reshape(1, D), flat[D:2 * D].reshape(1, D), flat[2 * D:3 * D].reshape(1, D),
                flat[3 * D:3 * D + 128].reshape(1, 128), flat[3 * D + 128:3 * D + 256].reshape(1, 128), flat[3 * D + 256])

    small = _all_reduce_small(pack(g_g_mix, g_g_mlp, g_g_ple, g_qn, g_kn, loss_part))
    sw = pack(g_mix, g_mlp, g_ple, qn_gain, kn_gain, jnp.zeros((128,), F32))
    sm = pack(m_g_mix, m_g_mlp, m_g_ple, m_qn_gain, m_kn_gain, jnp.zeros((128,), F32))
    sv = pack(v_g_mix, v_g_mlp, v_g_ple, v_qn_gain, v_kn_gain, jnp.ones((128,), F32))
    s_delta, s_nm, s_nv = _adamw("adamw_small", sw, small, sm, sv)
    sg_mix, sg_mlp, sg_ple, sg_qn, sg_kn, loss = unpack(small)
    small_out = {}
    for tag, blk in (("delta", s_delta), ("new_m", s_nm), ("new_v", s_nv)):
        u = unpack(blk)
        small_out[tag] = dict(g_mix=u[0], g_mlp=u[1], g_ple=u[2], qn_gain=u[3], kn_gain=u[4])
    small_grad = dict(g_mix=sg_mix, g_mlp=sg_mlp, g_ple=sg_ple, qn_gain=sg_qn, kn_gain=sg_kn)

    big_out = {"grad": {}, "delta": {}, "new_m": {}, "new_v": {}}
    for k, mine, theirs in zip(names, halves, others):
        shape = big[k].shape
        m2, v2 = (t.reshape(shape[1], shape[2]) for t in moments[k])
        res = _adamw_halves(f"adamw_{k}", shard2d[k], mine, theirs, m2, v2, c_idx)
        for tag, t in zip(("grad", "delta", "new_m", "new_v"), res):
            big_out[tag][k] = t.reshape(shape)

    order = ["g_mix", "w_in", "qn_gain", "kn_gain", "w_branch_a", "w_branch_b", "w_out", "g_mlp", "w_up", "w_down", "g_ple",
             "w_ple_gate", "w_ple_proj"]
    outs = [loss, grad_x.reshape(x.shape)]
    outs += [small_grad[k] if k in small_grad else big_out["grad"][k] for k in order]
    for tag in ("delta", "new_m", "new_v"):
        outs += [small_out[tag][k] if k in small_grad else big_out[tag][k] for k in order]
    return tuple(outs)
```

```python
import functools

import jax
import jax.numpy as jnp
from jax import lax
from jax.experimental import pallas as pl
from jax.experimental.pallas import tpu as pltpu

F32 = jnp.float32
MXU_DTYPE = jnp.bfloat16
HEAD_DIM = 128
ROT_DIM = HEAD_DIM // 4
ROPE_THETA = 500000.0
EPS = 1e-6
DILATIONS = (1, 4, 16)
BLOCK = 128
N_SHARD = 4
ADAM_LR, ADAM_B1, ADAM_B2, ADAM_EPS, ADAM_WD, ADAM_STEP = 0.001, 0.9, 0.999, 1e-08, 0.01, 10
MXU_WIDTH = 256
V7X_VMEM_BYTES = 64 * 1024 * 1024
VMEM_LIMIT = V7X_VMEM_BYTES - 8 * 1024 * 1024
MESH = pl.DeviceIdType.MESH
NEG = -1e30
SB_BQ, SB_BK = 1024, 256
LOG2E = 1.4426950408889634
CUMSUM_PASSES = 2


def _fit(dim, pref):
    if dim <= pref:
        return dim
    b = (pref // 128) * 128
    while dim % b:
        b -= 128
    return b


def _params(sem=None):
    return pltpu.CompilerParams(dimension_semantics=sem, vmem_limit_bytes=VMEM_LIMIT)


def _dot(a, b, dims=(((1,), (0,)), ((), ()))):
    return lax.dot_general(a, b, dims, preferred_element_type=F32)


NT = (((1,), (1,)), ((), ()))
TN = (((0,), (0,)), ((), ()))


def _split_dot(x, u, passes):
    out = None
    r = x
    for p in range(passes):
        hi = r.astype(MXU_DTYPE)
        part = _dot(hi, u)
        out = part if out is None else out + part
        if p + 1 < passes:
            r = r - hi.astype(F32)
    return out


def _matmul(name, a, b, *, mode, out_dtypes, bm=1024, bn=1024, bk=2048, b_cshard=False, b_off=0, n_out=None,
            extras=(), epilogue=None, out_cshard=False, ride=None):
    if mode == "tn":
        K, M = a.shape
        N = b.shape[1]
    else:
        M, K = a.shape
        if mode == "nn":
            N = n_out if n_out is not None else (N_SHARD * b.shape[2] if b_cshard else b.shape[1])
        else:
            N = b.shape[1] if b_cshard else b.shape[0]
    bm, bn, bk = _fit(M, bm), _fit(N, bn), _fit(K, bk)
    nk = K // bk
    grid = (M // bm, N // bn, nk)

    if mode == "tn":
        a_spec = pl.BlockSpec((bk, bm), lambda i, j, k: (k, i))
        b_spec = pl.BlockSpec((bk, bn), lambda i, j, k: (k, j))
        dims = TN
    elif mode == "nn":
        a_spec = pl.BlockSpec((bm, bk), lambda i, j, k: (i, k))
        if b_cshard:
            cb = b.shape[2] // bn
            b_spec = pl.BlockSpec((None, bk, bn), lambda i, j, k: ((j + b_off) // cb, k, (j + b_off) % cb))
        else:
            b_spec = pl.BlockSpec((bk, bn), lambda i, j, k: (k, j + b_off))
        dims = (((1,), (0,)), ((), ()))
    else:
        a_spec = pl.BlockSpec((bm, bk), lambda i, j, k: (i, k))
        if b_cshard:
            cb = b.shape[2] // bk
            b_spec = pl.BlockSpec((None, bn, bk), lambda i, j, k: (k // cb, j, k % cb))
        else:
            b_spec = pl.BlockSpec((bn, bk), lambda i, j, k: (j, k))
        dims = NT

    ex_arrays = [e[0] for e in extras]
    ex_specs = [pl.BlockSpec((bm, bn), functools.partial(lambda i, j, k, off: (i, j + off), off=e[1])) for e in extras]
    if out_cshard:
        cbo = (N // N_SHARD) // bn
        out_shape = [jax.ShapeDtypeStruct((N_SHARD, M, N // N_SHARD), dt) for dt in out_dtypes]
        out_specs = [pl.BlockSpec((None, bm, bn), lambda i, j, k: (j // cbo, i, j % cbo)) for _ in out_dtypes]
    else:
        out_shape = [jax.ShapeDtypeStruct((M, N), dt) for dt in out_dtypes]
        out_specs = [pl.BlockSpec((bm, bn), lambda i, j, k: (i, j)) for _ in out_dtypes]
    ne, no = len(extras), len(out_dtypes)
    cw = MXU_WIDTH if bn % MXU_WIDTH == 0 else bn

    def kern(*refs):
        a_ref, b_ref = refs[0], refs[1]
        ex_refs = refs[2:2 + ne]
        o_refs = refs[2 + ne:2 + ne + no]
        av = a_ref[...].astype(MXU_DTYPE)

        def finish(acc):
            vals = (acc,) * no if epilogue is None else epilogue(acc, *[r[...] for r in ex_refs])
            for r, v in zip(o_refs, vals):
                r[...] = v.astype(r.dtype)

        if nk == 1:
            finish(_dot(av, b_ref[...].astype(MXU_DTYPE), dims))
        else:
            acc_ref = refs[2 + ne + no]
            k = pl.program_id(2)

            @pl.when(k == 0)
            def _():
                acc_ref[...] = jnp.zeros_like(acc_ref)

            for c0 in range(0, bn, cw):
                bv = b_ref[c0:c0 + cw, :] if mode == "nt" else b_ref[:, c0:c0 + cw]
                acc_ref[:, c0:c0 + cw] += _dot(av, bv.astype(MXU_DTYPE), dims)

            @pl.when(k == nk - 1)
            def _():
                finish(acc_ref[...])

    outs, rode = _host_call(kern, name, grid, [a_spec, b_spec] + ex_specs, out_specs, out_shape, (a, b, *ex_arrays), ride,
                            [pltpu.VMEM((bm, bn), F32)] if nk > 1 else [])
    return outs if ride is None else (outs, rode)


def _row(tr, w, coff=0):
    return pl.BlockSpec((tr, w), lambda i: (i, coff))


def _vec(w):
    return pl.BlockSpec((1, w), lambda i: (0, 0))


def _rows_call(name, body, n_rows, tr, ins, outs, accs=()):
    n_in, n_out = len(ins), len(outs)

    def kern(*refs):
        acc_refs = refs[n_in + n_out:]
        if acc_refs:
            @pl.when(pl.program_id(0) == 0)
            def _():
                for r in acc_refs:
                    r[...] = jnp.zeros_like(r)
        body(*refs)

    out_shape = [jax.ShapeDtypeStruct((n_rows, w), dt) for w, dt in outs] + [jax.ShapeDtypeStruct((1, w), F32) for w in accs]
    out_specs = [_row(tr, w) for w, _ in outs] + [_vec(w) for w in accs]
    return pl.pallas_call(
        kern, name=name, grid=(n_rows // tr,),
        in_specs=[s for _, s in ins], out_specs=out_specs, out_shape=out_shape,
        compiler_params=_params(("arbitrary",)),
    )(*[a for a, _ in ins])


def _rmsnorm_fwd(name, x, g):
    S, D = x.shape
    tr = _fit(S, 256)

    def body(x_ref, g_ref, h_ref):
        xv = x_ref[...]
        r = lax.rsqrt(jnp.mean(xv * xv, axis=1, keepdims=True) + EPS)
        h_ref[...] = ((xv * r) * g_ref[...]).astype(h_ref.dtype)

    return _rows_call(name, body, S, tr, [(x, _row(tr, D)), (g, _vec(D))], [(D, MXU_DTYPE)])[0]


def _rmsnorm_bwd(name, dh, x, g, resid, with_loss):
    S, D = x.shape
    tr = _fit(S, 256)

    def body(dh_ref, x_ref, g_ref, res_ref, dx_ref, dg_ref, *loss_ref):
        xv = x_ref[...]
        r = lax.rsqrt(jnp.mean(xv * xv, axis=1, keepdims=True) + EPS)
        dhv = dh_ref[...]
        u = dhv * g_ref[...]
        xr = xv * r
        dx = r * u - xr * (r * r) * jnp.mean(xv * u, axis=1, keepdims=True)
        resv = res_ref[...]
        dx_ref[...] = resv + dx
        dg_ref[...] += jnp.sum(dhv * xr, axis=0, keepdims=True)
        if with_loss:
            loss_ref[0][...] += (0.5 * D) * jnp.sum(resv * resv)

    outs = _rows_call(name, body, S, tr, [(dh, _row(tr, D)), (x, _row(tr, D)), (g, _vec(D)), (resid, _row(tr, D))],
                      [(D, F32)], accs=(D, 128) if with_loss else (D,))
    return outs


def _rope_tables(S):
    half = ROT_DIM // 2
    pos = jnp.arange(S, dtype=F32)
    inv = ROPE_THETA ** (-jnp.arange(0, ROT_DIM, 2, dtype=F32) / ROT_DIM)
    ang = pos[:, None] * inv[None, :]
    cos, sin = jnp.cos(ang), jnp.sin(ang)
    pad = HEAD_DIM - ROT_DIM
    ctab = jnp.concatenate([cos, cos, jnp.ones((S, pad), F32)], axis=1)
    atab = jnp.concatenate([-sin, jnp.zeros((S, pad + half), F32)], axis=1)
    btab = jnp.concatenate([jnp.zeros((S, half), F32), sin, jnp.zeros((S, pad), F32)], axis=1)
    return ctab, atab, btab


def _qknorm_fwd(qk, qn, kn, tabs, HW):
    S = qk.shape[0]
    tr = _fit(S, 256)
    half = ROT_DIM // 2

    def body(qk_ref, qn_ref, kn_ref, c_ref, a_ref, b_ref, q_out, k_out):
        ct, at, bt = c_ref[...], a_ref[...], b_ref[...]
        for part, (g_ref, o_ref) in enumerate(((qn_ref, q_out), (kn_ref, k_out))):
            gv = g_ref[...]
            for h in range(HW // HEAD_DIM):
                xh = qk_ref[:, part * HW + h * HEAD_DIM: part * HW + (h + 1) * HEAD_DIM]
                r = lax.rsqrt(jnp.mean(xh * xh, axis=1, keepdims=True) + EPS)
                y = (xh * r) * gv
                o = y * ct + pltpu.roll(y, HEAD_DIM - half, 1) * at + pltpu.roll(y, half, 1) * bt
                o_ref[:, h * HEAD_DIM:(h + 1) * HEAD_DIM] = o.astype(o_ref.dtype)

    ins = [(qk, _row(tr, 2 * HW)), (qn, _vec(HEAD_DIM)), (kn, _vec(HEAD_DIM))] + [(t, _row(tr, HEAD_DIM)) for t in tabs]
    return _rows_call("qknorm_fwd", body, S, tr, ins, [(HW, MXU_DTYPE), (HW, MXU_DTYPE)])


def _shift_spec(tr, w, shift, nblk):
    return pl.BlockSpec((tr, w), lambda i: (jnp.minimum(i + shift, nblk - 1), 0))


def _qknorm_bwd(qk, qn, kn, tabs, dq_parts, dk_cur, dk_prev, HW):
    S = qk.shape[0]
    tr = BLOCK
    nblk = S // tr
    half = ROT_DIM // 2
    nd = len(DILATIONS)

    def body(*refs):
        qk_ref, qn_ref, kn_ref, c_ref, a_ref, b_ref = refs[:6]
        dq_refs = refs[6:6 + nd]
        dkc_refs = refs[6 + nd:6 + 2 * nd]
        dkp_refs = refs[6 + 2 * nd:6 + 3 * nd]
        d_out, dqn_ref, dkn_ref = refs[6 + 3 * nd:]
        i = pl.program_id(0)
        ct, at, bt = c_ref[...], a_ref[...], b_ref[...]
        live = [(i + d < nblk).astype(F32) for d in DILATIONS]
        for part, (g_ref, dg_ref) in enumerate(((qn_ref, dqn_ref), (kn_ref, dkn_ref))):
            gv = g_ref[...]
            dg = jnp.zeros((1, HEAD_DIM), F32)
            for h in range(HW // HEAD_DIM):
                hs = slice(h * HEAD_DIM, (h + 1) * HEAD_DIM)
                if part == 0:
                    do = dq_refs[0][:, hs] + dq_refs[1][:, hs] + dq_refs[2][:, hs]
                else:
                    do = dkc_refs[0][:, hs] + dkc_refs[1][:, hs] + dkc_refs[2][:, hs]
                    for n in range(nd):
                        do = do + dkp_refs[n][:, hs] * live[n]
                dy = do * ct + pltpu.roll(do * at, half, 1) + pltpu.roll(do * bt, HEAD_DIM - half, 1)
                xh = qk_ref[:, part * HW + h * HEAD_DIM: part * HW + (h + 1) * HEAD_DIM]
                r = lax.rsqrt(jnp.mean(xh * xh, axis=1, keepdims=True) + EPS)
                xr = xh * r
                u = dy * gv
                dx = r * u - xr * (r * r) * jnp.mean(xh * u, axis=1, keepdims=True)
                d_out[:, part * HW + h * HEAD_DIM: part * HW + (h + 1) * HEAD_DIM] = dx.astype(d_out.dtype)
                dg = dg + jnp.sum(dy * xr, axis=0, keepdims=True)
            dg_ref[...] += dg

    ins = [(qk, _row(tr, 2 * HW)), (qn, _vec(HEAD_DIM)), (kn, _vec(HEAD_DIM))] + [(t, _row(tr, HEAD_DIM)) for t in tabs]
    ins += [(a, _row(tr, HW)) for a in dq_parts] + [(a, _row(tr, HW)) for a in dk_cur]
    ins += [(a, _shift_spec(tr, HW, d, nblk)) for a, d in zip(dk_prev, DILATIONS)]
    return _rows_call("qknorm_bwd", body, S, tr, ins, [(2 * HW, MXU_DTYPE)], accs=(HEAD_DIM, HEAD_DIM))


def _dv_sum(dv_cur, dv_prev, HW):
    S = dv_cur[0].shape[0]
    tr = BLOCK
    nblk = S // tr
    nd = len(DILATIONS)

    def body(*refs):
        i = pl.program_id(0)
        out = refs[2 * nd]
        acc = refs[0][...] + refs[1][...] + refs[2][...]
        for n, d in enumerate(DILATIONS):
            acc = acc + refs[nd + n][...] * (i + d < nblk).astype(F32)
        out[...] = acc.astype(out.dtype)

    ins = [(a, _row(tr, HW)) for a in dv_cur] + [(a, _shift_spec(tr, HW, d, nblk)) for a, d in zip(dv_prev, DILATIONS)]
    return _rows_call("dilated_dv_sum", body, S, tr, ins, [(HW, MXU_DTYPE)])[0]


def _dil_geometry(d, HW):
    H = HW // HEAD_DIM
    hb = min(H, max(1, 8 // d))
    tb, w = BLOCK * d, hb * HEAD_DIM
    cur = pl.BlockSpec((tb, w), lambda n, g: (n, g))
    prev = pl.BlockSpec((tb, w), lambda n, g: (jnp.maximum(n - 1, 0), g))
    units = [(hh, r) for hh in range(hb) for r in range(d)]
    return H // hb, hb, tb, cur, prev, units


def _dil_mask(n):
    qi = lax.broadcasted_iota(jnp.int32, (BLOCK, 2 * BLOCK), 0)
    ki = lax.broadcasted_iota(jnp.int32, (BLOCK, 2 * BLOCK), 1)
    return (ki >= qi) & (ki <= qi + BLOCK) & ((ki >= BLOCK) | (n > 0))


def _dil_stage(ref, buf, row0=0):
    for hh in range(buf.shape[0]):
        buf[hh, row0:row0 + ref.shape[0], :] = ref[:, hh * HEAD_DIM:(hh + 1) * HEAD_DIM].astype(F32)


def _dil_unstage(buf, ref):
    for hh in range(buf.shape[0]):
        ref[:, hh * HEAD_DIM:(hh + 1) * HEAD_DIM] = buf[hh]


def _dil_rows(d, r, size):
    return pl.ds(0, size) if d == 1 else pl.ds(r, size, stride=d)


def _dil_operands(d, tb, units, q_ref, kc_ref, kp_ref, vc_ref, vp_ref, qs, kf, vf):
    _dil_stage(q_ref, qs)
    _dil_stage(kp_ref, kf)
    _dil_stage(kc_ref, kf, tb)
    _dil_stage(vp_ref, vf)
    _dil_stage(vc_ref, vf, tb)
    qu = [qs[hh, _dil_rows(d, r, BLOCK), :].astype(MXU_DTYPE) for hh, r in units]
    ku = [kf[hh, _dil_rows(d, r, 2 * BLOCK), :].astype(MXU_DTYPE) for hh, r in units]
    vu = [vf[hh, _dil_rows(d, r, 2 * BLOCK), :].astype(MXU_DTYPE) for hh, r in units]
    return qu, ku, vu


def _dilated_fwd(q, k, mid, d, HW):
    S = q.shape[0]
    scale = HEAD_DIM ** -0.5
    ng, hb, tb, cur, prev, units = _dil_geometry(d, HW)

    def kern(q_ref, kc_ref, kp_ref, vc_ref, vp_ref, o_ref, l_ref, qs, kf, vf, os_, ls):
        mask = _dil_mask(pl.program_id(0))
        qu, ku, vu = _dil_operands(d, tb, units, q_ref, kc_ref, kp_ref, vc_ref, vp_ref, qs, kf, vf)
        sc = [jnp.where(mask, _dot(a, b, NT) * scale, NEG) for a, b in zip(qu, ku)]
        mx = [jnp.max(t, axis=1, keepdims=True) for t in sc]
        ex = [jnp.exp(t - m) for t, m in zip(sc, mx)]
        den = [jnp.sum(t, axis=1, keepdims=True) for t in ex]
        out = [_dot(t.astype(MXU_DTYPE), v) / dn for t, v, dn in zip(ex, vu, den)]
        for (hh, r), o, m, dn in zip(units, out, mx, den):
            os_[hh, _dil_rows(d, r, BLOCK), :] = o
            ls[hh, _dil_rows(d, r, BLOCK), :] = jnp.broadcast_to(m + jnp.log(dn), (BLOCK, HEAD_DIM))
        _dil_unstage(os_, o_ref)
        _dil_unstage(ls, l_ref)

    return pl.pallas_call(
        kern, name=f"dilated_fwd_d{d}", grid=(S // tb, ng),
        in_specs=[cur, cur, prev, cur, prev],
        out_specs=[cur, cur],
        out_shape=[jax.ShapeDtypeStruct((S, HW), F32)] * 2,
        scratch_shapes=[pltpu.VMEM((hb, tb, HEAD_DIM), F32)] + [pltpu.VMEM((hb, 2 * tb, HEAD_DIM), F32)] * 2
        + [pltpu.VMEM((hb, tb, HEAD_DIM), F32)] * 2,
        compiler_params=_params(("arbitrary", "arbitrary")),
    )(q, k, k, mid, mid)


def _dilated_combine(os_, lses, HW):
    S = os_[0].shape[0]
    tr = _fit(S, 256)

    def body(o0, o1, o2, l0, l1, l2, ya_ref, lse_ref):
        a, b, c = l0[...], l1[...], l2[...]
        mx = jnp.maximum(jnp.maximum(a, b), c)
        ea, eb, ec = jnp.exp(a - mx), jnp.exp(b - mx), jnp.exp(c - mx)
        tot = ea + eb + ec
        ya_ref[...] = (ea * o0[...] + eb * o1[...] + ec * o2[...]) / tot
        lse_ref[...] = mx + jnp.log(tot)

    ins = [(a, _row(tr, HW)) for a in list(os_) + list(lses)]
    return _rows_call("dilated_combine", body, S, tr, ins, [(HW, F32), (HW, F32)])


def _dilated_bwd(q, k, mid, dya, ya, lse, d, HW):
    S = q.shape[0]
    scale = HEAD_DIM ** -0.5
    ng, hb, tb, cur, prev, units = _dil_geometry(d, HW)

    def kern(q_ref, kc_ref, kp_ref, vc_ref, vp_ref, dy_ref, y_ref, l_ref, dq_ref, dkc_ref, dkp_ref, dvc_ref, dvp_ref,
             qs, kf, vf, dys, ys, ls, dqs, dkcs, dkps, dvcs, dvps):
        mask = _dil_mask(pl.program_id(0))
        qu, ku, vu = _dil_operands(d, tb, units, q_ref, kc_ref, kp_ref, vc_ref, vp_ref, qs, kf, vf)
        _dil_stage(dy_ref, dys)
        _dil_stage(y_ref, ys)
        _dil_stage(l_ref, ls)
        dy = [dys[hh, _dil_rows(d, r, BLOCK), :] for hh, r in units]
        lt = [ls[hh, _dil_rows(d, r, BLOCK), :][:, 0:1] for hh, r in units]
        delta = [jnp.sum(t * ys[hh, _dil_rows(d, r, BLOCK), :], axis=1, keepdims=True) for t, (hh, r) in zip(dy, units)]
        dyb = [t.astype(MXU_DTYPE) for t in dy]
        p = [jnp.where(mask, jnp.exp(_dot(a, b, NT) * scale - l), 0.0) for a, b, l in zip(qu, ku, lt)]
        ds = [(t * (_dot(g, v, NT) - dl) * scale).astype(MXU_DTYPE) for t, g, v, dl in zip(p, dyb, vu, delta)]
        dq = [_dot(t, b) for t, b in zip(ds, ku)]
        dk = [_dot(t, a, TN) for t, a in zip(ds, qu)]
        dv = [_dot(t.astype(MXU_DTYPE), g, TN) for t, g in zip(p, dyb)]
        for (hh, r), tq, tk, tv in zip(units, dq, dk, dv):
            at = _dil_rows(d, r, BLOCK)
            dqs[hh, at, :] = tq
            dkps[hh, at, :] = tk[0:BLOCK]
            dkcs[hh, at, :] = tk[BLOCK:2 * BLOCK]
            dvps[hh, at, :] = tv[0:BLOCK]
            dvcs[hh, at, :] = tv[BLOCK:2 * BLOCK]
        for buf, ref in ((dqs, dq_ref), (dkcs, dkc_ref), (dkps, dkp_ref), (dvcs, dvc_ref), (dvps, dvp_ref)):
            _dil_unstage(buf, ref)

    return pl.pallas_call(
        kern, name=f"dilated_bwd_d{d}", grid=(S // tb, ng),
        in_specs=[cur, cur, prev, cur, prev, cur, cur, cur],
        out_specs=[cur] * 5,
        out_shape=[jax.ShapeDtypeStruct((S, HW), F32)] * 5,
        scratch_shapes=[pltpu.VMEM((hb, tb, HEAD_DIM), F32)] + [pltpu.VMEM((hb, 2 * tb, HEAD_DIM), F32)] * 2
        + [pltpu.VMEM((hb, tb, HEAD_DIM), F32)] * 8,
        compiler_params=_params(("arbitrary", "arbitrary")),
    )(q, k, k, mid, mid, dya, ya, lse)


def _softplus_parts(z2):
    lg = jnp.log(1.0 + jnp.exp2(-jnp.abs(z2))) * LOG2E
    return -jnp.maximum(z2, 0.0) - lg, jnp.minimum(z2, 0.0) - lg


def _sb_specs(S, H):
    q_spec = pl.BlockSpec((SB_BQ, HEAD_DIM), lambda h, i: (i, H + h))
    k_spec = pl.BlockSpec((S, HEAD_DIM), lambda h, i: (0, 2 * H + h))
    v_spec = pl.BlockSpec((S, HEAD_DIM), lambda h, i: (0, 3 * H + h))
    o_spec = pl.BlockSpec((SB_BQ, HEAD_DIM), lambda h, i: (i, h))
    return q_spec, k_spec, v_spec, o_spec


def _sb_tri(relation):
    tri_r = lax.broadcasted_iota(jnp.int32, (SB_BK, SB_BK), 0)
    tri_c = lax.broadcasted_iota(jnp.int32, (SB_BK, SB_BK), 1)
    return relation(tri_r, tri_c).astype(MXU_DTYPE)


def _sb_scratch():
    return [pltpu.VMEM((2, SB_BQ, SB_BQ), MXU_DTYPE), pltpu.VMEM((2, SB_BQ, SB_BQ), MXU_DTYPE), pltpu.SemaphoreType.DMA((2, 2))]


def _sb_fwd(mid, HW, ex=None):
    S = mid.shape[0]
    H = HW // HEAD_DIM
    BQ, CH = SB_BQ, SB_BK
    NC, nq = BQ // CH, S // BQ
    scale = HEAD_DIM ** -0.5
    q_spec, k_spec, v_spec, o_spec = _sb_specs(S, H)

    def kern(q_ref, k_ref, v_ref, o_ref, a_hbm, s_hbm, abuf, sbuf, sems):
        h, i = pl.program_id(0), pl.program_id(1)
        q = q_ref[...]
        upper = _sb_tri(lambda r, c: r > c)
        row = lax.broadcasted_iota(jnp.int32, (BQ, BQ), 0)
        causal = lax.broadcasted_iota(jnp.int32, (BQ, BQ), 1) < row

        def save(slot, j):
            return [pltpu.make_async_copy(buf.at[slot], hbm.at[h, i, j], sems.at[w, slot])
                    for w, (buf, hbm) in enumerate(((abuf, a_hbm), (sbuf, s_hbm)))]

        def block(n, run, acc, masked):
            j = i - n
            ks = pl.multiple_of(j * BQ, BQ)
            m, l = _softplus_parts(_dot(q, k_ref[pl.ds(ks, BQ), :], NT) * (scale * LOG2E))
            if masked:
                m = jnp.where(causal, m, 0.0)
            parts = []
            for c in reversed(range(NC)):
                mc = m[:, c * CH:(c + 1) * CH]
                parts.append(l[:, c * CH:(c + 1) * CH] + (_split_dot(mc, upper, CUMSUM_PASSES) + run))
                run = run + jnp.sum(mc, axis=1, keepdims=True)
            a = jnp.exp2(jnp.concatenate(parts[::-1], axis=1))
            sig = jnp.exp2(l)
            if masked:
                a = jnp.where(causal, a, 0.0)
                sig = jnp.where(causal, sig, 0.0)
            ab = a.astype(MXU_DTYPE)
            slot = n % 2
            if not masked:
                @pl.when(n >= 2)
                def _():
                    for cp in save(slot, j):
                        cp.wait()
            abuf[slot] = ab
            sbuf[slot] = sig.astype(MXU_DTYPE)
            for cp in save(slot, j):
                cp.start()
            return run, acc + _dot(ab, v_ref[pl.ds(ks, BQ), :])

        run, acc = block(0, jnp.zeros((BQ, 1), F32), jnp.zeros((BQ, HEAD_DIM), F32), True)
        run, acc = lax.fori_loop(1, i + 1, lambda n, carry: block(n, carry[0], carry[1], False), (run, acc))
        o_ref[...] = acc
        for cp in save(0, i):
            cp.wait()

        @pl.when(i >= 1)
        def _():
            for cp in save(1, i):
                cp.wait()

    tiles = jax.ShapeDtypeStruct((H, nq, nq, BQ, BQ), MXU_DTYPE)
    (o, a_t, s_t), rode = _host_call(kern, "stickbreak_fwd", (H, nq), [q_spec, k_spec, v_spec], [o_spec, ANY, ANY],
                                     [jax.ShapeDtypeStruct((S, HW), F32), tiles, tiles], (mid, mid, mid), ex, _sb_scratch())
    return o, a_t, s_t, rode


def _sb_bwd(mid, dyb, a_t, s_t, HW, ex=None):
    S = mid.shape[0]
    H = HW // HEAD_DIM
    BQ, CH = SB_BQ, SB_BK
    NC = BQ // CH
    scale = HEAD_DIM ** -0.5
    q_spec, k_spec, v_spec, o_spec = _sb_specs(S, H)
    full = pl.BlockSpec((S, HEAD_DIM), lambda h, i: (0, h))

    def kern(q_ref, k_ref, v_ref, do_ref, a_hbm, s_hbm, dq_ref, dk_ref, dv_ref, abuf, sbuf, sems):
        h, i = pl.program_id(0), pl.program_id(1)

        @pl.when(i == 0)
        def _():
            dk_ref[...] = jnp.zeros_like(dk_ref)
            dv_ref[...] = jnp.zeros_like(dv_ref)

        q = q_ref[...]
        do = do_ref[...].astype(MXU_DTYPE)
        excl = _sb_tri(lambda r, c: r < c)

        def fetch(slot, j):
            return [pltpu.make_async_copy(hbm.at[h, i, j], buf.at[slot], sems.at[w, slot])
                    for w, (buf, hbm) in enumerate(((abuf, a_hbm), (sbuf, s_hbm)))]

        for cp in fetch(0, 0):
            cp.start()

        def block(j, carry):
            prun, dq = carry
            slot = j % 2
            for cp in fetch(slot, j):
                cp.wait()

            @pl.when(j < i)
            def _():
                for cp in fetch(1 - slot, j + 1):
                    cp.start()

            ks = pl.multiple_of(j * BQ, BQ)
            k = k_ref[pl.ds(ks, BQ), :]
            ab = abuf[slot]
            p = ab.astype(F32) * _dot(do, v_ref[pl.ds(ks, BQ), :], NT)
            parts = []
            for c in range(NC):
                pc = p[:, c * CH:(c + 1) * CH]
                parts.append(_split_dot(pc, excl, CUMSUM_PASSES) + prun)
                prun = prun + jnp.sum(pc, axis=1, keepdims=True)
            before = jnp.concatenate(parts, axis=1)
            dzb = ((p - sbuf[slot].astype(F32) * (p + before)) * scale).astype(MXU_DTYPE)
            dk_ref[pl.ds(ks, BQ), :] += _dot(dzb, q, TN)
            dv_ref[pl.ds(ks, BQ), :] += _dot(ab, do, TN)
            return prun, dq + _dot(dzb, k)

        _, dq = lax.fori_loop(0, i + 1, block, (jnp.zeros((BQ, 1), F32), jnp.zeros((BQ, HEAD_DIM), F32)))
        dq_ref[...] = dq

    grads, rode = _host_call(kern, "stickbreak_bwd", (H, S // BQ), [q_spec, k_spec, v_spec, o_spec, ANY, ANY], [o_spec, full, full],
                             [jax.ShapeDtypeStruct((S, HW), F32)] * 3, (mid, mid, mid, dyb, a_t, s_t), ex, _sb_scratch())
    return grads, rode


ANY = pl.BlockSpec(memory_space=pl.ANY)


def _place():
    x, y, c = lax.axis_index("x"), lax.axis_index("y"), lax.axis_index("c")
    chips = [(1 - x, y), (x, 1 - y), (1 - x, 1 - y)]
    return x, y, c, chips


def _half(ref, shard, hc, rh):
    return ref.at[shard, pl.ds(pl.multiple_of(hc * rh, 8), rh), :]


def _cast_place(name, w, s_idx):
    R, C = w.shape
    tr = _fit(R, 256)

    def kern(s_ref, w_ref, o_ref):
        o_ref[...] = w_ref[...].astype(o_ref.dtype)

    return pl.pallas_call(
        kern, name=name,
        grid_spec=pltpu.PrefetchScalarGridSpec(
            num_scalar_prefetch=1, grid=(R // tr,),
            in_specs=[pl.BlockSpec((tr, C), lambda r, s_ref: (r, 0))],
            out_specs=pl.BlockSpec((None, tr, C), lambda r, s_ref: (s_ref[0], r, 0))),
        out_shape=jax.ShapeDtypeStruct((N_SHARD, R, C), MXU_DTYPE),
        compiler_params=_params(("arbitrary",)),
    )(s_idx, w)


class _Exchange:
    def __init__(self, inputs, out_shape, aliases, scratch, phases):
        self.inputs, self.out_shape, self.aliases, self.scratch, self.phases = inputs, out_shape, aliases, scratch, phases


def _run_alone(name, ex):
    ni, no = len(ex.inputs), len(ex.out_shape)

    def body(*refs):
        for phase in ex.phases:
            phase(refs[:ni], refs[ni:ni + no], refs[ni + no:])

    return pl.pallas_call(
        body, name=name, in_specs=[ANY] * ni, out_specs=[ANY] * no, out_shape=ex.out_shape,
        input_output_aliases=ex.aliases, scratch_shapes=ex.scratch,
    )(*ex.inputs)


def _host_call(kern, name, grid, in_specs, out_specs, out_shape, operands, ex, scratch=()):
    sem = ("arbitrary",) * len(grid)
    if ex is None:
        return pl.pallas_call(kern, name=name, grid=grid, in_specs=in_specs, out_specs=out_specs, out_shape=out_shape,
                              scratch_shapes=list(scratch), compiler_params=_params(sem))(*operands), []
    n_in, n_out, ri, ro, ns = len(in_specs), len(out_specs), len(ex.inputs), len(ex.out_shape), len(scratch)
    nsteps, nph = 1, len(ex.phases)
    for size in grid:
        nsteps *= size

    def body(*refs):
        r_in, r_out = refs[n_in:n_in + ri], refs[n_in + ri + n_out:n_in + ri + n_out + ro]
        host_scratch, ex_scratch = refs[n_in + ri + n_out + ro:][:ns], refs[n_in + ri + n_out + ro + ns:]
        step = 0
        for axis, size in enumerate(grid):
            step = step * size + pl.program_id(axis)
        for kph, phase in enumerate(ex.phases):
            pl.when(step == (kph * (nsteps - 1)) // (nph - 1))(functools.partial(phase, r_in, r_out, ex_scratch))
        kern(*refs[:n_in], *refs[n_in + ri:n_in + ri + n_out], *host_scratch)

    outs = pl.pallas_call(
        body, name=name, grid=grid, in_specs=list(in_specs) + [ANY] * ri, out_specs=list(out_specs) + [ANY] * ro,
        out_shape=list(out_shape) + list(ex.out_shape), scratch_shapes=list(scratch) + list(ex.scratch),
        input_output_aliases={n_in + a: n_out + b for a, b in ex.aliases.items()},
        compiler_params=_params(sem),
    )(*operands, *ex.inputs)
    return outs[:n_out], outs[n_out:]


def _gather_exchange(bufs):
    n = len(bufs)

    def between_chips(outs, sems, i, j, chip, c, shard):
        blk = _half(outs[i], shard, c, outs[i].shape[1] // 2)
        return pltpu.make_async_remote_copy(src_ref=blk, dst_ref=blk, send_sem=sems[0].at[i, j], recv_sem=sems[1].at[i, j],
                                            device_id=(chip[0], chip[1], c), device_id_type=MESH)

    def to_sibling(outs, sems, i, j, x, y, c, shard, hc):
        blk = _half(outs[i], shard, hc, outs[i].shape[1] // 2)
        return pltpu.make_async_remote_copy(src_ref=blk, dst_ref=blk, send_sem=sems[0].at[i, 3 + j], recv_sem=sems[1].at[i, 3 + j],
                                            device_id=(x, y, 1 - c), device_id_type=MESH)

    def send_mine(ins, outs, sems):
        x, y, c, chips = _place()
        for i in range(n):
            for j, chip in enumerate(chips):
                between_chips(outs, sems, i, j, chip, c, 2 * x + y).start()

    def pass_on(ins, outs, sems):
        x, y, c, chips = _place()
        for i in range(n):
            for j, chip in enumerate(chips):
                between_chips(outs, sems, i, j, chip, c, 2 * chip[0] + chip[1]).wait_recv()
                to_sibling(outs, sems, i, j, x, y, c, 2 * chip[0] + chip[1], c).start()

    def finish(ins, outs, sems):
        x, y, c, chips = _place()
        for i in range(n):
            for j, chip in enumerate(chips):
                to_sibling(outs, sems, i, j, x, y, c, 2 * chip[0] + chip[1], 1 - c).wait_recv()
        for i in range(n):
            for j, chip in enumerate(chips):
                between_chips(outs, sems, i, j, chip, c, 2 * x + y).wait_send()
                to_sibling(outs, sems, i, j, x, y, c, 2 * chip[0] + chip[1], c).wait_send()

    return _Exchange(list(bufs), [jax.ShapeDtypeStruct(b.shape, b.dtype) for b in bufs], {i: i for i in range(n)},
                     [pltpu.SemaphoreType.DMA((n, 6)), pltpu.SemaphoreType.DMA((n, 6))], [send_mine, pass_on, finish])


def _reduce_exchange(g16, g32):
    n = len(g16)

    def copies(ins, outs, sems):
        x, y, c, _ = _place()
        for i in range(n):
            rh = ins[i].shape[1] // 2
            for r in range(1, 8):
                px, py, pc = x ^ ((r >> 2) & 1), y ^ ((r >> 1) & 1), c ^ (r & 1)
                src = _half(ins[i] if r > 1 else ins[n + i], 2 * px + py, pc, rh)
                dst = outs[2 * i + 1].at[r - 2] if r > 1 else outs[2 * i]
                yield pltpu.make_async_remote_copy(src_ref=src, dst_ref=dst, send_sem=sems[0].at[i, r - 1], recv_sem=sems[1].at[i, r - 1],
                                                   device_id=(px, py, pc), device_id_type=MESH)

    def start(ins, outs, sems):
        for cp in copies(ins, outs, sems):
            cp.start()

    def finish(ins, outs, sems):
        for cp in copies(ins, outs, sems):
            cp.wait()

    out_shape = []
    for g in g16:
        rh, C = g.shape[1] // 2, g.shape[2]
        out_shape += [jax.ShapeDtypeStruct((rh, C), F32), jax.ShapeDtypeStruct((6, rh, C), g.dtype)]
    return _Exchange(list(g16) + list(g32), out_shape, {},
                     [pltpu.SemaphoreType.DMA((n, 7)), pltpu.SemaphoreType.DMA((n, 7))], [start, finish])


def _add_direct(name, g32, from_sibling, from_chips, s_idx, c_idx):
    _, R, C = g32.shape
    rh = R // 2
    tr = _fit(rh, 256)
    nrb = rh // tr

    def kern(s_ref, c_ref, g_ref, a_ref, b_ref, out_ref):
        acc = g_ref[...] + a_ref[...]
        for k in range(6):
            acc = acc + b_ref[k].astype(F32)
        out_ref[...] = acc

    return pl.pallas_call(
        kern, name=name,
        grid_spec=pltpu.PrefetchScalarGridSpec(
            num_scalar_prefetch=2, grid=(nrb,),
            in_specs=[pl.BlockSpec((None, tr, C), lambda r, s_ref, c_ref: (s_ref[0], c_ref[0] * nrb + r, 0)),
                      pl.BlockSpec((tr, C), lambda r, s_ref, c_ref: (r, 0)),
                      pl.BlockSpec((6, tr, C), lambda r, s_ref, c_ref: (0, r, 0))],
            out_specs=pl.BlockSpec((tr, C), lambda r, s_ref, c_ref: (r, 0))),
        out_shape=jax.ShapeDtypeStruct((rh, C), F32),
        compiler_params=_params(("arbitrary",)),
    )(s_idx, c_idx, g32, from_sibling, from_chips)


def _swap_reduced(halves):
    n = len(halves)

    def body(*refs):
        ins, outs = refs[:n], refs[n:2 * n]
        send, recv = refs[2 * n:]
        x, y, c, _ = _place()
        copies = []
        for i in range(n):
            cp = pltpu.make_async_remote_copy(src_ref=ins[i], dst_ref=outs[i], send_sem=send.at[i], recv_sem=recv.at[i],
                                              device_id=(x, y, 1 - c), device_id_type=MESH)
            cp.start()
            copies.append(cp)
        for cp in copies:
            cp.wait()

    return pl.pallas_call(
        body, name="grad_swap_reduced",
        in_specs=[ANY] * n, out_specs=[ANY] * n,
        out_shape=[jax.ShapeDtypeStruct(h.shape, F32) for h in halves],
        scratch_shapes=[pltpu.SemaphoreType.DMA((n,)), pltpu.SemaphoreType.DMA((n,))],
    )(*halves)


def _all_reduce_small(v):
    rows, W = v.shape
    ndev = 8

    def body(v_ref, out_ref, buf, send, recv):
        x, y, c, _ = _place()
        me = 4 * x + 2 * y + c
        buf[me] = v_ref[...]
        copies = []
        for r in range(1, ndev):
            fx, fy, fc = (r >> 2) & 1, (r >> 1) & 1, r & 1
            peer = (x ^ fx, y ^ fy, c ^ fc)
            cp = pltpu.make_async_remote_copy(src_ref=v_ref, dst_ref=buf.at[me], send_sem=send.at[r - 1], recv_sem=recv.at[r - 1],
                                              device_id=peer, device_id_type=MESH)
            cp.start()
            copies.append(cp)
        for cp in copies:
            cp.wait()
        acc = buf[0]
        for k in range(1, ndev):
            acc = acc + buf[k]
        out_ref[...] = acc

    return pl.pallas_call(
        body, name="small_all_reduce",
        in_specs=[pl.BlockSpec(memory_space=pltpu.VMEM)], out_specs=pl.BlockSpec(memory_space=pltpu.VMEM),
        out_shape=jax.ShapeDtypeStruct((rows, W), F32),
        scratch_shapes=[pltpu.VMEM((ndev, rows, W), F32), pltpu.SemaphoreType.DMA((ndev - 1,)), pltpu.SemaphoreType.DMA((ndev - 1,))],
    )(v)


def _adamw_update(gv, w_ref, m_ref, v_ref, d_ref, nm_ref, nv_ref):
    nm = ADAM_B1 * m_ref[...] + (1.0 - ADAM_B1) * gv
    nv = ADAM_B2 * v_ref[...] + (1.0 - ADAM_B2) * (gv * gv)
    m_hat = nm / (1.0 - ADAM_B1 ** ADAM_STEP)
    v_hat = nv / (1.0 - ADAM_B2 ** ADAM_STEP)
    d_ref[...] = -ADAM_LR * (m_hat / (jnp.sqrt(v_hat) + ADAM_EPS) + ADAM_WD * w_ref[...])
    nm_ref[...] = nm
    nv_ref[...] = nv


def _adamw(name, w, g, m, v):
    R, C = w.shape
    tr = _fit(R, 256)

    def body(w_ref, g_ref, m_ref, v_ref, d_ref, nm_ref, nv_ref):
        _adamw_update(g_ref[...], w_ref, m_ref, v_ref, d_ref, nm_ref, nv_ref)

    return _rows_call(name, body, R, tr, [(a, _row(tr, C)) for a in (w, g, m, v)], [(C, F32)] * 3)


def _adamw_halves(name, w, mine, theirs, m, v, c_idx):
    R, C = w.shape
    rh = R // 2
    tr = _fit(rh, 256)
    nrb = rh // tr

    def kern(c_ref, w_ref, a_ref, b_ref, m_ref, v_ref, g_ref, d_ref, nm_ref, nv_ref):
        gv = jnp.where(pl.program_id(0) // nrb == c_ref[0], a_ref[...], b_ref[...])
        g_ref[...] = gv
        _adamw_update(gv, w_ref, m_ref, v_ref, d_ref, nm_ref, nv_ref)

    full = pl.BlockSpec((tr, C), lambda r, c_ref: (r, 0))
    pick = lambda own: pl.BlockSpec((tr, C), lambda r, c_ref: (jnp.where((r // nrb == c_ref[0]) == own, r % nrb, 0), 0))
    return pl.pallas_call(
        kern, name=name,
        grid_spec=pltpu.PrefetchScalarGridSpec(
            num_scalar_prefetch=1, grid=(R // tr,),
            in_specs=[full, pick(True), pick(False), full, full], out_specs=[full] * 4),
        out_shape=[jax.ShapeDtypeStruct((R, C), F32)] * 4,
        compiler_params=_params(("arbitrary",)),
    )(c_idx, w, mine, theirs, m, v)


def _sigmoid(z):
    return 1.0 / (1.0 + jnp.exp(-z))


def kernel(x, p, g_mix, w_in, qn_gain, kn_gain, w_branch_a, w_branch_b, w_out, g_mlp, w_up, w_down, g_ple, w_ple_gate, w_ple_proj, loss_target, m_g_mix, m_w_in, m_qn_gain, m_kn_gain, m_w_branch_a, m_w_branch_b, m_w_out, m_g_mlp, m_w_up, m_w_down, m_g_ple, m_w_ple_gate, m_w_ple_proj, v_g_mix, v_w_in, v_qn_gain, v_kn_gain, v_w_branch_a, v_w_branch_b, v_w_out, v_g_mlp, v_w_up, v_w_down, v_g_ple, v_w_ple_gate, v_w_ple_proj):
    S, D = x.shape[1], x.shape[2]
    HW = w_branch_a.shape[1]
    x2d, tgt, p2d = x.reshape(S, D), loss_target.reshape(S, D), p.reshape(S, p.shape[-1])
    big = {"w_in": w_in, "w_branch_a": w_branch_a, "w_branch_b": w_branch_b, "w_out": w_out, "w_up": w_up,
           "w_down": w_down, "w_ple_gate": w_ple_gate, "w_ple_proj": w_ple_proj}
    moments = {"w_in": (m_w_in, v_w_in), "w_branch_a": (m_w_branch_a, v_w_branch_a), "w_branch_b": (m_w_branch_b, v_w_branch_b),
               "w_out": (m_w_out, v_w_out), "w_up": (m_w_up, v_w_up), "w_down": (m_w_down, v_w_down),
               "w_ple_gate": (m_w_ple_gate, v_w_ple_gate), "w_ple_proj": (m_w_ple_proj, v_w_ple_proj)}
    names = list(big)
    col_sharded = {"w_in", "w_branch_a", "w_branch_b", "w_up", "w_ple_proj"}
    shard2d = {k: w.reshape(w.shape[1], w.shape[2]) for k, w in big.items()}

    c_idx = lax.axis_index("c").astype(jnp.int32).reshape(1)
    s_idx = (2 * lax.axis_index("x") + lax.axis_index("y")).astype(jnp.int32).reshape(1)
    placed = {k: _cast_place(f"cast_{k}", shard2d[k], s_idx) for k in names}
    late = [k for k in names if k != "w_in"]
    W = {"w_in": _run_alone("gather_w_in", _gather_exchange([placed["w_in"]]))[0]}
    cin = W["w_in"].shape[2]
    bn_in = _fit(cin, 512)
    while (2 * HW) % bn_in:
        bn_in -= 128

    h = _rmsnorm_fwd("rmsnorm_mix", x2d, g_mix)
    (qk,) = _matmul("proj_qk", h, W["w_in"], mode="nn", out_dtypes=[F32], b_cshard=True, b_off=0, n_out=2 * HW, bn=bn_in, bk=D)
    (mid,) = _matmul("proj_mid", h, W["w_in"], mode="nn", out_dtypes=[MXU_DTYPE], b_cshard=True, b_off=2 * HW // bn_in,
                     n_out=4 * HW, bn=bn_in, bk=D)
    (sg,) = _matmul("proj_gates", h, W["w_in"], mode="nn", out_dtypes=[F32], b_cshard=True, b_off=6 * HW // bn_in,
                    n_out=2 * D, bn=bn_in, bk=D, epilogue=lambda acc: (_sigmoid(acc),))
    tabs = _rope_tables(S)
    qa, ka = _qknorm_fwd(qk, qn_gain, kn_gain, tabs, HW)
    dil = [_dilated_fwd(qa, ka, mid, d, HW) for d in DILATIONS]
    ya, lse = _dilated_combine([o for o, _ in dil], [l for _, l in dil], HW)
    yb, sb_a, sb_sig, gathered = _sb_fwd(mid, HW, _gather_exchange([placed[k] for k in late]))
    W.update({k: (g if k in col_sharded else g.reshape(-1, g.shape[2])) for k, g in zip(late, gathered)})

    gate_blocks = D // _fit(D, 1024)
    (ua,) = _matmul("branch_a", ya, W["w_branch_a"], mode="nn", out_dtypes=[F32], b_cshard=True, bn=_fit(W["w_branch_a"].shape[2], 1024))
    bn_b = _fit(W["w_branch_b"].shape[2], 1024)
    ub, merged = _matmul("branch_b_merge", yb, W["w_branch_b"], mode="nn", out_dtypes=[F32, MXU_DTYPE], b_cshard=True, bn=bn_b,
                         extras=[(sg, 0), (sg, D // bn_b), (ua, 0)],
                         epilogue=lambda acc, sga, sgb, uav: (acc, sga * uav + sgb * acc))
    (x1,) = _matmul("out_proj", merged, W["w_out"], mode="nn", out_dtypes=[F32], extras=[(x2d, 0)], epilogue=lambda acc, xv: (xv + acc,))
    hm = _rmsnorm_fwd("rmsnorm_mlp", x1, g_mlp)

    def up_epilogue(acc):
        r = jnp.maximum(acc, 0.0)
        return r * r, r

    act, rup = _matmul("mlp_up", hm, W["w_up"], mode="nn", out_dtypes=[MXU_DTYPE, MXU_DTYPE], b_cshard=True,
                       bn=_fit(W["w_up"].shape[2], 1024), bk=D, epilogue=up_epilogue)
    (x2,) = _matmul("mlp_down", act, W["w_down"], mode="nn", out_dtypes=[F32], extras=[(x1, 0)], epilogue=lambda acc, xv: (xv + acc,))
    hp = _rmsnorm_fwd("rmsnorm_ple", x2, g_ple)
    (pp,) = _matmul("ple_proj", p2d, W["w_ple_proj"], mode="nn", out_dtypes=[F32], b_cshard=True, bn=_fit(W["w_ple_proj"].shape[2], 1024))

    def ple_epilogue(acc, ppv, x2v, tv):
        s = _sigmoid(acc)
        dx3 = ((x2v + ppv * s) - tv) / D
        return dx3, dx3 * s, dx3 * ppv * (s * (1.0 - s))

    dx3, d_pp, d_gate = _matmul("ple_gate_loss", hp, W["w_ple_gate"], mode="nn", out_dtypes=[F32, MXU_DTYPE, MXU_DTYPE],
                                bm=512, extras=[(pp, 0), (x2, 0), (tgt, 0)], epilogue=ple_epilogue)

    G, G16 = {}, {}
    G["w_ple_proj"], G16["w_ple_proj"] = _matmul("grad_w_ple_proj", p2d, d_pp, mode="tn", out_dtypes=[F32, MXU_DTYPE], out_cshard=True,
                                 bn=_fit(d_pp.shape[1] // N_SHARD, 1024))
    G["w_ple_gate"], G16["w_ple_gate"] = _matmul("grad_w_ple_gate", hp, d_gate, mode="tn", out_dtypes=[F32, MXU_DTYPE])
    (d_hp,) = _matmul("ple_gate_bwd", d_gate, W["w_ple_gate"], mode="nt", out_dtypes=[F32])
    dx2, g_g_ple, loss_part = _rmsnorm_bwd("rmsnorm_ple_bwd", d_hp, x2, g_ple, dx3, True)
    G["w_down"], G16["w_down"] = _matmul("grad_w_down", act, dx2, mode="tn", out_dtypes=[F32, MXU_DTYPE])
    (d_up,) = _matmul("mlp_down_bwd", dx2, W["w_down"], mode="nt", out_dtypes=[MXU_DTYPE], extras=[(rup, 0)],
                      epilogue=lambda acc, r: (acc * (2.0 * r.astype(F32)),))
    G["w_up"], G16["w_up"] = _matmul("grad_w_up", hm, d_up, mode="tn", out_dtypes=[F32, MXU_DTYPE], out_cshard=True, bn=_fit(d_up.shape[1] // N_SHARD, 1024))
    (d_hm,) = _matmul("mlp_up_bwd", d_up, W["w_up"], mode="nt", out_dtypes=[F32], b_cshard=True, bk=_fit(W["w_up"].shape[2], 2048))
    dx1, g_g_mlp = _rmsnorm_bwd("rmsnorm_mlp_bwd", d_hm, x1, g_mlp, dx2, False)
    G["w_out"], G16["w_out"] = _matmul("grad_w_out", merged, dx1, mode="tn", out_dtypes=[F32, MXU_DTYPE])

    def merge_bwd(acc, sga, sgb, uav, ubv):
        return acc * sga, acc * sgb, acc * uav * (sga * (1.0 - sga)), acc * ubv * (sgb * (1.0 - sgb))

    bn_m = _fit(D, 1024)
    d_ua, d_ub, d_ga, d_gb = _matmul("out_proj_bwd", dx1, W["w_out"], mode="nt", out_dtypes=[MXU_DTYPE] * 4, bm=512, bn=bn_m,
                                     extras=[(sg, 0), (sg, D // bn_m), (ua, 0), (ub, 0)], epilogue=merge_bwd)
    bn_br = _fit(D // N_SHARD, 1024)
    G["w_branch_a"], G16["w_branch_a"] = _matmul("grad_w_branch_a", ya, d_ua, mode="tn", out_dtypes=[F32, MXU_DTYPE], out_cshard=True, bn=bn_br)
    G["w_branch_b"], G16["w_branch_b"] = _matmul("grad_w_branch_b", yb, d_ub, mode="tn", out_dtypes=[F32, MXU_DTYPE], out_cshard=True, bn=bn_br)
    (d_ya,) = _matmul("branch_a_bwd", d_ua, W["w_branch_a"], mode="nt", out_dtypes=[F32], b_cshard=True, bk=bn_br)
    (d_yb,) = _matmul("branch_b_bwd", d_ub, W["w_branch_b"], mode="nt", out_dtypes=[F32], b_cshard=True, bk=bn_br)

    as_shards = lambda k, g: g if k in col_sharded else g.reshape(N_SHARD, -1, g.shape[1])
    (dqb, dkb, dvb), partials = _sb_bwd(mid, d_yb, sb_a, sb_sig, HW, _reduce_exchange([as_shards(k, G16[k]) for k in late],
                                                                               [as_shards(k, G[k]) for k in late]))
    dil_b = [_dilated_bwd(qa, ka, mid, d_ya, ya, lse, d, HW) for d in DILATIONS]
    d_qk, g_qn, g_kn = _qknorm_bwd(qk, qn_gain, kn_gain, tabs, [t[0] for t in dil_b], [t[1] for t in dil_b],
                                   [t[2] for t in dil_b], HW)
    dva = _dv_sum([t[3] for t in dil_b], [t[4] for t in dil_b], HW)
    d_proj = jnp.concatenate([d_qk, dva, dqb.astype(MXU_DTYPE), dkb.astype(MXU_DTYPE), dvb.astype(MXU_DTYPE), d_ga, d_gb], axis=1)
    G["w_in"], G16["w_in"] = _matmul("grad_w_in", h, d_proj, mode="tn", out_dtypes=[F32, MXU_DTYPE], out_cshard=True, bn=_fit(cin, 1280))
    (d_h,), partials_in = _matmul("proj_bwd", d_proj, W["w_in"], mode="nt", out_dtypes=[F32], b_cshard=True, bk=_fit(cin, 1280),
                                  ride=_reduce_exchange([G16["w_in"]], [G["w_in"]]))
    grad_x, g_g_mix = _rmsnorm_bwd("rmsnorm_mix_bwd", d_h, x2d, g_mix, dx1, False)

    mine = {k: _add_direct(f"grad_add_{k}", as_shards(k, G[k]), partials[2 * n], partials[2 * n + 1], s_idx, c_idx)
            for n, k in enumerate(late)}
    mine["w_in"] = _add_direct("grad_add_w_in", G["w_in"], partials_in[0], partials_in[1], s_idx, c_idx)
    halves = [mine[k] for k in names]
    others = _swap_reduced(halves)

    pack_w = -(-(3 * D + 3 * 128) // (8 * 128)) * 128

    def pack(v_mix, v_mlp, v_ple, v_qn, v_kn, extra):
        flat = jnp.concatenate([v_mix.reshape(-1), v_mlp.reshape(-1), v_ple.reshape(-1), v_qn.reshape(-1), v_kn.reshape(-1), extra.reshape(-1)])
        return jnp.pad(flat, (0, 8 * pack_w - flat.shape[0])).reshape(8, pack_w)

    def unpack(blk):
        flat = blk.reshape(-1)
        return (flat[:D].reshape(1, D), flat[D:2 * D].reshape(1, D), flat[2 * D:3 * D].reshape(1, D),
                flat[3 * D:3 * D + 128].reshape(1, 128), flat[3 * D + 128:3 * D + 256].reshape(1, 128), flat[3 * D + 256])

    small = _all_reduce_small(pack(g_g_mix, g_g_mlp, g_g_ple, g_qn, g_kn, loss_part))
    sw = pack(g_mix, g_mlp, g_ple, qn_gain, kn_gain, jnp.zeros((128,), F32))
    sm = pack(m_g_mix, m_g_mlp, m_g_ple, m_qn_gain, m_kn_gain, jnp.zeros((128,), F32))
    sv = pack(v_g_mix, v_g_mlp, v_g_ple, v_qn_gain, v_kn_gain, jnp.ones((128,), F32))
    s_delta, s_nm, s_nv = _adamw("adamw_small", sw, small, sm, sv)
    sg_mix, sg_mlp, sg_ple, sg_qn, sg_kn, loss = unpack(small)
    small_out = {}
    for tag, blk in (("delta", s_delta), ("new_m", s_nm), ("new_v", s_nv)):
        u = unpack(blk)
        small_out[tag] = dict(g_mix=u[0], g_mlp=u[1], g_ple=u[2], qn_gain=u[3], kn_gain=u[4])
    small_grad = dict(g_mix=sg_mix, g_mlp=sg_mlp, g_ple=sg_ple, qn_gain=sg_qn, kn_gain=sg_kn)

    big_out = {"grad": {}, "delta": {}, "new_m": {}, "new_v": {}}
    for k, mine, theirs in zip(names, halves, others):
        shape = big[k].shape
        m2, v2 = (t.reshape(shape[1], shape[2]) for t in moments[k])
        res = _adamw_halves(f"adamw_{k}", shard2d[k], mine, theirs, m2, v2, c_idx)
        for tag, t in zip(("grad", "delta", "new_m", "new_v"), res):
            big_out[tag][k] = t.reshape(shape)

    order = ["g_mix", "w_in", "qn_gain", "kn_gain", "w_branch_a", "w_branch_b", "w_out", "g_mlp", "w_up", "w_down", "g_ple",
             "w_ple_gate", "w_ple_proj"]
    outs = [loss, grad_x.reshape(x.shape)]
    outs += [small_grad[k] if k in small_grad else big_out["grad"][k] for k in order]
    for tag in ("delta", "new_m", "new_v"):
        outs += [small_out[tag][k] if k in small_grad else big_out[tag][k] for k in order]
    return tuple(outs)
```

```python
import functools

import jax
import jax.numpy as jnp
from jax import lax
from jax.experimental import pallas as pl
from jax.experimental.pallas import tpu as pltpu

F32 = jnp.float32
MXU_DTYPE = jnp.bfloat16
HEAD_DIM = 128
ROT_DIM = HEAD_DIM // 4
ROPE_THETA = 500000.0
EPS = 1e-6
DILATIONS = (1, 4, 16)
BLOCK = 128
N_SHARD = 4
ADAM_LR, ADAM_B1, ADAM_B2, ADAM_EPS, ADAM_WD, ADAM_STEP = 0.001, 0.9, 0.999, 1e-08, 0.01, 10
MXU_WIDTH = 256
V7X_VMEM_BYTES = 64 * 1024 * 1024
VMEM_LIMIT = V7X_VMEM_BYTES - 8 * 1024 * 1024
MESH = pl.DeviceIdType.MESH
NEG = -1e30
SB_BQ, SB_BK = 1024, 256
LOG2E = 1.4426950408889634
CUMSUM_PASSES = 2


def _fit(dim, pref):
    if dim <= pref:
        return dim
    b = (pref // 128) * 128
    while dim % b:
        b -= 128
    return b


def _params(sem=None):
    return pltpu.CompilerParams(dimension_semantics=sem, vmem_limit_bytes=VMEM_LIMIT)


def _dot(a, b, dims=(((1,), (0,)), ((), ()))):
    return lax.dot_general(a, b, dims, preferred_element_type=F32)


NT = (((1,), (1,)), ((), ()))
TN = (((0,), (0,)), ((), ()))


def _split_dot(x, u, passes):
    out = None
    r = x
    for p in range(passes):
        hi = r.astype(MXU_DTYPE)
        part = _dot(hi, u)
        out = part if out is None else out + part
        if p + 1 < passes:
            r = r - hi.astype(F32)
    return out


def _matmul(name, a, b, *, mode, out_dtypes, bm=1024, bn=1024, bk=2048, b_cshard=False, b_off=0, n_out=None,
            extras=(), epilogue=None, out_cshard=False, ride=None):
    if mode == "tn":
        K, M = a.shape
        N = b.shape[1]
    else:
        M, K = a.shape
        if mode == "nn":
            N = n_out if n_out is not None else (N_SHARD * b.shape[2] if b_cshard else b.shape[1])
        else:
            N = b.shape[1] if b_cshard else b.shape[0]
    bm, bn, bk = _fit(M, bm), _fit(N, bn), _fit(K, bk)
    nk = K // bk
    grid = (M // bm, N // bn, nk)

    if mode == "tn":
        a_spec = pl.BlockSpec((bk, bm), lambda i, j, k: (k, i))
        b_spec = pl.BlockSpec((bk, bn), lambda i, j, k: (k, j))
        dims = TN
    elif mode == "nn":
        a_spec = pl.BlockSpec((bm, bk), lambda i, j, k: (i, k))
        if b_cshard:
            cb = b.shape[2] // bn
            b_spec = pl.BlockSpec((None, bk, bn), lambda i, j, k: ((j + b_off) // cb, k, (j + b_off) % cb))
        else:
            b_spec = pl.BlockSpec((bk, bn), lambda i, j, k: (k, j + b_off))
        dims = (((1,), (0,)), ((), ()))
    else:
        a_spec = pl.BlockSpec((bm, bk), lambda i, j, k: (i, k))
        if b_cshard:
            cb = b.shape[2] // bk
            b_spec = pl.BlockSpec((None, bn, bk), lambda i, j, k: (k // cb, j, k % cb))
        else:
            b_spec = pl.BlockSpec((bn, bk), lambda i, j, k: (j, k))
        dims = NT

    ex_arrays = [e[0] for e in extras]
    ex_specs = [pl.BlockSpec((bm, bn), functools.partial(lambda i, j, k, off: (i, j + off), off=e[1])) for e in extras]
    if out_cshard:
        cbo = (N // N_SHARD) // bn
        out_shape = [jax.ShapeDtypeStruct((N_SHARD, M, N // N_SHARD), dt) for dt in out_dtypes]
        out_specs = [pl.BlockSpec((None, bm, bn), lambda i, j, k: (j // cbo, i, j % cbo)) for _ in out_dtypes]
    else:
        out_shape = [jax.ShapeDtypeStruct((M, N), dt) for dt in out_dtypes]
        out_specs = [pl.BlockSpec((bm, bn), lambda i, j, k: (i, j)) for _ in out_dtypes]
    ne, no = len(extras), len(out_dtypes)
    cw = MXU_WIDTH if bn % MXU_WIDTH == 0 else bn

    def kern(*refs):
        a_ref, b_ref = refs[0], refs[1]
        ex_refs = refs[2:2 + ne]
        o_refs = refs[2 + ne:2 + ne + no]
        av = a_ref[...].astype(MXU_DTYPE)

        def finish(acc):
            vals = (acc,) * no if epilogue is None else epilogue(acc, *[r[...] for r in ex_refs])
            for r, v in zip(o_refs, vals):
                r[...] = v.astype(r.dtype)

        if nk == 1:
            finish(_dot(av, b_ref[...].astype(MXU_DTYPE), dims))
        else:
            acc_ref = refs[2 + ne + no]
            k = pl.program_id(2)

            @pl.when(k == 0)
            def _():
                acc_ref[...] = jnp.zeros_like(acc_ref)

            for c0 in range(0, bn, cw):
                bv = b_ref[c0:c0 + cw, :] if mode == "nt" else b_ref[:, c0:c0 + cw]
                acc_ref[:, c0:c0 + cw] += _dot(av, bv.astype(MXU_DTYPE), dims)

            @pl.when(k == nk - 1)
            def _():
                finish(acc_ref[...])

    outs, rode = _host_call(kern, name, grid, [a_spec, b_spec] + ex_specs, out_specs, out_shape, (a, b, *ex_arrays), ride,
                            [pltpu.VMEM((bm, bn), F32)] if nk > 1 else [])
    return outs if ride is None else (outs, rode)


def _row(tr, w, coff=0):
    return pl.BlockSpec((tr, w), lambda i: (i, coff))


def _vec(w):
    return pl.BlockSpec((1, w), lambda i: (0, 0))


def _rows_call(name, body, n_rows, tr, ins, outs, accs=()):
    n_in, n_out = len(ins), len(outs)

    def kern(*refs):
        acc_refs = refs[n_in + n_out:]
        if acc_refs:
            @pl.when(pl.program_id(0) == 0)
            def _():
                for r in acc_refs:
                    r[...] = jnp.zeros_like(r)
        body(*refs)

    out_shape = [jax.ShapeDtypeStruct((n_rows, w), dt) for w, dt in outs] + [jax.ShapeDtypeStruct((1, w), F32) for w in accs]
    out_specs = [_row(tr, w) for w, _ in outs] + [_vec(w) for w in accs]
    return pl.pallas_call(
        kern, name=name, grid=(n_rows // tr,),
        in_specs=[s for _, s in ins], out_specs=out_specs, out_shape=out_shape,
        compiler_params=_params(("arbitrary",)),
    )(*[a for a, _ in ins])


def _rmsnorm_fwd(name, x, g):
    S, D = x.shape
    tr = _fit(S, 256)

    def body(x_ref, g_ref, h_ref):
        xv = x_ref[...]
        r = lax.rsqrt(jnp.mean(xv * xv, axis=1, keepdims=True) + EPS)
        h_ref[...] = ((xv * r) * g_ref[...]).astype(h_ref.dtype)

    return _rows_call(name, body, S, tr, [(x, _row(tr, D)), (g, _vec(D))], [(D, MXU_DTYPE)])[0]


def _rmsnorm_bwd(name, dh, x, g, resid, with_loss, with_copy):
    S, D = x.shape
    tr = _fit(S, 256)

    def body(dh_ref, x_ref, g_ref, res_ref, dx_ref, *rest):
        copy_ref = rest[:1] if with_copy else ()
        dg_ref, loss_ref = rest[len(copy_ref)], rest[len(copy_ref) + 1:]
        xv = x_ref[...]
        r = lax.rsqrt(jnp.mean(xv * xv, axis=1, keepdims=True) + EPS)
        dhv = dh_ref[...]
        u = dhv * g_ref[...]
        xr = xv * r
        dx = r * u - xr * (r * r) * jnp.mean(xv * u, axis=1, keepdims=True)
        resv = res_ref[...]
        dx_ref[...] = resv + dx
        for r in copy_ref:
            r[...] = (resv + dx).astype(r.dtype)
        dg_ref[...] += jnp.sum(dhv * xr, axis=0, keepdims=True)
        if with_loss:
            loss_ref[0][...] += (0.5 * D) * jnp.sum(resv * resv)

    outs = _rows_call(name, body, S, tr, [(dh, _row(tr, D)), (x, _row(tr, D)), (g, _vec(D)), (resid, _row(tr, D))],
                      [(D, F32)] + ([(D, MXU_DTYPE)] if with_copy else []), accs=(D, 128) if with_loss else (D,))
    return outs


def _rope_tables(S):
    half = ROT_DIM // 2
    pos = jnp.arange(S, dtype=F32)
    inv = ROPE_THETA ** (-jnp.arange(0, ROT_DIM, 2, dtype=F32) / ROT_DIM)
    ang = pos[:, None] * inv[None, :]
    cos, sin = jnp.cos(ang), jnp.sin(ang)
    pad = HEAD_DIM - ROT_DIM
    ctab = jnp.concatenate([cos, cos, jnp.ones((S, pad), F32)], axis=1)
    atab = jnp.concatenate([-sin, jnp.zeros((S, pad + half), F32)], axis=1)
    btab = jnp.concatenate([jnp.zeros((S, half), F32), sin, jnp.zeros((S, pad), F32)], axis=1)
    return ctab, atab, btab


def _qknorm_fwd(qk, qn, kn, tabs, HW):
    S = qk.shape[0]
    tr = _fit(S, 256)
    half = ROT_DIM // 2

    def body(qk_ref, qn_ref, kn_ref, c_ref, a_ref, b_ref, q_out, k_out):
        ct, at, bt = c_ref[...], a_ref[...], b_ref[...]
        for part, (g_ref, o_ref) in enumerate(((qn_ref, q_out), (kn_ref, k_out))):
            gv = g_ref[...]
            for h in range(HW // HEAD_DIM):
                xh = qk_ref[:, part * HW + h * HEAD_DIM: part * HW + (h + 1) * HEAD_DIM]
                r = lax.rsqrt(jnp.mean(xh * xh, axis=1, keepdims=True) + EPS)
                y = (xh * r) * gv
                o = y * ct + pltpu.roll(y, HEAD_DIM - half, 1) * at + pltpu.roll(y, half, 1) * bt
                o_ref[:, h * HEAD_DIM:(h + 1) * HEAD_DIM] = o.astype(o_ref.dtype)

    ins = [(qk, _row(tr, 2 * HW)), (qn, _vec(HEAD_DIM)), (kn, _vec(HEAD_DIM))] + [(t, _row(tr, HEAD_DIM)) for t in tabs]
    return _rows_call("qknorm_fwd", body, S, tr, ins, [(HW, MXU_DTYPE), (HW, MXU_DTYPE)])


def _shift_spec(tr, w, shift, nblk):
    return pl.BlockSpec((tr, w), lambda i: (jnp.minimum(i + shift, nblk - 1), 0))


def _qknorm_bwd(qk, qn, kn, tabs, dq_parts, dk_cur, dk_prev, HW):
    S = qk.shape[0]
    tr = BLOCK
    nblk = S // tr
    half = ROT_DIM // 2
    nd = len(DILATIONS)

    def body(*refs):
        qk_ref, qn_ref, kn_ref, c_ref, a_ref, b_ref = refs[:6]
        dq_refs = refs[6:6 + nd]
        dkc_refs = refs[6 + nd:6 + 2 * nd]
        dkp_refs = refs[6 + 2 * nd:6 + 3 * nd]
        d_out, dqn_ref, dkn_ref = refs[6 + 3 * nd:]
        i = pl.program_id(0)
        ct, at, bt = c_ref[...], a_ref[...], b_ref[...]
        live = [(i + d < nblk).astype(F32) for d in DILATIONS]
        for part, (g_ref, dg_ref) in enumerate(((qn_ref, dqn_ref), (kn_ref, dkn_ref))):
            gv = g_ref[...]
            dg = jnp.zeros((1, HEAD_DIM), F32)
            for h in range(HW // HEAD_DIM):
                hs = slice(h * HEAD_DIM, (h + 1) * HEAD_DIM)
                if part == 0:
                    do = dq_refs[0][:, hs] + dq_refs[1][:, hs] + dq_refs[2][:, hs]
                else:
                    do = dkc_refs[0][:, hs] + dkc_refs[1][:, hs] + dkc_refs[2][:, hs]
                    for n in range(nd):
                        do = do + dkp_refs[n][:, hs] * live[n]
                dy = do * ct + pltpu.roll(do * at, half, 1) + pltpu.roll(do * bt, HEAD_DIM - half, 1)
                xh = qk_ref[:, part * HW + h * HEAD_DIM: part * HW + (h + 1) * HEAD_DIM]
                r = lax.rsqrt(jnp.mean(xh * xh, axis=1, keepdims=True) + EPS)
                xr = xh * r
                u = dy * gv
                dx = r * u - xr * (r * r) * jnp.mean(xh * u, axis=1, keepdims=True)
                d_out[:, part * HW + h * HEAD_DIM: part * HW + (h + 1) * HEAD_DIM] = dx.astype(d_out.dtype)
                dg = dg + jnp.sum(dy * xr, axis=0, keepdims=True)
            dg_ref[...] += dg

    ins = [(qk, _row(tr, 2 * HW)), (qn, _vec(HEAD_DIM)), (kn, _vec(HEAD_DIM))] + [(t, _row(tr, HEAD_DIM)) for t in tabs]
    ins += [(a, _row(tr, HW)) for a in dq_parts] + [(a, _row(tr, HW)) for a in dk_cur]
    ins += [(a, _shift_spec(tr, HW, d, nblk)) for a, d in zip(dk_prev, DILATIONS)]
    return _rows_call("qknorm_bwd", body, S, tr, ins, [(2 * HW, MXU_DTYPE)], accs=(HEAD_DIM, HEAD_DIM))


def _dv_sum(dv_cur, dv_prev, HW):
    S = dv_cur[0].shape[0]
    tr = BLOCK
    nblk = S // tr
    nd = len(DILATIONS)

    def body(*refs):
        i = pl.program_id(0)
        out = refs[2 * nd]
        acc = refs[0][...].astype(F32) + refs[1][...].astype(F32) + refs[2][...].astype(F32)
        for n, d in enumerate(DILATIONS):
            acc = acc + refs[nd + n][...].astype(F32) * (i + d < nblk).astype(F32)
        out[...] = acc.astype(out.dtype)

    ins = [(a, _row(tr, HW)) for a in dv_cur] + [(a, _shift_spec(tr, HW, d, nblk)) for a, d in zip(dv_prev, DILATIONS)]
    return _rows_call("dilated_dv_sum", body, S, tr, ins, [(HW, MXU_DTYPE)])[0]


def _dil_geometry(d, HW):
    H = HW // HEAD_DIM
    hb = min(H, max(1, 8 // d))
    tb, w = BLOCK * d, hb * HEAD_DIM
    cur = pl.BlockSpec((tb, w), lambda n, g: (n, g))
    prev = pl.BlockSpec((tb, w), lambda n, g: (jnp.maximum(n - 1, 0), g))
    units = [(hh, r) for hh in range(hb) for r in range(d)]
    return H // hb, hb, tb, cur, prev, units


def _dil_mask(n):
    qi = lax.broadcasted_iota(jnp.int32, (BLOCK, 2 * BLOCK), 0)
    ki = lax.broadcasted_iota(jnp.int32, (BLOCK, 2 * BLOCK), 1)
    return (ki >= qi) & (ki <= qi + BLOCK) & ((ki >= BLOCK) | (n > 0))


def _dil_stage(ref, buf, row0=0):
    for hh in range(buf.shape[0]):
        buf[hh, row0:row0 + ref.shape[0], :] = ref[:, hh * HEAD_DIM:(hh + 1) * HEAD_DIM].astype(F32)


def _dil_unstage(buf, ref):
    for hh in range(buf.shape[0]):
        ref[:, hh * HEAD_DIM:(hh + 1) * HEAD_DIM] = buf[hh].astype(ref.dtype)


def _dil_rows(d, r, size):
    return pl.ds(0, size) if d == 1 else pl.ds(r, size, stride=d)


def _dil_operands(d, tb, units, q_ref, kc_ref, kp_ref, vc_ref, vp_ref, qs, kf, vf):
    _dil_stage(q_ref, qs)
    _dil_stage(kp_ref, kf)
    _dil_stage(kc_ref, kf, tb)
    _dil_stage(vp_ref, vf)
    _dil_stage(vc_ref, vf, tb)
    qu = [qs[hh, _dil_rows(d, r, BLOCK), :].astype(MXU_DTYPE) for hh, r in units]
    ku = [kf[hh, _dil_rows(d, r, 2 * BLOCK), :].astype(MXU_DTYPE) for hh, r in units]
    vu = [vf[hh, _dil_rows(d, r, 2 * BLOCK), :].astype(MXU_DTYPE) for hh, r in units]
    return qu, ku, vu


def _dilated_fwd(q, k, mid, d, HW):
    S = q.shape[0]
    scale = HEAD_DIM ** -0.5
    ng, hb, tb, cur, prev, units = _dil_geometry(d, HW)

    def kern(q_ref, kc_ref, kp_ref, vc_ref, vp_ref, o_ref, l_ref, qs, kf, vf, os_, ls):
        mask = _dil_mask(pl.program_id(0))
        qu, ku, vu = _dil_operands(d, tb, units, q_ref, kc_ref, kp_ref, vc_ref, vp_ref, qs, kf, vf)
        sc = [jnp.where(mask, _dot(a, b, NT) * scale, NEG) for a, b in zip(qu, ku)]
        mx = [jnp.max(t, axis=1, keepdims=True) for t in sc]
        ex = [jnp.exp(t - m) for t, m in zip(sc, mx)]
        den = [jnp.sum(t, axis=1, keepdims=True) for t in ex]
        out = [_dot(t.astype(MXU_DTYPE), v) / dn for t, v, dn in zip(ex, vu, den)]
        for (hh, r), o, m, dn in zip(units, out, mx, den):
            os_[hh, _dil_rows(d, r, BLOCK), :] = o
            ls[hh, _dil_rows(d, r, BLOCK), :] = jnp.broadcast_to(m + jnp.log(dn), (BLOCK, HEAD_DIM))
        _dil_unstage(os_, o_ref)
        _dil_unstage(ls, l_ref)

    return pl.pallas_call(
        kern, name=f"dilated_fwd_d{d}", grid=(S // tb, ng),
        in_specs=[cur, cur, prev, cur, prev],
        out_specs=[cur, cur],
        out_shape=[jax.ShapeDtypeStruct((S, HW), F32)] * 2,
        scratch_shapes=[pltpu.VMEM((hb, tb, HEAD_DIM), F32)] + [pltpu.VMEM((hb, 2 * tb, HEAD_DIM), F32)] * 2
        + [pltpu.VMEM((hb, tb, HEAD_DIM), F32)] * 2,
        compiler_params=_params(("arbitrary", "arbitrary")),
    )(q, k, k, mid, mid)


def _dilated_combine(os_, lses, HW):
    S = os_[0].shape[0]
    tr = _fit(S, 256)

    def body(o0, o1, o2, l0, l1, l2, ya_ref, lse_ref):
        a, b, c = l0[...], l1[...], l2[...]
        mx = jnp.maximum(jnp.maximum(a, b), c)
        ea, eb, ec = jnp.exp(a - mx), jnp.exp(b - mx), jnp.exp(c - mx)
        tot = ea + eb + ec
        ya_ref[...] = (ea * o0[...] + eb * o1[...] + ec * o2[...]) / tot
        lse_ref[...] = mx + jnp.log(tot)

    ins = [(a, _row(tr, HW)) for a in list(os_) + list(lses)]
    return _rows_call("dilated_combine", body, S, tr, ins, [(HW, F32), (HW, F32)])


def _dilated_bwd(q, k, mid, dya, ya, lse, d, HW):
    S = q.shape[0]
    scale = HEAD_DIM ** -0.5
    ng, hb, tb, cur, prev, units = _dil_geometry(d, HW)

    def kern(q_ref, kc_ref, kp_ref, vc_ref, vp_ref, dy_ref, y_ref, l_ref, dq_ref, dkc_ref, dkp_ref, dvc_ref, dvp_ref,
             qs, kf, vf, dys, ys, ls, dqs, dkcs, dkps, dvcs, dvps):
        mask = _dil_mask(pl.program_id(0))
        qu, ku, vu = _dil_operands(d, tb, units, q_ref, kc_ref, kp_ref, vc_ref, vp_ref, qs, kf, vf)
        _dil_stage(dy_ref, dys)
        _dil_stage(y_ref, ys)
        _dil_stage(l_ref, ls)
        dy = [dys[hh, _dil_rows(d, r, BLOCK), :] for hh, r in units]
        lt = [ls[hh, _dil_rows(d, r, BLOCK), :][:, 0:1] for hh, r in units]
        delta = [jnp.sum(t * ys[hh, _dil_rows(d, r, BLOCK), :], axis=1, keepdims=True) for t, (hh, r) in zip(dy, units)]
        dyb = [t.astype(MXU_DTYPE) for t in dy]
        p = [jnp.where(mask, jnp.exp(_dot(a, b, NT) * scale - l), 0.0) for a, b, l in zip(qu, ku, lt)]
        ds = [(t * (_dot(g, v, NT) - dl) * scale).astype(MXU_DTYPE) for t, g, v, dl in zip(p, dyb, vu, delta)]
        dq = [_dot(t, b) for t, b in zip(ds, ku)]
        dk = [_dot(t, a, TN) for t, a in zip(ds, qu)]
        dv = [_dot(t.astype(MXU_DTYPE), g, TN) for t, g in zip(p, dyb)]
        for (hh, r), tq, tk, tv in zip(units, dq, dk, dv):
            at = _dil_rows(d, r, BLOCK)
            dqs[hh, at, :] = tq
            dkps[hh, at, :] = tk[0:BLOCK]
            dkcs[hh, at, :] = tk[BLOCK:2 * BLOCK]
            dvps[hh, at, :] = tv[0:BLOCK]
            dvcs[hh, at, :] = tv[BLOCK:2 * BLOCK]
        for buf, ref in ((dqs, dq_ref), (dkcs, dkc_ref), (dkps, dkp_ref), (dvcs, dvc_ref), (dvps, dvp_ref)):
            _dil_unstage(buf, ref)

    return pl.pallas_call(
        kern, name=f"dilated_bwd_d{d}", grid=(S // tb, ng),
        in_specs=[cur, cur, prev, cur, prev, cur, cur, cur],
        out_specs=[cur] * 5,
        out_shape=[jax.ShapeDtypeStruct((S, HW), F32)] * 3 + [jax.ShapeDtypeStruct((S, HW), MXU_DTYPE)] * 2,
        scratch_shapes=[pltpu.VMEM((hb, tb, HEAD_DIM), F32)] + [pltpu.VMEM((hb, 2 * tb, HEAD_DIM), F32)] * 2
        + [pltpu.VMEM((hb, tb, HEAD_DIM), F32)] * 8,
        compiler_params=_params(("arbitrary", "arbitrary")),
    )(q, k, k, mid, mid, dya, ya, lse)


def _softplus_parts(z2):
    lg = jnp.log(1.0 + jnp.exp2(-jnp.abs(z2))) * LOG2E
    return -jnp.maximum(z2, 0.0) - lg, jnp.minimum(z2, 0.0) - lg


def _sb_specs(S, H):
    q_spec = pl.BlockSpec((SB_BQ, HEAD_DIM), lambda h, i: (i, H + h))
    k_spec = pl.BlockSpec((S, HEAD_DIM), lambda h, i: (0, 2 * H + h))
    v_spec = pl.BlockSpec((S, HEAD_DIM), lambda h, i: (0, 3 * H + h))
    o_spec = pl.BlockSpec((SB_BQ, HEAD_DIM), lambda h, i: (i, h))
    return q_spec, k_spec, v_spec, o_spec


def _sb_tri(relation):
    tri_r = lax.broadcasted_iota(jnp.int32, (SB_BK, SB_BK), 0)
    tri_c = lax.broadcasted_iota(jnp.int32, (SB_BK, SB_BK), 1)
    return relation(tri_r, tri_c).astype(MXU_DTYPE)


def _sb_scratch():
    return [pltpu.VMEM((2, SB_BQ, SB_BQ), MXU_DTYPE), pltpu.VMEM((2, SB_BQ, SB_BQ), MXU_DTYPE), pltpu.SemaphoreType.DMA((2, 2))]


def _sb_fwd(mid, HW, ex=None):
    S = mid.shape[0]
    H = HW // HEAD_DIM
    BQ, CH = SB_BQ, SB_BK
    NC, nq = BQ // CH, S // BQ
    scale = HEAD_DIM ** -0.5
    q_spec, k_spec, v_spec, o_spec = _sb_specs(S, H)

    def kern(q_ref, k_ref, v_ref, o_ref, a_hbm, s_hbm, abuf, sbuf, sems):
        h, i = pl.program_id(0), pl.program_id(1)
        q = q_ref[...]
        upper = _sb_tri(lambda r, c: r > c)
        row = lax.broadcasted_iota(jnp.int32, (BQ, BQ), 0)
        causal = lax.broadcasted_iota(jnp.int32, (BQ, BQ), 1) < row

        def save(slot, j):
            return [pltpu.make_async_copy(buf.at[slot], hbm.at[h, i, j], sems.at[w, slot])
                    for w, (buf, hbm) in enumerate(((abuf, a_hbm), (sbuf, s_hbm)))]

        def block(n, run, acc, masked):
            j = i - n
            ks = pl.multiple_of(j * BQ, BQ)
            m, l = _softplus_parts(_dot(q, k_ref[pl.ds(ks, BQ), :], NT) * (scale * LOG2E))
            if masked:
                m = jnp.where(causal, m, 0.0)
            parts = []
            for c in reversed(range(NC)):
                mc = m[:, c * CH:(c + 1) * CH]
                parts.append(l[:, c * CH:(c + 1) * CH] + (_split_dot(mc, upper, CUMSUM_PASSES) + run))
                run = run + jnp.sum(mc, axis=1, keepdims=True)
            a = jnp.exp2(jnp.concatenate(parts[::-1], axis=1))
            sig = jnp.exp2(l)
            if masked:
                a = jnp.where(causal, a, 0.0)
                sig = jnp.where(causal, sig, 0.0)
            ab = a.astype(MXU_DTYPE)
            slot = n % 2
            if not masked:
                @pl.when(n >= 2)
                def _():
                    for cp in save(slot, j):
                        cp.wait()
            abuf[slot] = ab
            sbuf[slot] = sig.astype(MXU_DTYPE)
            for cp in save(slot, j):
                cp.start()
            return run, acc + _dot(ab, v_ref[pl.ds(ks, BQ), :])

        run, acc = block(0, jnp.zeros((BQ, 1), F32), jnp.zeros((BQ, HEAD_DIM), F32), True)
        run, acc = lax.fori_loop(1, i + 1, lambda n, carry: block(n, carry[0], carry[1], False), (run, acc))
        o_ref[...] = acc
        for cp in save(0, i):
            cp.wait()

        @pl.when(i >= 1)
        def _():
            for cp in save(1, i):
                cp.wait()

    tiles = jax.ShapeDtypeStruct((H, nq, nq, BQ, BQ), MXU_DTYPE)
    (o, a_t, s_t), rode = _host_call(kern, "stickbreak_fwd", (H, nq), [q_spec, k_spec, v_spec], [o_spec, ANY, ANY],
                                     [jax.ShapeDtypeStruct((S, HW), F32), tiles, tiles], (mid, mid, mid), ex, _sb_scratch())
    return o, a_t, s_t, rode


def _sb_bwd(mid, dyb, a_t, s_t, HW, ex=None):
    S = mid.shape[0]
    H = HW // HEAD_DIM
    BQ, CH = SB_BQ, SB_BK
    NC = BQ // CH
    scale = HEAD_DIM ** -0.5
    q_spec, k_spec, v_spec, o_spec = _sb_specs(S, H)
    full = pl.BlockSpec((S, HEAD_DIM), lambda h, i: (0, h))

    def kern(q_ref, k_ref, v_ref, do_ref, a_hbm, s_hbm, dq_ref, dk_ref, dv_ref, abuf, sbuf, sems):
        h, i = pl.program_id(0), pl.program_id(1)

        @pl.when(i == 0)
        def _():
            dk_ref[...] = jnp.zeros_like(dk_ref)
            dv_ref[...] = jnp.zeros_like(dv_ref)

        q = q_ref[...]
        do = do_ref[...].astype(MXU_DTYPE)
        excl = _sb_tri(lambda r, c: r < c)

        def fetch(slot, j):
            return [pltpu.make_async_copy(hbm.at[h, i, j], buf.at[slot], sems.at[w, slot])
                    for w, (buf, hbm) in enumerate(((abuf, a_hbm), (sbuf, s_hbm)))]

        for cp in fetch(0, 0):
            cp.start()

        def block(j, carry):
            prun, dq = carry
            slot = j % 2
            for cp in fetch(slot, j):
                cp.wait()

            @pl.when(j < i)
            def _():
                for cp in fetch(1 - slot, j + 1):
                    cp.start()

            ks = pl.multiple_of(j * BQ, BQ)
            k = k_ref[pl.ds(ks, BQ), :]
            ab = abuf[slot]
            p = ab.astype(F32) * _dot(do, v_ref[pl.ds(ks, BQ), :], NT)
            parts = []
            for c in range(NC):
                pc = p[:, c * CH:(c + 1) * CH]
                parts.append(_split_dot(pc, excl, CUMSUM_PASSES) + prun)
                prun = prun + jnp.sum(pc, axis=1, keepdims=True)
            before = jnp.concatenate(parts, axis=1)
            dzb = ((p - sbuf[slot].astype(F32) * (p + before)) * scale).astype(MXU_DTYPE)
            dk_ref[pl.ds(ks, BQ), :] += _dot(dzb, q, TN)
            dv_ref[pl.ds(ks, BQ), :] += _dot(ab, do, TN)
            return prun, dq + _dot(dzb, k)

        _, dq = lax.fori_loop(0, i + 1, block, (jnp.zeros((BQ, 1), F32), jnp.zeros((BQ, HEAD_DIM), F32)))
        dq_ref[...] = dq

    grads, rode = _host_call(kern, "stickbreak_bwd", (H, S // BQ), [q_spec, k_spec, v_spec, o_spec, ANY, ANY], [o_spec, full, full],
                             [jax.ShapeDtypeStruct((S, HW), F32)] * 3, (mid, mid, mid, dyb, a_t, s_t), ex, _sb_scratch())
    return grads, rode


ANY = pl.BlockSpec(memory_space=pl.ANY)


def _place():
    x, y, c = lax.axis_index("x"), lax.axis_index("y"), lax.axis_index("c")
    chips = [(1 - x, y), (x, 1 - y), (1 - x, 1 - y)]
    return x, y, c, chips


def _half(ref, shard, hc, rh):
    return ref.at[shard, pl.ds(pl.multiple_of(hc * rh, 8), rh), :]


def _cast_place(name, w, s_idx):
    R, C = w.shape
    tr = _fit(R, 256)

    def kern(s_ref, w_ref, o_ref):
        o_ref[...] = w_ref[...].astype(o_ref.dtype)

    return pl.pallas_call(
        kern, name=name,
        grid_spec=pltpu.PrefetchScalarGridSpec(
            num_scalar_prefetch=1, grid=(R // tr,),
            in_specs=[pl.BlockSpec((tr, C), lambda r, s_ref: (r, 0))],
            out_specs=pl.BlockSpec((None, tr, C), lambda r, s_ref: (s_ref[0], r, 0))),
        out_shape=jax.ShapeDtypeStruct((N_SHARD, R, C), MXU_DTYPE),
        compiler_params=_params(("arbitrary",)),
    )(s_idx, w)


class _Exchange:
    def __init__(self, inputs, out_shape, aliases, scratch, phases):
        self.inputs, self.out_shape, self.aliases, self.scratch, self.phases = inputs, out_shape, aliases, scratch, phases


def _run_alone(name, ex):
    ni, no = len(ex.inputs), len(ex.out_shape)

    def body(*refs):
        for phase in ex.phases:
            phase(refs[:ni], refs[ni:ni + no], refs[ni + no:])

    return pl.pallas_call(
        body, name=name, in_specs=[ANY] * ni, out_specs=[ANY] * no, out_shape=ex.out_shape,
        input_output_aliases=ex.aliases, scratch_shapes=ex.scratch,
    )(*ex.inputs)


def _host_call(kern, name, grid, in_specs, out_specs, out_shape, operands, ex, scratch=()):
    sem = ("arbitrary",) * len(grid)
    if ex is None:
        return pl.pallas_call(kern, name=name, grid=grid, in_specs=in_specs, out_specs=out_specs, out_shape=out_shape,
                              scratch_shapes=list(scratch), compiler_params=_params(sem))(*operands), []
    n_in, n_out, ri, ro, ns = len(in_specs), len(out_specs), len(ex.inputs), len(ex.out_shape), len(scratch)
    nsteps, nph = 1, len(ex.phases)
    for size in grid:
        nsteps *= size

    def body(*refs):
        r_in, r_out = refs[n_in:n_in + ri], refs[n_in + ri + n_out:n_in + ri + n_out + ro]
        host_scratch, ex_scratch = refs[n_in + ri + n_out + ro:][:ns], refs[n_in + ri + n_out + ro + ns:]
        step = 0
        for axis, size in enumerate(grid):
            step = step * size + pl.program_id(axis)
        for kph, phase in enumerate(ex.phases):
            pl.when(step == (kph * (nsteps - 1)) // (nph - 1))(functools.partial(phase, r_in, r_out, ex_scratch))
        kern(*refs[:n_in], *refs[n_in + ri:n_in + ri + n_out], *host_scratch)

    outs = pl.pallas_call(
        body, name=name, grid=grid, in_specs=list(in_specs) + [ANY] * ri, out_specs=list(out_specs) + [ANY] * ro,
        out_shape=list(out_shape) + list(ex.out_shape), scratch_shapes=list(scratch) + list(ex.scratch),
        input_output_aliases={n_in + a: n_out + b for a, b in ex.aliases.items()},
        compiler_params=_params(sem),
    )(*operands, *ex.inputs)
    return outs[:n_out], outs[n_out:]


def _gather_exchange(bufs):
    n = len(bufs)

    def between_chips(outs, sems, i, j, chip, c, shard):
        blk = _half(outs[i], shard, c, outs[i].shape[1] // 2)
        return pltpu.make_async_remote_copy(src_ref=blk, dst_ref=blk, send_sem=sems[0].at[i, j], recv_sem=sems[1].at[i, j],
                                            device_id=(chip[0], chip[1], c), device_id_type=MESH)

    def to_sibling(outs, sems, i, j, x, y, c, shard, hc):
        blk = _half(outs[i], shard, hc, outs[i].shape[1] // 2)
        return pltpu.make_async_remote_copy(src_ref=blk, dst_ref=blk, send_sem=sems[0].at[i, 3 + j], recv_sem=sems[1].at[i, 3 + j],
                                            device_id=(x, y, 1 - c), device_id_type=MESH)

    def send_mine(ins, outs, sems):
        x, y, c, chips = _place()
        for i in range(n):
            for j, chip in enumerate(chips):
                between_chips(outs, sems, i, j, chip, c, 2 * x + y).start()

    def pass_on(ins, outs, sems):
        x, y, c, chips = _place()
        for i in range(n):
            for j, chip in enumerate(chips):
                between_chips(outs, sems, i, j, chip, c, 2 * chip[0] + chip[1]).wait_recv()
                to_sibling(outs, sems, i, j, x, y, c, 2 * chip[0] + chip[1], c).start()

    def finish(ins, outs, sems):
        x, y, c, chips = _place()
        for i in range(n):
            for j, chip in enumerate(chips):
                to_sibling(outs, sems, i, j, x, y, c, 2 * chip[0] + chip[1], 1 - c).wait_recv()
        for i in range(n):
            for j, chip in enumerate(chips):
                between_chips(outs, sems, i, j, chip, c, 2 * x + y).wait_send()
                to_sibling(outs, sems, i, j, x, y, c, 2 * chip[0] + chip[1], c).wait_send()

    return _Exchange(list(bufs), [jax.ShapeDtypeStruct(b.shape, b.dtype) for b in bufs], {i: i for i in range(n)},
                     [pltpu.SemaphoreType.DMA((n, 6)), pltpu.SemaphoreType.DMA((n, 6))], [send_mine, pass_on, finish])


def _reduce_exchange(g16, g32):
    n = len(g16)

    def copies(ins, outs, sems):
        x, y, c, _ = _place()
        for i in range(n):
            rh = ins[i].shape[1] // 2
            for r in range(1, 8):
                px, py, pc = x ^ ((r >> 2) & 1), y ^ ((r >> 1) & 1), c ^ (r & 1)
                src = _half(ins[i] if r > 1 else ins[n + i], 2 * px + py, pc, rh)
                dst = outs[2 * i + 1].at[r - 2] if r > 1 else outs[2 * i]
                yield pltpu.make_async_remote_copy(src_ref=src, dst_ref=dst, send_sem=sems[0].at[i, r - 1], recv_sem=sems[1].at[i, r - 1],
                                                   device_id=(px, py, pc), device_id_type=MESH)

    def start(ins, outs, sems):
        for cp in copies(ins, outs, sems):
            cp.start()

    def finish(ins, outs, sems):
        for cp in copies(ins, outs, sems):
            cp.wait()

    out_shape = []
    for g in g16:
        rh, C = g.shape[1] // 2, g.shape[2]
        out_shape += [jax.ShapeDtypeStruct((rh, C), F32), jax.ShapeDtypeStruct((6, rh, C), g.dtype)]
    return _Exchange(list(g16) + list(g32), out_shape, {},
                     [pltpu.SemaphoreType.DMA((n, 7)), pltpu.SemaphoreType.DMA((n, 7))], [start, finish])


def _add_direct(name, g32, from_sibling, from_chips, s_idx, c_idx):
    _, R, C = g32.shape
    rh = R // 2
    tr = _fit(rh, 256)
    nrb = rh // tr

    def kern(s_ref, c_ref, g_ref, a_ref, b_ref, out_ref):
        acc = g_ref[...] + a_ref[...]
        for k in range(6):
            acc = acc + b_ref[k].astype(F32)
        out_ref[...] = acc

    return pl.pallas_call(
        kern, name=name,
        grid_spec=pltpu.PrefetchScalarGridSpec(
            num_scalar_prefetch=2, grid=(nrb,),
            in_specs=[pl.BlockSpec((None, tr, C), lambda r, s_ref, c_ref: (s_ref[0], c_ref[0] * nrb + r, 0)),
                      pl.BlockSpec((tr, C), lambda r, s_ref, c_ref: (r, 0)),
                      pl.BlockSpec((6, tr, C), lambda r, s_ref, c_ref: (0, r, 0))],
            out_specs=pl.BlockSpec((tr, C), lambda r, s_ref, c_ref: (r, 0))),
        out_shape=jax.ShapeDtypeStruct((rh, C), F32),
        compiler_params=_params(("arbitrary",)),
    )(s_idx, c_idx, g32, from_sibling, from_chips)


def _swap_reduced(halves):
    n = len(halves)

    def body(*refs):
        ins, outs = refs[:n], refs[n:2 * n]
        send, recv = refs[2 * n:]
        x, y, c, _ = _place()
        copies = []
        for i in range(n):
            cp = pltpu.make_async_remote_copy(src_ref=ins[i], dst_ref=outs[i], send_sem=send.at[i], recv_sem=recv.at[i],
                                              device_id=(x, y, 1 - c), device_id_type=MESH)
            cp.start()
            copies.append(cp)
        for cp in copies:
            cp.wait()

    return pl.pallas_call(
        body, name="grad_swap_reduced",
        in_specs=[ANY] * n, out_specs=[ANY] * n,
        out_shape=[jax.ShapeDtypeStruct(h.shape, F32) for h in halves],
        scratch_shapes=[pltpu.SemaphoreType.DMA((n,)), pltpu.SemaphoreType.DMA((n,))],
    )(*halves)


def _all_reduce_small(v):
    rows, W = v.shape
    ndev = 8

    def body(v_ref, out_ref, buf, send, recv):
        x, y, c, _ = _place()
        me = 4 * x + 2 * y + c
        buf[me] = v_ref[...]
        copies = []
        for r in range(1, ndev):
            fx, fy, fc = (r >> 2) & 1, (r >> 1) & 1, r & 1
            peer = (x ^ fx, y ^ fy, c ^ fc)
            cp = pltpu.make_async_remote_copy(src_ref=v_ref, dst_ref=buf.at[me], send_sem=send.at[r - 1], recv_sem=recv.at[r - 1],
                                              device_id=peer, device_id_type=MESH)
            cp.start()
            copies.append(cp)
        for cp in copies:
            cp.wait()
        acc = buf[0]
        for k in range(1, ndev):
            acc = acc + buf[k]
        out_ref[...] = acc

    return pl.pallas_call(
        body, name="small_all_reduce",
        in_specs=[pl.BlockSpec(memory_space=pltpu.VMEM)], out_specs=pl.BlockSpec(memory_space=pltpu.VMEM),
        out_shape=jax.ShapeDtypeStruct((rows, W), F32),
        scratch_shapes=[pltpu.VMEM((ndev, rows, W), F32), pltpu.SemaphoreType.DMA((ndev - 1,)), pltpu.SemaphoreType.DMA((ndev - 1,))],
    )(v)


def _adamw_update(gv, w_ref, m_ref, v_ref, d_ref, nm_ref, nv_ref):
    nm = ADAM_B1 * m_ref[...] + (1.0 - ADAM_B1) * gv
    nv = ADAM_B2 * v_ref[...] + (1.0 - ADAM_B2) * (gv * gv)
    m_hat = nm / (1.0 - ADAM_B1 ** ADAM_STEP)
    v_hat = nv / (1.0 - ADAM_B2 ** ADAM_STEP)
    d_ref[...] = -ADAM_LR * (m_hat / (jnp.sqrt(v_hat) + ADAM_EPS) + ADAM_WD * w_ref[...])
    nm_ref[...] = nm
    nv_ref[...] = nv


def _adamw(name, w, g, m, v):
    R, C = w.shape
    tr = _fit(R, 256)

    def body(w_ref, g_ref, m_ref, v_ref, d_ref, nm_ref, nv_ref):
        _adamw_update(g_ref[...], w_ref, m_ref, v_ref, d_ref, nm_ref, nv_ref)

    return _rows_call(name, body, R, tr, [(a, _row(tr, C)) for a in (w, g, m, v)], [(C, F32)] * 3)


def _adamw_halves(name, w, mine, theirs, m, v, c_idx):
    R, C = w.shape
    rh = R // 2
    tr = _fit(rh, 256)
    nrb = rh // tr

    def kern(c_ref, w_ref, a_ref, b_ref, m_ref, v_ref, g_ref, d_ref, nm_ref, nv_ref):
        gv = jnp.where(pl.program_id(0) // nrb == c_ref[0], a_ref[...], b_ref[...])
        g_ref[...] = gv
        _adamw_update(gv, w_ref, m_ref, v_ref, d_ref, nm_ref, nv_ref)

    full = pl.BlockSpec((tr, C), lambda r, c_ref: (r, 0))
    pick = lambda own: pl.BlockSpec((tr, C), lambda r, c_ref: (jnp.where((r // nrb == c_ref[0]) == own, r % nrb, 0), 0))
    return pl.pallas_call(
        kern, name=name,
        grid_spec=pltpu.PrefetchScalarGridSpec(
            num_scalar_prefetch=1, grid=(R // tr,),
            in_specs=[full, pick(True), pick(False), full, full], out_specs=[full] * 4),
        out_shape=[jax.ShapeDtypeStruct((R, C), F32)] * 4,
        compiler_params=_params(("arbitrary",)),
    )(c_idx, w, mine, theirs, m, v)


def _sigmoid(z):
    return 1.0 / (1.0 + jnp.exp(-z))


def kernel(x, p, g_mix, w_in, qn_gain, kn_gain, w_branch_a, w_branch_b, w_out, g_mlp, w_up, w_down, g_ple, w_ple_gate, w_ple_proj, loss_target, m_g_mix, m_w_in, m_qn_gain, m_kn_gain, m_w_branch_a, m_w_branch_b, m_w_out, m_g_mlp, m_w_up, m_w_down, m_g_ple, m_w_ple_gate, m_w_ple_proj, v_g_mix, v_w_in, v_qn_gain, v_kn_gain, v_w_branch_a, v_w_branch_b, v_w_out, v_g_mlp, v_w_up, v_w_down, v_g_ple, v_w_ple_gate, v_w_ple_proj):
    S, D = x.shape[1], x.shape[2]
    HW = w_branch_a.shape[1]
    x2d, tgt, p2d = x.reshape(S, D), loss_target.reshape(S, D), p.reshape(S, p.shape[-1])
    big = {"w_in": w_in, "w_branch_a": w_branch_a, "w_branch_b": w_branch_b, "w_out": w_out, "w_up": w_up,
           "w_down": w_down, "w_ple_gate": w_ple_gate, "w_ple_proj": w_ple_proj}
    moments = {"w_in": (m_w_in, v_w_in), "w_branch_a": (m_w_branch_a, v_w_branch_a), "w_branch_b": (m_w_branch_b, v_w_branch_b),
               "w_out": (m_w_out, v_w_out), "w_up": (m_w_up, v_w_up), "w_down": (m_w_down, v_w_down),
               "w_ple_gate": (m_w_ple_gate, v_w_ple_gate), "w_ple_proj": (m_w_ple_proj, v_w_ple_proj)}
    names = list(big)
    col_sharded = {"w_in", "w_branch_a", "w_branch_b", "w_up", "w_ple_proj"}
    shard2d = {k: w.reshape(w.shape[1], w.shape[2]) for k, w in big.items()}

    c_idx = lax.axis_index("c").astype(jnp.int32).reshape(1)
    s_idx = (2 * lax.axis_index("x") + lax.axis_index("y")).astype(jnp.int32).reshape(1)
    placed = {k: _cast_place(f"cast_{k}", shard2d[k], s_idx) for k in names}
    late = [k for k in names if k != "w_in"]
    W = {"w_in": _run_alone("gather_w_in", _gather_exchange([placed["w_in"]]))[0]}
    cin = W["w_in"].shape[2]
    bn_in = _fit(cin, 512)
    while (2 * HW) % bn_in:
        bn_in -= 128

    h = _rmsnorm_fwd("rmsnorm_mix", x2d, g_mix)
    (qk,) = _matmul("proj_qk", h, W["w_in"], mode="nn", out_dtypes=[F32], b_cshard=True, b_off=0, n_out=2 * HW, bn=bn_in, bk=D)
    (mid,) = _matmul("proj_mid", h, W["w_in"], mode="nn", out_dtypes=[MXU_DTYPE], b_cshard=True, b_off=2 * HW // bn_in,
                     n_out=4 * HW, bn=bn_in, bk=D)
    (sg,) = _matmul("proj_gates", h, W["w_in"], mode="nn", out_dtypes=[MXU_DTYPE], b_cshard=True, b_off=6 * HW // bn_in,
                    n_out=2 * D, bn=bn_in, bk=D, epilogue=lambda acc: (_sigmoid(acc),))
    tabs = _rope_tables(S)
    qa, ka = _qknorm_fwd(qk, qn_gain, kn_gain, tabs, HW)
    dil = [_dilated_fwd(qa, ka, mid, d, HW) for d in DILATIONS]
    ya, lse = _dilated_combine([o for o, _ in dil], [l for _, l in dil], HW)
    yb, sb_a, sb_sig, gathered = _sb_fwd(mid, HW, _gather_exchange([placed[k] for k in late]))
    W.update({k: (g if k in col_sharded else g.reshape(-1, g.shape[2])) for k, g in zip(late, gathered)})

    gate_blocks = D // _fit(D, 1024)
    (ua,) = _matmul("branch_a", ya, W["w_branch_a"], mode="nn", out_dtypes=[MXU_DTYPE], b_cshard=True, bn=_fit(W["w_branch_a"].shape[2], 1024))
    bn_b = _fit(W["w_branch_b"].shape[2], 1024)
    ub, merged = _matmul("branch_b_merge", yb, W["w_branch_b"], mode="nn", out_dtypes=[MXU_DTYPE, MXU_DTYPE], b_cshard=True, bn=bn_b,
                         extras=[(sg, 0), (sg, D // bn_b), (ua, 0)],
                         epilogue=lambda acc, sga, sgb, uav: (acc, sga.astype(F32) * uav.astype(F32) + sgb.astype(F32) * acc))
    (x1,) = _matmul("out_proj", merged, W["w_out"], mode="nn", out_dtypes=[F32], extras=[(x2d, 0)], epilogue=lambda acc, xv: (xv + acc,))
    hm = _rmsnorm_fwd("rmsnorm_mlp", x1, g_mlp)

    def up_epilogue(acc):
        r = jnp.maximum(acc, 0.0)
        return r * r, r

    act, rup = _matmul("mlp_up", hm, W["w_up"], mode="nn", out_dtypes=[MXU_DTYPE, MXU_DTYPE], b_cshard=True,
                       bn=_fit(W["w_up"].shape[2], 1024), bk=D, epilogue=up_epilogue)
    (x2,) = _matmul("mlp_down", act, W["w_down"], mode="nn", out_dtypes=[F32], extras=[(x1, 0)], epilogue=lambda acc, xv: (xv + acc,))
    hp = _rmsnorm_fwd("rmsnorm_ple", x2, g_ple)
    (pp,) = _matmul("ple_proj", p2d, W["w_ple_proj"], mode="nn", out_dtypes=[F32], b_cshard=True, bn=_fit(W["w_ple_proj"].shape[2], 1024))

    def ple_epilogue(acc, ppv, x2v, tv):
        s = _sigmoid(acc)
        dx3 = ((x2v + ppv * s) - tv) / D
        return dx3, dx3 * s, dx3 * ppv * (s * (1.0 - s))

    dx3, d_pp, d_gate = _matmul("ple_gate_loss", hp, W["w_ple_gate"], mode="nn", out_dtypes=[F32, MXU_DTYPE, MXU_DTYPE],
                                bm=512, extras=[(pp, 0), (x2, 0), (tgt, 0)], epilogue=ple_epilogue)

    G, G16 = {}, {}
    G["w_ple_proj"], G16["w_ple_proj"] = _matmul("grad_w_ple_proj", p2d, d_pp, mode="tn", out_dtypes=[F32, MXU_DTYPE], out_cshard=True,
                                 bn=_fit(d_pp.shape[1] // N_SHARD, 1024))
    G["w_ple_gate"], G16["w_ple_gate"] = _matmul("grad_w_ple_gate", hp, d_gate, mode="tn", out_dtypes=[F32, MXU_DTYPE])
    (d_hp,) = _matmul("ple_gate_bwd", d_gate, W["w_ple_gate"], mode="nt", out_dtypes=[F32])
    dx2, dx2_low, g_g_ple, loss_part = _rmsnorm_bwd("rmsnorm_ple_bwd", d_hp, x2, g_ple, dx3, True, True)
    G["w_down"], G16["w_down"] = _matmul("grad_w_down", act, dx2_low, mode="tn", out_dtypes=[F32, MXU_DTYPE])
    (d_up,) = _matmul("mlp_down_bwd", dx2_low, W["w_down"], mode="nt", out_dtypes=[MXU_DTYPE], extras=[(rup, 0)],
                      epilogue=lambda acc, r: (acc * (2.0 * r.astype(F32)),))
    G["w_up"], G16["w_up"] = _matmul("grad_w_up", hm, d_up, mode="tn", out_dtypes=[F32, MXU_DTYPE], out_cshard=True, bn=_fit(d_up.shape[1] // N_SHARD, 1024))
    (d_hm,) = _matmul("mlp_up_bwd", d_up, W["w_up"], mode="nt", out_dtypes=[F32], b_cshard=True, bk=_fit(W["w_up"].shape[2], 2048))
    dx1, dx1_low, g_g_mlp = _rmsnorm_bwd("rmsnorm_mlp_bwd", d_hm, x1, g_mlp, dx2, False, True)
    G["w_out"], G16["w_out"] = _matmul("grad_w_out", merged, dx1_low, mode="tn", out_dtypes=[F32, MXU_DTYPE])

    def merge_bwd(acc, sga, sgb, uav, ubv):
        sga, sgb, uav, ubv = (t.astype(F32) for t in (sga, sgb, uav, ubv))
        return acc * sga, acc * sgb, acc * uav * (sga * (1.0 - sga)), acc * ubv * (sgb * (1.0 - sgb))

    bn_m = _fit(D, 1024)
    d_ua, d_ub, d_ga, d_gb = _matmul("out_proj_bwd", dx1_low, W["w_out"], mode="nt", out_dtypes=[MXU_DTYPE] * 4, bm=512, bn=bn_m,
                                     extras=[(sg, 0), (sg, D // bn_m), (ua, 0), (ub, 0)], epilogue=merge_bwd)
    bn_br = _fit(D // N_SHARD, 1024)
    G["w_branch_a"], G16["w_branch_a"] = _matmul("grad_w_branch_a", ya, d_ua, mode="tn", out_dtypes=[F32, MXU_DTYPE], out_cshard=True, bn=bn_br)
    G["w_branch_b"], G16["w_branch_b"] = _matmul("grad_w_branch_b", yb, d_ub, mode="tn", out_dtypes=[F32, MXU_DTYPE], out_cshard=True, bn=bn_br)
    (d_ya,) = _matmul("branch_a_bwd", d_ua, W["w_branch_a"], mode="nt", out_dtypes=[F32], b_cshard=True, bk=bn_br)
    (d_yb,) = _matmul("branch_b_bwd", d_ub, W["w_branch_b"], mode="nt", out_dtypes=[F32], b_cshard=True, bk=bn_br)

    as_shards = lambda k, g: g if k in col_sharded else g.reshape(N_SHARD, -1, g.shape[1])
    (dqb, dkb, dvb), partials = _sb_bwd(mid, d_yb, sb_a, sb_sig, HW, _reduce_exchange([as_shards(k, G16[k]) for k in late],
                                                                               [as_shards(k, G[k]) for k in late]))
    dil_b = [_dilated_bwd(qa, ka, mid, d_ya, ya, lse, d, HW) for d in DILATIONS]
    d_qk, g_qn, g_kn = _qknorm_bwd(qk, qn_gain, kn_gain, tabs, [t[0] for t in dil_b], [t[1] for t in dil_b],
                                   [t[2] for t in dil_b], HW)
    dva = _dv_sum([t[3] for t in dil_b], [t[4] for t in dil_b], HW)
    d_proj = jnp.concatenate([d_qk, dva, dqb.astype(MXU_DTYPE), dkb.astype(MXU_DTYPE), dvb.astype(MXU_DTYPE), d_ga, d_gb], axis=1)
    G["w_in"], G16["w_in"] = _matmul("grad_w_in", h, d_proj, mode="tn", out_dtypes=[F32, MXU_DTYPE], out_cshard=True, bn=_fit(cin, 1280))
    (d_h,), partials_in = _matmul("proj_bwd", d_proj, W["w_in"], mode="nt", out_dtypes=[F32], b_cshard=True, bk=_fit(cin, 1280),
                                  ride=_reduce_exchange([G16["w_in"]], [G["w_in"]]))
    grad_x, g_g_mix = _rmsnorm_bwd("rmsnorm_mix_bwd", d_h, x2d, g_mix, dx1, False, False)

    mine = {k: _add_direct(f"grad_add_{k}", as_shards(k, G[k]), partials[2 * n], partials[2 * n + 1], s_idx, c_idx)
            for n, k in enumerate(late)}
    mine["w_in"] = _add_direct("grad_add_w_in", G["w_in"], partials_in[0], partials_in[1], s_idx, c_idx)
    halves = [mine[k] for k in names]
    others = _swap_reduced(halves)

    pack_w = -(-(3 * D + 3 * 128) // (8 * 128)) * 128

    def pack(v_mix, v_mlp, v_ple, v_qn, v_kn, extra):
        flat = jnp.concatenate([v_mix.reshape(-1), v_mlp.reshape(-1), v_ple.reshape(-1), v_qn.reshape(-1), v_kn.reshape(-1), extra.reshape(-1)])
        return jnp.pad(flat, (0, 8 * pack_w - flat.shape[0])).reshape(8, pack_w)

    def unpack(blk):
        flat = blk.reshape(-1)
        return (flat[:D].reshape(1, D), flat[D:2 * D].reshape(1, D), flat[2 * D:3 * D].reshape(1, D),
                flat[3 * D:3 * D + 128].reshape(1, 128), flat[3 * D + 128:3 * D + 256].reshape(1, 128), flat[3 * D + 256])

    small = _all_reduce_small(pack(g_g_mix, g_g_mlp, g_g_ple, g_qn, g_kn, loss_part))
    sw = pack(g_mix, g_mlp, g_ple, qn_gain, kn_gain, jnp.zeros((128,), F32))
    sm = pack(m_g_mix, m_g_mlp, m_g_ple, m_qn_gain, m_kn_gain, jnp.zeros((128,), F32))
    sv = pack(v_g_mix, v_g_mlp, v_g_ple, v_qn_gain, v_kn_gain, jnp.ones((128,), F32))
    s_delta, s_nm, s_nv = _adamw("adamw_small", sw, small, sm, sv)
    sg_mix, sg_mlp, sg_ple, sg_qn, sg_kn, loss = unpack(small)
    small_out = {}
    for tag, blk in (("delta", s_delta), ("new_m", s_nm), ("new_v", s_nv)):
        u = unpack(blk)
        small_out[tag] = dict(g_mix=u[0], g_mlp=u[1], g_ple=u[2], qn_gain=u[3], kn_gain=u[4])
    small_grad = dict(g_mix=sg_mix, g_mlp=sg_mlp, g_ple=sg_ple, qn_gain=sg_qn, kn_gain=sg_kn)

    big_out = {"grad": {}, "delta": {}, "new_m": {}, "new_v": {}}
    for k, mine, theirs in zip(names, halves, others):
        shape = big[k].shape
        m2, v2 = (t.reshape(shape[1], shape[2]) for t in moments[k])
        res = _adamw_halves(f"adamw_{k}", shard2d[k], mine, theirs, m2, v2, c_idx)
        for tag, t in zip(("grad", "delta", "new_m", "new_v"), res):
            big_out[tag][k] = t.reshape(shape)

    order = ["g_mix", "w_in", "qn_gain", "kn_gain", "w_branch_a", "w_branch_b", "w_out", "g_mlp", "w_up", "w_down", "g_ple",
             "w_ple_gate", "w_ple_proj"]
    outs = [loss, grad_x.reshape(x.shape)]
    outs += [small_grad[k] if k in small_grad else big_out["grad"][k] for k in order]
    for tag in ("delta", "new_m", "new_v"):
        outs += [small_out[tag][k] if k in small_grad else big_out[tag][k] for k in order]
    return tuple(outs)
```

```python
import functools

import jax
import jax.numpy as jnp
from jax import lax
from jax.experimental import pallas as pl
from jax.experimental.pallas import tpu as pltpu

F32 = jnp.float32
MXU_DTYPE = jnp.bfloat16
HEAD_DIM = 128
ROT_DIM = HEAD_DIM // 4
ROPE_THETA = 500000.0
EPS = 1e-6
DILATIONS = (1, 4, 16)
BLOCK = 128
N_SHARD = 4
ADAM_LR, ADAM_B1, ADAM_B2, ADAM_EPS, ADAM_WD, ADAM_STEP = 0.001, 0.9, 0.999, 1e-08, 0.01, 10
MXU_WIDTH = 256
V7X_VMEM_BYTES = 64 * 1024 * 1024
VMEM_LIMIT = V7X_VMEM_BYTES - 8 * 1024 * 1024
MESH = pl.DeviceIdType.MESH
NEG = -1e30
SB_BQ, SB_BK = 1024, 256
LOG2E = 1.4426950408889634
CUMSUM_PASSES = 2


def _fit(dim, pref):
    if dim <= pref:
        return dim
    b = (pref // 128) * 128
    while dim % b:
        b -= 128
    return b


def _params(sem=None):
    return pltpu.CompilerParams(dimension_semantics=sem, vmem_limit_bytes=VMEM_LIMIT)


def _dot(a, b, dims=(((1,), (0,)), ((), ()))):
    return lax.dot_general(a, b, dims, preferred_element_type=F32)


NT = (((1,), (1,)), ((), ()))
TN = (((0,), (0,)), ((), ()))


def _split_dot(x, u, passes):
    out = None
    r = x
    for p in range(passes):
        hi = r.astype(MXU_DTYPE)
        part = _dot(hi, u)
        out = part if out is None else out + part
        if p + 1 < passes:
            r = r - hi.astype(F32)
    return out


def _matmul(name, a, b, *, mode, out_dtypes, bm=1024, bn=1024, bk=2048, b_cshard=False, b_off=0, n_out=None,
            extras=(), epilogue=None, out_cshard=False, ride=None):
    if mode == "tn":
        K, M = a.shape
        N = b.shape[1]
    else:
        M, K = a.shape
        if mode == "nn":
            N = n_out if n_out is not None else (N_SHARD * b.shape[2] if b_cshard else b.shape[1])
        else:
            N = b.shape[1] if b_cshard else b.shape[0]
    bm, bn, bk = _fit(M, bm), _fit(N, bn), _fit(K, bk)
    nk = K // bk
    grid = (M // bm, N // bn, nk)

    if mode == "tn":
        a_spec = pl.BlockSpec((bk, bm), lambda i, j, k: (k, i))
        b_spec = pl.BlockSpec((bk, bn), lambda i, j, k: (k, j))
        dims = TN
    elif mode == "nn":
        a_spec = pl.BlockSpec((bm, bk), lambda i, j, k: (i, k))
        if b_cshard:
            cb = b.shape[2] // bn
            b_spec = pl.BlockSpec((None, bk, bn), lambda i, j, k: ((j + b_off) // cb, k, (j + b_off) % cb))
        else:
            b_spec = pl.BlockSpec((bk, bn), lambda i, j, k: (k, j + b_off))
        dims = (((1,), (0,)), ((), ()))
    else:
        a_spec = pl.BlockSpec((bm, bk), lambda i, j, k: (i, k))
        if b_cshard:
            cb = b.shape[2] // bk
            b_spec = pl.BlockSpec((None, bn, bk), lambda i, j, k: (k // cb, j, k % cb))
        else:
            b_spec = pl.BlockSpec((bn, bk), lambda i, j, k: (j, k))
        dims = NT

    ex_arrays = [e[0] for e in extras]
    ex_specs = [pl.BlockSpec((bm, bn), functools.partial(lambda i, j, k, off: (i, j + off), off=e[1])) for e in extras]
    if out_cshard:
        cbo = (N // N_SHARD) // bn
        out_shape = [jax.ShapeDtypeStruct((N_SHARD, M, N // N_SHARD), dt) for dt in out_dtypes]
        out_specs = [pl.BlockSpec((None, bm, bn), lambda i, j, k: (j // cbo, i, j % cbo)) for _ in out_dtypes]
    else:
        out_shape = [jax.ShapeDtypeStruct((M, N), dt) for dt in out_dtypes]
        out_specs = [pl.BlockSpec((bm, bn), lambda i, j, k: (i, j)) for _ in out_dtypes]
    ne, no = len(extras), len(out_dtypes)
    cw = MXU_WIDTH if bn % MXU_WIDTH == 0 else bn

    def kern(*refs):
        a_ref, b_ref = refs[0], refs[1]
        ex_refs = refs[2:2 + ne]
        o_refs = refs[2 + ne:2 + ne + no]
        av = a_ref[...].astype(MXU_DTYPE)

        def finish(acc):
            vals = (acc,) * no if epilogue is None else epilogue(acc, *[r[...] for r in ex_refs])
            for r, v in zip(o_refs, vals):
                r[...] = v.astype(r.dtype)

        if nk == 1:
            finish(_dot(av, b_ref[...].astype(MXU_DTYPE), dims))
        else:
            acc_ref = refs[2 + ne + no]
            k = pl.program_id(2)

            @pl.when(k == 0)
            def _():
                acc_ref[...] = jnp.zeros_like(acc_ref)

            for c0 in range(0, bn, cw):
                bv = b_ref[c0:c0 + cw, :] if mode == "nt" else b_ref[:, c0:c0 + cw]
                acc_ref[:, c0:c0 + cw] += _dot(av, bv.astype(MXU_DTYPE), dims)

            @pl.when(k == nk - 1)
            def _():
                finish(acc_ref[...])

    outs, rode = _host_call(kern, name, grid, [a_spec, b_spec] + ex_specs, out_specs, out_shape, (a, b, *ex_arrays), ride,
                            [pltpu.VMEM((bm, bn), F32)] if nk > 1 else [])
    return outs if ride is None else (outs, rode)


def _row(tr, w, coff=0):
    return pl.BlockSpec((tr, w), lambda i: (i, coff))


def _vec(w):
    return pl.BlockSpec((1, w), lambda i: (0, 0))


def _rows_call(name, body, n_rows, tr, ins, outs, accs=()):
    n_in, n_out = len(ins), len(outs)

    def kern(*refs):
        acc_refs = refs[n_in + n_out:]
        if acc_refs:
            @pl.when(pl.program_id(0) == 0)
            def _():
                for r in acc_refs:
                    r[...] = jnp.zeros_like(r)
        body(*refs)

    out_shape = [jax.ShapeDtypeStruct((n_rows, w), dt) for w, dt in outs] + [jax.ShapeDtypeStruct((1, w), F32) for w in accs]
    out_specs = [_row(tr, w) for w, _ in outs] + [_vec(w) for w in accs]
    return pl.pallas_call(
        kern, name=name, grid=(n_rows // tr,),
        in_specs=[s for _, s in ins], out_specs=out_specs, out_shape=out_shape,
        compiler_params=_params(("arbitrary",)),
    )(*[a for a, _ in ins])


def _rmsnorm_fwd(name, x, g, ex=None):
    S, D = x.shape
    tr = _fit(S, 256)

    def body(x_ref, g_ref, h_ref):
        xv = x_ref[...]
        r = lax.rsqrt(jnp.mean(xv * xv, axis=1, keepdims=True) + EPS)
        h_ref[...] = ((xv * r) * g_ref[...]).astype(h_ref.dtype)

    (h,), rode = _host_call(body, name, (S // tr,), [_row(tr, D), _vec(D)], [_row(tr, D)], [jax.ShapeDtypeStruct((S, D), MXU_DTYPE)],
                            (x, g), ex)
    return h if ex is None else (h, rode)


def _rmsnorm_bwd(name, dh, x, g, resid, with_loss, with_copy):
    S, D = x.shape
    tr = _fit(S, 256)

    def body(dh_ref, x_ref, g_ref, res_ref, dx_ref, *rest):
        copy_ref = rest[:1] if with_copy else ()
        dg_ref, loss_ref = rest[len(copy_ref)], rest[len(copy_ref) + 1:]
        xv = x_ref[...]
        r = lax.rsqrt(jnp.mean(xv * xv, axis=1, keepdims=True) + EPS)
        dhv = dh_ref[...]
        u = dhv * g_ref[...]
        xr = xv * r
        dx = r * u - xr * (r * r) * jnp.mean(xv * u, axis=1, keepdims=True)
        resv = res_ref[...]
        dx_ref[...] = resv + dx
        for r in copy_ref:
            r[...] = (resv + dx).astype(r.dtype)
        dg_ref[...] += jnp.sum(dhv * xr, axis=0, keepdims=True)
        if with_loss:
            loss_ref[0][...] += (0.5 * D) * jnp.sum(resv * resv)

    outs = _rows_call(name, body, S, tr, [(dh, _row(tr, D)), (x, _row(tr, D)), (g, _vec(D)), (resid, _row(tr, D))],
                      [(D, F32)] + ([(D, MXU_DTYPE)] if with_copy else []), accs=(D, 128) if with_loss else (D,))
    return outs


def _rope_tables(S):
    half = ROT_DIM // 2
    pos = jnp.arange(S, dtype=F32)
    inv = ROPE_THETA ** (-jnp.arange(0, ROT_DIM, 2, dtype=F32) / ROT_DIM)
    ang = pos[:, None] * inv[None, :]
    cos, sin = jnp.cos(ang), jnp.sin(ang)
    pad = HEAD_DIM - ROT_DIM
    ctab = jnp.concatenate([cos, cos, jnp.ones((S, pad), F32)], axis=1)
    atab = jnp.concatenate([-sin, jnp.zeros((S, pad + half), F32)], axis=1)
    btab = jnp.concatenate([jnp.zeros((S, half), F32), sin, jnp.zeros((S, pad), F32)], axis=1)
    return ctab, atab, btab


def _qknorm_fwd(qk, qn, kn, tabs, HW):
    S = qk.shape[0]
    tr = _fit(S, 256)
    half = ROT_DIM // 2

    def body(qk_ref, qn_ref, kn_ref, c_ref, a_ref, b_ref, q_out, k_out):
        ct, at, bt = c_ref[...], a_ref[...], b_ref[...]
        for part, (g_ref, o_ref) in enumerate(((qn_ref, q_out), (kn_ref, k_out))):
            gv = g_ref[...]
            for h in range(HW // HEAD_DIM):
                xh = qk_ref[:, part * HW + h * HEAD_DIM: part * HW + (h + 1) * HEAD_DIM]
                r = lax.rsqrt(jnp.mean(xh * xh, axis=1, keepdims=True) + EPS)
                y = (xh * r) * gv
                o = y * ct + pltpu.roll(y, HEAD_DIM - half, 1) * at + pltpu.roll(y, half, 1) * bt
                o_ref[:, h * HEAD_DIM:(h + 1) * HEAD_DIM] = o.astype(o_ref.dtype)

    ins = [(qk, _row(tr, 2 * HW)), (qn, _vec(HEAD_DIM)), (kn, _vec(HEAD_DIM))] + [(t, _row(tr, HEAD_DIM)) for t in tabs]
    return _rows_call("qknorm_fwd", body, S, tr, ins, [(HW, MXU_DTYPE), (HW, MXU_DTYPE)])


def _shift_spec(tr, w, shift, nblk):
    return pl.BlockSpec((tr, w), lambda i: (jnp.minimum(i + shift, nblk - 1), 0))


def _qknorm_bwd(qk, qn, kn, tabs, dq_parts, dk_cur, dk_prev, HW):
    S = qk.shape[0]
    tr = BLOCK
    nblk = S // tr
    half = ROT_DIM // 2
    nd = len(DILATIONS)

    def body(*refs):
        qk_ref, qn_ref, kn_ref, c_ref, a_ref, b_ref = refs[:6]
        dq_refs = refs[6:6 + nd]
        dkc_refs = refs[6 + nd:6 + 2 * nd]
        dkp_refs = refs[6 + 2 * nd:6 + 3 * nd]
        d_out, dqn_ref, dkn_ref = refs[6 + 3 * nd:]
        i = pl.program_id(0)
        ct, at, bt = c_ref[...], a_ref[...], b_ref[...]
        live = [(i + d < nblk).astype(F32) for d in DILATIONS]
        for part, (g_ref, dg_ref) in enumerate(((qn_ref, dqn_ref), (kn_ref, dkn_ref))):
            gv = g_ref[...]
            dg = jnp.zeros((1, HEAD_DIM), F32)
            for h in range(HW // HEAD_DIM):
                hs = slice(h * HEAD_DIM, (h + 1) * HEAD_DIM)
                if part == 0:
                    do = dq_refs[0][:, hs] + dq_refs[1][:, hs] + dq_refs[2][:, hs]
                else:
                    do = dkc_refs[0][:, hs] + dkc_refs[1][:, hs] + dkc_refs[2][:, hs]
                    for n in range(nd):
                        do = do + dkp_refs[n][:, hs] * live[n]
                dy = do * ct + pltpu.roll(do * at, half, 1) + pltpu.roll(do * bt, HEAD_DIM - half, 1)
                xh = qk_ref[:, part * HW + h * HEAD_DIM: part * HW + (h + 1) * HEAD_DIM]
                r = lax.rsqrt(jnp.mean(xh * xh, axis=1, keepdims=True) + EPS)
                xr = xh * r
                u = dy * gv
                dx = r * u - xr * (r * r) * jnp.mean(xh * u, axis=1, keepdims=True)
                d_out[:, part * HW + h * HEAD_DIM: part * HW + (h + 1) * HEAD_DIM] = dx.astype(d_out.dtype)
                dg = dg + jnp.sum(dy * xr, axis=0, keepdims=True)
            dg_ref[...] += dg

    ins = [(qk, _row(tr, 2 * HW)), (qn, _vec(HEAD_DIM)), (kn, _vec(HEAD_DIM))] + [(t, _row(tr, HEAD_DIM)) for t in tabs]
    ins += [(a, _row(tr, HW)) for a in dq_parts] + [(a, _row(tr, HW)) for a in dk_cur]
    ins += [(a, _shift_spec(tr, HW, d, nblk)) for a, d in zip(dk_prev, DILATIONS)]
    return _rows_call("qknorm_bwd", body, S, tr, ins, [(2 * HW, MXU_DTYPE)], accs=(HEAD_DIM, HEAD_DIM))


def _dv_sum(dv_cur, dv_prev, HW):
    S = dv_cur[0].shape[0]
    tr = BLOCK
    nblk = S // tr
    nd = len(DILATIONS)

    def body(*refs):
        i = pl.program_id(0)
        out = refs[2 * nd]
        acc = refs[0][...].astype(F32) + refs[1][...].astype(F32) + refs[2][...].astype(F32)
        for n, d in enumerate(DILATIONS):
            acc = acc + refs[nd + n][...].astype(F32) * (i + d < nblk).astype(F32)
        out[...] = acc.astype(out.dtype)

    ins = [(a, _row(tr, HW)) for a in dv_cur] + [(a, _shift_spec(tr, HW, d, nblk)) for a, d in zip(dv_prev, DILATIONS)]
    return _rows_call("dilated_dv_sum", body, S, tr, ins, [(HW, MXU_DTYPE)])[0]


def _dil_geometry(d, HW):
    H = HW // HEAD_DIM
    hb = min(H, max(1, 8 // d))
    tb, w = BLOCK * d, hb * HEAD_DIM
    cur = pl.BlockSpec((tb, w), lambda n, g: (n, g))
    prev = pl.BlockSpec((tb, w), lambda n, g: (jnp.maximum(n - 1, 0), g))
    units = [(hh, r) for hh in range(hb) for r in range(d)]
    return H // hb, hb, tb, cur, prev, units


def _dil_mask(n):
    qi = lax.broadcasted_iota(jnp.int32, (BLOCK, 2 * BLOCK), 0)
    ki = lax.broadcasted_iota(jnp.int32, (BLOCK, 2 * BLOCK), 1)
    return (ki >= qi) & (ki <= qi + BLOCK) & ((ki >= BLOCK) | (n > 0))


def _dil_stage(ref, buf, row0=0):
    for hh in range(buf.shape[0]):
        buf[hh, row0:row0 + ref.shape[0], :] = ref[:, hh * HEAD_DIM:(hh + 1) * HEAD_DIM].astype(F32)


def _dil_unstage(buf, ref):
    for hh in range(buf.shape[0]):
        ref[:, hh * HEAD_DIM:(hh + 1) * HEAD_DIM] = buf[hh].astype(ref.dtype)


def _dil_rows(d, r, size):
    return pl.ds(0, size) if d == 1 else pl.ds(r, size, stride=d)


def _dil_operands(d, tb, units, q_ref, kc_ref, kp_ref, vc_ref, vp_ref, qs, kf, vf):
    _dil_stage(q_ref, qs)
    _dil_stage(kp_ref, kf)
    _dil_stage(kc_ref, kf, tb)
    _dil_stage(vp_ref, vf)
    _dil_stage(vc_ref, vf, tb)
    qu = [qs[hh, _dil_rows(d, r, BLOCK), :].astype(MXU_DTYPE) for hh, r in units]
    ku = [kf[hh, _dil_rows(d, r, 2 * BLOCK), :].astype(MXU_DTYPE) for hh, r in units]
    vu = [vf[hh, _dil_rows(d, r, 2 * BLOCK), :].astype(MXU_DTYPE) for hh, r in units]
    return qu, ku, vu


def _dilated_fwd(q, k, mid, d, HW):
    S = q.shape[0]
    scale = HEAD_DIM ** -0.5
    ng, hb, tb, cur, prev, units = _dil_geometry(d, HW)

    def kern(q_ref, kc_ref, kp_ref, vc_ref, vp_ref, o_ref, l_ref, qs, kf, vf, os_, ls):
        mask = _dil_mask(pl.program_id(0))
        qu, ku, vu = _dil_operands(d, tb, units, q_ref, kc_ref, kp_ref, vc_ref, vp_ref, qs, kf, vf)
        sc = [jnp.where(mask, _dot(a, b, NT) * scale, NEG) for a, b in zip(qu, ku)]
        mx = [jnp.max(t, axis=1, keepdims=True) for t in sc]
        ex = [jnp.exp(t - m) for t, m in zip(sc, mx)]
        den = [jnp.sum(t, axis=1, keepdims=True) for t in ex]
        out = [_dot(t.astype(MXU_DTYPE), v) / dn for t, v, dn in zip(ex, vu, den)]
        for (hh, r), o, m, dn in zip(units, out, mx, den):
            os_[hh, _dil_rows(d, r, BLOCK), :] = o
            ls[hh, _dil_rows(d, r, BLOCK), :] = jnp.broadcast_to(m + jnp.log(dn), (BLOCK, HEAD_DIM))
        _dil_unstage(os_, o_ref)
        _dil_unstage(ls, l_ref)

    return pl.pallas_call(
        kern, name=f"dilated_fwd_d{d}", grid=(S // tb, ng),
        in_specs=[cur, cur, prev, cur, prev],
        out_specs=[cur, cur],
        out_shape=[jax.ShapeDtypeStruct((S, HW), F32)] * 2,
        scratch_shapes=[pltpu.VMEM((hb, tb, HEAD_DIM), F32)] + [pltpu.VMEM((hb, 2 * tb, HEAD_DIM), F32)] * 2
        + [pltpu.VMEM((hb, tb, HEAD_DIM), F32)] * 2,
        compiler_params=_params(("arbitrary", "arbitrary")),
    )(q, k, k, mid, mid)


def _dilated_combine(os_, lses, HW):
    S = os_[0].shape[0]
    tr = _fit(S, 256)

    def body(o0, o1, o2, l0, l1, l2, ya_ref, lse_ref):
        a, b, c = l0[...], l1[...], l2[...]
        mx = jnp.maximum(jnp.maximum(a, b), c)
        ea, eb, ec = jnp.exp(a - mx), jnp.exp(b - mx), jnp.exp(c - mx)
        tot = ea + eb + ec
        ya_ref[...] = (ea * o0[...] + eb * o1[...] + ec * o2[...]) / tot
        lse_ref[...] = mx + jnp.log(tot)

    ins = [(a, _row(tr, HW)) for a in list(os_) + list(lses)]
    return _rows_call("dilated_combine", body, S, tr, ins, [(HW, F32), (HW, F32)])


def _dilated_bwd(q, k, mid, dya, ya, lse, d, HW):
    S = q.shape[0]
    scale = HEAD_DIM ** -0.5
    ng, hb, tb, cur, prev, units = _dil_geometry(d, HW)

    def kern(q_ref, kc_ref, kp_ref, vc_ref, vp_ref, dy_ref, y_ref, l_ref, dq_ref, dkc_ref, dkp_ref, dvc_ref, dvp_ref,
             qs, kf, vf, dys, ys, ls, dqs, dkcs, dkps, dvcs, dvps):
        mask = _dil_mask(pl.program_id(0))
        qu, ku, vu = _dil_operands(d, tb, units, q_ref, kc_ref, kp_ref, vc_ref, vp_ref, qs, kf, vf)
        _dil_stage(dy_ref, dys)
        _dil_stage(y_ref, ys)
        _dil_stage(l_ref, ls)
        dy = [dys[hh, _dil_rows(d, r, BLOCK), :] for hh, r in units]
        lt = [ls[hh, _dil_rows(d, r, BLOCK), :][:, 0:1] for hh, r in units]
        delta = [jnp.sum(t * ys[hh, _dil_rows(d, r, BLOCK), :], axis=1, keepdims=True) for t, (hh, r) in zip(dy, units)]
        dyb = [t.astype(MXU_DTYPE) for t in dy]
        p = [jnp.where(mask, jnp.exp(_dot(a, b, NT) * scale - l), 0.0) for a, b, l in zip(qu, ku, lt)]
        ds = [(t * (_dot(g, v, NT) - dl) * scale).astype(MXU_DTYPE) for t, g, v, dl in zip(p, dyb, vu, delta)]
        dq = [_dot(t, b) for t, b in zip(ds, ku)]
        dk = [_dot(t, a, TN) for t, a in zip(ds, qu)]
        dv = [_dot(t.astype(MXU_DTYPE), g, TN) for t, g in zip(p, dyb)]
        for (hh, r), tq, tk, tv in zip(units, dq, dk, dv):
            at = _dil_rows(d, r, BLOCK)
            dqs[hh, at, :] = tq
            dkps[hh, at, :] = tk[0:BLOCK]
            dkcs[hh, at, :] = tk[BLOCK:2 * BLOCK]
            dvps[hh, at, :] = tv[0:BLOCK]
            dvcs[hh, at, :] = tv[BLOCK:2 * BLOCK]
        for buf, ref in ((dqs, dq_ref), (dkcs, dkc_ref), (dkps, dkp_ref), (dvcs, dvc_ref), (dvps, dvp_ref)):
            _dil_unstage(buf, ref)

    return pl.pallas_call(
        kern, name=f"dilated_bwd_d{d}", grid=(S // tb, ng),
        in_specs=[cur, cur, prev, cur, prev, cur, cur, cur],
        out_specs=[cur] * 5,
        out_shape=[jax.ShapeDtypeStruct((S, HW), F32)] * 3 + [jax.ShapeDtypeStruct((S, HW), MXU_DTYPE)] * 2,
        scratch_shapes=[pltpu.VMEM((hb, tb, HEAD_DIM), F32)] + [pltpu.VMEM((hb, 2 * tb, HEAD_DIM), F32)] * 2
        + [pltpu.VMEM((hb, tb, HEAD_DIM), F32)] * 8,
        compiler_params=_params(("arbitrary", "arbitrary")),
    )(q, k, k, mid, mid, dya, ya, lse)


def _softplus_parts(z2):
    m = jnp.minimum(-z2, 0.0) - jnp.log(1.0 + jnp.exp2(-jnp.abs(z2))) * LOG2E
    return m, m + z2


def _sb_specs(S, H):
    q_spec = pl.BlockSpec((SB_BQ, HEAD_DIM), lambda h, i: (i, H + h))
    k_spec = pl.BlockSpec((S, HEAD_DIM), lambda h, i: (0, 2 * H + h))
    v_spec = pl.BlockSpec((S, HEAD_DIM), lambda h, i: (0, 3 * H + h))
    o_spec = pl.BlockSpec((SB_BQ, HEAD_DIM), lambda h, i: (i, h))
    return q_spec, k_spec, v_spec, o_spec


def _sb_tri(relation):
    tri_r = lax.broadcasted_iota(jnp.int32, (SB_BK, SB_BK), 0)
    tri_c = lax.broadcasted_iota(jnp.int32, (SB_BK, SB_BK), 1)
    return relation(tri_r, tri_c).astype(MXU_DTYPE)


def _sb_scratch():
    return [pltpu.VMEM((2, SB_BQ, SB_BQ), MXU_DTYPE), pltpu.VMEM((2, SB_BQ, SB_BQ), MXU_DTYPE), pltpu.SemaphoreType.DMA((2, 2))]


def _sb_fwd(mid, HW, ex=None):
    S = mid.shape[0]
    H = HW // HEAD_DIM
    BQ, CH = SB_BQ, SB_BK
    NC, nq = BQ // CH, S // BQ
    scale = HEAD_DIM ** -0.5
    q_spec, k_spec, v_spec, o_spec = _sb_specs(S, H)

    def kern(q_ref, k_ref, v_ref, o_ref, a_hbm, s_hbm, abuf, sbuf, sems):
        h, i = pl.program_id(0), pl.program_id(1)
        q = q_ref[...]
        upper = _sb_tri(lambda r, c: r > c)
        row = lax.broadcasted_iota(jnp.int32, (BQ, BQ), 0)
        causal = lax.broadcasted_iota(jnp.int32, (BQ, BQ), 1) < row

        def save(slot, j):
            return [pltpu.make_async_copy(buf.at[slot], hbm.at[h, i, j], sems.at[w, slot])
                    for w, (buf, hbm) in enumerate(((abuf, a_hbm), (sbuf, s_hbm)))]

        def block(n, run, acc, masked):
            j = i - n
            ks = pl.multiple_of(j * BQ, BQ)
            m, l = _softplus_parts(_dot(q, k_ref[pl.ds(ks, BQ), :], NT) * (scale * LOG2E))
            if masked:
                m = jnp.where(causal, m, 0.0)
            parts = []
            for c in reversed(range(NC)):
                mc = m[:, c * CH:(c + 1) * CH]
                parts.append(l[:, c * CH:(c + 1) * CH] + (_split_dot(mc, upper, CUMSUM_PASSES) + run))
                run = run + jnp.sum(mc, axis=1, keepdims=True)
            a = jnp.exp2(jnp.concatenate(parts[::-1], axis=1))
            sig = jnp.exp2(l)
            if masked:
                a = jnp.where(causal, a, 0.0)
                sig = jnp.where(causal, sig, 0.0)
            ab = a.astype(MXU_DTYPE)
            slot = n % 2
            if not masked:
                @pl.when(n >= 2)
                def _():
                    for cp in save(slot, j):
                        cp.wait()
            abuf[slot] = ab
            sbuf[slot] = sig.astype(MXU_DTYPE)
            for cp in save(slot, j):
                cp.start()
            return run, acc + _dot(ab, v_ref[pl.ds(ks, BQ), :])

        run, acc = block(0, jnp.zeros((BQ, 1), F32), jnp.zeros((BQ, HEAD_DIM), F32), True)
        run, acc = lax.fori_loop(1, i + 1, lambda n, carry: block(n, carry[0], carry[1], False), (run, acc))
        o_ref[...] = acc
        for cp in save(0, i):
            cp.wait()

        @pl.when(i >= 1)
        def _():
            for cp in save(1, i):
                cp.wait()

    tiles = jax.ShapeDtypeStruct((H, nq, nq, BQ, BQ), MXU_DTYPE)
    (o, a_t, s_t), rode = _host_call(kern, "stickbreak_fwd", (H, nq), [q_spec, k_spec, v_spec], [o_spec, ANY, ANY],
                                     [jax.ShapeDtypeStruct((S, HW), F32), tiles, tiles], (mid, mid, mid), ex, _sb_scratch())
    return o, a_t, s_t, rode


def _sb_bwd(mid, dyb, a_t, s_t, HW, ex=None):
    S = mid.shape[0]
    H = HW // HEAD_DIM
    BQ, CH = SB_BQ, SB_BK
    NC = BQ // CH
    scale = HEAD_DIM ** -0.5
    q_spec, k_spec, v_spec, o_spec = _sb_specs(S, H)
    full = pl.BlockSpec((S, HEAD_DIM), lambda h, i: (0, h))

    def kern(q_ref, k_ref, v_ref, do_ref, a_hbm, s_hbm, dq_ref, dk_ref, dv_ref, abuf, sbuf, sems):
        h, i = pl.program_id(0), pl.program_id(1)

        @pl.when(i == 0)
        def _():
            dk_ref[...] = jnp.zeros_like(dk_ref)
            dv_ref[...] = jnp.zeros_like(dv_ref)

        q = q_ref[...]
        do = do_ref[...].astype(MXU_DTYPE)
        excl = _sb_tri(lambda r, c: r < c)

        def fetch(slot, j):
            return [pltpu.make_async_copy(hbm.at[h, i, j], buf.at[slot], sems.at[w, slot])
                    for w, (buf, hbm) in enumerate(((abuf, a_hbm), (sbuf, s_hbm)))]

        for cp in fetch(0, 0):
            cp.start()

        def block(j, carry):
            prun, dq = carry
            slot = j % 2
            for cp in fetch(slot, j):
                cp.wait()

            @pl.when(j < i)
            def _():
                for cp in fetch(1 - slot, j + 1):
                    cp.start()

            ks = pl.multiple_of(j * BQ, BQ)
            k = k_ref[pl.ds(ks, BQ), :]
            ab = abuf[slot]
            p = ab.astype(F32) * _dot(do, v_ref[pl.ds(ks, BQ), :], NT)
            parts = []
            for c in range(NC):
                pc = p[:, c * CH:(c + 1) * CH]
                parts.append(_split_dot(pc, excl, CUMSUM_PASSES) + prun)
                prun = prun + jnp.sum(pc, axis=1, keepdims=True)
            before = jnp.concatenate(parts, axis=1)
            dzb = ((p - sbuf[slot].astype(F32) * (p + before)) * scale).astype(MXU_DTYPE)
            dk_ref[pl.ds(ks, BQ), :] += _dot(dzb, q, TN)
            dv_ref[pl.ds(ks, BQ), :] += _dot(ab, do, TN)
            return prun, dq + _dot(dzb, k)

        _, dq = lax.fori_loop(0, i + 1, block, (jnp.zeros((BQ, 1), F32), jnp.zeros((BQ, HEAD_DIM), F32)))
        dq_ref[...] = dq

    grads, rode = _host_call(kern, "stickbreak_bwd", (H, S // BQ), [q_spec, k_spec, v_spec, o_spec, ANY, ANY], [o_spec, full, full],
                             [jax.ShapeDtypeStruct((S, HW), F32)] * 3, (mid, mid, mid, dyb, a_t, s_t), ex, _sb_scratch())
    return grads, rode


ANY = pl.BlockSpec(memory_space=pl.ANY)


def _place():
    x, y, c = lax.axis_index("x"), lax.axis_index("y"), lax.axis_index("c")
    chips = [(1 - x, y), (x, 1 - y), (1 - x, 1 - y)]
    return x, y, c, chips


def _half(ref, shard, hc, rh):
    return ref.at[shard, pl.ds(pl.multiple_of(hc * rh, 8), rh), :]


def _cast_place(name, w, s_idx):
    R, C = w.shape
    tr = _fit(R, 256)

    def kern(s_ref, w_ref, o_ref):
        o_ref[...] = w_ref[...].astype(o_ref.dtype)

    return pl.pallas_call(
        kern, name=name,
        grid_spec=pltpu.PrefetchScalarGridSpec(
            num_scalar_prefetch=1, grid=(R // tr,),
            in_specs=[pl.BlockSpec((tr, C), lambda r, s_ref: (r, 0))],
            out_specs=pl.BlockSpec((None, tr, C), lambda r, s_ref: (s_ref[0], r, 0))),
        out_shape=jax.ShapeDtypeStruct((N_SHARD, R, C), MXU_DTYPE),
        compiler_params=_params(("arbitrary",)),
    )(s_idx, w)


class _Exchange:
    def __init__(self, inputs, out_shape, aliases, scratch, phases):
        self.inputs, self.out_shape, self.aliases, self.scratch, self.phases = inputs, out_shape, aliases, scratch, phases


def _run_alone(name, ex):
    ni, no = len(ex.inputs), len(ex.out_shape)

    def body(*refs):
        for phase in ex.phases:
            phase(refs[:ni], refs[ni:ni + no], refs[ni + no:])

    return pl.pallas_call(
        body, name=name, in_specs=[ANY] * ni, out_specs=[ANY] * no, out_shape=ex.out_shape,
        input_output_aliases=ex.aliases, scratch_shapes=ex.scratch,
    )(*ex.inputs)


def _host_call(kern, name, grid, in_specs, out_specs, out_shape, operands, ex, scratch=()):
    sem = ("arbitrary",) * len(grid)
    if ex is None:
        return pl.pallas_call(kern, name=name, grid=grid, in_specs=in_specs, out_specs=out_specs, out_shape=out_shape,
                              scratch_shapes=list(scratch), compiler_params=_params(sem))(*operands), []
    n_in, n_out, ri, ro, ns = len(in_specs), len(out_specs), len(ex.inputs), len(ex.out_shape), len(scratch)
    nsteps, nph = 1, len(ex.phases)
    for size in grid:
        nsteps *= size

    def body(*refs):
        r_in, r_out = refs[n_in:n_in + ri], refs[n_in + ri + n_out:n_in + ri + n_out + ro]
        host_scratch, ex_scratch = refs[n_in + ri + n_out + ro:][:ns], refs[n_in + ri + n_out + ro + ns:]
        step = 0
        for axis, size in enumerate(grid):
            step = step * size + pl.program_id(axis)
        for kph, phase in enumerate(ex.phases):
            pl.when(step == (kph * (nsteps - 1)) // (nph - 1))(functools.partial(phase, r_in, r_out, ex_scratch))
        kern(*refs[:n_in], *refs[n_in + ri:n_in + ri + n_out], *host_scratch)

    outs = pl.pallas_call(
        body, name=name, grid=grid, in_specs=list(in_specs) + [ANY] * ri, out_specs=list(out_specs) + [ANY] * ro,
        out_shape=list(out_shape) + list(ex.out_shape), scratch_shapes=list(scratch) + list(ex.scratch),
        input_output_aliases={n_in + a: n_out + b for a, b in ex.aliases.items()},
        compiler_params=_params(sem),
    )(*operands, *ex.inputs)
    return outs[:n_out], outs[n_out:]


def _gather_exchange(bufs):
    n = len(bufs)

    def between_chips(outs, sems, i, j, chip, c, shard):
        blk = _half(outs[i], shard, c, outs[i].shape[1] // 2)
        return pltpu.make_async_remote_copy(src_ref=blk, dst_ref=blk, send_sem=sems[0].at[i, j], recv_sem=sems[1].at[i, j],
                                            device_id=(chip[0], chip[1], c), device_id_type=MESH)

    def to_sibling(outs, sems, i, j, x, y, c, shard, hc):
        blk = _half(outs[i], shard, hc, outs[i].shape[1] // 2)
        return pltpu.make_async_remote_copy(src_ref=blk, dst_ref=blk, send_sem=sems[0].at[i, 3 + j], recv_sem=sems[1].at[i, 3 + j],
                                            device_id=(x, y, 1 - c), device_id_type=MESH)

    def send_mine(ins, outs, sems):
        x, y, c, chips = _place()
        for i in range(n):
            for j, chip in enumerate(chips):
                between_chips(outs, sems, i, j, chip, c, 2 * x + y).start()

    def pass_on(ins, outs, sems):
        x, y, c, chips = _place()
        for i in range(n):
            for j, chip in enumerate(chips):
                between_chips(outs, sems, i, j, chip, c, 2 * chip[0] + chip[1]).wait_recv()
                to_sibling(outs, sems, i, j, x, y, c, 2 * chip[0] + chip[1], c).start()

    def finish(ins, outs, sems):
        x, y, c, chips = _place()
        for i in range(n):
            for j, chip in enumerate(chips):
                to_sibling(outs, sems, i, j, x, y, c, 2 * chip[0] + chip[1], 1 - c).wait_recv()
        for i in range(n):
            for j, chip in enumerate(chips):
                between_chips(outs, sems, i, j, chip, c, 2 * x + y).wait_send()
                to_sibling(outs, sems, i, j, x, y, c, 2 * chip[0] + chip[1], c).wait_send()

    return _Exchange(list(bufs), [jax.ShapeDtypeStruct(b.shape, b.dtype) for b in bufs], {i: i for i in range(n)},
                     [pltpu.SemaphoreType.DMA((n, 6)), pltpu.SemaphoreType.DMA((n, 6))], [send_mine, pass_on, finish])


def _reduce_exchange(g16, g32):
    n = len(g16)

    def copies(ins, outs, sems):
        x, y, c, _ = _place()
        for i in range(n):
            rh = ins[i].shape[1] // 2
            for r in range(1, 8):
                px, py, pc = x ^ ((r >> 2) & 1), y ^ ((r >> 1) & 1), c ^ (r & 1)
                src = _half(ins[i] if r > 1 else ins[n + i], 2 * px + py, pc, rh)
                dst = outs[2 * i + 1].at[r - 2] if r > 1 else outs[2 * i]
                yield pltpu.make_async_remote_copy(src_ref=src, dst_ref=dst, send_sem=sems[0].at[i, r - 1], recv_sem=sems[1].at[i, r - 1],
                                                   device_id=(px, py, pc), device_id_type=MESH)

    def start(ins, outs, sems):
        for cp in copies(ins, outs, sems):
            cp.start()

    def finish(ins, outs, sems):
        for cp in copies(ins, outs, sems):
            cp.wait()

    out_shape = []
    for g in g16:
        rh, C = g.shape[1] // 2, g.shape[2]
        out_shape += [jax.ShapeDtypeStruct((rh, C), F32), jax.ShapeDtypeStruct((6, rh, C), g.dtype)]
    return _Exchange(list(g16) + list(g32), out_shape, {},
                     [pltpu.SemaphoreType.DMA((n, 7)), pltpu.SemaphoreType.DMA((n, 7))], [start, finish])


def _add_direct(name, g32, from_sibling, from_chips, s_idx, c_idx):
    _, R, C = g32.shape
    rh = R // 2
    tr = _fit(rh, 256)
    nrb = rh // tr

    def kern(s_ref, c_ref, g_ref, a_ref, b_ref, out_ref):
        acc = g_ref[...] + a_ref[...]
        for k in range(6):
            acc = acc + b_ref[k].astype(F32)
        out_ref[...] = acc

    return pl.pallas_call(
        kern, name=name,
        grid_spec=pltpu.PrefetchScalarGridSpec(
            num_scalar_prefetch=2, grid=(nrb,),
            in_specs=[pl.BlockSpec((None, tr, C), lambda r, s_ref, c_ref: (s_ref[0], c_ref[0] * nrb + r, 0)),
                      pl.BlockSpec((tr, C), lambda r, s_ref, c_ref: (r, 0)),
                      pl.BlockSpec((6, tr, C), lambda r, s_ref, c_ref: (0, r, 0))],
            out_specs=pl.BlockSpec((tr, C), lambda r, s_ref, c_ref: (r, 0))),
        out_shape=jax.ShapeDtypeStruct((rh, C), F32),
        compiler_params=_params(("arbitrary",)),
    )(s_idx, c_idx, g32, from_sibling, from_chips)


def _swap_reduced(halves):
    n = len(halves)

    def body(*refs):
        ins, outs = refs[:n], refs[n:2 * n]
        send, recv = refs[2 * n:]
        x, y, c, _ = _place()
        copies = []
        for i in range(n):
            cp = pltpu.make_async_remote_copy(src_ref=ins[i], dst_ref=outs[i], send_sem=send.at[i], recv_sem=recv.at[i],
                                              device_id=(x, y, 1 - c), device_id_type=MESH)
            cp.start()
            copies.append(cp)
        for cp in copies:
            cp.wait()

    return pl.pallas_call(
        body, name="grad_swap_reduced",
        in_specs=[ANY] * n, out_specs=[ANY] * n,
        out_shape=[jax.ShapeDtypeStruct(h.shape, F32) for h in halves],
        scratch_shapes=[pltpu.SemaphoreType.DMA((n,)), pltpu.SemaphoreType.DMA((n,))],
    )(*halves)


def _all_reduce_small(v):
    rows, W = v.shape
    ndev = 8

    def body(v_ref, out_ref, buf, send, recv):
        x, y, c, _ = _place()
        me = 4 * x + 2 * y + c
        buf[me] = v_ref[...]
        copies = []
        for r in range(1, ndev):
            fx, fy, fc = (r >> 2) & 1, (r >> 1) & 1, r & 1
            peer = (x ^ fx, y ^ fy, c ^ fc)
            cp = pltpu.make_async_remote_copy(src_ref=v_ref, dst_ref=buf.at[me], send_sem=send.at[r - 1], recv_sem=recv.at[r - 1],
                                              device_id=peer, device_id_type=MESH)
            cp.start()
            copies.append(cp)
        for cp in copies:
            cp.wait()
        acc = buf[0]
        for k in range(1, ndev):
            acc = acc + buf[k]
        out_ref[...] = acc

    return pl.pallas_call(
        body, name="small_all_reduce",
        in_specs=[pl.BlockSpec(memory_space=pltpu.VMEM)], out_specs=pl.BlockSpec(memory_space=pltpu.VMEM),
        out_shape=jax.ShapeDtypeStruct((rows, W), F32),
        scratch_shapes=[pltpu.VMEM((ndev, rows, W), F32), pltpu.SemaphoreType.DMA((ndev - 1,)), pltpu.SemaphoreType.DMA((ndev - 1,))],
    )(v)


def _adamw_update(gv, w_ref, m_ref, v_ref, d_ref, nm_ref, nv_ref):
    nm = ADAM_B1 * m_ref[...] + (1.0 - ADAM_B1) * gv
    nv = ADAM_B2 * v_ref[...] + (1.0 - ADAM_B2) * (gv * gv)
    m_hat = nm / (1.0 - ADAM_B1 ** ADAM_STEP)
    v_hat = nv / (1.0 - ADAM_B2 ** ADAM_STEP)
    d_ref[...] = -ADAM_LR * (m_hat / (jnp.sqrt(v_hat) + ADAM_EPS) + ADAM_WD * w_ref[...])
    nm_ref[...] = nm
    nv_ref[...] = nv


def _adamw(name, w, g, m, v):
    R, C = w.shape
    tr = _fit(R, 256)

    def body(w_ref, g_ref, m_ref, v_ref, d_ref, nm_ref, nv_ref):
        _adamw_update(g_ref[...], w_ref, m_ref, v_ref, d_ref, nm_ref, nv_ref)

    return _rows_call(name, body, R, tr, [(a, _row(tr, C)) for a in (w, g, m, v)], [(C, F32)] * 3)


def _adamw_halves(name, w, mine, theirs, m, v, c_idx):
    R, C = w.shape
    rh = R // 2
    tr = _fit(rh, 256)
    nrb = rh // tr

    def kern(c_ref, w_ref, a_ref, b_ref, m_ref, v_ref, g_ref, d_ref, nm_ref, nv_ref):
        gv = jnp.where(pl.program_id(0) // nrb == c_ref[0], a_ref[...], b_ref[...])
        g_ref[...] = gv
        _adamw_update(gv, w_ref, m_ref, v_ref, d_ref, nm_ref, nv_ref)

    full = pl.BlockSpec((tr, C), lambda r, c_ref: (r, 0))
    pick = lambda own: pl.BlockSpec((tr, C), lambda r, c_ref: (jnp.where((r // nrb == c_ref[0]) == own, r % nrb, 0), 0))
    return pl.pallas_call(
        kern, name=name,
        grid_spec=pltpu.PrefetchScalarGridSpec(
            num_scalar_prefetch=1, grid=(R // tr,),
            in_specs=[full, pick(True), pick(False), full, full], out_specs=[full] * 4),
        out_shape=[jax.ShapeDtypeStruct((R, C), F32)] * 4,
        compiler_params=_params(("arbitrary",)),
    )(c_idx, w, mine, theirs, m, v)


def _sigmoid(z):
    return 1.0 / (1.0 + jnp.exp(-z))


def kernel(x, p, g_mix, w_in, qn_gain, kn_gain, w_branch_a, w_branch_b, w_out, g_mlp, w_up, w_down, g_ple, w_ple_gate, w_ple_proj, loss_target, m_g_mix, m_w_in, m_qn_gain, m_kn_gain, m_w_branch_a, m_w_branch_b, m_w_out, m_g_mlp, m_w_up, m_w_down, m_g_ple, m_w_ple_gate, m_w_ple_proj, v_g_mix, v_w_in, v_qn_gain, v_kn_gain, v_w_branch_a, v_w_branch_b, v_w_out, v_g_mlp, v_w_up, v_w_down, v_g_ple, v_w_ple_gate, v_w_ple_proj):
    S, D = x.shape[1], x.shape[2]
    HW = w_branch_a.shape[1]
    x2d, tgt, p2d = x.reshape(S, D), loss_target.reshape(S, D), p.reshape(S, p.shape[-1])
    big = {"w_in": w_in, "w_branch_a": w_branch_a, "w_branch_b": w_branch_b, "w_out": w_out, "w_up": w_up,
           "w_down": w_down, "w_ple_gate": w_ple_gate, "w_ple_proj": w_ple_proj}
    moments = {"w_in": (m_w_in, v_w_in), "w_branch_a": (m_w_branch_a, v_w_branch_a), "w_branch_b": (m_w_branch_b, v_w_branch_b),
               "w_out": (m_w_out, v_w_out), "w_up": (m_w_up, v_w_up), "w_down": (m_w_down, v_w_down),
               "w_ple_gate": (m_w_ple_gate, v_w_ple_gate), "w_ple_proj": (m_w_ple_proj, v_w_ple_proj)}
    names = list(big)
    col_sharded = {"w_in", "w_branch_a", "w_branch_b", "w_up", "w_ple_proj"}
    shard2d = {k: w.reshape(w.shape[1], w.shape[2]) for k, w in big.items()}

    c_idx = lax.axis_index("c").astype(jnp.int32).reshape(1)
    s_idx = (2 * lax.axis_index("x") + lax.axis_index("y")).astype(jnp.int32).reshape(1)
    placed = {k: _cast_place(f"cast_{k}", shard2d[k], s_idx) for k in names}
    late = [k for k in names if k != "w_in"]
    h, (w_in_all,) = _rmsnorm_fwd("rmsnorm_mix", x2d, g_mix, _gather_exchange([placed["w_in"]]))
    W = {"w_in": w_in_all}
    cin = W["w_in"].shape[2]
    bn_in = _fit(cin, 512)
    while (2 * HW) % bn_in:
        bn_in -= 128

    (qk,) = _matmul("proj_qk", h, W["w_in"], mode="nn", bm=2048, out_dtypes=[F32], b_cshard=True, b_off=0, n_out=2 * HW, bn=bn_in, bk=D)
    (mid,) = _matmul("proj_mid", h, W["w_in"], mode="nn", bm=2048, out_dtypes=[MXU_DTYPE], b_cshard=True, b_off=2 * HW // bn_in,
                     n_out=4 * HW, bn=bn_in, bk=D)
    (sg,) = _matmul("proj_gates", h, W["w_in"], mode="nn", bm=2048, out_dtypes=[MXU_DTYPE], b_cshard=True, b_off=6 * HW // bn_in,
                    n_out=2 * D, bn=bn_in, bk=D, epilogue=lambda acc: (_sigmoid(acc),))
    tabs = _rope_tables(S)
    qa, ka = _qknorm_fwd(qk, qn_gain, kn_gain, tabs, HW)
    dil = [_dilated_fwd(qa, ka, mid, d, HW) for d in DILATIONS]
    ya, lse = _dilated_combine([o for o, _ in dil], [l for _, l in dil], HW)
    yb, sb_a, sb_sig, gathered = _sb_fwd(mid, HW, _gather_exchange([placed[k] for k in late]))
    W.update({k: (g if k in col_sharded else g.reshape(-1, g.shape[2])) for k, g in zip(late, gathered)})

    gate_blocks = D // _fit(D, 1024)
    (ua,) = _matmul("branch_a", ya, W["w_branch_a"], mode="nn", out_dtypes=[MXU_DTYPE], b_cshard=True, bn=_fit(W["w_branch_a"].shape[2], 1024))
    bn_b = _fit(W["w_branch_b"].shape[2], 1024)
    ub, merged = _matmul("branch_b_merge", yb, W["w_branch_b"], mode="nn", out_dtypes=[MXU_DTYPE, MXU_DTYPE], b_cshard=True, bn=bn_b,
                         extras=[(sg, 0), (sg, D // bn_b), (ua, 0)],
                         epilogue=lambda acc, sga, sgb, uav: (acc, sga.astype(F32) * uav.astype(F32) + sgb.astype(F32) * acc))
    (x1,) = _matmul("out_proj", merged, W["w_out"], mode="nn", out_dtypes=[F32], extras=[(x2d, 0)], epilogue=lambda acc, xv: (xv + acc,))
    hm = _rmsnorm_fwd("rmsnorm_mlp", x1, g_mlp)

    def up_epilogue(acc):
        r = jnp.maximum(acc, 0.0)
        return r * r, r

    act, rup = _matmul("mlp_up", hm, W["w_up"], mode="nn", out_dtypes=[MXU_DTYPE, MXU_DTYPE], b_cshard=True,
                       bn=_fit(W["w_up"].shape[2], 1024), bk=D, epilogue=up_epilogue)
    (x2,) = _matmul("mlp_down", act, W["w_down"], mode="nn", out_dtypes=[F32], extras=[(x1, 0)], epilogue=lambda acc, xv: (xv + acc,))
    hp = _rmsnorm_fwd("rmsnorm_ple", x2, g_ple)
    (pp,) = _matmul("ple_proj", p2d, W["w_ple_proj"], mode="nn", out_dtypes=[F32], b_cshard=True, bn=_fit(W["w_ple_proj"].shape[2], 1024))

    def ple_epilogue(acc, ppv, x2v, tv):
        s = _sigmoid(acc)
        dx3 = ((x2v + ppv * s) - tv) / D
        return dx3, dx3 * s, dx3 * ppv * (s * (1.0 - s))

    dx3, d_pp, d_gate = _matmul("ple_gate_loss", hp, W["w_ple_gate"], mode="nn", out_dtypes=[F32, MXU_DTYPE, MXU_DTYPE],
                                bm=512, extras=[(pp, 0), (x2, 0), (tgt, 0)], epilogue=ple_epilogue)

    G, G16 = {}, {}
    G["w_ple_proj"], G16["w_ple_proj"] = _matmul("grad_w_ple_proj", p2d, d_pp, mode="tn", out_dtypes=[F32, MXU_DTYPE], out_cshard=True,
                                 bn=_fit(d_pp.shape[1] // N_SHARD, 1024))
    G["w_ple_gate"], G16["w_ple_gate"] = _matmul("grad_w_ple_gate", hp, d_gate, mode="tn", out_dtypes=[F32, MXU_DTYPE])
    (d_hp,) = _matmul("ple_gate_bwd", d_gate, W["w_ple_gate"], mode="nt", out_dtypes=[F32])
    dx2, dx2_low, g_g_ple, loss_part = _rmsnorm_bwd("rmsnorm_ple_bwd", d_hp, x2, g_ple, dx3, True, True)
    G["w_down"], G16["w_down"] = _matmul("grad_w_down", act, dx2_low, mode="tn", out_dtypes=[F32, MXU_DTYPE])
    (d_up,) = _matmul("mlp_down_bwd", dx2_low, W["w_down"], mode="nt", out_dtypes=[MXU_DTYPE], extras=[(rup, 0)],
                      epilogue=lambda acc, r: (acc * (2.0 * r.astype(F32)),))
    G["w_up"], G16["w_up"] = _matmul("grad_w_up", hm, d_up, mode="tn", out_dtypes=[F32, MXU_DTYPE], out_cshard=True, bn=_fit(d_up.shape[1] // N_SHARD, 1024))
    (d_hm,) = _matmul("mlp_up_bwd", d_up, W["w_up"], mode="nt", out_dtypes=[F32], b_cshard=True, bk=_fit(W["w_up"].shape[2], 2048))
    dx1, dx1_low, g_g_mlp = _rmsnorm_bwd("rmsnorm_mlp_bwd", d_hm, x1, g_mlp, dx2, False, True)
    G["w_out"], G16["w_out"] = _matmul("grad_w_out", merged, dx1_low, mode="tn", out_dtypes=[F32, MXU_DTYPE])

    def merge_bwd(acc, sga, sgb, uav, ubv):
        sga, sgb, uav, ubv = (t.astype(F32) for t in (sga, sgb, uav, ubv))
        return acc * sga, acc * sgb, acc * uav * (sga * (1.0 - sga)), acc * ubv * (sgb * (1.0 - sgb))

    bn_m = _fit(D, 1024)
    d_ua, d_ub, d_ga, d_gb = _matmul("out_proj_bwd", dx1_low, W["w_out"], mode="nt", out_dtypes=[MXU_DTYPE] * 4, bm=512, bn=bn_m,
                                     extras=[(sg, 0), (sg, D // bn_m), (ua, 0), (ub, 0)], epilogue=merge_bwd)
    bn_br = _fit(D // N_SHARD, 1024)
    G["w_branch_a"], G16["w_branch_a"] = _matmul("grad_w_branch_a", ya, d_ua, mode="tn", out_dtypes=[F32, MXU_DTYPE], out_cshard=True, bn=bn_br)
    G["w_branch_b"], G16["w_branch_b"] = _matmul("grad_w_branch_b", yb, d_ub, mode="tn", out_dtypes=[F32, MXU_DTYPE], out_cshard=True, bn=bn_br)
    (d_ya,) = _matmul("branch_a_bwd", d_ua, W["w_branch_a"], mode="nt", out_dtypes=[F32], b_cshard=True, bk=bn_br)
    (d_yb,) = _matmul("branch_b_bwd", d_ub, W["w_branch_b"], mode="nt", out_dtypes=[F32], b_cshard=True, bk=bn_br)

    as_shards = lambda k, g: g if k in col_sharded else g.reshape(N_SHARD, -1, g.shape[1])
    (dqb, dkb, dvb), partials = _sb_bwd(mid, d_yb, sb_a, sb_sig, HW, _reduce_exchange([as_shards(k, G16[k]) for k in late],
                                                                               [as_shards(k, G[k]) for k in late]))
    dil_b = [_dilated_bwd(qa, ka, mid, d_ya, ya, lse, d, HW) for d in DILATIONS]
    d_qk, g_qn, g_kn = _qknorm_bwd(qk, qn_gain, kn_gain, tabs, [t[0] for t in dil_b], [t[1] for t in dil_b],
                                   [t[2] for t in dil_b], HW)
    dva = _dv_sum([t[3] for t in dil_b], [t[4] for t in dil_b], HW)
    d_proj = jnp.concatenate([d_qk, dva, dqb.astype(MXU_DTYPE), dkb.astype(MXU_DTYPE), dvb.astype(MXU_DTYPE), d_ga, d_gb], axis=1)
    G["w_in"], G16["w_in"] = _matmul("grad_w_in", h, d_proj, mode="tn", out_dtypes=[F32, MXU_DTYPE], out_cshard=True, bn=_fit(cin, 1280))
    (d_h,), partials_in = _matmul("proj_bwd", d_proj, W["w_in"], mode="nt", out_dtypes=[F32], b_cshard=True, bk=_fit(cin, 1280),
                                  ride=_reduce_exchange([G16["w_in"]], [G["w_in"]]))
    grad_x, g_g_mix = _rmsnorm_bwd("rmsnorm_mix_bwd", d_h, x2d, g_mix, dx1, False, False)

    mine = {k: _add_direct(f"grad_add_{k}", as_shards(k, G[k]), partials[2 * n], partials[2 * n + 1], s_idx, c_idx)
            for n, k in enumerate(late)}
    mine["w_in"] = _add_direct("grad_add_w_in", G["w_in"], partials_in[0], partials_in[1], s_idx, c_idx)
    halves = [mine[k] for k in names]
    others = _swap_reduced(halves)

    pack_w = -(-(3 * D + 3 * 128) // (8 * 128)) * 128

    def pack(v_mix, v_mlp, v_ple, v_qn, v_kn, extra):
        flat = jnp.concatenate([v_mix.reshape(-1), v_mlp.reshape(-1), v_ple.reshape(-1), v_qn.reshape(-1), v_kn.reshape(-1), extra.reshape(-1)])
        return jnp.pad(flat, (0, 8 * pack_w - flat.shape[0])).reshape(8, pack_w)

    def unpack(blk):
        flat = blk.reshape(-1)
        return (flat[:D].reshape(1, D), flat[D:2 * D].reshape(1, D), flat[2 * D:3 * D].reshape(1, D),
                flat[3 * D:3 * D + 128].reshape(1, 128), flat[3 * D + 128:3 * D + 256].reshape(1, 128), flat[3 * D + 256])

    small = _all_reduce_small(pack(g_g_mix, g_g_mlp, g_g_ple, g_qn, g_kn, loss_part))
    sw = pack(g_mix, g_mlp, g_ple, qn_gain, kn_gain, jnp.zeros((128,), F32))
    sm = pack(m_g_mix, m_g_mlp, m_g_ple, m_qn_gain, m_kn_gain, jnp.zeros((128,), F32))
    sv = pack(v_g_mix, v_g_mlp, v_g_ple, v_qn_gain, v_kn_gain, jnp.ones((128,), F32))
    s_delta, s_nm, s_nv = _adamw("adamw_small", sw, small, sm, sv)
    sg_mix, sg_mlp, sg_ple, sg_qn, sg_kn, loss = unpack(small)
    small_out = {}
    for tag, blk in (("delta", s_delta), ("new_m", s_nm), ("new_v", s_nv)):
        u = unpack(blk)
        small_out[tag] = dict(g_mix=u[0], g_mlp=u[1], g_ple=u[2], qn_gain=u[3], kn_gain=u[4])
    small_grad = dict(g_mix=sg_mix, g_mlp=sg_mlp, g_ple=sg_ple, qn_gain=sg_qn, kn_gain=sg_kn)

    big_out = {"grad": {}, "delta": {}, "new_m": {}, "new_v": {}}
    for k, mine, theirs in zip(names, halves, others):
        shape = big[k].shape
        m2, v2 = (t.reshape(shape[1], shape[2]) for t in moments[k])
        res = _adamw_halves(f"adamw_{k}", shard2d[k], mine, theirs, m2, v2, c_idx)
        for tag, t in zip(("grad", "delta", "new_m", "new_v"), res):
            big_out[tag][k] = t.reshape(shape)

    order = ["g_mix", "w_in", "qn_gain", "kn_gain", "w_branch_a", "w_branch_b", "w_out", "g_mlp", "w_up", "w_down", "g_ple",
             "w_ple_gate", "w_ple_proj"]
    outs = [loss, grad_x.reshape(x.shape)]
    outs += [small_grad[k] if k in small_grad else big_out["grad"][k] for k in order]
    for tag in ("delta", "new_m", "new_v"):
        outs += [small_out[tag][k] if k in small_grad else big_out[tag][k] for k in order]
    return tuple(outs)
```

```python
import functools

import jax
import jax.numpy as jnp
from jax import lax
from jax.experimental import pallas as pl
from jax.experimental.pallas import tpu as pltpu

F32 = jnp.float32
MXU_DTYPE = jnp.bfloat16
HEAD_DIM = 128
ROT_DIM = HEAD_DIM // 4
ROPE_THETA = 500000.0
EPS = 1e-6
DILATIONS = (1, 4, 16)
BLOCK = 128
N_SHARD = 4
ADAM_LR, ADAM_B1, ADAM_B2, ADAM_EPS, ADAM_WD, ADAM_STEP = 0.001, 0.9, 0.999, 1e-08, 0.01, 10
MXU_WIDTH = 256
V7X_VMEM_BYTES = 64 * 1024 * 1024
VMEM_LIMIT = V7X_VMEM_BYTES - 8 * 1024 * 1024
MESH = pl.DeviceIdType.MESH
NEG = -1e30
SB_BQ, SB_BK = 1024, 256
LOG2E = 1.4426950408889634
CUMSUM_PASSES = 2


def _fit(dim, pref):
    if dim <= pref:
        return dim
    b = (pref // 128) * 128
    while dim % b:
        b -= 128
    return b


def _params(sem=None):
    return pltpu.CompilerParams(dimension_semantics=sem, vmem_limit_bytes=VMEM_LIMIT)


def _dot(a, b, dims=(((1,), (0,)), ((), ()))):
    return lax.dot_general(a, b, dims, preferred_element_type=F32)


NT = (((1,), (1,)), ((), ()))
TN = (((0,), (0,)), ((), ()))


def _split_dot(x, u, passes):
    out = None
    r = x
    for p in range(passes):
        hi = r.astype(MXU_DTYPE)
        part = _dot(hi, u)
        out = part if out is None else out + part
        if p + 1 < passes:
            r = r - hi.astype(F32)
    return out


def _matmul(name, a, b, *, mode, out_dtypes, bm=1024, bn=1024, bk=2048, b_cshard=False, b_off=0, n_out=None,
            extras=(), epilogue=None, out_cshard=False, ride=None):
    if mode == "tn":
        K, M = a.shape
        N = b.shape[1]
    else:
        M, K = a.shape
        if mode == "nn":
            N = n_out if n_out is not None else (N_SHARD * b.shape[2] if b_cshard else b.shape[1])
        else:
            N = b.shape[1] if b_cshard else b.shape[0]
    bm, bn, bk = _fit(M, bm), _fit(N, bn), _fit(K, bk)
    nk = K // bk
    grid = (M // bm, N // bn, nk)

    if mode == "tn":
        a_spec = pl.BlockSpec((bk, bm), lambda i, j, k: (k, i))
        b_spec = pl.BlockSpec((bk, bn), lambda i, j, k: (k, j))
        dims = TN
    elif mode == "nn":
        a_spec = pl.BlockSpec((bm, bk), lambda i, j, k: (i, k))
        if b_cshard:
            cb = b.shape[2] // bn
            b_spec = pl.BlockSpec((None, bk, bn), lambda i, j, k: ((j + b_off) // cb, k, (j + b_off) % cb))
        else:
            b_spec = pl.BlockSpec((bk, bn), lambda i, j, k: (k, j + b_off))
        dims = (((1,), (0,)), ((), ()))
    else:
        a_spec = pl.BlockSpec((bm, bk), lambda i, j, k: (i, k))
        if b_cshard:
            cb = b.shape[2] // bk
            b_spec = pl.BlockSpec((None, bn, bk), lambda i, j, k: (k // cb, j, k % cb))
        else:
            b_spec = pl.BlockSpec((bn, bk), lambda i, j, k: (j, k))
        dims = NT

    ex_arrays = [e[0] for e in extras]
    ex_specs = [pl.BlockSpec((bm, bn), functools.partial(lambda i, j, k, off: (i, j + off), off=e[1])) for e in extras]
    if out_cshard:
        cbo = (N // N_SHARD) // bn
        out_shape = [jax.ShapeDtypeStruct((N_SHARD, M, N // N_SHARD), dt) for dt in out_dtypes]
        out_specs = [pl.BlockSpec((None, bm, bn), lambda i, j, k: (j // cbo, i, j % cbo)) for _ in out_dtypes]
    else:
        out_shape = [jax.ShapeDtypeStruct((M, N), dt) for dt in out_dtypes]
        out_specs = [pl.BlockSpec((bm, bn), lambda i, j, k: (i, j)) for _ in out_dtypes]
    ne, no = len(extras), len(out_dtypes)
    cw = MXU_WIDTH if bn % MXU_WIDTH == 0 else bn

    def kern(*refs):
        a_ref, b_ref = refs[0], refs[1]
        ex_refs = refs[2:2 + ne]
        o_refs = refs[2 + ne:2 + ne + no]
        av = a_ref[...].astype(MXU_DTYPE)

        def finish(acc):
            vals = (acc,) * no if epilogue is None else epilogue(acc, *[r[...] for r in ex_refs])
            for r, v in zip(o_refs, vals):
                r[...] = v.astype(r.dtype)

        if nk == 1:
            finish(_dot(av, b_ref[...].astype(MXU_DTYPE), dims))
        else:
            acc_ref = refs[2 + ne + no]
            k = pl.program_id(2)

            @pl.when(k == 0)
            def _():
                acc_ref[...] = jnp.zeros_like(acc_ref)

            for c0 in range(0, bn, cw):
                bv = b_ref[c0:c0 + cw, :] if mode == "nt" else b_ref[:, c0:c0 + cw]
                acc_ref[:, c0:c0 + cw] += _dot(av, bv.astype(MXU_DTYPE), dims)

            @pl.when(k == nk - 1)
            def _():
                finish(acc_ref[...])

    outs, rode = _host_call(kern, name, grid, [a_spec, b_spec] + ex_specs, out_specs, out_shape, (a, b, *ex_arrays), ride,
                            [pltpu.VMEM((bm, bn), F32)] if nk > 1 else [])
    return outs if ride is None else (outs, rode)


def _row(tr, w, coff=0):
    return pl.BlockSpec((tr, w), lambda i: (i, coff))


def _vec(w):
    return pl.BlockSpec((1, w), lambda i: (0, 0))


def _rows_call(name, body, n_rows, tr, ins, outs, accs=()):
    n_in, n_out = len(ins), len(outs)

    def kern(*refs):
        acc_refs = refs[n_in + n_out:]
        if acc_refs:
            @pl.when(pl.program_id(0) == 0)
            def _():
                for r in acc_refs:
                    r[...] = jnp.zeros_like(r)
        body(*refs)

    out_shape = [jax.ShapeDtypeStruct((n_rows, w), dt) for w, dt in outs] + [jax.ShapeDtypeStruct((1, w), F32) for w in accs]
    out_specs = [_row(tr, w) for w, _ in outs] + [_vec(w) for w in accs]
    return pl.pallas_call(
        kern, name=name, grid=(n_rows // tr,),
        in_specs=[s for _, s in ins], out_specs=out_specs, out_shape=out_shape,
        compiler_params=_params(("arbitrary",)),
    )(*[a for a, _ in ins])


def _rmsnorm_fwd(name, x, g, ex=None):
    S, D = x.shape
    tr = _fit(S, 256)

    def body(x_ref, g_ref, h_ref):
        xv = x_ref[...]
        r = lax.rsqrt(jnp.mean(xv * xv, axis=1, keepdims=True) + EPS)
        h_ref[...] = ((xv * r) * g_ref[...]).astype(h_ref.dtype)

    (h,), rode = _host_call(body, name, (S // tr,), [_row(tr, D), _vec(D)], [_row(tr, D)], [jax.ShapeDtypeStruct((S, D), MXU_DTYPE)],
                            (x, g), ex)
    return h if ex is None else (h, rode)


def _rmsnorm_bwd(name, dh, x, g, resid, with_loss, with_copy):
    S, D = x.shape
    tr = _fit(S, 256)

    def body(dh_ref, x_ref, g_ref, res_ref, dx_ref, *rest):
        copy_ref = rest[:1] if with_copy else ()
        dg_ref, loss_ref = rest[len(copy_ref)], rest[len(copy_ref) + 1:]
        xv = x_ref[...]
        r = lax.rsqrt(jnp.mean(xv * xv, axis=1, keepdims=True) + EPS)
        dhv = dh_ref[...]
        u = dhv * g_ref[...]
        xr = xv * r
        dx = r * u - xr * (r * r) * jnp.mean(xv * u, axis=1, keepdims=True)
        resv = res_ref[...]
        dx_ref[...] = resv + dx
        for r in copy_ref:
            r[...] = (resv + dx).astype(r.dtype)
        dg_ref[...] += jnp.sum(dhv * xr, axis=0, keepdims=True)
        if with_loss:
            loss_ref[0][...] += (0.5 * D) * jnp.sum(resv * resv)

    outs = _rows_call(name, body, S, tr, [(dh, _row(tr, D)), (x, _row(tr, D)), (g, _vec(D)), (resid, _row(tr, D))],
                      [(D, F32)] + ([(D, MXU_DTYPE)] if with_copy else []), accs=(D, 128) if with_loss else (D,))
    return outs


def _rope_tables(S):
    half = ROT_DIM // 2
    pos = jnp.arange(S, dtype=F32)
    inv = ROPE_THETA ** (-jnp.arange(0, ROT_DIM, 2, dtype=F32) / ROT_DIM)
    ang = pos[:, None] * inv[None, :]
    cos, sin = jnp.cos(ang), jnp.sin(ang)
    pad = HEAD_DIM - ROT_DIM
    ctab = jnp.concatenate([cos, cos, jnp.ones((S, pad), F32)], axis=1)
    atab = jnp.concatenate([-sin, jnp.zeros((S, pad + half), F32)], axis=1)
    btab = jnp.concatenate([jnp.zeros((S, half), F32), sin, jnp.zeros((S, pad), F32)], axis=1)
    return ctab, atab, btab


def _qknorm_fwd(qk, qn, kn, tabs, HW):
    S = qk.shape[0]
    tr = _fit(S, 256)
    half = ROT_DIM // 2

    def body(qk_ref, qn_ref, kn_ref, c_ref, a_ref, b_ref, q_out, k_out):
        ct, at, bt = c_ref[...], a_ref[...], b_ref[...]
        for part, (g_ref, o_ref) in enumerate(((qn_ref, q_out), (kn_ref, k_out))):
            gv = g_ref[...]
            for h in range(HW // HEAD_DIM):
                xh = qk_ref[:, part * HW + h * HEAD_DIM: part * HW + (h + 1) * HEAD_DIM]
                r = lax.rsqrt(jnp.mean(xh * xh, axis=1, keepdims=True) + EPS)
                y = (xh * r) * gv
                o = y * ct + pltpu.roll(y, HEAD_DIM - half, 1) * at + pltpu.roll(y, half, 1) * bt
                o_ref[:, h * HEAD_DIM:(h + 1) * HEAD_DIM] = o.astype(o_ref.dtype)

    ins = [(qk, _row(tr, 2 * HW)), (qn, _vec(HEAD_DIM)), (kn, _vec(HEAD_DIM))] + [(t, _row(tr, HEAD_DIM)) for t in tabs]
    return _rows_call("qknorm_fwd", body, S, tr, ins, [(HW, MXU_DTYPE), (HW, MXU_DTYPE)])


def _shift_spec(tr, w, shift, nblk):
    return pl.BlockSpec((tr, w), lambda i: (jnp.minimum(i + shift, nblk - 1), 0))


def _qknorm_bwd(qk, qn, kn, tabs, dq_parts, dk_cur, dk_prev, HW):
    S = qk.shape[0]
    tr = BLOCK
    nblk = S // tr
    half = ROT_DIM // 2
    nd = len(DILATIONS)

    def body(*refs):
        qk_ref, qn_ref, kn_ref, c_ref, a_ref, b_ref = refs[:6]
        dq_refs = refs[6:6 + nd]
        dkc_refs = refs[6 + nd:6 + 2 * nd]
        dkp_refs = refs[6 + 2 * nd:6 + 3 * nd]
        d_out, dqn_ref, dkn_ref = refs[6 + 3 * nd:]
        i = pl.program_id(0)
        ct, at, bt = c_ref[...], a_ref[...], b_ref[...]
        live = [(i + d < nblk).astype(F32) for d in DILATIONS]
        for part, (g_ref, dg_ref) in enumerate(((qn_ref, dqn_ref), (kn_ref, dkn_ref))):
            gv = g_ref[...]
            dg = jnp.zeros((1, HEAD_DIM), F32)
            for h in range(HW // HEAD_DIM):
                hs = slice(h * HEAD_DIM, (h + 1) * HEAD_DIM)
                if part == 0:
                    do = dq_refs[0][:, hs] + dq_refs[1][:, hs] + dq_refs[2][:, hs]
                else:
                    do = dkc_refs[0][:, hs] + dkc_refs[1][:, hs] + dkc_refs[2][:, hs]
                    for n in range(nd):
                        do = do + dkp_refs[n][:, hs] * live[n]
                dy = do * ct + pltpu.roll(do * at, half, 1) + pltpu.roll(do * bt, HEAD_DIM - half, 1)
                xh = qk_ref[:, part * HW + h * HEAD_DIM: part * HW + (h + 1) * HEAD_DIM]
                r = lax.rsqrt(jnp.mean(xh * xh, axis=1, keepdims=True) + EPS)
                xr = xh * r
                u = dy * gv
                dx = r * u - xr * (r * r) * jnp.mean(xh * u, axis=1, keepdims=True)
                d_out[:, part * HW + h * HEAD_DIM: part * HW + (h + 1) * HEAD_DIM] = dx.astype(d_out.dtype)
                dg = dg + jnp.sum(dy * xr, axis=0, keepdims=True)
            dg_ref[...] += dg

    ins = [(qk, _row(tr, 2 * HW)), (qn, _vec(HEAD_DIM)), (kn, _vec(HEAD_DIM))] + [(t, _row(tr, HEAD_DIM)) for t in tabs]
    ins += [(a, _row(tr, HW)) for a in dq_parts] + [(a, _row(tr, HW)) for a in dk_cur]
    ins += [(a, _shift_spec(tr, HW, d, nblk)) for a, d in zip(dk_prev, DILATIONS)]
    return _rows_call("qknorm_bwd", body, S, tr, ins, [(2 * HW, MXU_DTYPE)], accs=(HEAD_DIM, HEAD_DIM))


def _dv_sum(dv_cur, dv_prev, HW):
    S = dv_cur[0].shape[0]
    tr = BLOCK
    nblk = S // tr
    nd = len(DILATIONS)

    def body(*refs):
        i = pl.program_id(0)
        out = refs[2 * nd]
        acc = refs[0][...].astype(F32) + refs[1][...].astype(F32) + refs[2][...].astype(F32)
        for n, d in enumerate(DILATIONS):
            acc = acc + refs[nd + n][...].astype(F32) * (i + d < nblk).astype(F32)
        out[...] = acc.astype(out.dtype)

    ins = [(a, _row(tr, HW)) for a in dv_cur] + [(a, _shift_spec(tr, HW, d, nblk)) for a, d in zip(dv_prev, DILATIONS)]
    return _rows_call("dilated_dv_sum", body, S, tr, ins, [(HW, MXU_DTYPE)])[0]


def _dil_geometry(d, HW):
    H = HW // HEAD_DIM
    hb = min(H, max(1, 8 // d))
    tb, w = BLOCK * d, hb * HEAD_DIM
    cur = pl.BlockSpec((tb, w), lambda n, g: (n, g))
    prev = pl.BlockSpec((tb, w), lambda n, g: (jnp.maximum(n - 1, 0), g))
    units = [(hh, r) for hh in range(hb) for r in range(d)]
    return H // hb, hb, tb, cur, prev, units


def _dil_mask(n):
    qi = lax.broadcasted_iota(jnp.int32, (BLOCK, 2 * BLOCK), 0)
    ki = lax.broadcasted_iota(jnp.int32, (BLOCK, 2 * BLOCK), 1)
    return (ki >= qi) & (ki <= qi + BLOCK) & ((ki >= BLOCK) | (n > 0))


def _dil_cols(hh):
    return slice(hh * HEAD_DIM, (hh + 1) * HEAD_DIM)


def _dil_stage(d, ref, buf, row0=0):
    if d > 1:
        for hh in range(buf.shape[0]):
            buf[hh, row0:row0 + ref.shape[0], :] = ref[:, _dil_cols(hh)].astype(F32)


def _dil_unstage(d, buf, ref):
    if d > 1:
        for hh in range(buf.shape[0]):
            ref[:, _dil_cols(hh)] = buf[hh].astype(ref.dtype)


def _dil_read(d, ref, buf, hh, r):
    return ref[:, _dil_cols(hh)].astype(F32) if d == 1 else buf[hh, pl.ds(r, BLOCK, stride=d), :]


def _dil_write(d, ref, buf, hh, r, val):
    if d == 1:
        ref[:, _dil_cols(hh)] = val.astype(ref.dtype)
    else:
        buf[hh, pl.ds(r, BLOCK, stride=d), :] = val


def _dil_operands(d, tb, units, q_ref, kc_ref, kp_ref, vc_ref, vp_ref, qs, kf, vf):
    if d == 1:
        window = lambda p_ref, c_ref, hh: jnp.concatenate([p_ref[:, _dil_cols(hh)], c_ref[:, _dil_cols(hh)]], axis=0)
        return ([q_ref[:, _dil_cols(hh)] for hh, _ in units], [window(kp_ref, kc_ref, hh) for hh, _ in units],
                [window(vp_ref, vc_ref, hh) for hh, _ in units])
    _dil_stage(d, q_ref, qs)
    _dil_stage(d, kp_ref, kf)
    _dil_stage(d, kc_ref, kf, tb)
    _dil_stage(d, vp_ref, vf)
    _dil_stage(d, vc_ref, vf, tb)
    qu = [qs[hh, pl.ds(r, BLOCK, stride=d), :].astype(MXU_DTYPE) for hh, r in units]
    ku = [kf[hh, pl.ds(r, 2 * BLOCK, stride=d), :].astype(MXU_DTYPE) for hh, r in units]
    vu = [vf[hh, pl.ds(r, 2 * BLOCK, stride=d), :].astype(MXU_DTYPE) for hh, r in units]
    return qu, ku, vu


def _dilated_fwd(q, k, mid, d, HW):
    S = q.shape[0]
    scale = HEAD_DIM ** -0.5
    ng, hb, tb, cur, prev, units = _dil_geometry(d, HW)

    def kern(q_ref, kc_ref, kp_ref, vc_ref, vp_ref, o_ref, l_ref, qs, kf, vf, os_, ls):
        mask = _dil_mask(pl.program_id(0))
        qu, ku, vu = _dil_operands(d, tb, units, q_ref, kc_ref, kp_ref, vc_ref, vp_ref, qs, kf, vf)
        sc = [jnp.where(mask, _dot(a, b, NT) * scale, NEG) for a, b in zip(qu, ku)]
        mx = [jnp.max(t, axis=1, keepdims=True) for t in sc]
        ex = [jnp.exp(t - m) for t, m in zip(sc, mx)]
        den = [jnp.sum(t, axis=1, keepdims=True) for t in ex]
        out = [_dot(t.astype(MXU_DTYPE), v) / dn for t, v, dn in zip(ex, vu, den)]
        for (hh, r), o, m, dn in zip(units, out, mx, den):
            _dil_write(d, o_ref, os_, hh, r, o)
            _dil_write(d, l_ref, ls, hh, r, jnp.broadcast_to(m + jnp.log(dn), (BLOCK, HEAD_DIM)))
        _dil_unstage(d, os_, o_ref)
        _dil_unstage(d, ls, l_ref)

    return pl.pallas_call(
        kern, name=f"dilated_fwd_d{d}", grid=(S // tb, ng),
        in_specs=[cur, cur, prev, cur, prev],
        out_specs=[cur, cur],
        out_shape=[jax.ShapeDtypeStruct((S, HW), F32)] * 2,
        scratch_shapes=[pltpu.VMEM((hb, tb, HEAD_DIM), F32)] + [pltpu.VMEM((hb, 2 * tb, HEAD_DIM), F32)] * 2
        + [pltpu.VMEM((hb, tb, HEAD_DIM), F32)] * 2,
        compiler_params=_params(("arbitrary", "arbitrary")),
    )(q, k, k, mid, mid)


def _dilated_combine(os_, lses, HW):
    S = os_[0].shape[0]
    tr = _fit(S, 256)

    def body(o0, o1, o2, l0, l1, l2, ya_ref, lse_ref):
        a, b, c = l0[...], l1[...], l2[...]
        mx = jnp.maximum(jnp.maximum(a, b), c)
        ea, eb, ec = jnp.exp(a - mx), jnp.exp(b - mx), jnp.exp(c - mx)
        tot = ea + eb + ec
        ya_ref[...] = (ea * o0[...] + eb * o1[...] + ec * o2[...]) / tot
        lse_ref[...] = mx + jnp.log(tot)

    ins = [(a, _row(tr, HW)) for a in list(os_) + list(lses)]
    return _rows_call("dilated_combine", body, S, tr, ins, [(HW, F32), (HW, F32)])


def _dilated_bwd(q, k, mid, dya, ya, lse, d, HW):
    S = q.shape[0]
    scale = HEAD_DIM ** -0.5
    ng, hb, tb, cur, prev, units = _dil_geometry(d, HW)

    def kern(q_ref, kc_ref, kp_ref, vc_ref, vp_ref, dy_ref, y_ref, l_ref, dq_ref, dkc_ref, dkp_ref, dvc_ref, dvp_ref,
             qs, kf, vf, dys, ys, ls, dqs, dkcs, dkps, dvcs, dvps):
        mask = _dil_mask(pl.program_id(0))
        qu, ku, vu = _dil_operands(d, tb, units, q_ref, kc_ref, kp_ref, vc_ref, vp_ref, qs, kf, vf)
        _dil_stage(d, dy_ref, dys)
        _dil_stage(d, y_ref, ys)
        _dil_stage(d, l_ref, ls)
        dy = [_dil_read(d, dy_ref, dys, hh, r) for hh, r in units]
        lt = [_dil_read(d, l_ref, ls, hh, r)[:, 0:1] for hh, r in units]
        delta = [jnp.sum(t * _dil_read(d, y_ref, ys, hh, r), axis=1, keepdims=True) for t, (hh, r) in zip(dy, units)]
        dyb = [t.astype(MXU_DTYPE) for t in dy]
        p = [jnp.where(mask, jnp.exp(_dot(a, b, NT) * scale - l), 0.0) for a, b, l in zip(qu, ku, lt)]
        ds = [(t * (_dot(g, v, NT) - dl) * scale).astype(MXU_DTYPE) for t, g, v, dl in zip(p, dyb, vu, delta)]
        dq = [_dot(t, b) for t, b in zip(ds, ku)]
        dk = [_dot(t, a, TN) for t, a in zip(ds, qu)]
        dv = [_dot(t.astype(MXU_DTYPE), g, TN) for t, g in zip(p, dyb)]
        for (hh, r), tq, tk, tv in zip(units, dq, dk, dv):
            _dil_write(d, dq_ref, dqs, hh, r, tq)
            _dil_write(d, dkp_ref, dkps, hh, r, tk[0:BLOCK])
            _dil_write(d, dkc_ref, dkcs, hh, r, tk[BLOCK:2 * BLOCK])
            _dil_write(d, dvp_ref, dvps, hh, r, tv[0:BLOCK])
            _dil_write(d, dvc_ref, dvcs, hh, r, tv[BLOCK:2 * BLOCK])
        for buf, ref in ((dqs, dq_ref), (dkcs, dkc_ref), (dkps, dkp_ref), (dvcs, dvc_ref), (dvps, dvp_ref)):
            _dil_unstage(d, buf, ref)

    return pl.pallas_call(
        kern, name=f"dilated_bwd_d{d}", grid=(S // tb, ng),
        in_specs=[cur, cur, prev, cur, prev, cur, cur, cur],
        out_specs=[cur] * 5,
        out_shape=[jax.ShapeDtypeStruct((S, HW), F32)] * 3 + [jax.ShapeDtypeStruct((S, HW), MXU_DTYPE)] * 2,
        scratch_shapes=[pltpu.VMEM((hb, tb, HEAD_DIM), F32)] + [pltpu.VMEM((hb, 2 * tb, HEAD_DIM), F32)] * 2
        + [pltpu.VMEM((hb, tb, HEAD_DIM), F32)] * 8,
        compiler_params=_params(("arbitrary", "arbitrary")),
    )(q, k, k, mid, mid, dya, ya, lse)


def _softplus_parts(z2):
    m = jnp.minimum(-z2, 0.0) - jnp.log(1.0 + jnp.exp2(-jnp.abs(z2))) * LOG2E
    return m, m + z2


def _sb_specs(S, H):
    q_spec = pl.BlockSpec((SB_BQ, HEAD_DIM), lambda h, i: (i, H + h))
    k_spec = pl.BlockSpec((S, HEAD_DIM), lambda h, i: (0, 2 * H + h))
    v_spec = pl.BlockSpec((S, HEAD_DIM), lambda h, i: (0, 3 * H + h))
    o_spec = pl.BlockSpec((SB_BQ, HEAD_DIM), lambda h, i: (i, h))
    return q_spec, k_spec, v_spec, o_spec


def _sb_tri(relation):
    tri_r = lax.broadcasted_iota(jnp.int32, (SB_BK, SB_BK), 0)
    tri_c = lax.broadcasted_iota(jnp.int32, (SB_BK, SB_BK), 1)
    return relation(tri_r, tri_c).astype(MXU_DTYPE)


def _sb_scratch():
    return [pltpu.VMEM((2, SB_BQ, SB_BQ), MXU_DTYPE), pltpu.VMEM((2, SB_BQ, SB_BQ), MXU_DTYPE), pltpu.SemaphoreType.DMA((2, 2))]


def _sb_fwd(mid, HW, ex=None):
    S = mid.shape[0]
    H = HW // HEAD_DIM
    BQ, CH = SB_BQ, SB_BK
    NC, nq = BQ // CH, S // BQ
    scale = HEAD_DIM ** -0.5
    q_spec, k_spec, v_spec, o_spec = _sb_specs(S, H)

    def kern(q_ref, k_ref, v_ref, o_ref, a_hbm, s_hbm, abuf, sbuf, sems):
        h, i = pl.program_id(0), pl.program_id(1)
        q = q_ref[...]
        upper = _sb_tri(lambda r, c: r > c)
        row = lax.broadcasted_iota(jnp.int32, (BQ, BQ), 0)
        causal = lax.broadcasted_iota(jnp.int32, (BQ, BQ), 1) < row

        def save(slot, j):
            return [pltpu.make_async_copy(buf.at[slot], hbm.at[h, i, j], sems.at[w, slot])
                    for w, (buf, hbm) in enumerate(((abuf, a_hbm), (sbuf, s_hbm)))]

        def block(n, run, acc, masked):
            j = i - n
            ks = pl.multiple_of(j * BQ, BQ)
            m, l = _softplus_parts(_dot(q, k_ref[pl.ds(ks, BQ), :], NT) * (scale * LOG2E))
            if masked:
                m = jnp.where(causal, m, 0.0)
            parts = []
            for c in reversed(range(NC)):
                mc = m[:, c * CH:(c + 1) * CH]
                parts.append(l[:, c * CH:(c + 1) * CH] + (_split_dot(mc, upper, CUMSUM_PASSES) + run))
                run = run + jnp.sum(mc, axis=1, keepdims=True)
            a = jnp.exp2(jnp.concatenate(parts[::-1], axis=1))
            sig = jnp.exp2(l)
            if masked:
                a = jnp.where(causal, a, 0.0)
                sig = jnp.where(causal, sig, 0.0)
            ab = a.astype(MXU_DTYPE)
            slot = n % 2
            if not masked:
                @pl.when(n >= 2)
                def _():
                    for cp in save(slot, j):
                        cp.wait()
            abuf[slot] = ab
            sbuf[slot] = sig.astype(MXU_DTYPE)
            for cp in save(slot, j):
                cp.start()
            return run, acc + _dot(ab, v_ref[pl.ds(ks, BQ), :])

        run, acc = block(0, jnp.zeros((BQ, 1), F32), jnp.zeros((BQ, HEAD_DIM), F32), True)
        run, acc = lax.fori_loop(1, i + 1, lambda n, carry: block(n, carry[0], carry[1], False), (run, acc))
        o_ref[...] = acc
        for cp in save(0, i):
            cp.wait()

        @pl.when(i >= 1)
        def _():
            for cp in save(1, i):
                cp.wait()

    tiles = jax.ShapeDtypeStruct((H, nq, nq, BQ, BQ), MXU_DTYPE)
    (o, a_t, s_t), rode = _host_call(kern, "stickbreak_fwd", (H, nq), [q_spec, k_spec, v_spec], [o_spec, ANY, ANY],
                                     [jax.ShapeDtypeStruct((S, HW), F32), tiles, tiles], (mid, mid, mid), ex, _sb_scratch())
    return o, a_t, s_t, rode


def _sb_bwd(mid, dyb, a_t, s_t, HW, ex=None):
    S = mid.shape[0]
    H = HW // HEAD_DIM
    BQ, CH = SB_BQ, SB_BK
    NC = BQ // CH
    scale = HEAD_DIM ** -0.5
    q_spec, k_spec, v_spec, o_spec = _sb_specs(S, H)
    full = pl.BlockSpec((S, HEAD_DIM), lambda h, i: (0, h))

    def kern(q_ref, k_ref, v_ref, do_ref, a_hbm, s_hbm, dq_ref, dk_ref, dv_ref, abuf, sbuf, sems):
        h, i = pl.program_id(0), pl.program_id(1)

        @pl.when(i == 0)
        def _():
            dk_ref[...] = jnp.zeros_like(dk_ref)
            dv_ref[...] = jnp.zeros_like(dv_ref)

        q = q_ref[...]
        do = do_ref[...].astype(MXU_DTYPE)
        excl = _sb_tri(lambda r, c: r < c)

        def fetch(slot, j):
            return [pltpu.make_async_copy(hbm.at[h, i, j], buf.at[slot], sems.at[w, slot])
                    for w, (buf, hbm) in enumerate(((abuf, a_hbm), (sbuf, s_hbm)))]

        for cp in fetch(0, 0):
            cp.start()

        def block(j, carry):
            prun, dq = carry
            slot = j % 2
            for cp in fetch(slot, j):
                cp.wait()

            @pl.when(j < i)
            def _():
                for cp in fetch(1 - slot, j + 1):
                    cp.start()

            ks = pl.multiple_of(j * BQ, BQ)
            k = k_ref[pl.ds(ks, BQ), :]
            ab = abuf[slot]
            p = ab.astype(F32) * _dot(do, v_ref[pl.ds(ks, BQ), :], NT)
            parts = []
            for c in range(NC):
                pc = p[:, c * CH:(c + 1) * CH]
                parts.append(_split_dot(pc, excl, CUMSUM_PASSES) + prun)
                prun = prun + jnp.sum(pc, axis=1, keepdims=True)
            before = jnp.concatenate(parts, axis=1)
            dzb = ((p - sbuf[slot].astype(F32) * (p + before)) * scale).astype(MXU_DTYPE)
            dk_ref[pl.ds(ks, BQ), :] += _dot(dzb, q, TN)
            dv_ref[pl.ds(ks, BQ), :] += _dot(ab, do, TN)
            return prun, dq + _dot(dzb, k)

        _, dq = lax.fori_loop(0, i + 1, block, (jnp.zeros((BQ, 1), F32), jnp.zeros((BQ, HEAD_DIM), F32)))
        dq_ref[...] = dq

    grads, rode = _host_call(kern, "stickbreak_bwd", (H, S // BQ), [q_spec, k_spec, v_spec, o_spec, ANY, ANY], [o_spec, full, full],
                             [jax.ShapeDtypeStruct((S, HW), F32)] * 3, (mid, mid, mid, dyb, a_t, s_t), ex, _sb_scratch())
    return grads, rode


ANY = pl.BlockSpec(memory_space=pl.ANY)


def _place():
    x, y, c = lax.axis_index("x"), lax.axis_index("y"), lax.axis_index("c")
    chips = [(1 - x, y), (x, 1 - y), (1 - x, 1 - y)]
    return x, y, c, chips


def _half(ref, shard, hc, rh):
    return ref.at[shard, pl.ds(pl.multiple_of(hc * rh, 8), rh), :]


def _cast_place(name, w, s_idx):
    R, C = w.shape
    tr = _fit(R, 256)

    def kern(s_ref, w_ref, o_ref):
        o_ref[...] = w_ref[...].astype(o_ref.dtype)

    return pl.pallas_call(
        kern, name=name,
        grid_spec=pltpu.PrefetchScalarGridSpec(
            num_scalar_prefetch=1, grid=(R // tr,),
            in_specs=[pl.BlockSpec((tr, C), lambda r, s_ref: (r, 0))],
            out_specs=pl.BlockSpec((None, tr, C), lambda r, s_ref: (s_ref[0], r, 0))),
        out_shape=jax.ShapeDtypeStruct((N_SHARD, R, C), MXU_DTYPE),
        compiler_params=_params(("arbitrary",)),
    )(s_idx, w)


class _Exchange:
    def __init__(self, inputs, out_shape, aliases, scratch, phases):
        self.inputs, self.out_shape, self.aliases, self.scratch, self.phases = inputs, out_shape, aliases, scratch, phases


def _host_call(kern, name, grid, in_specs, out_specs, out_shape, operands, ex, scratch=()):
    sem = ("arbitrary",) * len(grid)
    if ex is None:
        return pl.pallas_call(kern, name=name, grid=grid, in_specs=in_specs, out_specs=out_specs, out_shape=out_shape,
                              scratch_shapes=list(scratch), compiler_params=_params(sem))(*operands), []
    n_in, n_out, ri, ro, ns = len(in_specs), len(out_specs), len(ex.inputs), len(ex.out_shape), len(scratch)
    nsteps, nph = 1, len(ex.phases)
    for size in grid:
        nsteps *= size

    def body(*refs):
        r_in, r_out = refs[n_in:n_in + ri], refs[n_in + ri + n_out:n_in + ri + n_out + ro]
        host_scratch, ex_scratch = refs[n_in + ri + n_out + ro:][:ns], refs[n_in + ri + n_out + ro + ns:]
        step = 0
        for axis, size in enumerate(grid):
            step = step * size + pl.program_id(axis)
        for kph, phase in enumerate(ex.phases):
            pl.when(step == (kph * (nsteps - 1)) // (nph - 1))(functools.partial(phase, r_in, r_out, ex_scratch))
        kern(*refs[:n_in], *refs[n_in + ri:n_in + ri + n_out], *host_scratch)

    outs = pl.pallas_call(
        body, name=name, grid=grid, in_specs=list(in_specs) + [ANY] * ri, out_specs=list(out_specs) + [ANY] * ro,
        out_shape=list(out_shape) + list(ex.out_shape), scratch_shapes=list(scratch) + list(ex.scratch),
        input_output_aliases={n_in + a: n_out + b for a, b in ex.aliases.items()},
        compiler_params=_params(sem),
    )(*operands, *ex.inputs)
    return outs[:n_out], outs[n_out:]


def _gather_exchange(bufs):
    n = len(bufs)

    def between_chips(outs, sems, i, j, chip, c, shard):
        blk = _half(outs[i], shard, c, outs[i].shape[1] // 2)
        return pltpu.make_async_remote_copy(src_ref=blk, dst_ref=blk, send_sem=sems[0].at[i, j], recv_sem=sems[1].at[i, j],
                                            device_id=(chip[0], chip[1], c), device_id_type=MESH)

    def to_sibling(outs, sems, i, j, x, y, c, shard, hc):
        blk = _half(outs[i], shard, hc, outs[i].shape[1] // 2)
        return pltpu.make_async_remote_copy(src_ref=blk, dst_ref=blk, send_sem=sems[0].at[i, 3 + j], recv_sem=sems[1].at[i, 3 + j],
                                            device_id=(x, y, 1 - c), device_id_type=MESH)

    def send_mine(ins, outs, sems):
        x, y, c, chips = _place()
        for i in range(n):
            for j, chip in enumerate(chips):
                between_chips(outs, sems, i, j, chip, c, 2 * x + y).start()

    def pass_on(ins, outs, sems):
        x, y, c, chips = _place()
        for i in range(n):
            for j, chip in enumerate(chips):
                between_chips(outs, sems, i, j, chip, c, 2 * chip[0] + chip[1]).wait_recv()
                to_sibling(outs, sems, i, j, x, y, c, 2 * chip[0] + chip[1], c).start()

    def finish(ins, outs, sems):
        x, y, c, chips = _place()
        for i in range(n):
            for j, chip in enumerate(chips):
                to_sibling(outs, sems, i, j, x, y, c, 2 * chip[0] + chip[1], 1 - c).wait_recv()
        for i in range(n):
            for j, chip in enumerate(chips):
                between_chips(outs, sems, i, j, chip, c, 2 * x + y).wait_send()
                to_sibling(outs, sems, i, j, x, y, c, 2 * chip[0] + chip[1], c).wait_send()

    return _Exchange(list(bufs), [jax.ShapeDtypeStruct(b.shape, b.dtype) for b in bufs], {i: i for i in range(n)},
                     [pltpu.SemaphoreType.DMA((n, 6)), pltpu.SemaphoreType.DMA((n, 6))], [send_mine, pass_on, finish])


def _reduce_exchange(g16, g32):
    n = len(g16)

    def copies(ins, outs, sems):
        x, y, c, _ = _place()
        for i in range(n):
            rh = ins[i].shape[1] // 2
            for r in range(1, 8):
                px, py, pc = x ^ ((r >> 2) & 1), y ^ ((r >> 1) & 1), c ^ (r & 1)
                src = _half(ins[i] if r > 1 else ins[n + i], 2 * px + py, pc, rh)
                dst = outs[2 * i + 1].at[r - 2] if r > 1 else outs[2 * i]
                yield pltpu.make_async_remote_copy(src_ref=src, dst_ref=dst, send_sem=sems[0].at[i, r - 1], recv_sem=sems[1].at[i, r - 1],
                                                   device_id=(px, py, pc), device_id_type=MESH)

    def start(ins, outs, sems):
        for cp in copies(ins, outs, sems):
            cp.start()

    def finish(ins, outs, sems):
        for cp in copies(ins, outs, sems):
            cp.wait()

    out_shape = []
    for g in g16:
        rh, C = g.shape[1] // 2, g.shape[2]
        out_shape += [jax.ShapeDtypeStruct((rh, C), F32), jax.ShapeDtypeStruct((6, rh, C), g.dtype)]
    return _Exchange(list(g16) + list(g32), out_shape, {},
                     [pltpu.SemaphoreType.DMA((n, 7)), pltpu.SemaphoreType.DMA((n, 7))], [start, finish])


def _add_direct(name, g32, from_sibling, from_chips, s_idx, c_idx):
    _, R, C = g32.shape
    rh = R // 2
    tr = _fit(rh, 256)
    nrb = rh // tr

    def kern(s_ref, c_ref, g_ref, a_ref, b_ref, out_ref):
        acc = g_ref[...] + a_ref[...]
        for k in range(6):
            acc = acc + b_ref[k].astype(F32)
        out_ref[...] = acc

    return pl.pallas_call(
        kern, name=name,
        grid_spec=pltpu.PrefetchScalarGridSpec(
            num_scalar_prefetch=2, grid=(nrb,),
            in_specs=[pl.BlockSpec((None, tr, C), lambda r, s_ref, c_ref: (s_ref[0], c_ref[0] * nrb + r, 0)),
                      pl.BlockSpec((tr, C), lambda r, s_ref, c_ref: (r, 0)),
                      pl.BlockSpec((6, tr, C), lambda r, s_ref, c_ref: (0, r, 0))],
            out_specs=pl.BlockSpec((tr, C), lambda r, s_ref, c_ref: (r, 0))),
        out_shape=jax.ShapeDtypeStruct((rh, C), F32),
        compiler_params=_params(("arbitrary",)),
    )(s_idx, c_idx, g32, from_sibling, from_chips)


def _swap_reduced(halves):
    n = len(halves)

    def body(*refs):
        ins, outs = refs[:n], refs[n:2 * n]
        send, recv = refs[2 * n:]
        x, y, c, _ = _place()
        copies = []
        for i in range(n):
            cp = pltpu.make_async_remote_copy(src_ref=ins[i], dst_ref=outs[i], send_sem=send.at[i], recv_sem=recv.at[i],
                                              device_id=(x, y, 1 - c), device_id_type=MESH)
            cp.start()
            copies.append(cp)
        for cp in copies:
            cp.wait()

    return pl.pallas_call(
        body, name="grad_swap_reduced",
        in_specs=[ANY] * n, out_specs=[ANY] * n,
        out_shape=[jax.ShapeDtypeStruct(h.shape, F32) for h in halves],
        scratch_shapes=[pltpu.SemaphoreType.DMA((n,)), pltpu.SemaphoreType.DMA((n,))],
    )(*halves)


def _all_reduce_small(v):
    rows, W = v.shape
    ndev = 8

    def body(v_ref, out_ref, buf, send, recv):
        x, y, c, _ = _place()
        me = 4 * x + 2 * y + c
        buf[me] = v_ref[...]
        copies = []
        for r in range(1, ndev):
            fx, fy, fc = (r >> 2) & 1, (r >> 1) & 1, r & 1
            peer = (x ^ fx, y ^ fy, c ^ fc)
            cp = pltpu.make_async_remote_copy(src_ref=v_ref, dst_ref=buf.at[me], send_sem=send.at[r - 1], recv_sem=recv.at[r - 1],
                                              device_id=peer, device_id_type=MESH)
            cp.start()
            copies.append(cp)
        for cp in copies:
            cp.wait()
        acc = buf[0]
        for k in range(1, ndev):
            acc = acc + buf[k]
        out_ref[...] = acc

    return pl.pallas_call(
        body, name="small_all_reduce",
        in_specs=[pl.BlockSpec(memory_space=pltpu.VMEM)], out_specs=pl.BlockSpec(memory_space=pltpu.VMEM),
        out_shape=jax.ShapeDtypeStruct((rows, W), F32),
        scratch_shapes=[pltpu.VMEM((ndev, rows, W), F32), pltpu.SemaphoreType.DMA((ndev - 1,)), pltpu.SemaphoreType.DMA((ndev - 1,))],
    )(v)


def _adamw_update(gv, w_ref, m_ref, v_ref, d_ref, nm_ref, nv_ref):
    nm = ADAM_B1 * m_ref[...] + (1.0 - ADAM_B1) * gv
    nv = ADAM_B2 * v_ref[...] + (1.0 - ADAM_B2) * (gv * gv)
    m_hat = nm / (1.0 - ADAM_B1 ** ADAM_STEP)
    v_hat = nv / (1.0 - ADAM_B2 ** ADAM_STEP)
    d_ref[...] = -ADAM_LR * (m_hat / (jnp.sqrt(v_hat) + ADAM_EPS) + ADAM_WD * w_ref[...])
    nm_ref[...] = nm
    nv_ref[...] = nv


def _adamw(name, w, g, m, v):
    R, C = w.shape
    tr = _fit(R, 256)

    def body(w_ref, g_ref, m_ref, v_ref, d_ref, nm_ref, nv_ref):
        _adamw_update(g_ref[...], w_ref, m_ref, v_ref, d_ref, nm_ref, nv_ref)

    return _rows_call(name, body, R, tr, [(a, _row(tr, C)) for a in (w, g, m, v)], [(C, F32)] * 3)


def _adamw_halves(name, w, mine, theirs, m, v, c_idx):
    R, C = w.shape
    rh = R // 2
    tr = _fit(rh, 256)
    nrb = rh // tr

    def kern(c_ref, w_ref, a_ref, b_ref, m_ref, v_ref, g_ref, d_ref, nm_ref, nv_ref):
        gv = jnp.where(pl.program_id(0) // nrb == c_ref[0], a_ref[...], b_ref[...])
        g_ref[...] = gv
        _adamw_update(gv, w_ref, m_ref, v_ref, d_ref, nm_ref, nv_ref)

    full = pl.BlockSpec((tr, C), lambda r, c_ref: (r, 0))
    pick = lambda own: pl.BlockSpec((tr, C), lambda r, c_ref: (jnp.where((r // nrb == c_ref[0]) == own, r % nrb, 0), 0))
    return pl.pallas_call(
        kern, name=name,
        grid_spec=pltpu.PrefetchScalarGridSpec(
            num_scalar_prefetch=1, grid=(R // tr,),
            in_specs=[full, pick(True), pick(False), full, full], out_specs=[full] * 4),
        out_shape=[jax.ShapeDtypeStruct((R, C), F32)] * 4,
        compiler_params=_params(("arbitrary",)),
    )(c_idx, w, mine, theirs, m, v)


def _sigmoid(z):
    return 1.0 / (1.0 + jnp.exp(-z))


def kernel(x, p, g_mix, w_in, qn_gain, kn_gain, w_branch_a, w_branch_b, w_out, g_mlp, w_up, w_down, g_ple, w_ple_gate, w_ple_proj, loss_target, m_g_mix, m_w_in, m_qn_gain, m_kn_gain, m_w_branch_a, m_w_branch_b, m_w_out, m_g_mlp, m_w_up, m_w_down, m_g_ple, m_w_ple_gate, m_w_ple_proj, v_g_mix, v_w_in, v_qn_gain, v_kn_gain, v_w_branch_a, v_w_branch_b, v_w_out, v_g_mlp, v_w_up, v_w_down, v_g_ple, v_w_ple_gate, v_w_ple_proj):
    S, D = x.shape[1], x.shape[2]
    HW = w_branch_a.shape[1]
    x2d, tgt, p2d = x.reshape(S, D), loss_target.reshape(S, D), p.reshape(S, p.shape[-1])
    big = {"w_in": w_in, "w_branch_a": w_branch_a, "w_branch_b": w_branch_b, "w_out": w_out, "w_up": w_up,
           "w_down": w_down, "w_ple_gate": w_ple_gate, "w_ple_proj": w_ple_proj}
    moments = {"w_in": (m_w_in, v_w_in), "w_branch_a": (m_w_branch_a, v_w_branch_a), "w_branch_b": (m_w_branch_b, v_w_branch_b),
               "w_out": (m_w_out, v_w_out), "w_up": (m_w_up, v_w_up), "w_down": (m_w_down, v_w_down),
               "w_ple_gate": (m_w_ple_gate, v_w_ple_gate), "w_ple_proj": (m_w_ple_proj, v_w_ple_proj)}
    names = list(big)
    col_sharded = {"w_in", "w_branch_a", "w_branch_b", "w_up", "w_ple_proj"}
    shard2d = {k: w.reshape(w.shape[1], w.shape[2]) for k, w in big.items()}

    c_idx = lax.axis_index("c").astype(jnp.int32).reshape(1)
    s_idx = (2 * lax.axis_index("x") + lax.axis_index("y")).astype(jnp.int32).reshape(1)
    placed = {k: _cast_place(f"cast_{k}", shard2d[k], s_idx) for k in names}
    late = [k for k in names if k != "w_in"]
    h, (w_in_all,) = _rmsnorm_fwd("rmsnorm_mix", x2d, g_mix, _gather_exchange([placed["w_in"]]))
    W = {"w_in": w_in_all}
    cin = W["w_in"].shape[2]
    bn_in = _fit(cin, 512)
    while (2 * HW) % bn_in:
        bn_in -= 128

    (qk,) = _matmul("proj_qk", h, W["w_in"], mode="nn", bm=2048, out_dtypes=[F32], b_cshard=True, b_off=0, n_out=2 * HW, bn=bn_in, bk=D)
    (mid,) = _matmul("proj_mid", h, W["w_in"], mode="nn", bm=2048, out_dtypes=[MXU_DTYPE], b_cshard=True, b_off=2 * HW // bn_in,
                     n_out=4 * HW, bn=bn_in, bk=D)
    (sg,) = _matmul("proj_gates", h, W["w_in"], mode="nn", bm=2048, out_dtypes=[MXU_DTYPE], b_cshard=True, b_off=6 * HW // bn_in,
                    n_out=2 * D, bn=bn_in, bk=D, epilogue=lambda acc: (_sigmoid(acc),))
    tabs = _rope_tables(S)
    qa, ka = _qknorm_fwd(qk, qn_gain, kn_gain, tabs, HW)
    dil = [_dilated_fwd(qa, ka, mid, d, HW) for d in DILATIONS]
    ya, lse = _dilated_combine([o for o, _ in dil], [l for _, l in dil], HW)
    yb, sb_a, sb_sig, gathered = _sb_fwd(mid, HW, _gather_exchange([placed[k] for k in late]))
    W.update({k: (g if k in col_sharded else g.reshape(-1, g.shape[2])) for k, g in zip(late, gathered)})

    gate_blocks = D // _fit(D, 1024)
    (ua,) = _matmul("branch_a", ya, W["w_branch_a"], mode="nn", out_dtypes=[MXU_DTYPE], b_cshard=True, bn=_fit(W["w_branch_a"].shape[2], 1024))
    bn_b = _fit(W["w_branch_b"].shape[2], 1024)
    ub, merged = _matmul("branch_b_merge", yb, W["w_branch_b"], mode="nn", out_dtypes=[MXU_DTYPE, MXU_DTYPE], b_cshard=True, bn=bn_b,
                         extras=[(sg, 0), (sg, D // bn_b), (ua, 0)],
                         epilogue=lambda acc, sga, sgb, uav: (acc, sga.astype(F32) * uav.astype(F32) + sgb.astype(F32) * acc))
    (x1,) = _matmul("out_proj", merged, W["w_out"], mode="nn", out_dtypes=[F32], extras=[(x2d, 0)], epilogue=lambda acc, xv: (xv + acc,))
    hm = _rmsnorm_fwd("rmsnorm_mlp", x1, g_mlp)

    def up_epilogue(acc):
        r = jnp.maximum(acc, 0.0)
        return r * r, r

    act, rup = _matmul("mlp_up", hm, W["w_up"], mode="nn", out_dtypes=[MXU_DTYPE, MXU_DTYPE], b_cshard=True,
                       bn=_fit(W["w_up"].shape[2], 1024), bk=D, epilogue=up_epilogue)
    (x2,) = _matmul("mlp_down", act, W["w_down"], mode="nn", out_dtypes=[F32], extras=[(x1, 0)], epilogue=lambda acc, xv: (xv + acc,))
    hp = _rmsnorm_fwd("rmsnorm_ple", x2, g_ple)
    (pp,) = _matmul("ple_proj", p2d, W["w_ple_proj"], mode="nn", out_dtypes=[F32], b_cshard=True, bn=_fit(W["w_ple_proj"].shape[2], 1024))

    def ple_epilogue(acc, ppv, x2v, tv):
        s = _sigmoid(acc)
        dx3 = ((x2v + ppv * s) - tv) / D
        return dx3, dx3 * s, dx3 * ppv * (s * (1.0 - s))

    dx3, d_pp, d_gate = _matmul("ple_gate_loss", hp, W["w_ple_gate"], mode="nn", out_dtypes=[F32, MXU_DTYPE, MXU_DTYPE],
                                bm=512, extras=[(pp, 0), (x2, 0), (tgt, 0)], epilogue=ple_epilogue)

    G, G16 = {}, {}
    G["w_ple_proj"], G16["w_ple_proj"] = _matmul("grad_w_ple_proj", p2d, d_pp, mode="tn", out_dtypes=[F32, MXU_DTYPE], out_cshard=True,
                                 bn=_fit(d_pp.shape[1] // N_SHARD, 1024))
    G["w_ple_gate"], G16["w_ple_gate"] = _matmul("grad_w_ple_gate", hp, d_gate, mode="tn", out_dtypes=[F32, MXU_DTYPE])
    (d_hp,) = _matmul("ple_gate_bwd", d_gate, W["w_ple_gate"], mode="nt", out_dtypes=[F32])
    dx2, dx2_low, g_g_ple, loss_part = _rmsnorm_bwd("rmsnorm_ple_bwd", d_hp, x2, g_ple, dx3, True, True)
    G["w_down"], G16["w_down"] = _matmul("grad_w_down", act, dx2_low, mode="tn", out_dtypes=[F32, MXU_DTYPE])
    (d_up,) = _matmul("mlp_down_bwd", dx2_low, W["w_down"], mode="nt", out_dtypes=[MXU_DTYPE], extras=[(rup, 0)],
                      epilogue=lambda acc, r: (acc * (2.0 * r.astype(F32)),))
    G["w_up"], G16["w_up"] = _matmul("grad_w_up", hm, d_up, mode="tn", out_dtypes=[F32, MXU_DTYPE], out_cshard=True, bn=_fit(d_up.shape[1] // N_SHARD, 1024))
    (d_hm,) = _matmul("mlp_up_bwd", d_up, W["w_up"], mode="nt", out_dtypes=[F32], b_cshard=True, bk=_fit(W["w_up"].shape[2], 2048))
    dx1, dx1_low, g_g_mlp = _rmsnorm_bwd("rmsnorm_mlp_bwd", d_hm, x1, g_mlp, dx2, False, True)
    G["w_out"], G16["w_out"] = _matmul("grad_w_out", merged, dx1_low, mode="tn", out_dtypes=[F32, MXU_DTYPE])

    def merge_bwd(acc, sga, sgb, uav, ubv):
        sga, sgb, uav, ubv = (t.astype(F32) for t in (sga, sgb, uav, ubv))
        return acc * sga, acc * sgb, acc * uav * (sga * (1.0 - sga)), acc * ubv * (sgb * (1.0 - sgb))

    bn_m = _fit(D, 1024)
    d_ua, d_ub, d_ga, d_gb = _matmul("out_proj_bwd", dx1_low, W["w_out"], mode="nt", out_dtypes=[MXU_DTYPE] * 4, bm=512, bn=bn_m,
                                     extras=[(sg, 0), (sg, D // bn_m), (ua, 0), (ub, 0)], epilogue=merge_bwd)
    bn_br = _fit(D // N_SHARD, 1024)
    G["w_branch_a"], G16["w_branch_a"] = _matmul("grad_w_branch_a", ya, d_ua, mode="tn", out_dtypes=[F32, MXU_DTYPE], out_cshard=True, bn=bn_br)
    G["w_branch_b"], G16["w_branch_b"] = _matmul("grad_w_branch_b", yb, d_ub, mode="tn", out_dtypes=[F32, MXU_DTYPE], out_cshard=True, bn=bn_br)
    (d_ya,) = _matmul("branch_a_bwd", d_ua, W["w_branch_a"], mode="nt", out_dtypes=[F32], b_cshard=True, bk=bn_br)
    (d_yb,) = _matmul("branch_b_bwd", d_ub, W["w_branch_b"], mode="nt", out_dtypes=[F32], b_cshard=True, bk=bn_br)

    as_shards = lambda k, g: g if k in col_sharded else g.reshape(N_SHARD, -1, g.shape[1])
    (dqb, dkb, dvb), partials = _sb_bwd(mid, d_yb, sb_a, sb_sig, HW, _reduce_exchange([as_shards(k, G16[k]) for k in late],
                                                                               [as_shards(k, G[k]) for k in late]))
    dil_b = [_dilated_bwd(qa, ka, mid, d_ya, ya, lse, d, HW) for d in DILATIONS]
    d_qk, g_qn, g_kn = _qknorm_bwd(qk, qn_gain, kn_gain, tabs, [t[0] for t in dil_b], [t[1] for t in dil_b],
                                   [t[2] for t in dil_b], HW)
    dva = _dv_sum([t[3] for t in dil_b], [t[4] for t in dil_b], HW)
    d_proj = jnp.concatenate([d_qk, dva, dqb.astype(MXU_DTYPE), dkb.astype(MXU_DTYPE), dvb.astype(MXU_DTYPE), d_ga, d_gb], axis=1)
    G["w_in"], G16["w_in"] = _matmul("grad_w_in", h, d_proj, mode="tn", out_dtypes=[F32, MXU_DTYPE], out_cshard=True, bn=_fit(cin, 1280))
    (d_h,), partials_in = _matmul("proj_bwd", d_proj, W["w_in"], mode="nt", out_dtypes=[F32], b_cshard=True, bk=_fit(cin, 1280),
                                  ride=_reduce_exchange([G16["w_in"]], [G["w_in"]]))
    grad_x, g_g_mix = _rmsnorm_bwd("rmsnorm_mix_bwd", d_h, x2d, g_mix, dx1, False, False)

    mine = {k: _add_direct(f"grad_add_{k}", as_shards(k, G[k]), partials[2 * n], partials[2 * n + 1], s_idx, c_idx)
            for n, k in enumerate(late)}
    mine["w_in"] = _add_direct("grad_add_w_in", G["w_in"], partials_in[0], partials_in[1], s_idx, c_idx)
    halves = [mine[k] for k in names]
    others = _swap_reduced(halves)

    pack_w = -(-(3 * D + 3 * 128) // (8 * 128)) * 128

    def pack(v_mix, v_mlp, v_ple, v_qn, v_kn, extra):
        flat = jnp.concatenate([v_mix.reshape(-1), v_mlp.reshape(-1), v_ple.reshape(-1), v_qn.reshape(-1), v_kn.reshape(-1), extra.reshape(-1)])
        return jnp.pad(flat, (0, 8 * pack_w - flat.shape[0])).reshape(8, pack_w)

    def unpack(blk):
        flat = blk.reshape(-1)
        return (flat[:D].reshape(1, D), flat[D:2 * D].reshape(1, D), flat[2 * D:3 * D].reshape(1, D),
                flat[3 * D:3 * D + 128].reshape(1, 128), flat[3 * D + 128:3 * D + 256].reshape(1, 128), flat[3 * D + 256])

    small = _all_reduce_small(pack(g_g_mix, g_g_mlp, g_g_ple, g_qn, g_kn, loss_part))
    sw = pack(g_mix, g_mlp, g_ple, qn_gain, kn_gain, jnp.zeros((128,), F32))
    sm = pack(m_g_mix, m_g_mlp, m_g_ple, m_qn_gain, m_kn_gain, jnp.zeros((128,), F32))
    sv = pack(v_g_mix, v_g_mlp, v_g_ple, v_qn_gain, v_kn_gain, jnp.ones((128,), F32))
    s_delta, s_nm, s_nv = _adamw("adamw_small", sw, small, sm, sv)
    sg_mix, sg_mlp, sg_ple, sg_qn, sg_kn, loss = unpack(small)
    small_out = {}
    for tag, blk in (("delta", s_delta), ("new_m", s_nm), ("new_v", s_nv)):
        u = unpack(blk)
        small_out[tag] = dict(g_mix=u[0], g_mlp=u[1], g_ple=u[2], qn_gain=u[3], kn_gain=u[4])
    small_grad = dict(g_mix=sg_mix, g_mlp=sg_mlp, g_ple=sg_ple, qn_gain=sg_qn, kn_gain=sg_kn)

    big_out = {"grad": {}, "delta": {}, "new_m": {}, "new_v": {}}
    for k, mine, theirs in zip(names, halves, others):
        shape = big[k].shape
        m2, v2 = (t.reshape(shape[1], shape[2]) for t in moments[k])
        res = _adamw_halves(f"adamw_{k}", shard2d[k], mine, theirs, m2, v2, c_idx)
        for tag, t in zip(("grad", "delta", "new_m", "new_v"), res):
            big_out[tag][k] = t.reshape(shape)

    order = ["g_mix", "w_in", "qn_gain", "kn_gain", "w_branch_a", "w_branch_b", "w_out", "g_mlp", "w_up", "w_down", "g_ple",
             "w_ple_gate", "w_ple_proj"]
    outs = [loss, grad_x.reshape(x.shape)]
    outs += [small_grad[k] if k in small_grad else big_out["grad"][k] for k in order]
    for tag in ("delta", "new_m", "new_v"):
        outs += [small_out[tag][k] if k in small_grad else big_out[tag][k] for k in order]
    return tuple(outs)
```

```python
import functools

import jax
import jax.numpy as jnp
from jax import lax
from jax.experimental import pallas as pl
from jax.experimental.pallas import tpu as pltpu

F32 = jnp.float32
MXU_DTYPE = jnp.bfloat16
HEAD_DIM = 128
ROT_DIM = HEAD_DIM // 4
ROPE_THETA = 500000.0
EPS = 1e-6
DILATIONS = (1, 4, 16)
BLOCK = 128
N_SHARD = 4
ADAM_LR, ADAM_B1, ADAM_B2, ADAM_EPS, ADAM_WD, ADAM_STEP = 0.001, 0.9, 0.999, 1e-08, 0.01, 10
MXU_WIDTH = 256
V7X_VMEM_BYTES = 64 * 1024 * 1024
VMEM_LIMIT = V7X_VMEM_BYTES - 8 * 1024 * 1024
MESH = pl.DeviceIdType.MESH
NEG = -1e30
SB_BQ, SB_BK = 1024, 256
LOG2E = 1.4426950408889634
CUMSUM_PASSES = 2


def _fit(dim, pref):
    if dim <= pref:
        return dim
    b = (pref // 128) * 128
    while dim % b:
        b -= 128
    return b


def _params(sem=None):
    return pltpu.CompilerParams(dimension_semantics=sem, vmem_limit_bytes=VMEM_LIMIT)


def _dot(a, b, dims=(((1,), (0,)), ((), ()))):
    return lax.dot_general(a, b, dims, preferred_element_type=F32)


NT = (((1,), (1,)), ((), ()))
TN = (((0,), (0,)), ((), ()))


def _split_dot(x, u, passes):
    out = None
    r = x
    for p in range(passes):
        hi = r.astype(MXU_DTYPE)
        part = _dot(hi, u)
        out = part if out is None else out + part
        if p + 1 < passes:
            r = r - hi.astype(F32)
    return out


def _matmul(name, a, b, *, mode, out_dtypes, bm=1024, bn=1024, bk=2048, b_cshard=False, b_off=0, n_out=None,
            extras=(), epilogue=None, out_cshard=False, ride=None):
    if mode == "tn":
        K, M = a.shape
        N = b.shape[1]
    else:
        M, K = a.shape
        if mode == "nn":
            N = n_out if n_out is not None else (N_SHARD * b.shape[2] if b_cshard else b.shape[1])
        else:
            N = b.shape[1] if b_cshard else b.shape[0]
    bm, bn, bk = _fit(M, bm), _fit(N, bn), _fit(K, bk)
    nk = K // bk
    grid = (M // bm, N // bn, nk)

    if mode == "tn":
        a_spec = pl.BlockSpec((bk, bm), lambda i, j, k: (k, i))
        b_spec = pl.BlockSpec((bk, bn), lambda i, j, k: (k, j))
        dims = TN
    elif mode == "nn":
        a_spec = pl.BlockSpec((bm, bk), lambda i, j, k: (i, k))
        if b_cshard:
            cb = b.shape[2] // bn
            b_spec = pl.BlockSpec((None, bk, bn), lambda i, j, k: ((j + b_off) // cb, k, (j + b_off) % cb))
        else:
            b_spec = pl.BlockSpec((bk, bn), lambda i, j, k: (k, j + b_off))
        dims = (((1,), (0,)), ((), ()))
    else:
        a_spec = pl.BlockSpec((bm, bk), lambda i, j, k: (i, k))
        if b_cshard:
            cb = b.shape[2] // bk
            b_spec = pl.BlockSpec((None, bn, bk), lambda i, j, k: (k // cb, j, k % cb))
        else:
            b_spec = pl.BlockSpec((bn, bk), lambda i, j, k: (j, k))
        dims = NT

    ex_arrays = [e[0] for e in extras]
    ex_specs = [pl.BlockSpec((bm, bn), functools.partial(lambda i, j, k, off: (i, j + off), off=e[1])) for e in extras]
    if out_cshard:
        cbo = (N // N_SHARD) // bn
        out_shape = [jax.ShapeDtypeStruct((N_SHARD, M, N // N_SHARD), dt) for dt in out_dtypes]
        out_specs = [pl.BlockSpec((None, bm, bn), lambda i, j, k: (j // cbo, i, j % cbo)) for _ in out_dtypes]
    else:
        out_shape = [jax.ShapeDtypeStruct((M, N), dt) for dt in out_dtypes]
        out_specs = [pl.BlockSpec((bm, bn), lambda i, j, k: (i, j)) for _ in out_dtypes]
    ne, no = len(extras), len(out_dtypes)
    cw = MXU_WIDTH if bn % MXU_WIDTH == 0 else bn

    def kern(*refs):
        a_ref, b_ref = refs[0], refs[1]
        ex_refs = refs[2:2 + ne]
        o_refs = refs[2 + ne:2 + ne + no]
        av = a_ref[...].astype(MXU_DTYPE)

        def finish(acc):
            vals = (acc,) * no if epilogue is None else epilogue(acc, *[r[...] for r in ex_refs])
            for r, v in zip(o_refs, vals):
                r[...] = v.astype(r.dtype)

        if nk == 1:
            finish(_dot(av, b_ref[...].astype(MXU_DTYPE), dims))
        else:
            acc_ref = refs[2 + ne + no]
            k = pl.program_id(2)

            @pl.when(k == 0)
            def _():
                acc_ref[...] = jnp.zeros_like(acc_ref)

            for c0 in range(0, bn, cw):
                bv = b_ref[c0:c0 + cw, :] if mode == "nt" else b_ref[:, c0:c0 + cw]
                acc_ref[:, c0:c0 + cw] += _dot(av, bv.astype(MXU_DTYPE), dims)

            @pl.when(k == nk - 1)
            def _():
                finish(acc_ref[...])

    outs, rode = _host_call(kern, name, grid, [a_spec, b_spec] + ex_specs, out_specs, out_shape, (a, b, *ex_arrays), ride,
                            [pltpu.VMEM((bm, bn), F32)] if nk > 1 else [])
    return outs if ride is None else (outs, rode)


def _row(tr, w, coff=0):
    return pl.BlockSpec((tr, w), lambda i: (i, coff))


def _vec(w):
    return pl.BlockSpec((1, w), lambda i: (0, 0))


def _rows_call(name, body, n_rows, tr, ins, outs, accs=()):
    n_in, n_out = len(ins), len(outs)

    def kern(*refs):
        acc_refs = refs[n_in + n_out:]
        if acc_refs:
            @pl.when(pl.program_id(0) == 0)
            def _():
                for r in acc_refs:
                    r[...] = jnp.zeros_like(r)
        body(*refs)

    out_shape = [jax.ShapeDtypeStruct((n_rows, w), dt) for w, dt in outs] + [jax.ShapeDtypeStruct((1, w), F32) for w in accs]
    out_specs = [_row(tr, w) for w, _ in outs] + [_vec(w) for w in accs]
    return pl.pallas_call(
        kern, name=name, grid=(n_rows // tr,),
        in_specs=[s for _, s in ins], out_specs=out_specs, out_shape=out_shape,
        compiler_params=_params(("arbitrary",)),
    )(*[a for a, _ in ins])


def _rmsnorm_fwd(name, x, g, ex=None):
    S, D = x.shape
    tr = _fit(S, 256)

    def body(x_ref, g_ref, h_ref):
        xv = x_ref[...]
        r = lax.rsqrt(jnp.mean(xv * xv, axis=1, keepdims=True) + EPS)
        h_ref[...] = ((xv * r) * g_ref[...]).astype(h_ref.dtype)

    (h,), rode = _host_call(body, name, (S // tr,), [_row(tr, D), _vec(D)], [_row(tr, D)], [jax.ShapeDtypeStruct((S, D), MXU_DTYPE)],
                            (x, g), ex)
    return h if ex is None else (h, rode)


def _rmsnorm_bwd(name, dh, x, g, resid, with_loss, with_copy):
    S, D = x.shape
    tr = _fit(S, 256)

    def body(dh_ref, x_ref, g_ref, res_ref, dx_ref, *rest):
        copy_ref = rest[:1] if with_copy else ()
        dg_ref, loss_ref = rest[len(copy_ref)], rest[len(copy_ref) + 1:]
        xv = x_ref[...]
        r = lax.rsqrt(jnp.mean(xv * xv, axis=1, keepdims=True) + EPS)
        dhv = dh_ref[...]
        u = dhv * g_ref[...]
        xr = xv * r
        dx = r * u - xr * (r * r) * jnp.mean(xv * u, axis=1, keepdims=True)
        resv = res_ref[...]
        dx_ref[...] = resv + dx
        for r in copy_ref:
            r[...] = (resv + dx).astype(r.dtype)
        dg_ref[...] += jnp.sum(dhv * xr, axis=0, keepdims=True)
        if with_loss:
            loss_ref[0][...] += (0.5 * D) * jnp.sum(resv * resv)

    outs = _rows_call(name, body, S, tr, [(dh, _row(tr, D)), (x, _row(tr, D)), (g, _vec(D)), (resid, _row(tr, D))],
                      [(D, F32)] + ([(D, MXU_DTYPE)] if with_copy else []), accs=(D, 128) if with_loss else (D,))
    return outs


def _rope_tables(S):
    half = ROT_DIM // 2
    pos = jnp.arange(S, dtype=F32)
    inv = ROPE_THETA ** (-jnp.arange(0, ROT_DIM, 2, dtype=F32) / ROT_DIM)
    ang = pos[:, None] * inv[None, :]
    cos, sin = jnp.cos(ang), jnp.sin(ang)
    pad = HEAD_DIM - ROT_DIM
    ctab = jnp.concatenate([cos, cos, jnp.ones((S, pad), F32)], axis=1)
    atab = jnp.concatenate([-sin, jnp.zeros((S, pad + half), F32)], axis=1)
    btab = jnp.concatenate([jnp.zeros((S, half), F32), sin, jnp.zeros((S, pad), F32)], axis=1)
    return ctab, atab, btab


def _qknorm_fwd(qk, qn, kn, tabs, HW):
    S = qk.shape[0]
    tr = _fit(S, 256)
    half = ROT_DIM // 2

    def body(qk_ref, qn_ref, kn_ref, c_ref, a_ref, b_ref, q_out, k_out):
        ct, at, bt = c_ref[...], a_ref[...], b_ref[...]
        for part, (g_ref, o_ref) in enumerate(((qn_ref, q_out), (kn_ref, k_out))):
            gv = g_ref[...]
            for h in range(HW // HEAD_DIM):
                xh = qk_ref[:, part * HW + h * HEAD_DIM: part * HW + (h + 1) * HEAD_DIM]
                r = lax.rsqrt(jnp.mean(xh * xh, axis=1, keepdims=True) + EPS)
                y = (xh * r) * gv
                o = y * ct + pltpu.roll(y, HEAD_DIM - half, 1) * at + pltpu.roll(y, half, 1) * bt
                o_ref[:, h * HEAD_DIM:(h + 1) * HEAD_DIM] = o.astype(o_ref.dtype)

    ins = [(qk, _row(tr, 2 * HW)), (qn, _vec(HEAD_DIM)), (kn, _vec(HEAD_DIM))] + [(t, _row(tr, HEAD_DIM)) for t in tabs]
    return _rows_call("qknorm_fwd", body, S, tr, ins, [(HW, MXU_DTYPE), (HW, MXU_DTYPE)])


def _shift_spec(tr, w, shift, nblk):
    return pl.BlockSpec((tr, w), lambda i: (jnp.minimum(i + shift, nblk - 1), 0))


def _qknorm_bwd(qk, qn, kn, tabs, dq_parts, dk_cur, dk_prev, HW):
    S = qk.shape[0]
    tr = BLOCK
    nblk = S // tr
    half = ROT_DIM // 2
    nd = len(DILATIONS)

    def body(*refs):
        qk_ref, qn_ref, kn_ref, c_ref, a_ref, b_ref = refs[:6]
        dq_refs = refs[6:6 + nd]
        dkc_refs = refs[6 + nd:6 + 2 * nd]
        dkp_refs = refs[6 + 2 * nd:6 + 3 * nd]
        d_out, dqn_ref, dkn_ref = refs[6 + 3 * nd:]
        i = pl.program_id(0)
        ct, at, bt = c_ref[...], a_ref[...], b_ref[...]
        live = [(i + d < nblk).astype(F32) for d in DILATIONS]
        for part, (g_ref, dg_ref) in enumerate(((qn_ref, dqn_ref), (kn_ref, dkn_ref))):
            gv = g_ref[...]
            dg = jnp.zeros((1, HEAD_DIM), F32)
            for h in range(HW // HEAD_DIM):
                hs = slice(h * HEAD_DIM, (h + 1) * HEAD_DIM)
                if part == 0:
                    do = dq_refs[0][:, hs] + dq_refs[1][:, hs] + dq_refs[2][:, hs]
                else:
                    do = dkc_refs[0][:, hs] + dkc_refs[1][:, hs] + dkc_refs[2][:, hs]
                    for n in range(nd):
                        do = do + dkp_refs[n][:, hs] * live[n]
                dy = do * ct + pltpu.roll(do * at, half, 1) + pltpu.roll(do * bt, HEAD_DIM - half, 1)
                xh = qk_ref[:, part * HW + h * HEAD_DIM: part * HW + (h + 1) * HEAD_DIM]
                r = lax.rsqrt(jnp.mean(xh * xh, axis=1, keepdims=True) + EPS)
                xr = xh * r
                u = dy * gv
                dx = r * u - xr * (r * r) * jnp.mean(xh * u, axis=1, keepdims=True)
                d_out[:, part * HW + h * HEAD_DIM: part * HW + (h + 1) * HEAD_DIM] = dx.astype(d_out.dtype)
                dg = dg + jnp.sum(dy * xr, axis=0, keepdims=True)
            dg_ref[...] += dg

    ins = [(qk, _row(tr, 2 * HW)), (qn, _vec(HEAD_DIM)), (kn, _vec(HEAD_DIM))] + [(t, _row(tr, HEAD_DIM)) for t in tabs]
    ins += [(a, _row(tr, HW)) for a in dq_parts] + [(a, _row(tr, HW)) for a in dk_cur]
    ins += [(a, _shift_spec(tr, HW, d, nblk)) for a, d in zip(dk_prev, DILATIONS)]
    return _rows_call("qknorm_bwd", body, S, tr, ins, [(2 * HW, MXU_DTYPE)], accs=(HEAD_DIM, HEAD_DIM))


def _dv_sum(dv_cur, dv_prev, HW):
    S = dv_cur[0].shape[0]
    tr = BLOCK
    nblk = S // tr
    nd = len(DILATIONS)

    def body(*refs):
        i = pl.program_id(0)
        out = refs[2 * nd]
        acc = refs[0][...].astype(F32) + refs[1][...].astype(F32) + refs[2][...].astype(F32)
        for n, d in enumerate(DILATIONS):
            acc = acc + refs[nd + n][...].astype(F32) * (i + d < nblk).astype(F32)
        out[...] = acc.astype(out.dtype)

    ins = [(a, _row(tr, HW)) for a in dv_cur] + [(a, _shift_spec(tr, HW, d, nblk)) for a, d in zip(dv_prev, DILATIONS)]
    return _rows_call("dilated_dv_sum", body, S, tr, ins, [(HW, MXU_DTYPE)])[0]


def _dil_geometry(d, HW):
    H = HW // HEAD_DIM
    hb = min(H, max(1, 8 // d))
    tb, w = BLOCK * d, hb * HEAD_DIM
    cur = pl.BlockSpec((tb, w), lambda n, g: (n, g))
    prev = pl.BlockSpec((tb, w), lambda n, g: (jnp.maximum(n - 1, 0), g))
    units = [(hh, r) for hh in range(hb) for r in range(d)]
    return H // hb, hb, tb, cur, prev, units


def _dil_mask(n):
    qi = lax.broadcasted_iota(jnp.int32, (BLOCK, 2 * BLOCK), 0)
    ki = lax.broadcasted_iota(jnp.int32, (BLOCK, 2 * BLOCK), 1)
    return (ki >= qi) & (ki <= qi + BLOCK) & ((ki >= BLOCK) | (n > 0))


def _dil_cols(hh):
    return slice(hh * HEAD_DIM, (hh + 1) * HEAD_DIM)


def _dil_stage(d, ref, buf, row0=0):
    if d > 1:
        for hh in range(buf.shape[0]):
            buf[hh, row0:row0 + ref.shape[0], :] = ref[:, _dil_cols(hh)].astype(F32)


def _dil_unstage(d, buf, ref):
    if d > 1:
        for hh in range(buf.shape[0]):
            ref[:, _dil_cols(hh)] = buf[hh].astype(ref.dtype)


def _dil_read(d, ref, buf, hh, r):
    return ref[:, _dil_cols(hh)].astype(F32) if d == 1 else buf[hh, pl.ds(r, BLOCK, stride=d), :]


def _dil_write(d, ref, buf, hh, r, val):
    if d == 1:
        ref[:, _dil_cols(hh)] = val.astype(ref.dtype)
    else:
        buf[hh, pl.ds(r, BLOCK, stride=d), :] = val


def _dil_operands(d, tb, units, q_ref, kc_ref, kp_ref, vc_ref, vp_ref, qs, kf, vf):
    if d == 1:
        window = lambda p_ref, c_ref, hh: jnp.concatenate([p_ref[:, _dil_cols(hh)], c_ref[:, _dil_cols(hh)]], axis=0)
        return ([q_ref[:, _dil_cols(hh)] for hh, _ in units], [window(kp_ref, kc_ref, hh) for hh, _ in units],
                [window(vp_ref, vc_ref, hh) for hh, _ in units])
    _dil_stage(d, q_ref, qs)
    _dil_stage(d, kp_ref, kf)
    _dil_stage(d, kc_ref, kf, tb)
    _dil_stage(d, vp_ref, vf)
    _dil_stage(d, vc_ref, vf, tb)
    qu = [qs[hh, pl.ds(r, BLOCK, stride=d), :].astype(MXU_DTYPE) for hh, r in units]
    ku = [kf[hh, pl.ds(r, 2 * BLOCK, stride=d), :].astype(MXU_DTYPE) for hh, r in units]
    vu = [vf[hh, pl.ds(r, 2 * BLOCK, stride=d), :].astype(MXU_DTYPE) for hh, r in units]
    return qu, ku, vu


def _dilated_fwd(q, k, mid, d, HW):
    S = q.shape[0]
    scale = HEAD_DIM ** -0.5
    ng, hb, tb, cur, prev, units = _dil_geometry(d, HW)

    def kern(q_ref, kc_ref, kp_ref, vc_ref, vp_ref, o_ref, l_ref, qs, kf, vf, os_, ls):
        mask = _dil_mask(pl.program_id(0))
        qu, ku, vu = _dil_operands(d, tb, units, q_ref, kc_ref, kp_ref, vc_ref, vp_ref, qs, kf, vf)
        sc = [jnp.where(mask, _dot(a, b, NT) * scale, NEG) for a, b in zip(qu, ku)]
        mx = [jnp.max(t, axis=1, keepdims=True) for t in sc]
        ex = [jnp.exp(t - m) for t, m in zip(sc, mx)]
        den = [jnp.sum(t, axis=1, keepdims=True) for t in ex]
        out = [_dot(t.astype(MXU_DTYPE), v) / dn for t, v, dn in zip(ex, vu, den)]
        for (hh, r), o, m, dn in zip(units, out, mx, den):
            _dil_write(d, o_ref, os_, hh, r, o)
            _dil_write(d, l_ref, ls, hh, r, jnp.broadcast_to(m + jnp.log(dn), (BLOCK, HEAD_DIM)))
        _dil_unstage(d, os_, o_ref)
        _dil_unstage(d, ls, l_ref)

    return pl.pallas_call(
        kern, name=f"dilated_fwd_d{d}", grid=(S // tb, ng),
        in_specs=[cur, cur, prev, cur, prev],
        out_specs=[cur, cur],
        out_shape=[jax.ShapeDtypeStruct((S, HW), F32)] * 2,
        scratch_shapes=[pltpu.VMEM((hb, tb, HEAD_DIM), F32)] + [pltpu.VMEM((hb, 2 * tb, HEAD_DIM), F32)] * 2
        + [pltpu.VMEM((hb, tb, HEAD_DIM), F32)] * 2,
        compiler_params=_params(("arbitrary", "arbitrary")),
    )(q, k, k, mid, mid)


def _dilated_combine(os_, lses, HW):
    S = os_[0].shape[0]
    tr = _fit(S, 256)

    def body(o0, o1, o2, l0, l1, l2, ya_ref, lse_ref):
        a, b, c = l0[...], l1[...], l2[...]
        mx = jnp.maximum(jnp.maximum(a, b), c)
        ea, eb, ec = jnp.exp(a - mx), jnp.exp(b - mx), jnp.exp(c - mx)
        tot = ea + eb + ec
        ya_ref[...] = (ea * o0[...] + eb * o1[...] + ec * o2[...]) / tot
        lse_ref[...] = mx + jnp.log(tot)

    ins = [(a, _row(tr, HW)) for a in list(os_) + list(lses)]
    return _rows_call("dilated_combine", body, S, tr, ins, [(HW, F32), (HW, F32)])


def _dilated_bwd(q, k, mid, dya, ya, lse, d, HW):
    S = q.shape[0]
    scale = HEAD_DIM ** -0.5
    ng, hb, tb, cur, prev, units = _dil_geometry(d, HW)

    def kern(q_ref, kc_ref, kp_ref, vc_ref, vp_ref, dy_ref, y_ref, l_ref, dq_ref, dkc_ref, dkp_ref, dvc_ref, dvp_ref,
             qs, kf, vf, dys, ys, ls, dqs, dkcs, dkps, dvcs, dvps):
        mask = _dil_mask(pl.program_id(0))
        qu, ku, vu = _dil_operands(d, tb, units, q_ref, kc_ref, kp_ref, vc_ref, vp_ref, qs, kf, vf)
        _dil_stage(d, dy_ref, dys)
        _dil_stage(d, y_ref, ys)
        _dil_stage(d, l_ref, ls)
        dy = [_dil_read(d, dy_ref, dys, hh, r) for hh, r in units]
        lt = [_dil_read(d, l_ref, ls, hh, r)[:, 0:1] for hh, r in units]
        delta = [jnp.sum(t * _dil_read(d, y_ref, ys, hh, r), axis=1, keepdims=True) for t, (hh, r) in zip(dy, units)]
        dyb = [t.astype(MXU_DTYPE) for t in dy]
        p = [jnp.where(mask, jnp.exp(_dot(a, b, NT) * scale - l), 0.0) for a, b, l in zip(qu, ku, lt)]
        ds = [(t * (_dot(g, v, NT) - dl) * scale).astype(MXU_DTYPE) for t, g, v, dl in zip(p, dyb, vu, delta)]
        dq = [_dot(t, b) for t, b in zip(ds, ku)]
        dk = [_dot(t, a, TN) for t, a in zip(ds, qu)]
        dv = [_dot(t.astype(MXU_DTYPE), g, TN) for t, g in zip(p, dyb)]
        for (hh, r), tq, tk, tv in zip(units, dq, dk, dv):
            _dil_write(d, dq_ref, dqs, hh, r, tq)
            _dil_write(d, dkp_ref, dkps, hh, r, tk[0:BLOCK])
            _dil_write(d, dkc_ref, dkcs, hh, r, tk[BLOCK:2 * BLOCK])
            _dil_write(d, dvp_ref, dvps, hh, r, tv[0:BLOCK])
            _dil_write(d, dvc_ref, dvcs, hh, r, tv[BLOCK:2 * BLOCK])
        for buf, ref in ((dqs, dq_ref), (dkcs, dkc_ref), (dkps, dkp_ref), (dvcs, dvc_ref), (dvps, dvp_ref)):
            _dil_unstage(d, buf, ref)

    return pl.pallas_call(
        kern, name=f"dilated_bwd_d{d}", grid=(S // tb, ng),
        in_specs=[cur, cur, prev, cur, prev, cur, cur, cur],
        out_specs=[cur] * 5,
        out_shape=[jax.ShapeDtypeStruct((S, HW), F32)] * 3 + [jax.ShapeDtypeStruct((S, HW), MXU_DTYPE)] * 2,
        scratch_shapes=[pltpu.VMEM((hb, tb, HEAD_DIM), F32)] + [pltpu.VMEM((hb, 2 * tb, HEAD_DIM), F32)] * 2
        + [pltpu.VMEM((hb, tb, HEAD_DIM), F32)] * 8,
        compiler_params=_params(("arbitrary", "arbitrary")),
    )(q, k, k, mid, mid, dya, ya, lse)


def _softplus_parts(z2):
    m = jnp.minimum(-z2, 0.0) - jnp.log(1.0 + jnp.exp2(-jnp.abs(z2))) * LOG2E
    return m, m + z2


def _sb_specs(S, H):
    q_spec = pl.BlockSpec((SB_BQ, HEAD_DIM), lambda h, i: (i, H + h))
    k_spec = pl.BlockSpec((S, HEAD_DIM), lambda h, i: (0, 2 * H + h))
    v_spec = pl.BlockSpec((S, HEAD_DIM), lambda h, i: (0, 3 * H + h))
    o_spec = pl.BlockSpec((SB_BQ, HEAD_DIM), lambda h, i: (i, h))
    return q_spec, k_spec, v_spec, o_spec


def _sb_tri(relation):
    tri_r = lax.broadcasted_iota(jnp.int32, (SB_BK, SB_BK), 0)
    tri_c = lax.broadcasted_iota(jnp.int32, (SB_BK, SB_BK), 1)
    return relation(tri_r, tri_c).astype(MXU_DTYPE)


def _sb_scratch():
    return [pltpu.VMEM((2, SB_BQ, SB_BQ), MXU_DTYPE), pltpu.VMEM((2, SB_BQ, SB_BQ), MXU_DTYPE), pltpu.SemaphoreType.DMA((2, 2))]


def _sb_fwd(mid, HW, ex=None):
    S = mid.shape[0]
    H = HW // HEAD_DIM
    BQ, CH = SB_BQ, SB_BK
    NC, nq = BQ // CH, S // BQ
    scale = HEAD_DIM ** -0.5
    q_spec, k_spec, v_spec, o_spec = _sb_specs(S, H)

    def kern(q_ref, k_ref, v_ref, o_ref, a_hbm, s_hbm, abuf, sbuf, sems):
        h, i = pl.program_id(0), pl.program_id(1)
        q = q_ref[...]
        upper = _sb_tri(lambda r, c: r > c)
        row = lax.broadcasted_iota(jnp.int32, (BQ, BQ), 0)
        causal = lax.broadcasted_iota(jnp.int32, (BQ, BQ), 1) < row

        def save(slot, j):
            return [pltpu.make_async_copy(buf.at[slot], hbm.at[h, i, j], sems.at[w, slot])
                    for w, (buf, hbm) in enumerate(((abuf, a_hbm), (sbuf, s_hbm)))]

        def block(n, run, acc, masked):
            j = i - n
            ks = pl.multiple_of(j * BQ, BQ)
            m, l = _softplus_parts(_dot(q, k_ref[pl.ds(ks, BQ), :], NT) * (scale * LOG2E))
            if masked:
                m = jnp.where(causal, m, 0.0)
            parts = []
            for c in reversed(range(NC)):
                mc = m[:, c * CH:(c + 1) * CH]
                parts.append(l[:, c * CH:(c + 1) * CH] + (_split_dot(mc, upper, CUMSUM_PASSES) + run))
                run = run + jnp.sum(mc, axis=1, keepdims=True)
            a = jnp.exp2(jnp.concatenate(parts[::-1], axis=1))
            sig = jnp.exp2(l)
            if masked:
                a = jnp.where(causal, a, 0.0)
                sig = jnp.where(causal, sig, 0.0)
            ab = a.astype(MXU_DTYPE)
            slot = n % 2
            if not masked:
                @pl.when(n >= 2)
                def _():
                    for cp in save(slot, j):
                        cp.wait()
            abuf[slot] = ab
            sbuf[slot] = sig.astype(MXU_DTYPE)
            for cp in save(slot, j):
                cp.start()
            return run, acc + _dot(ab, v_ref[pl.ds(ks, BQ), :])

        run, acc = block(0, jnp.zeros((BQ, 1), F32), jnp.zeros((BQ, HEAD_DIM), F32), True)
        run, acc = lax.fori_loop(1, i + 1, lambda n, carry: block(n, carry[0], carry[1], False), (run, acc))
        o_ref[...] = acc
        for cp in save(0, i):
            cp.wait()

        @pl.when(i >= 1)
        def _():
            for cp in save(1, i):
                cp.wait()

    tiles = jax.ShapeDtypeStruct((H, nq, nq, BQ, BQ), MXU_DTYPE)
    (o, a_t, s_t), rode = _host_call(kern, "stickbreak_fwd", (H, nq), [q_spec, k_spec, v_spec], [o_spec, ANY, ANY],
                                     [jax.ShapeDtypeStruct((S, HW), F32), tiles, tiles], (mid, mid, mid), ex, _sb_scratch())
    return o, a_t, s_t, rode


def _sb_bwd(mid, dyb, a_t, s_t, HW, ex=None):
    S = mid.shape[0]
    H = HW // HEAD_DIM
    BQ, CH = SB_BQ, SB_BK
    NC = BQ // CH
    scale = HEAD_DIM ** -0.5
    q_spec, k_spec, v_spec, o_spec = _sb_specs(S, H)
    full = pl.BlockSpec((S, HEAD_DIM), lambda h, i: (0, h))

    def kern(q_ref, k_ref, v_ref, do_ref, a_hbm, s_hbm, dq_ref, dk_ref, dv_ref, abuf, sbuf, sems):
        h, i = pl.program_id(0), pl.program_id(1)

        @pl.when(i == 0)
        def _():
            dk_ref[...] = jnp.zeros_like(dk_ref)
            dv_ref[...] = jnp.zeros_like(dv_ref)

        q = q_ref[...]
        do = do_ref[...].astype(MXU_DTYPE)
        excl = _sb_tri(lambda r, c: r < c)

        def fetch(slot, j):
            return [pltpu.make_async_copy(hbm.at[h, i, j], buf.at[slot], sems.at[w, slot])
                    for w, (buf, hbm) in enumerate(((abuf, a_hbm), (sbuf, s_hbm)))]

        for cp in fetch(0, 0):
            cp.start()

        def block(j, carry):
            prun, dq = carry
            slot = j % 2
            for cp in fetch(slot, j):
                cp.wait()

            @pl.when(j < i)
            def _():
                for cp in fetch(1 - slot, j + 1):
                    cp.start()

            ks = pl.multiple_of(j * BQ, BQ)
            k = k_ref[pl.ds(ks, BQ), :]
            ab = abuf[slot]
            p = ab.astype(F32) * _dot(do, v_ref[pl.ds(ks, BQ), :], NT)
            parts = []
            for c in range(NC):
                pc = p[:, c * CH:(c + 1) * CH]
                parts.append(_split_dot(pc, excl, CUMSUM_PASSES) + prun)
                prun = prun + jnp.sum(pc, axis=1, keepdims=True)
            before = jnp.concatenate(parts, axis=1)
            dzb = ((p - sbuf[slot].astype(F32) * (p + before)) * scale).astype(MXU_DTYPE)
            dk_ref[pl.ds(ks, BQ), :] += _dot(dzb, q, TN)
            dv_ref[pl.ds(ks, BQ), :] += _dot(ab, do, TN)
            return prun, dq + _dot(dzb, k)

        _, dq = lax.fori_loop(0, i + 1, block, (jnp.zeros((BQ, 1), F32), jnp.zeros((BQ, HEAD_DIM), F32)))
        dq_ref[...] = dq

    grads, rode = _host_call(kern, "stickbreak_bwd", (H, S // BQ), [q_spec, k_spec, v_spec, o_spec, ANY, ANY], [o_spec, full, full],
                             [jax.ShapeDtypeStruct((S, HW), F32)] * 3, (mid, mid, mid, dyb, a_t, s_t), ex, _sb_scratch())
    return grads, rode


ANY = pl.BlockSpec(memory_space=pl.ANY)


def _place():
    x, y, c = lax.axis_index("x"), lax.axis_index("y"), lax.axis_index("c")
    chips = [(1 - x, y), (x, 1 - y), (1 - x, 1 - y)]
    return x, y, c, chips


def _half(ref, shard, hc, rh):
    return ref.at[shard, pl.ds(pl.multiple_of(hc * rh, 8), rh), :]


def _cast_place(name, w, s_idx):
    R, C = w.shape
    tr = _fit(R, 256)

    def kern(s_ref, w_ref, o_ref):
        o_ref[...] = w_ref[...].astype(o_ref.dtype)

    return pl.pallas_call(
        kern, name=name,
        grid_spec=pltpu.PrefetchScalarGridSpec(
            num_scalar_prefetch=1, grid=(R // tr,),
            in_specs=[pl.BlockSpec((tr, C), lambda r, s_ref: (r, 0))],
            out_specs=pl.BlockSpec((None, tr, C), lambda r, s_ref: (s_ref[0], r, 0))),
        out_shape=jax.ShapeDtypeStruct((N_SHARD, R, C), MXU_DTYPE),
        compiler_params=_params(("arbitrary",)),
    )(s_idx, w)


class _Exchange:
    def __init__(self, inputs, out_shape, aliases, scratch, phases):
        self.inputs, self.out_shape, self.aliases, self.scratch, self.phases = inputs, out_shape, aliases, scratch, phases


def _host_call(kern, name, grid, in_specs, out_specs, out_shape, operands, ex, scratch=()):
    sem = ("arbitrary",) * len(grid)
    if ex is None:
        return pl.pallas_call(kern, name=name, grid=grid, in_specs=in_specs, out_specs=out_specs, out_shape=out_shape,
                              scratch_shapes=list(scratch), compiler_params=_params(sem))(*operands), []
    n_in, n_out, ri, ro, ns = len(in_specs), len(out_specs), len(ex.inputs), len(ex.out_shape), len(scratch)
    nsteps, nph = 1, len(ex.phases)
    for size in grid:
        nsteps *= size

    def body(*refs):
        r_in, r_out = refs[n_in:n_in + ri], refs[n_in + ri + n_out:n_in + ri + n_out + ro]
        host_scratch, ex_scratch = refs[n_in + ri + n_out + ro:][:ns], refs[n_in + ri + n_out + ro + ns:]
        step = 0
        for axis, size in enumerate(grid):
            step = step * size + pl.program_id(axis)
        for kph, phase in enumerate(ex.phases):
            pl.when(step == (kph * (nsteps - 1)) // (nph - 1))(functools.partial(phase, r_in, r_out, ex_scratch))
        kern(*refs[:n_in], *refs[n_in + ri:n_in + ri + n_out], *host_scratch)

    outs = pl.pallas_call(
        body, name=name, grid=grid, in_specs=list(in_specs) + [ANY] * ri, out_specs=list(out_specs) + [ANY] * ro,
        out_shape=list(out_shape) + list(ex.out_shape), scratch_shapes=list(scratch) + list(ex.scratch),
        input_output_aliases={n_in + a: n_out + b for a, b in ex.aliases.items()},
        compiler_params=_params(sem),
    )(*operands, *ex.inputs)
    return outs[:n_out], outs[n_out:]


def _gather_exchange(bufs):
    n = len(bufs)

    def between_chips(outs, sems, i, j, chip, c, shard):
        blk = _half(outs[i], shard, c, outs[i].shape[1] // 2)
        return pltpu.make_async_remote_copy(src_ref=blk, dst_ref=blk, send_sem=sems[0].at[i, j], recv_sem=sems[1].at[i, j],
                                            device_id=(chip[0], chip[1], c), device_id_type=MESH)

    def to_sibling(outs, sems, i, j, x, y, c, shard, hc):
        blk = _half(outs[i], shard, hc, outs[i].shape[1] // 2)
        return pltpu.make_async_remote_copy(src_ref=blk, dst_ref=blk, send_sem=sems[0].at[i, 3 + j], recv_sem=sems[1].at[i, 3 + j],
                                            device_id=(x, y, 1 - c), device_id_type=MESH)

    def send_mine(ins, outs, sems):
        x, y, c, chips = _place()
        for i in range(n):
            for j, chip in enumerate(chips):
                between_chips(outs, sems, i, j, chip, c, 2 * x + y).start()

    def pass_on(ins, outs, sems):
        x, y, c, chips = _place()
        for i in range(n):
            for j, chip in enumerate(chips):
                between_chips(outs, sems, i, j, chip, c, 2 * chip[0] + chip[1]).wait_recv()
                to_sibling(outs, sems, i, j, x, y, c, 2 * chip[0] + chip[1], c).start()

    def finish(ins, outs, sems):
        x, y, c, chips = _place()
        for i in range(n):
            for j, chip in enumerate(chips):
                to_sibling(outs, sems, i, j, x, y, c, 2 * chip[0] + chip[1], 1 - c).wait_recv()
        for i in range(n):
            for j, chip in enumerate(chips):
                between_chips(outs, sems, i, j, chip, c, 2 * x + y).wait_send()
                to_sibling(outs, sems, i, j, x, y, c, 2 * chip[0] + chip[1], c).wait_send()

    return _Exchange(list(bufs), [jax.ShapeDtypeStruct(b.shape, b.dtype) for b in bufs], {i: i for i in range(n)},
                     [pltpu.SemaphoreType.DMA((n, 6)), pltpu.SemaphoreType.DMA((n, 6))], [send_mine, pass_on, finish])


def _reduce_exchange(g16, g32):
    n = len(g16)

    def copies(ins, outs, sems):
        x, y, c, _ = _place()
        for i in range(n):
            rh = ins[i].shape[1] // 2
            for r in range(1, 8):
                px, py, pc = x ^ ((r >> 2) & 1), y ^ ((r >> 1) & 1), c ^ (r & 1)
                src = _half(ins[i] if r > 1 else ins[n + i], 2 * px + py, pc, rh)
                dst = outs[2 * i + 1].at[r - 2] if r > 1 else outs[2 * i]
                yield pltpu.make_async_remote_copy(src_ref=src, dst_ref=dst, send_sem=sems[0].at[i, r - 1], recv_sem=sems[1].at[i, r - 1],
                                                   device_id=(px, py, pc), device_id_type=MESH)

    def start(ins, outs, sems):
        for cp in copies(ins, outs, sems):
            cp.start()

    def finish(ins, outs, sems):
        for cp in copies(ins, outs, sems):
            cp.wait()

    out_shape = []
    for g in g16:
        rh, C = g.shape[1] // 2, g.shape[2]
        out_shape += [jax.ShapeDtypeStruct((rh, C), F32), jax.ShapeDtypeStruct((6, rh, C), g.dtype)]
    return _Exchange(list(g16) + list(g32), out_shape, {},
                     [pltpu.SemaphoreType.DMA((n, 7)), pltpu.SemaphoreType.DMA((n, 7))], [start, finish])


def _add_direct(name, g32, from_sibling, from_chips, s_idx, c_idx):
    _, R, C = g32.shape
    rh = R // 2
    tr = _fit(rh, 256)
    nrb = rh // tr

    def kern(s_ref, c_ref, g_ref, a_ref, b_ref, out_ref):
        acc = g_ref[...] + a_ref[...]
        for k in range(6):
            acc = acc + b_ref[k].astype(F32)
        out_ref[...] = acc

    return pl.pallas_call(
        kern, name=name,
        grid_spec=pltpu.PrefetchScalarGridSpec(
            num_scalar_prefetch=2, grid=(nrb,),
            in_specs=[pl.BlockSpec((None, tr, C), lambda r, s_ref, c_ref: (s_ref[0], c_ref[0] * nrb + r, 0)),
                      pl.BlockSpec((tr, C), lambda r, s_ref, c_ref: (r, 0)),
                      pl.BlockSpec((6, tr, C), lambda r, s_ref, c_ref: (0, r, 0))],
            out_specs=pl.BlockSpec((tr, C), lambda r, s_ref, c_ref: (r, 0))),
        out_shape=jax.ShapeDtypeStruct((rh, C), F32),
        compiler_params=_params(("arbitrary",)),
    )(s_idx, c_idx, g32, from_sibling, from_chips)


def _swap_reduced(halves):
    n = len(halves)

    def body(*refs):
        ins, outs = refs[:n], refs[n:2 * n]
        send, recv = refs[2 * n:]
        x, y, c, _ = _place()
        copies = []
        for i in range(n):
            cp = pltpu.make_async_remote_copy(src_ref=ins[i], dst_ref=outs[i], send_sem=send.at[i], recv_sem=recv.at[i],
                                              device_id=(x, y, 1 - c), device_id_type=MESH)
            cp.start()
            copies.append(cp)
        for cp in copies:
            cp.wait()

    return pl.pallas_call(
        body, name="grad_swap_reduced",
        in_specs=[ANY] * n, out_specs=[ANY] * n,
        out_shape=[jax.ShapeDtypeStruct(h.shape, F32) for h in halves],
        scratch_shapes=[pltpu.SemaphoreType.DMA((n,)), pltpu.SemaphoreType.DMA((n,))],
    )(*halves)


def _all_reduce_small(v):
    rows, W = v.shape
    ndev = 8

    def body(v_ref, out_ref, buf, send, recv):
        x, y, c, _ = _place()
        me = 4 * x + 2 * y + c
        buf[me] = v_ref[...]
        copies = []
        for r in range(1, ndev):
            fx, fy, fc = (r >> 2) & 1, (r >> 1) & 1, r & 1
            peer = (x ^ fx, y ^ fy, c ^ fc)
            cp = pltpu.make_async_remote_copy(src_ref=v_ref, dst_ref=buf.at[me], send_sem=send.at[r - 1], recv_sem=recv.at[r - 1],
                                              device_id=peer, device_id_type=MESH)
            cp.start()
            copies.append(cp)
        for cp in copies:
            cp.wait()
        acc = buf[0]
        for k in range(1, ndev):
            acc = acc + buf[k]
        out_ref[...] = acc

    return pl.pallas_call(
        body, name="small_all_reduce",
        in_specs=[pl.BlockSpec(memory_space=pltpu.VMEM)], out_specs=pl.BlockSpec(memory_space=pltpu.VMEM),
        out_shape=jax.ShapeDtypeStruct((rows, W), F32),
        scratch_shapes=[pltpu.VMEM((ndev, rows, W), F32), pltpu.SemaphoreType.DMA((ndev - 1,)), pltpu.SemaphoreType.DMA((ndev - 1,))],
    )(v)


def _adamw_update(gv, w_ref, m_ref, v_ref, d_ref, nm_ref, nv_ref):
    nm = ADAM_B1 * m_ref[...] + (1.0 - ADAM_B1) * gv
    nv = ADAM_B2 * v_ref[...] + (1.0 - ADAM_B2) * (gv * gv)
    m_hat = nm / (1.0 - ADAM_B1 ** ADAM_STEP)
    v_hat = nv / (1.0 - ADAM_B2 ** ADAM_STEP)
    d_ref[...] = -ADAM_LR * (m_hat / (jnp.sqrt(v_hat) + ADAM_EPS) + ADAM_WD * w_ref[...])
    nm_ref[...] = nm
    nv_ref[...] = nv


def _adamw(name, w, g, m, v):
    R, C = w.shape
    tr = _fit(R, 256)

    def body(w_ref, g_ref, m_ref, v_ref, d_ref, nm_ref, nv_ref):
        _adamw_update(g_ref[...], w_ref, m_ref, v_ref, d_ref, nm_ref, nv_ref)

    return _rows_call(name, body, R, tr, [(a, _row(tr, C)) for a in (w, g, m, v)], [(C, F32)] * 3)


def _adamw_halves(name, w, mine, theirs, m, v, c_idx):
    R, C = w.shape
    rh = R // 2
    tr = _fit(rh, 256)
    nrb = rh // tr

    def kern(c_ref, w_ref, a_ref, b_ref, m_ref, v_ref, g_ref, d_ref, nm_ref, nv_ref):
        gv = jnp.where(pl.program_id(0) // nrb == c_ref[0], a_ref[...], b_ref[...])
        g_ref[...] = gv
        _adamw_update(gv, w_ref, m_ref, v_ref, d_ref, nm_ref, nv_ref)

    full = pl.BlockSpec((tr, C), lambda r, c_ref: (r, 0))
    pick = lambda own: pl.BlockSpec((tr, C), lambda r, c_ref: (jnp.where((r // nrb == c_ref[0]) == own, r % nrb, 0), 0))
    return pl.pallas_call(
        kern, name=name,
        grid_spec=pltpu.PrefetchScalarGridSpec(
            num_scalar_prefetch=1, grid=(R // tr,),
            in_specs=[full, pick(True), pick(False), full, full], out_specs=[full] * 4),
        out_shape=[jax.ShapeDtypeStruct((R, C), F32)] * 4,
        compiler_params=_params(("arbitrary",)),
    )(c_idx, w, mine, theirs, m, v)


def _sigmoid(z):
    return 1.0 / (1.0 + jnp.exp(-z))


def kernel(x, p, g_mix, w_in, qn_gain, kn_gain, w_branch_a, w_branch_b, w_out, g_mlp, w_up, w_down, g_ple, w_ple_gate, w_ple_proj, loss_target, m_g_mix, m_w_in, m_qn_gain, m_kn_gain, m_w_branch_a, m_w_branch_b, m_w_out, m_g_mlp, m_w_up, m_w_down, m_g_ple, m_w_ple_gate, m_w_ple_proj, v_g_mix, v_w_in, v_qn_gain, v_kn_gain, v_w_branch_a, v_w_branch_b, v_w_out, v_g_mlp, v_w_up, v_w_down, v_g_ple, v_w_ple_gate, v_w_ple_proj):
    S, D = x.shape[1], x.shape[2]
    HW = w_branch_a.shape[1]
    x2d, tgt, p2d = x.reshape(S, D), loss_target.reshape(S, D), p.reshape(S, p.shape[-1])
    big = {"w_in": w_in, "w_branch_a": w_branch_a, "w_branch_b": w_branch_b, "w_out": w_out, "w_up": w_up,
           "w_down": w_down, "w_ple_gate": w_ple_gate, "w_ple_proj": w_ple_proj}
    moments = {"w_in": (m_w_in, v_w_in), "w_branch_a": (m_w_branch_a, v_w_branch_a), "w_branch_b": (m_w_branch_b, v_w_branch_b),
               "w_out": (m_w_out, v_w_out), "w_up": (m_w_up, v_w_up), "w_down": (m_w_down, v_w_down),
               "w_ple_gate": (m_w_ple_gate, v_w_ple_gate), "w_ple_proj": (m_w_ple_proj, v_w_ple_proj)}
    names = list(big)
    col_sharded = {"w_in", "w_branch_a", "w_branch_b", "w_up", "w_ple_proj"}
    shard2d = {k: w.reshape(w.shape[1], w.shape[2]) for k, w in big.items()}

    c_idx = lax.axis_index("c").astype(jnp.int32).reshape(1)
    s_idx = (2 * lax.axis_index("x") + lax.axis_index("y")).astype(jnp.int32).reshape(1)
    placed = {k: _cast_place(f"cast_{k}", shard2d[k], s_idx) for k in names}
    late = [k for k in names if k != "w_in"]
    h, (w_in_all,) = _rmsnorm_fwd("rmsnorm_mix", x2d, g_mix, _gather_exchange([placed["w_in"]]))
    W = {"w_in": w_in_all}
    cin = W["w_in"].shape[2]
    bn_in = _fit(cin, 512)
    while (2 * HW) % bn_in:
        bn_in -= 128

    (qk,) = _matmul("proj_qk", h, W["w_in"], mode="nn", bm=2048, out_dtypes=[F32], b_cshard=True, b_off=0, n_out=2 * HW, bn=bn_in, bk=D)
    (mid,) = _matmul("proj_mid", h, W["w_in"], mode="nn", bm=2048, out_dtypes=[MXU_DTYPE], b_cshard=True, b_off=2 * HW // bn_in,
                     n_out=4 * HW, bn=bn_in, bk=D)
    (sg,) = _matmul("proj_gates", h, W["w_in"], mode="nn", bm=2048, out_dtypes=[MXU_DTYPE], b_cshard=True, b_off=6 * HW // bn_in,
                    n_out=2 * D, bn=bn_in, bk=D, epilogue=lambda acc: (_sigmoid(acc),))
    tabs = _rope_tables(S)
    qa, ka = _qknorm_fwd(qk, qn_gain, kn_gain, tabs, HW)
    dil = [_dilated_fwd(qa, ka, mid, d, HW) for d in DILATIONS]
    ya, lse = _dilated_combine([o for o, _ in dil], [l for _, l in dil], HW)
    yb, sb_a, sb_sig, gathered = _sb_fwd(mid, HW, _gather_exchange([placed[k] for k in late]))
    W.update({k: (g if k in col_sharded else g.reshape(-1, g.shape[2])) for k, g in zip(late, gathered)})

    gate_blocks = D // _fit(D, 1024)
    (ua,) = _matmul("branch_a", ya, W["w_branch_a"], mode="nn", out_dtypes=[MXU_DTYPE], b_cshard=True, bn=_fit(W["w_branch_a"].shape[2], 1024))
    bn_b = _fit(W["w_branch_b"].shape[2], 1024)
    ub, merged = _matmul("branch_b_merge", yb, W["w_branch_b"], mode="nn", out_dtypes=[MXU_DTYPE, MXU_DTYPE], b_cshard=True, bn=bn_b,
                         extras=[(sg, 0), (sg, D // bn_b), (ua, 0)],
                         epilogue=lambda acc, sga, sgb, uav: (acc, sga.astype(F32) * uav.astype(F32) + sgb.astype(F32) * acc))
    (x1,) = _matmul("out_proj", merged, W["w_out"], mode="nn", out_dtypes=[F32], extras=[(x2d, 0)], epilogue=lambda acc, xv: (xv + acc,))
    hm = _rmsnorm_fwd("rmsnorm_mlp", x1, g_mlp)

    def up_epilogue(acc):
        r = jnp.maximum(acc, 0.0)
        return r * r, r

    act, rup = _matmul("mlp_up", hm, W["w_up"], mode="nn", out_dtypes=[MXU_DTYPE, MXU_DTYPE], b_cshard=True,
                       bn=_fit(W["w_up"].shape[2], 1024), bk=D, epilogue=up_epilogue)
    (x2,) = _matmul("mlp_down", act, W["w_down"], mode="nn", out_dtypes=[F32], extras=[(x1, 0)], epilogue=lambda acc, xv: (xv + acc,))
    hp = _rmsnorm_fwd("rmsnorm_ple", x2, g_ple)
    (pp,) = _matmul("ple_proj", p2d, W["w_ple_proj"], mode="nn", out_dtypes=[F32], b_cshard=True, bn=_fit(W["w_ple_proj"].shape[2], 1024))

    def ple_epilogue(acc, ppv, x2v, tv):
        s = _sigmoid(acc)
        dx3 = ((x2v + ppv * s) - tv) / D
        return dx3, dx3 * s, dx3 * ppv * (s * (1.0 - s))

    dx3, d_pp, d_gate = _matmul("ple_gate_loss", hp, W["w_ple_gate"], mode="nn", out_dtypes=[F32, MXU_DTYPE, MXU_DTYPE],
                                bm=512, extras=[(pp, 0), (x2, 0), (tgt, 0)], epilogue=ple_epilogue)

    G, G16 = {}, {}
    G["w_ple_proj"], G16["w_ple_proj"] = _matmul("grad_w_ple_proj", p2d, d_pp, mode="tn", out_dtypes=[F32, MXU_DTYPE], out_cshard=True,
                                 bn=_fit(d_pp.shape[1] // N_SHARD, 1024))
    G["w_ple_gate"], G16["w_ple_gate"] = _matmul("grad_w_ple_gate", hp, d_gate, mode="tn", out_dtypes=[F32, MXU_DTYPE])
    (d_hp,) = _matmul("ple_gate_bwd", d_gate, W["w_ple_gate"], mode="nt", out_dtypes=[F32])
    dx2, dx2_low, g_g_ple, loss_part = _rmsnorm_bwd("rmsnorm_ple_bwd", d_hp, x2, g_ple, dx3, True, True)
    G["w_down"], G16["w_down"] = _matmul("grad_w_down", act, dx2_low, mode="tn", out_dtypes=[F32, MXU_DTYPE], bk=4096)
    (d_up,) = _matmul("mlp_down_bwd", dx2_low, W["w_down"], mode="nt", out_dtypes=[MXU_DTYPE], extras=[(rup, 0)],
                      epilogue=lambda acc, r: (acc * (2.0 * r.astype(F32)),))
    G["w_up"], G16["w_up"] = _matmul("grad_w_up", hm, d_up, mode="tn", out_dtypes=[F32, MXU_DTYPE], out_cshard=True, bn=_fit(d_up.shape[1] // N_SHARD, 1024), bk=4096)
    (d_hm,) = _matmul("mlp_up_bwd", d_up, W["w_up"], mode="nt", out_dtypes=[F32], b_cshard=True, bk=_fit(W["w_up"].shape[2], 2048))
    dx1, dx1_low, g_g_mlp = _rmsnorm_bwd("rmsnorm_mlp_bwd", d_hm, x1, g_mlp, dx2, False, True)
    G["w_out"], G16["w_out"] = _matmul("grad_w_out", merged, dx1_low, mode="tn", out_dtypes=[F32, MXU_DTYPE])

    def merge_bwd(acc, sga, sgb, uav, ubv):
        sga, sgb, uav, ubv = (t.astype(F32) for t in (sga, sgb, uav, ubv))
        return acc * sga, acc * sgb, acc * uav * (sga * (1.0 - sga)), acc * ubv * (sgb * (1.0 - sgb))

    bn_m = _fit(D, 1024)
    d_ua, d_ub, d_ga, d_gb = _matmul("out_proj_bwd", dx1_low, W["w_out"], mode="nt", out_dtypes=[MXU_DTYPE] * 4, bm=512, bn=bn_m,
                                     extras=[(sg, 0), (sg, D // bn_m), (ua, 0), (ub, 0)], epilogue=merge_bwd)
    bn_br = _fit(D // N_SHARD, 1024)
    G["w_branch_a"], G16["w_branch_a"] = _matmul("grad_w_branch_a", ya, d_ua, mode="tn", out_dtypes=[F32, MXU_DTYPE], out_cshard=True, bn=bn_br)
    G["w_branch_b"], G16["w_branch_b"] = _matmul("grad_w_branch_b", yb, d_ub, mode="tn", out_dtypes=[F32, MXU_DTYPE], out_cshard=True, bn=bn_br)
    (d_ya,) = _matmul("branch_a_bwd", d_ua, W["w_branch_a"], mode="nt", out_dtypes=[F32], b_cshard=True, bk=bn_br)
    (d_yb,) = _matmul("branch_b_bwd", d_ub, W["w_branch_b"], mode="nt", out_dtypes=[F32], b_cshard=True, bk=bn_br)

    as_shards = lambda k, g: g if k in col_sharded else g.reshape(N_SHARD, -1, g.shape[1])
    (dqb, dkb, dvb), partials = _sb_bwd(mid, d_yb, sb_a, sb_sig, HW, _reduce_exchange([as_shards(k, G16[k]) for k in late],
                                                                               [as_shards(k, G[k]) for k in late]))
    dil_b = [_dilated_bwd(qa, ka, mid, d_ya, ya, lse, d, HW) for d in DILATIONS]
    d_qk, g_qn, g_kn = _qknorm_bwd(qk, qn_gain, kn_gain, tabs, [t[0] for t in dil_b], [t[1] for t in dil_b],
                                   [t[2] for t in dil_b], HW)
    dva = _dv_sum([t[3] for t in dil_b], [t[4] for t in dil_b], HW)
    d_proj = jnp.concatenate([d_qk, dva, dqb.astype(MXU_DTYPE), dkb.astype(MXU_DTYPE), dvb.astype(MXU_DTYPE), d_ga, d_gb], axis=1)
    G["w_in"], G16["w_in"] = _matmul("grad_w_in", h, d_proj, mode="tn", out_dtypes=[F32, MXU_DTYPE], out_cshard=True, bn=_fit(cin, 1280))
    (d_h,), partials_in = _matmul("proj_bwd", d_proj, W["w_in"], mode="nt", out_dtypes=[F32], b_cshard=True, bk=_fit(cin, 2560),
                                  ride=_reduce_exchange([G16["w_in"]], [G["w_in"]]))
    grad_x, g_g_mix = _rmsnorm_bwd("rmsnorm_mix_bwd", d_h, x2d, g_mix, dx1, False, False)

    mine = {k: _add_direct(f"grad_add_{k}", as_shards(k, G[k]), partials[2 * n], partials[2 * n + 1], s_idx, c_idx)
            for n, k in enumerate(late)}
    mine["w_in"] = _add_direct("grad_add_w_in", G["w_in"], partials_in[0], partials_in[1], s_idx, c_idx)
    halves = [mine[k] for k in names]
    others = _swap_reduced(halves)

    pack_w = -(-(3 * D + 3 * 128) // (8 * 128)) * 128

    def pack(v_mix, v_mlp, v_ple, v_qn, v_kn, extra):
        flat = jnp.concatenate([v_mix.reshape(-1), v_mlp.reshape(-1), v_ple.reshape(-1), v_qn.reshape(-1), v_kn.reshape(-1), extra.reshape(-1)])
        return jnp.pad(flat, (0, 8 * pack_w - flat.shape[0])).reshape(8, pack_w)

    def unpack(blk):
        flat = blk.reshape(-1)
        return (flat[:D].reshape(1, D), flat[D:2 * D].reshape(1, D), flat[2 * D:3 * D].reshape(1, D),
                flat[3 * D:3 * D + 128].reshape(1, 128), flat[3 * D + 128:3 * D + 256].reshape(1, 128), flat[3 * D + 256])

    small = _all_reduce_small(pack(g_g_mix, g_g_mlp, g_g_ple, g_qn, g_kn, loss_part))
    sw = pack(g_mix, g_mlp, g_ple, qn_gain, kn_gain, jnp.zeros((128,), F32))
    sm = pack(m_g_mix, m_g_mlp, m_g_ple, m_qn_gain, m_kn_gain, jnp.zeros((128,), F32))
    sv = pack(v_g_mix, v_g_mlp, v_g_ple, v_qn_gain, v_kn_gain, jnp.ones((128,), F32))
    s_delta, s_nm, s_nv = _adamw("adamw_small", sw, small, sm, sv)
    sg_mix, sg_mlp, sg_ple, sg_qn, sg_kn, loss = unpack(small)
    small_out = {}
    for tag, blk in (("delta", s_delta), ("new_m", s_nm), ("new_v", s_nv)):
        u = unpack(blk)
        small_out[tag] = dict(g_mix=u[0], g_mlp=u[1], g_ple=u[2], qn_gain=u[3], kn_gain=u[4])
    small_grad = dict(g_mix=sg_mix, g_mlp=sg_mlp, g_ple=sg_ple, qn_gain=sg_qn, kn_gain=sg_kn)

    big_out = {"grad": {}, "delta": {}, "new_m": {}, "new_v": {}}
    for k, mine, theirs in zip(names, halves, others):
        shape = big[k].shape
        m2, v2 = (t.reshape(shape[1], shape[2]) for t in moments[k])
        res = _adamw_halves(f"adamw_{k}", shard2d[k], mine, theirs, m2, v2, c_idx)
        for tag, t in zip(("grad", "delta", "new_m", "new_v"), res):
            big_out[tag][k] = t.reshape(shape)

    order = ["g_mix", "w_in", "qn_gain", "kn_gain", "w_branch_a", "w_branch_b", "w_out", "g_mlp", "w_up", "w_down", "g_ple",
             "w_ple_gate", "w_ple_proj"]
    outs = [loss, grad_x.reshape(x.shape)]
    outs += [small_grad[k] if k in small_grad else big_out["grad"][k] for k in order]
    for tag in ("delta", "new_m", "new_v"):
        outs += [small_out[tag][k] if k in small_grad else big_out[tag][k] for k in order]
    return tuple(outs)
```

```python
import functools

import jax
import jax.numpy as jnp
from jax import lax
from jax.experimental import pallas as pl
from jax.experimental.pallas import tpu as pltpu

F32 = jnp.float32
MXU_DTYPE = jnp.bfloat16
HEAD_DIM = 128
ROT_DIM = HEAD_DIM // 4
ROPE_THETA = 500000.0
EPS = 1e-6
DILATIONS = (1, 4, 16)
BLOCK = 128
N_SHARD = 4
ADAM_LR, ADAM_B1, ADAM_B2, ADAM_EPS, ADAM_WD, ADAM_STEP = 0.001, 0.9, 0.999, 1e-08, 0.01, 10
MXU_WIDTH = 256
V7X_VMEM_BYTES = 64 * 1024 * 1024
VMEM_LIMIT = V7X_VMEM_BYTES - 8 * 1024 * 1024
MESH = pl.DeviceIdType.MESH
NEG = -1e30
SB_BQ, SB_BK = 1024, 256
LOG2E = 1.4426950408889634
CUMSUM_PASSES = 2


def _fit(dim, pref):
    if dim <= pref:
        return dim
    b = (pref // 128) * 128
    while dim % b:
        b -= 128
    return b


def _params(sem=None):
    return pltpu.CompilerParams(dimension_semantics=sem, vmem_limit_bytes=VMEM_LIMIT)


def _dot(a, b, dims=(((1,), (0,)), ((), ()))):
    return lax.dot_general(a, b, dims, preferred_element_type=F32)


NT = (((1,), (1,)), ((), ()))
TN = (((0,), (0,)), ((), ()))


def _split_dot(x, u, passes):
    out = None
    r = x
    for p in range(passes):
        hi = r.astype(MXU_DTYPE)
        part = _dot(hi, u)
        out = part if out is None else out + part
        if p + 1 < passes:
            r = r - hi.astype(F32)
    return out


def _matmul(name, a, b, *, mode, out_dtypes, bm=1024, bn=1024, bk=2048, b_cshard=False, b_off=0, n_out=None,
            extras=(), epilogue=None, out_cshard=False, ride=None):
    if mode == "tn":
        K, M = a.shape
        N = b.shape[1]
    else:
        M, K = a.shape
        if mode == "nn":
            N = n_out if n_out is not None else (N_SHARD * b.shape[2] if b_cshard else b.shape[1])
        else:
            N = b.shape[1] if b_cshard else b.shape[0]
    bm, bn, bk = _fit(M, bm), _fit(N, bn), _fit(K, bk)
    nk = K // bk
    grid = (M // bm, N // bn, nk)

    if mode == "tn":
        a_spec = pl.BlockSpec((bk, bm), lambda i, j, k: (k, i))
        b_spec = pl.BlockSpec((bk, bn), lambda i, j, k: (k, j))
        dims = TN
    elif mode == "nn":
        a_spec = pl.BlockSpec((bm, bk), lambda i, j, k: (i, k))
        if b_cshard:
            cb = b.shape[2] // bn
            b_spec = pl.BlockSpec((None, bk, bn), lambda i, j, k: ((j + b_off) // cb, k, (j + b_off) % cb))
        else:
            b_spec = pl.BlockSpec((bk, bn), lambda i, j, k: (k, j + b_off))
        dims = (((1,), (0,)), ((), ()))
    else:
        a_spec = pl.BlockSpec((bm, bk), lambda i, j, k: (i, k))
        if b_cshard:
            cb = b.shape[2] // bk
            b_spec = pl.BlockSpec((None, bn, bk), lambda i, j, k: (k // cb, j, k % cb))
        else:
            b_spec = pl.BlockSpec((bn, bk), lambda i, j, k: (j, k))
        dims = NT

    ex_arrays = [e[0] for e in extras]
    ex_specs = [pl.BlockSpec((bm, bn), functools.partial(lambda i, j, k, off: (i, j + off), off=e[1])) for e in extras]
    if out_cshard:
        cbo = (N // N_SHARD) // bn
        out_shape = [jax.ShapeDtypeStruct((N_SHARD, M, N // N_SHARD), dt) for dt in out_dtypes]
        out_specs = [pl.BlockSpec((None, bm, bn), lambda i, j, k: (j // cbo, i, j % cbo)) for _ in out_dtypes]
    else:
        out_shape = [jax.ShapeDtypeStruct((M, N), dt) for dt in out_dtypes]
        out_specs = [pl.BlockSpec((bm, bn), lambda i, j, k: (i, j)) for _ in out_dtypes]
    ne, no = len(extras), len(out_dtypes)
    cw = MXU_WIDTH if bn % MXU_WIDTH == 0 else bn

    def kern(*refs):
        a_ref, b_ref = refs[0], refs[1]
        ex_refs = refs[2:2 + ne]
        o_refs = refs[2 + ne:2 + ne + no]
        av = a_ref[...].astype(MXU_DTYPE)

        def finish(acc):
            vals = (acc,) * no if epilogue is None else epilogue(acc, *[r[...] for r in ex_refs])
            for r, v in zip(o_refs, vals):
                r[...] = v.astype(r.dtype)

        if nk == 1:
            finish(_dot(av, b_ref[...].astype(MXU_DTYPE), dims))
        else:
            acc_ref = refs[2 + ne + no]
            k = pl.program_id(2)

            @pl.when(k == 0)
            def _():
                acc_ref[...] = jnp.zeros_like(acc_ref)

            for c0 in range(0, bn, cw):
                bv = b_ref[c0:c0 + cw, :] if mode == "nt" else b_ref[:, c0:c0 + cw]
                acc_ref[:, c0:c0 + cw] += _dot(av, bv.astype(MXU_DTYPE), dims)

            @pl.when(k == nk - 1)
            def _():
                finish(acc_ref[...])

    outs, rode = _host_call(kern, name, grid, [a_spec, b_spec] + ex_specs, out_specs, out_shape, (a, b, *ex_arrays), ride,
                            [pltpu.VMEM((bm, bn), F32)] if nk > 1 else [])
    return outs if ride is None else (outs, rode)


def _row(tr, w, coff=0):
    return pl.BlockSpec((tr, w), lambda i: (i, coff))


def _vec(w):
    return pl.BlockSpec((1, w), lambda i: (0, 0))


def _rows_call(name, body, n_rows, tr, ins, outs, accs=()):
    n_in, n_out = len(ins), len(outs)

    def kern(*refs):
        acc_refs = refs[n_in + n_out:]
        if acc_refs:
            @pl.when(pl.program_id(0) == 0)
            def _():
                for r in acc_refs:
                    r[...] = jnp.zeros_like(r)
        body(*refs)

    out_shape = [jax.ShapeDtypeStruct((n_rows, w), dt) for w, dt in outs] + [jax.ShapeDtypeStruct((1, w), F32) for w in accs]
    out_specs = [_row(tr, w) for w, _ in outs] + [_vec(w) for w in accs]
    return pl.pallas_call(
        kern, name=name, grid=(n_rows // tr,),
        in_specs=[s for _, s in ins], out_specs=out_specs, out_shape=out_shape,
        compiler_params=_params(("arbitrary",)),
    )(*[a for a, _ in ins])


def _rmsnorm_fwd(name, x, g, ex=None):
    S, D = x.shape
    tr = _fit(S, 256)

    def body(x_ref, g_ref, h_ref):
        xv = x_ref[...]
        r = lax.rsqrt(jnp.mean(xv * xv, axis=1, keepdims=True) + EPS)
        h_ref[...] = ((xv * r) * g_ref[...]).astype(h_ref.dtype)

    (h,), rode = _host_call(body, name, (S // tr,), [_row(tr, D), _vec(D)], [_row(tr, D)], [jax.ShapeDtypeStruct((S, D), MXU_DTYPE)],
                            (x, g), ex)
    return h if ex is None else (h, rode)


def _rmsnorm_bwd(name, dh, x, g, resid, with_loss, with_copy):
    S, D = x.shape
    tr = _fit(S, 256)

    def body(dh_ref, x_ref, g_ref, res_ref, dx_ref, *rest):
        copy_ref = rest[:1] if with_copy else ()
        dg_ref, loss_ref = rest[len(copy_ref)], rest[len(copy_ref) + 1:]
        xv = x_ref[...]
        r = lax.rsqrt(jnp.mean(xv * xv, axis=1, keepdims=True) + EPS)
        dhv = dh_ref[...]
        u = dhv * g_ref[...]
        xr = xv * r
        dx = r * u - xr * (r * r) * jnp.mean(xv * u, axis=1, keepdims=True)
        resv = res_ref[...]
        dx_ref[...] = resv + dx
        for r in copy_ref:
            r[...] = (resv + dx).astype(r.dtype)
        dg_ref[...] += jnp.sum(dhv * xr, axis=0, keepdims=True)
        if with_loss:
            loss_ref[0][...] += (0.5 * D) * jnp.sum(resv * resv)

    outs = _rows_call(name, body, S, tr, [(dh, _row(tr, D)), (x, _row(tr, D)), (g, _vec(D)), (resid, _row(tr, D))],
                      [(D, F32)] + ([(D, MXU_DTYPE)] if with_copy else []), accs=(D, 128) if with_loss else (D,))
    return outs


def _rope_tables(S):
    half = ROT_DIM // 2
    pos = jnp.arange(S, dtype=F32)
    inv = ROPE_THETA ** (-jnp.arange(0, ROT_DIM, 2, dtype=F32) / ROT_DIM)
    ang = pos[:, None] * inv[None, :]
    cos, sin = jnp.cos(ang), jnp.sin(ang)
    pad = HEAD_DIM - ROT_DIM
    ctab = jnp.concatenate([cos, cos, jnp.ones((S, pad), F32)], axis=1)
    atab = jnp.concatenate([-sin, jnp.zeros((S, pad + half), F32)], axis=1)
    btab = jnp.concatenate([jnp.zeros((S, half), F32), sin, jnp.zeros((S, pad), F32)], axis=1)
    return ctab, atab, btab


def _qknorm_fwd(qk, qn, kn, tabs, HW):
    S = qk.shape[0]
    tr = _fit(S, 256)
    half = ROT_DIM // 2

    def body(qk_ref, qn_ref, kn_ref, c_ref, a_ref, b_ref, q_out, k_out):
        ct, at, bt = c_ref[...], a_ref[...], b_ref[...]
        for part, (g_ref, o_ref) in enumerate(((qn_ref, q_out), (kn_ref, k_out))):
            gv = g_ref[...]
            for h in range(HW // HEAD_DIM):
                xh = qk_ref[:, part * HW + h * HEAD_DIM: part * HW + (h + 1) * HEAD_DIM]
                r = lax.rsqrt(jnp.mean(xh * xh, axis=1, keepdims=True) + EPS)
                y = (xh * r) * gv
                o = y * ct + pltpu.roll(y, HEAD_DIM - half, 1) * at + pltpu.roll(y, half, 1) * bt
                o_ref[:, h * HEAD_DIM:(h + 1) * HEAD_DIM] = o.astype(o_ref.dtype)

    ins = [(qk, _row(tr, 2 * HW)), (qn, _vec(HEAD_DIM)), (kn, _vec(HEAD_DIM))] + [(t, _row(tr, HEAD_DIM)) for t in tabs]
    return _rows_call("qknorm_fwd", body, S, tr, ins, [(HW, MXU_DTYPE), (HW, MXU_DTYPE)])


def _shift_spec(tr, w, shift, nblk):
    return pl.BlockSpec((tr, w), lambda i: (jnp.minimum(i + shift, nblk - 1), 0))


def _qknorm_bwd(qk, qn, kn, tabs, dq_parts, dk_cur, dk_prev, HW):
    S = qk.shape[0]
    tr = BLOCK
    nblk = S // tr
    half = ROT_DIM // 2
    nd = len(DILATIONS)

    def body(*refs):
        qk_ref, qn_ref, kn_ref, c_ref, a_ref, b_ref = refs[:6]
        dq_refs = refs[6:6 + nd]
        dkc_refs = refs[6 + nd:6 + 2 * nd]
        dkp_refs = refs[6 + 2 * nd:6 + 3 * nd]
        d_out, dqn_ref, dkn_ref = refs[6 + 3 * nd:]
        i = pl.program_id(0)
        ct, at, bt = c_ref[...], a_ref[...], b_ref[...]
        live = [(i + d < nblk).astype(F32) for d in DILATIONS]
        for part, (g_ref, dg_ref) in enumerate(((qn_ref, dqn_ref), (kn_ref, dkn_ref))):
            gv = g_ref[...]
            dg = jnp.zeros((1, HEAD_DIM), F32)
            for h in range(HW // HEAD_DIM):
                hs = slice(h * HEAD_DIM, (h + 1) * HEAD_DIM)
                if part == 0:
                    do = dq_refs[0][:, hs] + dq_refs[1][:, hs] + dq_refs[2][:, hs]
                else:
                    do = dkc_refs[0][:, hs] + dkc_refs[1][:, hs] + dkc_refs[2][:, hs]
                    for n in range(nd):
                        do = do + dkp_refs[n][:, hs] * live[n]
                dy = do * ct + pltpu.roll(do * at, half, 1) + pltpu.roll(do * bt, HEAD_DIM - half, 1)
                xh = qk_ref[:, part * HW + h * HEAD_DIM: part * HW + (h + 1) * HEAD_DIM]
                r = lax.rsqrt(jnp.mean(xh * xh, axis=1, keepdims=True) + EPS)
                xr = xh * r
                u = dy * gv
                dx = r * u - xr * (r * r) * jnp.mean(xh * u, axis=1, keepdims=True)
                d_out[:, part * HW + h * HEAD_DIM: part * HW + (h + 1) * HEAD_DIM] = dx.astype(d_out.dtype)
                dg = dg + jnp.sum(dy * xr, axis=0, keepdims=True)
            dg_ref[...] += dg

    ins = [(qk, _row(tr, 2 * HW)), (qn, _vec(HEAD_DIM)), (kn, _vec(HEAD_DIM))] + [(t, _row(tr, HEAD_DIM)) for t in tabs]
    ins += [(a, _row(tr, HW)) for a in dq_parts] + [(a, _row(tr, HW)) for a in dk_cur]
    ins += [(a, _shift_spec(tr, HW, d, nblk)) for a, d in zip(dk_prev, DILATIONS)]
    return _rows_call("qknorm_bwd", body, S, tr, ins, [(2 * HW, MXU_DTYPE)], accs=(HEAD_DIM, HEAD_DIM))


def _dv_sum(dv_cur, dv_prev, HW):
    S = dv_cur[0].shape[0]
    tr = BLOCK
    nblk = S // tr
    nd = len(DILATIONS)

    def body(*refs):
        i = pl.program_id(0)
        out = refs[2 * nd]
        acc = refs[0][...].astype(F32) + refs[1][...].astype(F32) + refs[2][...].astype(F32)
        for n, d in enumerate(DILATIONS):
            acc = acc + refs[nd + n][...].astype(F32) * (i + d < nblk).astype(F32)
        out[...] = acc.astype(out.dtype)

    ins = [(a, _row(tr, HW)) for a in dv_cur] + [(a, _shift_spec(tr, HW, d, nblk)) for a, d in zip(dv_prev, DILATIONS)]
    return _rows_call("dilated_dv_sum", body, S, tr, ins, [(HW, MXU_DTYPE)])[0]


def _dil_geometry(d, HW):
    H = HW // HEAD_DIM
    hb = min(H, max(1, 8 // d))
    tb, w = BLOCK * d, hb * HEAD_DIM
    cur = pl.BlockSpec((tb, w), lambda n, g: (n, g))
    prev = pl.BlockSpec((tb, w), lambda n, g: (jnp.maximum(n - 1, 0), g))
    units = [(hh, r) for hh in range(hb) for r in range(d)]
    return H // hb, hb, tb, cur, prev, units


def _dil_mask(n):
    qi = lax.broadcasted_iota(jnp.int32, (BLOCK, 2 * BLOCK), 0)
    ki = lax.broadcasted_iota(jnp.int32, (BLOCK, 2 * BLOCK), 1)
    return (ki >= qi) & (ki <= qi + BLOCK) & ((ki >= BLOCK) | (n > 0))


def _dil_cols(hh):
    return slice(hh * HEAD_DIM, (hh + 1) * HEAD_DIM)


def _dil_stage(d, ref, buf, row0=0):
    if d > 1:
        for hh in range(buf.shape[0]):
            buf[hh, row0:row0 + ref.shape[0], :] = ref[:, _dil_cols(hh)].astype(F32)


def _dil_unstage(d, buf, ref):
    if d > 1:
        for hh in range(buf.shape[0]):
            ref[:, _dil_cols(hh)] = buf[hh].astype(ref.dtype)


def _dil_read(d, ref, buf, hh, r):
    return ref[:, _dil_cols(hh)].astype(F32) if d == 1 else buf[hh, pl.ds(r, BLOCK, stride=d), :]


def _dil_write(d, ref, buf, hh, r, val):
    if d == 1:
        ref[:, _dil_cols(hh)] = val.astype(ref.dtype)
    else:
        buf[hh, pl.ds(r, BLOCK, stride=d), :] = val


def _dil_operands(d, tb, units, q_ref, kc_ref, kp_ref, vc_ref, vp_ref, qs, kf, vf):
    if d == 1:
        window = lambda p_ref, c_ref, hh: jnp.concatenate([p_ref[:, _dil_cols(hh)], c_ref[:, _dil_cols(hh)]], axis=0)
        return ([q_ref[:, _dil_cols(hh)] for hh, _ in units], [window(kp_ref, kc_ref, hh) for hh, _ in units],
                [window(vp_ref, vc_ref, hh) for hh, _ in units])
    _dil_stage(d, q_ref, qs)
    _dil_stage(d, kp_ref, kf)
    _dil_stage(d, kc_ref, kf, tb)
    _dil_stage(d, vp_ref, vf)
    _dil_stage(d, vc_ref, vf, tb)
    qu = [qs[hh, pl.ds(r, BLOCK, stride=d), :].astype(MXU_DTYPE) for hh, r in units]
    ku = [kf[hh, pl.ds(r, 2 * BLOCK, stride=d), :].astype(MXU_DTYPE) for hh, r in units]
    vu = [vf[hh, pl.ds(r, 2 * BLOCK, stride=d), :].astype(MXU_DTYPE) for hh, r in units]
    return qu, ku, vu


def _dilated_fwd(q, k, mid, d, HW):
    S = q.shape[0]
    scale = HEAD_DIM ** -0.5
    ng, hb, tb, cur, prev, units = _dil_geometry(d, HW)

    def kern(q_ref, kc_ref, kp_ref, vc_ref, vp_ref, o_ref, l_ref, qs, kf, vf, os_, ls):
        mask = _dil_mask(pl.program_id(0))
        qu, ku, vu = _dil_operands(d, tb, units, q_ref, kc_ref, kp_ref, vc_ref, vp_ref, qs, kf, vf)
        sc = [jnp.where(mask, _dot(a, b, NT) * scale, NEG) for a, b in zip(qu, ku)]
        mx = [jnp.max(t, axis=1, keepdims=True) for t in sc]
        ex = [jnp.exp(t - m) for t, m in zip(sc, mx)]
        den = [jnp.sum(t, axis=1, keepdims=True) for t in ex]
        out = [_dot(t.astype(MXU_DTYPE), v) / dn for t, v, dn in zip(ex, vu, den)]
        for (hh, r), o, m, dn in zip(units, out, mx, den):
            _dil_write(d, o_ref, os_, hh, r, o)
            _dil_write(d, l_ref, ls, hh, r, jnp.broadcast_to(m + jnp.log(dn), (BLOCK, HEAD_DIM)))
        _dil_unstage(d, os_, o_ref)
        _dil_unstage(d, ls, l_ref)

    return pl.pallas_call(
        kern, name=f"dilated_fwd_d{d}", grid=(S // tb, ng),
        in_specs=[cur, cur, prev, cur, prev],
        out_specs=[cur, cur],
        out_shape=[jax.ShapeDtypeStruct((S, HW), F32)] * 2,
        scratch_shapes=[pltpu.VMEM((hb, tb, HEAD_DIM), F32)] + [pltpu.VMEM((hb, 2 * tb, HEAD_DIM), F32)] * 2
        + [pltpu.VMEM((hb, tb, HEAD_DIM), F32)] * 2,
        compiler_params=_params(("arbitrary", "arbitrary")),
    )(q, k, k, mid, mid)


def _dilated_combine(os_, lses, HW):
    S = os_[0].shape[0]
    tr = _fit(S, 256)

    def body(o0, o1, o2, l0, l1, l2, ya_ref, lse_ref):
        a, b, c = l0[...], l1[...], l2[...]
        mx = jnp.maximum(jnp.maximum(a, b), c)
        ea, eb, ec = jnp.exp(a - mx), jnp.exp(b - mx), jnp.exp(c - mx)
        tot = ea + eb + ec
        ya_ref[...] = (ea * o0[...] + eb * o1[...] + ec * o2[...]) / tot
        lse_ref[...] = mx + jnp.log(tot)

    ins = [(a, _row(tr, HW)) for a in list(os_) + list(lses)]
    return _rows_call("dilated_combine", body, S, tr, ins, [(HW, F32), (HW, F32)])


def _dilated_bwd(q, k, mid, dya, ya, lse, d, HW):
    S = q.shape[0]
    scale = HEAD_DIM ** -0.5
    ng, hb, tb, cur, prev, units = _dil_geometry(d, HW)

    def kern(q_ref, kc_ref, kp_ref, vc_ref, vp_ref, dy_ref, y_ref, l_ref, dq_ref, dkc_ref, dkp_ref, dvc_ref, dvp_ref,
             qs, kf, vf, dys, ys, ls, dqs, dkcs, dkps, dvcs, dvps):
        mask = _dil_mask(pl.program_id(0))
        qu, ku, vu = _dil_operands(d, tb, units, q_ref, kc_ref, kp_ref, vc_ref, vp_ref, qs, kf, vf)
        _dil_stage(d, dy_ref, dys)
        _dil_stage(d, y_ref, ys)
        _dil_stage(d, l_ref, ls)
        dy = [_dil_read(d, dy_ref, dys, hh, r) for hh, r in units]
        lt = [_dil_read(d, l_ref, ls, hh, r)[:, 0:1] for hh, r in units]
        delta = [jnp.sum(t * _dil_read(d, y_ref, ys, hh, r), axis=1, keepdims=True) for t, (hh, r) in zip(dy, units)]
        dyb = [t.astype(MXU_DTYPE) for t in dy]
        p = [jnp.where(mask, jnp.exp(_dot(a, b, NT) * scale - l), 0.0) for a, b, l in zip(qu, ku, lt)]
        ds = [(t * (_dot(g, v, NT) - dl) * scale).astype(MXU_DTYPE) for t, g, v, dl in zip(p, dyb, vu, delta)]
        dq = [_dot(t, b) for t, b in zip(ds, ku)]
        dk = [_dot(t, a, TN) for t, a in zip(ds, qu)]
        dv = [_dot(t.astype(MXU_DTYPE), g, TN) for t, g in zip(p, dyb)]
        for (hh, r), tq, tk, tv in zip(units, dq, dk, dv):
            _dil_write(d, dq_ref, dqs, hh, r, tq)
            _dil_write(d, dkp_ref, dkps, hh, r, tk[0:BLOCK])
            _dil_write(d, dkc_ref, dkcs, hh, r, tk[BLOCK:2 * BLOCK])
            _dil_write(d, dvp_ref, dvps, hh, r, tv[0:BLOCK])
            _dil_write(d, dvc_ref, dvcs, hh, r, tv[BLOCK:2 * BLOCK])
        for buf, ref in ((dqs, dq_ref), (dkcs, dkc_ref), (dkps, dkp_ref), (dvcs, dvc_ref), (dvps, dvp_ref)):
            _dil_unstage(d, buf, ref)

    return pl.pallas_call(
        kern, name=f"dilated_bwd_d{d}", grid=(S // tb, ng),
        in_specs=[cur, cur, prev, cur, prev, cur, cur, cur],
        out_specs=[cur] * 5,
        out_shape=[jax.ShapeDtypeStruct((S, HW), F32)] * 3 + [jax.ShapeDtypeStruct((S, HW), MXU_DTYPE)] * 2,
        scratch_shapes=[pltpu.VMEM((hb, tb, HEAD_DIM), F32)] + [pltpu.VMEM((hb, 2 * tb, HEAD_DIM), F32)] * 2
        + [pltpu.VMEM((hb, tb, HEAD_DIM), F32)] * 8,
        compiler_params=_params(("arbitrary", "arbitrary")),
    )(q, k, k, mid, mid, dya, ya, lse)


def _softplus_parts(z2):
    m = jnp.minimum(-z2, 0.0) - jnp.log(1.0 + jnp.exp2(-jnp.abs(z2))) * LOG2E
    return m, m + z2


def _sb_specs(S, H):
    q_spec = pl.BlockSpec((SB_BQ, HEAD_DIM), lambda h, i: (i, H + h))
    k_spec = pl.BlockSpec((S, HEAD_DIM), lambda h, i: (0, 2 * H + h))
    v_spec = pl.BlockSpec((S, HEAD_DIM), lambda h, i: (0, 3 * H + h))
    o_spec = pl.BlockSpec((SB_BQ, HEAD_DIM), lambda h, i: (i, h))
    return q_spec, k_spec, v_spec, o_spec


def _sb_tri(relation):
    tri_r = lax.broadcasted_iota(jnp.int32, (SB_BK, SB_BK), 0)
    tri_c = lax.broadcasted_iota(jnp.int32, (SB_BK, SB_BK), 1)
    return relation(tri_r, tri_c).astype(MXU_DTYPE)


def _sb_scratch():
    return [pltpu.VMEM((2, SB_BQ, SB_BQ), MXU_DTYPE), pltpu.VMEM((2, SB_BQ, SB_BQ), MXU_DTYPE), pltpu.SemaphoreType.DMA((2, 2))]


def _sb_fwd(mid, HW, ex=None):
    S = mid.shape[0]
    H = HW // HEAD_DIM
    BQ, CH = SB_BQ, SB_BK
    NC, nq = BQ // CH, S // BQ
    scale = HEAD_DIM ** -0.5
    q_spec, k_spec, v_spec, o_spec = _sb_specs(S, H)

    def kern(q_ref, k_ref, v_ref, o_ref, a_hbm, s_hbm, abuf, sbuf, sems):
        h, i = pl.program_id(0), pl.program_id(1)
        q = q_ref[...]
        upper = _sb_tri(lambda r, c: r > c)
        row = lax.broadcasted_iota(jnp.int32, (BQ, BQ), 0)
        causal = lax.broadcasted_iota(jnp.int32, (BQ, BQ), 1) < row

        def save(slot, j):
            return [pltpu.make_async_copy(buf.at[slot], hbm.at[h, i, j], sems.at[w, slot])
                    for w, (buf, hbm) in enumerate(((abuf, a_hbm), (sbuf, s_hbm)))]

        def block(n, run, acc, masked):
            j = i - n
            ks = pl.multiple_of(j * BQ, BQ)
            m, l = _softplus_parts(_dot(q, k_ref[pl.ds(ks, BQ), :], NT) * (scale * LOG2E))
            if masked:
                m = jnp.where(causal, m, 0.0)
            parts = []
            for c in reversed(range(NC)):
                mc = m[:, c * CH:(c + 1) * CH]
                parts.append(l[:, c * CH:(c + 1) * CH] + (_split_dot(mc, upper, CUMSUM_PASSES) + run))
                run = run + jnp.sum(mc, axis=1, keepdims=True)
            a = jnp.exp2(jnp.concatenate(parts[::-1], axis=1))
            sig = jnp.exp2(l)
            if masked:
                a = jnp.where(causal, a, 0.0)
                sig = jnp.where(causal, sig, 0.0)
            ab = a.astype(MXU_DTYPE)
            slot = n % 2
            if not masked:
                @pl.when(n >= 2)
                def _():
                    for cp in save(slot, j):
                        cp.wait()
            abuf[slot] = ab
            sbuf[slot] = sig.astype(MXU_DTYPE)
            for cp in save(slot, j):
                cp.start()
            return run, acc + _dot(ab, v_ref[pl.ds(ks, BQ), :])

        run, acc = block(0, jnp.zeros((BQ, 1), F32), jnp.zeros((BQ, HEAD_DIM), F32), True)
        run, acc = lax.fori_loop(1, i + 1, lambda n, carry: block(n, carry[0], carry[1], False), (run, acc))
        o_ref[...] = acc
        for cp in save(0, i):
            cp.wait()

        @pl.when(i >= 1)
        def _():
            for cp in save(1, i):
                cp.wait()

    tiles = jax.ShapeDtypeStruct((H, nq, nq, BQ, BQ), MXU_DTYPE)
    (o, a_t, s_t), rode = _host_call(kern, "stickbreak_fwd", (H, nq), [q_spec, k_spec, v_spec], [o_spec, ANY, ANY],
                                     [jax.ShapeDtypeStruct((S, HW), F32), tiles, tiles], (mid, mid, mid), ex, _sb_scratch())
    return o, a_t, s_t, rode


def _sb_bwd(mid, dyb, a_t, s_t, HW, ex=None):
    S = mid.shape[0]
    H = HW // HEAD_DIM
    BQ, CH = SB_BQ, SB_BK
    NC = BQ // CH
    scale = HEAD_DIM ** -0.5
    q_spec, k_spec, v_spec, o_spec = _sb_specs(S, H)
    full = pl.BlockSpec((S, HEAD_DIM), lambda h, i: (0, h))

    def kern(q_ref, k_ref, v_ref, do_ref, a_hbm, s_hbm, dq_ref, dk_ref, dv_ref, abuf, sbuf, sems):
        h, i = pl.program_id(0), pl.program_id(1)

        @pl.when(i == 0)
        def _():
            dk_ref[...] = jnp.zeros_like(dk_ref)
            dv_ref[...] = jnp.zeros_like(dv_ref)

        q = q_ref[...]
        do = do_ref[...].astype(MXU_DTYPE)
        excl = _sb_tri(lambda r, c: r < c)

        def fetch(slot, j):
            return [pltpu.make_async_copy(hbm.at[h, i, j], buf.at[slot], sems.at[w, slot])
                    for w, (buf, hbm) in enumerate(((abuf, a_hbm), (sbuf, s_hbm)))]

        for cp in fetch(0, 0):
            cp.start()

        def block(j, carry):
            prun, dq = carry
            slot = j % 2
            for cp in fetch(slot, j):
                cp.wait()

            @pl.when(j < i)
            def _():
                for cp in fetch(1 - slot, j + 1):
                    cp.start()

            ks = pl.multiple_of(j * BQ, BQ)
            k = k_ref[pl.ds(ks, BQ), :]
            ab = abuf[slot]
            p = ab.astype(F32) * _dot(do, v_ref[pl.ds(ks, BQ), :], NT)
            parts = []
            for c in range(NC):
                pc = p[:, c * CH:(c + 1) * CH]
                parts.append(_split_dot(pc, excl, CUMSUM_PASSES) + prun)
                prun = prun + jnp.sum(pc, axis=1, keepdims=True)
            before = jnp.concatenate(parts, axis=1)
            dzb = ((p - sbuf[slot].astype(F32) * (p + before)) * scale).astype(MXU_DTYPE)
            dk_ref[pl.ds(ks, BQ), :] += _dot(dzb, q, TN)
            dv_ref[pl.ds(ks, BQ), :] += _dot(ab, do, TN)
            return prun, dq + _dot(dzb, k)

        _, dq = lax.fori_loop(0, i + 1, block, (jnp.zeros((BQ, 1), F32), jnp.zeros((BQ, HEAD_DIM), F32)))
        dq_ref[...] = dq

    grads, rode = _host_call(kern, "stickbreak_bwd", (H, S // BQ), [q_spec, k_spec, v_spec, o_spec, ANY, ANY], [o_spec, full, full],
                             [jax.ShapeDtypeStruct((S, HW), F32)] * 3, (mid, mid, mid, dyb, a_t, s_t), ex, _sb_scratch())
    return grads, rode


ANY = pl.BlockSpec(memory_space=pl.ANY)


def _place():
    x, y, c = lax.axis_index("x"), lax.axis_index("y"), lax.axis_index("c")
    chips = [(1 - x, y), (x, 1 - y), (1 - x, 1 - y)]
    return x, y, c, chips


def _half(ref, shard, hc, rh):
    return ref.at[shard, pl.ds(pl.multiple_of(hc * rh, 8), rh), :]


def _cast_place(name, w, s_idx):
    R, C = w.shape
    tr = _fit(R, 256)

    def kern(s_ref, w_ref, o_ref):
        o_ref[...] = w_ref[...].astype(o_ref.dtype)

    return pl.pallas_call(
        kern, name=name,
        grid_spec=pltpu.PrefetchScalarGridSpec(
            num_scalar_prefetch=1, grid=(R // tr,),
            in_specs=[pl.BlockSpec((tr, C), lambda r, s_ref: (r, 0))],
            out_specs=pl.BlockSpec((None, tr, C), lambda r, s_ref: (s_ref[0], r, 0))),
        out_shape=jax.ShapeDtypeStruct((N_SHARD, R, C), MXU_DTYPE),
        compiler_params=_params(("arbitrary",)),
    )(s_idx, w)


class _Exchange:
    def __init__(self, inputs, out_shape, aliases, scratch, phases):
        self.inputs, self.out_shape, self.aliases, self.scratch, self.phases = inputs, out_shape, aliases, scratch, phases


def _host_call(kern, name, grid, in_specs, out_specs, out_shape, operands, ex, scratch=()):
    sem = ("arbitrary",) * len(grid)
    if ex is None:
        return pl.pallas_call(kern, name=name, grid=grid, in_specs=in_specs, out_specs=out_specs, out_shape=out_shape,
                              scratch_shapes=list(scratch), compiler_params=_params(sem))(*operands), []
    n_in, n_out, ri, ro, ns = len(in_specs), len(out_specs), len(ex.inputs), len(ex.out_shape), len(scratch)
    nsteps, nph = 1, len(ex.phases)
    for size in grid:
        nsteps *= size

    def body(*refs):
        r_in, r_out = refs[n_in:n_in + ri], refs[n_in + ri + n_out:n_in + ri + n_out + ro]
        host_scratch, ex_scratch = refs[n_in + ri + n_out + ro:][:ns], refs[n_in + ri + n_out + ro + ns:]
        step = 0
        for axis, size in enumerate(grid):
            step = step * size + pl.program_id(axis)
        for kph, phase in enumerate(ex.phases):
            pl.when(step == (kph * (nsteps - 1)) // (nph - 1))(functools.partial(phase, r_in, r_out, ex_scratch))
        kern(*refs[:n_in], *refs[n_in + ri:n_in + ri + n_out], *host_scratch)

    outs = pl.pallas_call(
        body, name=name, grid=grid, in_specs=list(in_specs) + [ANY] * ri, out_specs=list(out_specs) + [ANY] * ro,
        out_shape=list(out_shape) + list(ex.out_shape), scratch_shapes=list(scratch) + list(ex.scratch),
        input_output_aliases={n_in + a: n_out + b for a, b in ex.aliases.items()},
        compiler_params=_params(sem),
    )(*operands, *ex.inputs)
    return outs[:n_out], outs[n_out:]


def _gather_exchange(bufs):
    n = len(bufs)

    def between_chips(outs, sems, i, j, chip, c, shard):
        blk = _half(outs[i], shard, c, outs[i].shape[1] // 2)
        return pltpu.make_async_remote_copy(src_ref=blk, dst_ref=blk, send_sem=sems[0].at[i, j], recv_sem=sems[1].at[i, j],
                                            device_id=(chip[0], chip[1], c), device_id_type=MESH)

    def to_sibling(outs, sems, i, j, x, y, c, shard, hc):
        blk = _half(outs[i], shard, hc, outs[i].shape[1] // 2)
        return pltpu.make_async_remote_copy(src_ref=blk, dst_ref=blk, send_sem=sems[0].at[i, 3 + j], recv_sem=sems[1].at[i, 3 + j],
                                            device_id=(x, y, 1 - c), device_id_type=MESH)

    def send_mine(ins, outs, sems):
        x, y, c, chips = _place()
        for i in range(n):
            for j, chip in enumerate(chips):
                between_chips(outs, sems, i, j, chip, c, 2 * x + y).start()

    def pass_on(ins, outs, sems):
        x, y, c, chips = _place()
        for i in range(n):
            for j, chip in enumerate(chips):
                between_chips(outs, sems, i, j, chip, c, 2 * chip[0] + chip[1]).wait_recv()
                to_sibling(outs, sems, i, j, x, y, c, 2 * chip[0] + chip[1], c).start()

    def finish(ins, outs, sems):
        x, y, c, chips = _place()
        for i in range(n):
            for j, chip in enumerate(chips):
                to_sibling(outs, sems, i, j, x, y, c, 2 * chip[0] + chip[1], 1 - c).wait_recv()
        for i in range(n):
            for j, chip in enumerate(chips):
                between_chips(outs, sems, i, j, chip, c, 2 * x + y).wait_send()
                to_sibling(outs, sems, i, j, x, y, c, 2 * chip[0] + chip[1], c).wait_send()

    return _Exchange(list(bufs), [jax.ShapeDtypeStruct(b.shape, b.dtype) for b in bufs], {i: i for i in range(n)},
                     [pltpu.SemaphoreType.DMA((n, 6)), pltpu.SemaphoreType.DMA((n, 6))], [send_mine, pass_on, finish])


def _reduce_exchange(g16, g32):
    n = len(g16)

    def copies(ins, outs, sems):
        x, y, c, _ = _place()
        for i in range(n):
            rh = ins[i].shape[1] // 2
            for r in range(1, 8):
                px, py, pc = x ^ ((r >> 2) & 1), y ^ ((r >> 1) & 1), c ^ (r & 1)
                src = _half(ins[i] if r > 1 else ins[n + i], 2 * px + py, pc, rh)
                dst = outs[2 * i + 1].at[r - 2] if r > 1 else outs[2 * i]
                yield pltpu.make_async_remote_copy(src_ref=src, dst_ref=dst, send_sem=sems[0].at[i, r - 1], recv_sem=sems[1].at[i, r - 1],
                                                   device_id=(px, py, pc), device_id_type=MESH)

    def start(ins, outs, sems):
        for cp in copies(ins, outs, sems):
            cp.start()

    def finish(ins, outs, sems):
        for cp in copies(ins, outs, sems):
            cp.wait()

    out_shape = []
    for g in g16:
        rh, C = g.shape[1] // 2, g.shape[2]
        out_shape += [jax.ShapeDtypeStruct((rh, C), F32), jax.ShapeDtypeStruct((6, rh, C), g.dtype)]
    return _Exchange(list(g16) + list(g32), out_shape, {},
                     [pltpu.SemaphoreType.DMA((n, 7)), pltpu.SemaphoreType.DMA((n, 7))], [start, finish])


def _add_direct(name, g32, from_sibling, from_chips, s_idx, c_idx):
    _, R, C = g32.shape
    rh = R // 2
    tr = _fit(rh, 256)
    nrb = rh // tr

    def kern(s_ref, c_ref, g_ref, a_ref, b_ref, out_ref):
        acc = g_ref[...] + a_ref[...]
        for k in range(6):
            acc = acc + b_ref[k].astype(F32)
        out_ref[...] = acc

    return pl.pallas_call(
        kern, name=name,
        grid_spec=pltpu.PrefetchScalarGridSpec(
            num_scalar_prefetch=2, grid=(nrb,),
            in_specs=[pl.BlockSpec((None, tr, C), lambda r, s_ref, c_ref: (s_ref[0], c_ref[0] * nrb + r, 0)),
                      pl.BlockSpec((tr, C), lambda r, s_ref, c_ref: (r, 0)),
                      pl.BlockSpec((6, tr, C), lambda r, s_ref, c_ref: (0, r, 0))],
            out_specs=pl.BlockSpec((tr, C), lambda r, s_ref, c_ref: (r, 0))),
        out_shape=jax.ShapeDtypeStruct((rh, C), F32),
        compiler_params=_params(("arbitrary",)),
    )(s_idx, c_idx, g32, from_sibling, from_chips)


def _swap_reduced(halves):
    n = len(halves)

    def body(*refs):
        ins, outs = refs[:n], refs[n:2 * n]
        send, recv = refs[2 * n:]
        x, y, c, _ = _place()
        copies = []
        for i in range(n):
            cp = pltpu.make_async_remote_copy(src_ref=ins[i], dst_ref=outs[i], send_sem=send.at[i], recv_sem=recv.at[i],
                                              device_id=(x, y, 1 - c), device_id_type=MESH)
            cp.start()
            copies.append(cp)
        for cp in copies:
            cp.wait()

    return pl.pallas_call(
        body, name="grad_swap_reduced",
        in_specs=[ANY] * n, out_specs=[ANY] * n,
        out_shape=[jax.ShapeDtypeStruct(h.shape, F32) for h in halves],
        scratch_shapes=[pltpu.SemaphoreType.DMA((n,)), pltpu.SemaphoreType.DMA((n,))],
    )(*halves)


def _all_reduce_small(v):
    rows, W = v.shape
    ndev = 8

    def body(v_ref, out_ref, buf, send, recv):
        x, y, c, _ = _place()
        me = 4 * x + 2 * y + c
        buf[me] = v_ref[...]
        copies = []
        for r in range(1, ndev):
            fx, fy, fc = (r >> 2) & 1, (r >> 1) & 1, r & 1
            peer = (x ^ fx, y ^ fy, c ^ fc)
            cp = pltpu.make_async_remote_copy(src_ref=v_ref, dst_ref=buf.at[me], send_sem=send.at[r - 1], recv_sem=recv.at[r - 1],
                                              device_id=peer, device_id_type=MESH)
            cp.start()
            copies.append(cp)
        for cp in copies:
            cp.wait()
        acc = buf[0]
        for k in range(1, ndev):
            acc = acc + buf[k]
        out_ref[...] = acc

    return pl.pallas_call(
        body, name="small_all_reduce",
        in_specs=[pl.BlockSpec(memory_space=pltpu.VMEM)], out_specs=pl.BlockSpec(memory_space=pltpu.VMEM),
        out_shape=jax.ShapeDtypeStruct((rows, W), F32),
        scratch_shapes=[pltpu.VMEM((ndev, rows, W), F32), pltpu.SemaphoreType.DMA((ndev - 1,)), pltpu.SemaphoreType.DMA((ndev - 1,))],
    )(v)


def _adamw_update(gv, w_ref, m_ref, v_ref, d_ref, nm_ref, nv_ref):
    nm = ADAM_B1 * m_ref[...] + (1.0 - ADAM_B1) * gv
    nv = ADAM_B2 * v_ref[...] + (1.0 - ADAM_B2) * (gv * gv)
    m_hat = nm / (1.0 - ADAM_B1 ** ADAM_STEP)
    v_hat = nv / (1.0 - ADAM_B2 ** ADAM_STEP)
    d_ref[...] = -ADAM_LR * (m_hat / (jnp.sqrt(v_hat) + ADAM_EPS) + ADAM_WD * w_ref[...])
    nm_ref[...] = nm
    nv_ref[...] = nv


def _adamw(name, w, g, m, v):
    R, C = w.shape
    tr = _fit(R, 256)

    def body(w_ref, g_ref, m_ref, v_ref, d_ref, nm_ref, nv_ref):
        _adamw_update(g_ref[...], w_ref, m_ref, v_ref, d_ref, nm_ref, nv_ref)

    return _rows_call(name, body, R, tr, [(a, _row(tr, C)) for a in (w, g, m, v)], [(C, F32)] * 3)


def _adamw_halves(name, w, mine, theirs, m, v, c_idx):
    R, C = w.shape
    rh = R // 2
    tr = _fit(rh, 256)
    nrb = rh // tr

    def kern(c_ref, w_ref, a_ref, b_ref, m_ref, v_ref, g_ref, d_ref, nm_ref, nv_ref):
        gv = jnp.where(pl.program_id(0) // nrb == c_ref[0], a_ref[...], b_ref[...])
        g_ref[...] = gv
        _adamw_update(gv, w_ref, m_ref, v_ref, d_ref, nm_ref, nv_ref)

    full = pl.BlockSpec((tr, C), lambda r, c_ref: (r, 0))
    pick = lambda own: pl.BlockSpec((tr, C), lambda r, c_ref: (jnp.where((r // nrb == c_ref[0]) == own, r % nrb, 0), 0))
    return pl.pallas_call(
        kern, name=name,
        grid_spec=pltpu.PrefetchScalarGridSpec(
            num_scalar_prefetch=1, grid=(R // tr,),
            in_specs=[full, pick(True), pick(False), full, full], out_specs=[full] * 4),
        out_shape=[jax.ShapeDtypeStruct((R, C), F32)] * 4,
        compiler_params=_params(("arbitrary",)),
    )(c_idx, w, mine, theirs, m, v)


def _sigmoid(z):
    return 1.0 / (1.0 + jnp.exp(-z))


def kernel(x, p, g_mix, w_in, qn_gain, kn_gain, w_branch_a, w_branch_b, w_out, g_mlp, w_up, w_down, g_ple, w_ple_gate, w_ple_proj, loss_target, m_g_mix, m_w_in, m_qn_gain, m_kn_gain, m_w_branch_a, m_w_branch_b, m_w_out, m_g_mlp, m_w_up, m_w_down, m_g_ple, m_w_ple_gate, m_w_ple_proj, v_g_mix, v_w_in, v_qn_gain, v_kn_gain, v_w_branch_a, v_w_branch_b, v_w_out, v_g_mlp, v_w_up, v_w_down, v_g_ple, v_w_ple_gate, v_w_ple_proj):
    S, D = x.shape[1], x.shape[2]
    HW = w_branch_a.shape[1]
    x2d, tgt, p2d = x.reshape(S, D), loss_target.reshape(S, D), p.reshape(S, p.shape[-1])
    big = {"w_in": w_in, "w_branch_a": w_branch_a, "w_branch_b": w_branch_b, "w_out": w_out, "w_up": w_up,
           "w_down": w_down, "w_ple_gate": w_ple_gate, "w_ple_proj": w_ple_proj}
    moments = {"w_in": (m_w_in, v_w_in), "w_branch_a": (m_w_branch_a, v_w_branch_a), "w_branch_b": (m_w_branch_b, v_w_branch_b),
               "w_out": (m_w_out, v_w_out), "w_up": (m_w_up, v_w_up), "w_down": (m_w_down, v_w_down),
               "w_ple_gate": (m_w_ple_gate, v_w_ple_gate), "w_ple_proj": (m_w_ple_proj, v_w_ple_proj)}
    names = list(big)
    col_sharded = {"w_in", "w_branch_a", "w_branch_b", "w_up", "w_ple_proj"}
    shard2d = {k: w.reshape(w.shape[1], w.shape[2]) for k, w in big.items()}

    c_idx = lax.axis_index("c").astype(jnp.int32).reshape(1)
    s_idx = (2 * lax.axis_index("x") + lax.axis_index("y")).astype(jnp.int32).reshape(1)
    placed = {k: _cast_place(f"cast_{k}", shard2d[k], s_idx) for k in names}
    late = [k for k in names if k != "w_in"]
    h, (w_in_all,) = _rmsnorm_fwd("rmsnorm_mix", x2d, g_mix, _gather_exchange([placed["w_in"]]))
    W = {"w_in": w_in_all}
    cin = W["w_in"].shape[2]
    bn_in = _fit(cin, 512)
    while (2 * HW) % bn_in:
        bn_in -= 128

    (qk,) = _matmul("proj_qk", h, W["w_in"], mode="nn", bm=2048, out_dtypes=[F32], b_cshard=True, b_off=0, n_out=2 * HW, bn=bn_in, bk=D)
    (mid,) = _matmul("proj_mid", h, W["w_in"], mode="nn", bm=2048, out_dtypes=[MXU_DTYPE], b_cshard=True, b_off=2 * HW // bn_in,
                     n_out=4 * HW, bn=bn_in, bk=D)
    (sg,) = _matmul("proj_gates", h, W["w_in"], mode="nn", bm=2048, out_dtypes=[MXU_DTYPE], b_cshard=True, b_off=6 * HW // bn_in,
                    n_out=2 * D, bn=bn_in, bk=D, epilogue=lambda acc: (_sigmoid(acc),))
    tabs = _rope_tables(S)
    qa, ka = _qknorm_fwd(qk, qn_gain, kn_gain, tabs, HW)
    dil = [_dilated_fwd(qa, ka, mid, d, HW) for d in DILATIONS]
    ya, lse = _dilated_combine([o for o, _ in dil], [l for _, l in dil], HW)
    yb, sb_a, sb_sig, gathered = _sb_fwd(mid, HW, _gather_exchange([placed[k] for k in late]))
    W.update({k: (g if k in col_sharded else g.reshape(-1, g.shape[2])) for k, g in zip(late, gathered)})

    gate_blocks = D // _fit(D, 1024)
    (ua,) = _matmul("branch_a", ya, W["w_branch_a"], mode="nn", out_dtypes=[MXU_DTYPE], b_cshard=True, bn=_fit(W["w_branch_a"].shape[2], 1024))
    bn_b = _fit(W["w_branch_b"].shape[2], 1024)
    ub, merged = _matmul("branch_b_merge", yb, W["w_branch_b"], mode="nn", out_dtypes=[MXU_DTYPE, MXU_DTYPE], b_cshard=True, bn=bn_b,
                         extras=[(sg, 0), (sg, D // bn_b), (ua, 0)],
                         epilogue=lambda acc, sga, sgb, uav: (acc, sga.astype(F32) * uav.astype(F32) + sgb.astype(F32) * acc))
    (x1,) = _matmul("out_proj", merged, W["w_out"], mode="nn", out_dtypes=[F32], extras=[(x2d, 0)], epilogue=lambda acc, xv: (xv + acc,))
    hm = _rmsnorm_fwd("rmsnorm_mlp", x1, g_mlp)

    def up_epilogue(acc):
        r = jnp.maximum(acc, 0.0)
        return r * r, r

    act, rup = _matmul("mlp_up", hm, W["w_up"], mode="nn", out_dtypes=[MXU_DTYPE, MXU_DTYPE], b_cshard=True,
                       bn=_fit(W["w_up"].shape[2], 1024), bk=D, epilogue=up_epilogue)
    (x2,) = _matmul("mlp_down", act, W["w_down"], mode="nn", out_dtypes=[F32], extras=[(x1, 0)], epilogue=lambda acc, xv: (xv + acc,))
    hp = _rmsnorm_fwd("rmsnorm_ple", x2, g_ple)
    (pp,) = _matmul("ple_proj", p2d, W["w_ple_proj"], mode="nn", out_dtypes=[F32], b_cshard=True, bn=_fit(W["w_ple_proj"].shape[2], 1024))

    def ple_epilogue(acc, ppv, x2v, tv):
        s = _sigmoid(acc)
        dx3 = ((x2v + ppv * s) - tv) / D
        return dx3, dx3 * s, dx3 * ppv * (s * (1.0 - s))

    dx3, d_pp, d_gate = _matmul("ple_gate_loss", hp, W["w_ple_gate"], mode="nn", out_dtypes=[F32, MXU_DTYPE, MXU_DTYPE],
                                bm=512, extras=[(pp, 0), (x2, 0), (tgt, 0)], epilogue=ple_epilogue)

    G, G16 = {}, {}
    G["w_ple_proj"], G16["w_ple_proj"] = _matmul("grad_w_ple_proj", p2d, d_pp, mode="tn", out_dtypes=[F32, MXU_DTYPE], out_cshard=True,
                                 bn=_fit(d_pp.shape[1] // N_SHARD, 1024))
    G["w_ple_gate"], G16["w_ple_gate"] = _matmul("grad_w_ple_gate", hp, d_gate, mode="tn", out_dtypes=[F32, MXU_DTYPE])
    (d_hp,) = _matmul("ple_gate_bwd", d_gate, W["w_ple_gate"], mode="nt", out_dtypes=[F32])
    dx2, dx2_low, g_g_ple, loss_part = _rmsnorm_bwd("rmsnorm_ple_bwd", d_hp, x2, g_ple, dx3, True, True)
    G["w_down"], G16["w_down"] = _matmul("grad_w_down", act, dx2_low, mode="tn", out_dtypes=[F32, MXU_DTYPE], bk=4096)
    (d_up,) = _matmul("mlp_down_bwd", dx2_low, W["w_down"], mode="nt", out_dtypes=[MXU_DTYPE], extras=[(rup, 0)],
                      epilogue=lambda acc, r: (acc * (2.0 * r.astype(F32)),))
    G["w_up"], G16["w_up"] = _matmul("grad_w_up", hm, d_up, mode="tn", out_dtypes=[F32, MXU_DTYPE], out_cshard=True, bn=_fit(d_up.shape[1] // N_SHARD, 1024), bk=4096)
    (d_hm,) = _matmul("mlp_up_bwd", d_up, W["w_up"], mode="nt", out_dtypes=[F32], b_cshard=True, bk=_fit(W["w_up"].shape[2], 2048))
    dx1, dx1_low, g_g_mlp = _rmsnorm_bwd("rmsnorm_mlp_bwd", d_hm, x1, g_mlp, dx2, False, True)
    G["w_out"], G16["w_out"] = _matmul("grad_w_out", merged, dx1_low, mode="tn", out_dtypes=[F32, MXU_DTYPE])

    def merge_bwd(acc, sga, sgb, uav, ubv):
        sga, sgb, uav, ubv = (t.astype(F32) for t in (sga, sgb, uav, ubv))
        return acc * sga, acc * sgb, acc * uav * (sga * (1.0 - sga)), acc * ubv * (sgb * (1.0 - sgb))

    bn_m = _fit(D, 1024)
    d_ua, d_ub, d_ga, d_gb = _matmul("out_proj_bwd", dx1_low, W["w_out"], mode="nt", out_dtypes=[MXU_DTYPE] * 4, bm=512, bn=bn_m,
                                     extras=[(sg, 0), (sg, D // bn_m), (ua, 0), (ub, 0)], epilogue=merge_bwd)
    bn_br = _fit(D // N_SHARD, 1024)
    G["w_branch_a"], G16["w_branch_a"] = _matmul("grad_w_branch_a", ya, d_ua, mode="tn", out_dtypes=[F32, MXU_DTYPE], out_cshard=True, bn=bn_br)
    G["w_branch_b"], G16["w_branch_b"] = _matmul("grad_w_branch_b", yb, d_ub, mode="tn", out_dtypes=[F32, MXU_DTYPE], out_cshard=True, bn=bn_br)
    (d_ya,) = _matmul("branch_a_bwd", d_ua, W["w_branch_a"], mode="nt", out_dtypes=[F32], b_cshard=True, bk=bn_br)
    (d_yb,) = _matmul("branch_b_bwd", d_ub, W["w_branch_b"], mode="nt", out_dtypes=[F32], b_cshard=True, bk=bn_br)

    as_shards = lambda k, g: g if k in col_sharded else g.reshape(N_SHARD, -1, g.shape[1])
    (dqb, dkb, dvb), partials = _sb_bwd(mid, d_yb, sb_a, sb_sig, HW, _reduce_exchange([as_shards(k, G16[k]) for k in late],
                                                                               [as_shards(k, G[k]) for k in late]))
    dil_b = [_dilated_bwd(qa, ka, mid, d_ya, ya, lse, d, HW) for d in DILATIONS]
    d_qk, g_qn, g_kn = _qknorm_bwd(qk, qn_gain, kn_gain, tabs, [t[0] for t in dil_b], [t[1] for t in dil_b],
                                   [t[2] for t in dil_b], HW)
    dva = _dv_sum([t[3] for t in dil_b], [t[4] for t in dil_b], HW)
    d_proj = jnp.concatenate([d_qk, dva, dqb.astype(MXU_DTYPE), dkb.astype(MXU_DTYPE), dvb.astype(MXU_DTYPE), d_ga, d_gb], axis=1)
    G["w_in"], G16["w_in"] = _matmul("grad_w_in", h, d_proj, mode="tn", out_dtypes=[F32, MXU_DTYPE], out_cshard=True, bn=_fit(cin, 1280))
    (d_h,), partials_in = _matmul("proj_bwd", d_proj, W["w_in"], mode="nt", out_dtypes=[F32], b_cshard=True, bk=_fit(cin, 1280),
                                  ride=_reduce_exchange([G16["w_in"]], [G["w_in"]]))
    grad_x, g_g_mix = _rmsnorm_bwd("rmsnorm_mix_bwd", d_h, x2d, g_mix, dx1, False, False)

    mine = {k: _add_direct(f"grad_add_{k}", as_shards(k, G[k]), partials[2 * n], partials[2 * n + 1], s_idx, c_idx)
            for n, k in enumerate(late)}
    mine["w_in"] = _add_direct("grad_add_w_in", G["w_in"], partials_in[0], partials_in[1], s_idx, c_idx)
    halves = [mine[k] for k in names]
    others = _swap_reduced(halves)

    pack_w = -(-(3 * D + 3 * 128) // (8 * 128)) * 128

    def pack(v_mix, v_mlp, v_ple, v_qn, v_kn, extra):
        flat = jnp.concatenate([v_mix.reshape(-1), v_mlp.reshape(-1), v_ple.reshape(-1), v_qn.reshape(-1), v_kn.reshape(-1), extra.reshape(-1)])
        return jnp.pad(flat, (0, 8 * pack_w - flat.shape[0])).reshape(8, pack_w)

    def unpack(blk):
        flat = blk.reshape(-1)
        return (flat[:D].reshape(1, D), flat[D:2 * D].reshape(1, D), flat[2 * D:3 * D].reshape(1, D),
                flat[3 * D:3 * D + 128].reshape(1, 128), flat[3 * D + 128:3 * D + 256].reshape(1, 128), flat[3 * D + 256])

    small = _all_reduce_small(pack(g_g_mix, g_g_mlp, g_g_ple, g_qn, g_kn, loss_part))
    sw = pack(g_mix, g_mlp, g_ple, qn_gain, kn_gain, jnp.zeros((128,), F32))
    sm = pack(m_g_mix, m_g_mlp, m_g_ple, m_qn_gain, m_kn_gain, jnp.zeros((128,), F32))
    sv = pack(v_g_mix, v_g_mlp, v_g_ple, v_qn_gain, v_kn_gain, jnp.ones((128,), F32))
    s_delta, s_nm, s_nv = _adamw("adamw_small", sw, small, sm, sv)
    sg_mix, sg_mlp, sg_ple, sg_qn, sg_kn, loss = unpack(small)
    small_out = {}
    for tag, blk in (("delta", s_delta), ("new_m", s_nm), ("new_v", s_nv)):
        u = unpack(blk)
        small_out[tag] = dict(g_mix=u[0], g_mlp=u[1], g_ple=u[2], qn_gain=u[3], kn_gain=u[4])
    small_grad = dict(g_mix=sg_mix, g_mlp=sg_mlp, g_ple=sg_ple, qn_gain=sg_qn, kn_gain=sg_kn)

    big_out = {"grad": {}, "delta": {}, "new_m": {}, "new_v": {}}
    for k, mine, theirs in zip(names, halves, others):
        shape = big[k].shape
        m2, v2 = (t.reshape(shape[1], shape[2]) for t in moments[k])
        res = _adamw_halves(f"adamw_{k}", shard2d[k], mine, theirs, m2, v2, c_idx)
        for tag, t in zip(("grad", "delta", "new_m", "new_v"), res):
            big_out[tag][k] = t.reshape(shape)

    order = ["g_mix", "w_in", "qn_gain", "kn_gain", "w_branch_a", "w_branch_b", "w_out", "g_mlp", "w_up", "w_down", "g_ple",
             "w_ple_gate", "w_ple_proj"]
    outs = [loss, grad_x.reshape(x.shape)]
    outs += [small_grad[k] if k in small_grad else big_out["grad"][k] for k in order]
    for tag in ("delta", "new_m", "new_v"):
        outs += [small_out[tag][k] if k in small_grad else big_out[tag][k] for k in order]
    return tuple(outs)
```
